```python
import math
import jax
import jax.numpy as jnp
from jax import lax
import numpy as np

D_MODEL = 1024
BATCH = 8
SEQ = 4096
DEPTH = 2

GRID_W = 64
CTX_LEN = 256
RMS_EPS = 1e-6
N_ADA = 6
DIFF_HEADS = 8
DIFF_HEAD_DIM = 64
DIFF_V_DIM = 2 * DIFF_HEAD_DIM
DIFF_QK_W = DIFF_HEADS * 2 * DIFF_HEAD_DIM
DIFF_V_W = DIFF_HEADS * DIFF_V_DIM
Q_BLOCK = 128
ROPE_BASE = 10000.0
ROT_AXIS_DIM = DIFF_HEAD_DIM // 2
CONV_W = D_MODEL // 2
CONV_K = 3
GLA_HEADS = 4
GLA_DK = 64
GLA_DV = 128
GLA_K_W = GLA_HEADS * GLA_DK
GLA_V_W = GLA_HEADS * GLA_DV
GLA_RANK = 16
GLA_TAU = 16.0
GLA_CHUNK = 64
N_BRANCH = 3
N_EXPERTS = 32
TOP_K = 4
D_EXPERT = D_MODEL
SWIGLU_LIMIT = 7.0
SWIGLU_ALPHA = 1.702
MOE_BLOCK = 128
SPLITS = (DIFF_QK_W, DIFF_QK_W, DIFF_V_W,
          CONV_W, CONV_W, CONV_W,
          GLA_K_W, GLA_K_W, GLA_V_W, GLA_V_W,
          2 * GLA_RANK,
          N_BRANCH * D_MODEL)
W_IN_COLS = int(sum(SPLITS))
SPLIT_POINTS = tuple(int(p) for p in np.cumsum(SPLITS)[:-1])

kernel_name = 'hybrid_diffattn_conv_gla_moe_dit'


def rms_norm(x, g):
    xf = x.astype(jnp.float32)
    y = xf * lax.rsqrt(jnp.mean(xf * xf, axis=-1, keepdims=True) + RMS_EPS)
    return (y * g.astype(jnp.float32)).astype(x.dtype)


def modulate(h, shift, scale):
    return h * (1.0 + scale) + shift


def rope_2d(x, row, col):
    inv_freq = ROPE_BASE ** (-jnp.arange(0, ROT_AXIS_DIM, 2, dtype=jnp.float32) / ROT_AXIS_DIM)

    def rotate(xp, pos):
        ang = pos.astype(jnp.float32)[:, None] * inv_freq[None, :]
        ang = jnp.concatenate([ang, ang], axis=-1)[:, None, None, :]
        cos = jnp.cos(ang).astype(xp.dtype)
        sin = jnp.sin(ang).astype(xp.dtype)
        x1, x2 = jnp.split(xp, 2, axis=-1)
        return xp * cos + jnp.concatenate([-x2, x1], axis=-1) * sin

    x_row, x_col = jnp.split(x, 2, axis=-1)
    return jnp.concatenate([rotate(x_row, row), rotate(x_col, col)], axis=-1)


def diff_attend(q, k, v, lam):
    s = jnp.einsum('bqhid,bkhid->bhiqk', q, k).astype(jnp.float32) * (DIFF_HEAD_DIM ** -0.5)
    p = jax.nn.softmax(s, axis=-1)
    a = p[:, :, 0] - lam * p[:, :, 1]
    return jnp.einsum('bhqk,bkhe->bqhe', a.astype(v.dtype), v)


def short_conv(u, w):
    up = jnp.pad(u, ((0, 0), (1, 1), (0, 0)))
    return up[:, :-2] * w[0] + up[:, 1:-1] * w[1] + up[:, 2:] * w[2]


def gla_chunked(q, k, v, g, s0):
    b, h, t, dk = q.shape
    dv = v.shape[-1]
    n = t // GLA_CHUNK
    q, k, g = (a.reshape(b, h, n, GLA_CHUNK, dk) for a in (q, k, g))
    v = v.reshape(b, h, n, GLA_CHUNK, dv)
    g_cum = jnp.cumsum(g, axis=3)
    g_tot = g_cum[:, :, :, -1]
    q_dec = q * jnp.exp(g_cum)
    k_inv = k * jnp.exp(-g_cum)
    k_end = k * jnp.exp(g_tot[:, :, :, None] - g_cum)
    lower_tri = jnp.tril(jnp.ones((GLA_CHUNK, GLA_CHUNK), dtype=bool))
    a = jnp.where(lower_tri, jnp.einsum('bhnld,bhnmd->bhnlm', q_dec, k_inv), 0.0)
    o_intra = jnp.einsum('bhnlm,bhnme->bhnle', a, v)
    s_local = jnp.einsum('bhnld,bhnle->bhnde', k_end, v)

    def step(s, inp):
        dec, s_loc = inp
        return jnp.exp(dec)[..., None] * s + s_loc, s

    s_fin, s_prev = lax.scan(step, s0, (jnp.moveaxis(g_tot, 2, 0), jnp.moveaxis(s_local, 2, 0)))
    o_inter = jnp.einsum('bhnld,bhnde->bhnle', q_dec, jnp.moveaxis(s_prev, 0, 2))
    return (o_intra + o_inter).reshape(b, h, t, dv), s_fin


def gla_bidir(q, k, v, g_f, g_b, s_f0, s_b0):
    o_f, s_f = gla_chunked(q, k, v, g_f, s_f0)
    flip = lambda a: jnp.flip(a, axis=2)
    o_b, s_b = gla_chunked(flip(q), flip(k), flip(v), flip(g_b), s_b0)
    return o_f + flip(o_b), s_f, s_b


def clamped_swiglu(hh):
    x_glu, x_lin = hh[..., ::2], hh[..., 1::2]
    x_glu = jnp.minimum(x_glu, SWIGLU_LIMIT)
    x_lin = jnp.clip(x_lin, -SWIGLU_LIMIT, SWIGLU_LIMIT)
    return x_glu * jax.nn.sigmoid(SWIGLU_ALPHA * x_glu) * (x_lin + 1.0)


def moe_ffn(h, router_w, router_b, w1, b1, w2, b2):
    t, d = h.shape
    logits = (h @ router_w + router_b).astype(jnp.float32)
    top_val, top_idx = lax.top_k(logits, TOP_K)
    gate = jax.nn.softmax(top_val, axis=-1)
    e_flat = top_idx.reshape(-1)
    g_flat = gate.reshape(-1)
    tk = t * TOP_K
    order = jnp.argsort(e_flat)
    e_sorted = e_flat[order]
    counts = jnp.bincount(e_flat, length=N_EXPERTS)
    padded = (counts + MOE_BLOCK - 1) // MOE_BLOCK * MOE_BLOCK
    pad_end = jnp.cumsum(padded)
    pad_start = pad_end - padded
    start = jnp.cumsum(counts) - counts
    dest = pad_start[e_sorted] + (jnp.arange(tk) - start[e_sorted])
    n_blocks = -(-tk // MOE_BLOCK) + N_EXPERTS
    n_rows = n_blocks * MOE_BLOCK
    row_tok = jnp.full((n_rows,), t, dtype=jnp.int32).at[dest].set((order // TOP_K).astype(jnp.int32))
    row_gate = jnp.zeros((n_rows,), jnp.float32).at[dest].set(g_flat[order])
    block_exp = jnp.minimum(jnp.searchsorted(pad_end, jnp.arange(n_blocks) * MOE_BLOCK, side='right'),
                            N_EXPERTS - 1)
    h_pad = jnp.concatenate([h, jnp.zeros((1, d), h.dtype)], axis=0)
    xs = h_pad[row_tok].reshape(n_blocks, MOE_BLOCK, d)

    def expert_block(args):
        xb, e = args
        return clamped_swiglu(xb @ w1[e] + b1[e]) @ w2[e] + b2[e]

    ys = lax.map(expert_block, (xs, block_exp)).reshape(n_rows, d)
    out = jnp.zeros((t + 1, d), ys.dtype).at[row_tok].add(ys * row_gate[:, None].astype(ys.dtype))
    return out[:t]


def token_mixer(h, hc, row, col, w_in, diff_lambda, lam_init, diff_subln_g, diff_w_out,
                conv_w, conv_w_out, gla_w_a2, gla_b_a, gla_norm_g, gla_w_out, w_o, ctx_out):
    b, s, _ = h.shape
    (d_q, d_k, d_v, cv_b, cv_c, cv_x, gl_q, gl_k, gl_v, gl_r, gl_a, gates) = jnp.split(
        h @ w_in, SPLIT_POINTS, axis=-1)
    (d_q_c, d_k_c, d_v_c, cv_b_c, cv_c_c, cv_x_c, gl_q_c, gl_k_c, gl_v_c, gl_r_c, gl_a_c, gates_c) = jnp.split(
        hc @ w_in, SPLIT_POINTS, axis=-1)

    lam_v = diff_lambda.astype(jnp.float32)
    lam = jnp.exp(jnp.sum(lam_v[0] * lam_v[1])) - jnp.exp(jnp.sum(lam_v[2] * lam_v[3])) + lam_init
    qk_heads = lambda a: a.reshape(a.shape[0], a.shape[1], DIFF_HEADS, 2, DIFF_HEAD_DIM)
    v_heads = lambda a: a.reshape(a.shape[0], a.shape[1], DIFF_HEADS, DIFF_V_DIM)
    q_lat = rope_2d(qk_heads(d_q), row, col)
    k_all = jnp.concatenate([rope_2d(qk_heads(d_k), row, col), qk_heads(d_k_c)], axis=1)
    v_all = jnp.concatenate([v_heads(d_v), v_heads(d_v_c)], axis=1)
    n_qb = s // Q_BLOCK
    q_blocks = jnp.moveaxis(q_lat.reshape(b, n_qb, Q_BLOCK, DIFF_HEADS, 2, DIFF_HEAD_DIM), 1, 0)
    o_blocks = lax.map(lambda qi: diff_attend(qi, k_all, v_all, lam), q_blocks)
    o_lat = jnp.moveaxis(o_blocks, 0, 1).reshape(b, s, DIFF_HEADS, DIFF_V_DIM)

    def diff_out(o_heads):
        o_heads = rms_norm(o_heads, diff_subln_g) * (1.0 - lam_init)
        return o_heads.reshape(o_heads.shape[0], o_heads.shape[1], DIFF_V_W) @ diff_w_out

    def conv_out(gb, gc, u):
        return (gb * short_conv(gc * u, conv_w)) @ conv_w_out

    def gla_inputs(gq, gk, gv, ga):
        t = gq.shape[1]
        heads = lambda a, dh: jnp.moveaxis(a.reshape(b, t, GLA_HEADS, dh), 2, 1).astype(jnp.float32)
        a_f, a_b = jnp.split(ga, 2, axis=-1)
        g_f = jax.nn.log_sigmoid((a_f @ gla_w_a2[0] + gla_b_a[0]).astype(jnp.float32)) / GLA_TAU
        g_b = jax.nn.log_sigmoid((a_b @ gla_w_a2[1] + gla_b_a[1]).astype(jnp.float32)) / GLA_TAU
        return (heads(gq, GLA_DK) * (GLA_DK ** -0.5), heads(gk, GLA_DK), heads(gv, GLA_DV),
                heads(g_f, GLA_DK), heads(g_b, GLA_DK))

    s0 = jnp.zeros((b, GLA_HEADS, GLA_DK, GLA_DV), jnp.float32)
    qc_, kc_, vc_, gfc_, gbc_ = gla_inputs(gl_q_c, gl_k_c, gl_v_c, gl_a_c)
    o_gla_c, s_f, s_b = gla_bidir(qc_, kc_, vc_, gfc_, gbc_, s0, s0)
    q_, k_, v_, gf_, gb_ = gla_inputs(gl_q, gl_k, gl_v, gl_a)
    o_gla, _, _ = gla_bidir(q_, k_, v_, gf_, gb_, s_f, s_b)

    def gla_out(o_heads, r):
        o_heads = rms_norm(jnp.moveaxis(o_heads, 1, 2), gla_norm_g).astype(r.dtype)
        return (o_heads.reshape(r.shape[0], r.shape[1], GLA_V_W) * jax.nn.silu(r)) @ gla_w_out

    def merge(gt, y_diff, y_conv, y_gla):
        g = jax.nn.sigmoid(gt).reshape(gt.shape[0], gt.shape[1], N_BRANCH, D_MODEL)
        return (g[:, :, 0] * y_diff + g[:, :, 1] * y_conv + g[:, :, 2] * y_gla) @ w_o

    m = merge(gates, diff_out(o_lat), conv_out(cv_b, cv_c, cv_x), gla_out(o_gla, gl_r))
    if not ctx_out:
        return m, None
    o_ctx = diff_attend(qk_heads(d_q_c), qk_heads(d_k_c), v_heads(d_v_c), lam)
    mc = merge(gates_c, diff_out(o_ctx), conv_out(cv_b_c, cv_c_c, cv_x_c), gla_out(o_gla_c, gl_r_c))
    return m, mc


def setup_inputs(seed: int = 0) -> dict:
    key = jax.random.key(seed)
    ks = jax.random.split(key, 32)
    L = DEPTH

    def nrm(k, shape, scale):
        return jax.random.normal(k, shape, jnp.float32) * scale

    return {
        'x': nrm(ks[0], (BATCH, SEQ, D_MODEL), 1.0),
        'c': nrm(ks[1], (BATCH, D_MODEL), 1.0),
        'ctx': nrm(ks[2], (BATCH, CTX_LEN, D_MODEL), 1.0),
        'c_ctx': nrm(ks[3], (D_MODEL,), 1.0),
        'ada_w': nrm(ks[4], (L, D_MODEL, N_ADA * D_MODEL), 0.5 * D_MODEL ** -0.5),
        'ada_b': nrm(ks[5], (L, N_ADA * D_MODEL), 0.01),
        'norm1_g': 1.0 + nrm(ks[6], (L, D_MODEL), 0.05),
        'norm2_g': 1.0 + nrm(ks[7], (L, D_MODEL), 0.05),
        'w_in': nrm(ks[8], (L, D_MODEL, W_IN_COLS), D_MODEL ** -0.5),
        'diff_lambda': nrm(ks[9], (L, 4, DIFF_HEAD_DIM), 0.1),
        'diff_subln_g': 1.0 + nrm(ks[10], (L, DIFF_V_DIM), 0.05),
        'diff_w_out': nrm(ks[11], (L, DIFF_V_W, D_MODEL), DIFF_V_W ** -0.5),
        'conv_w': nrm(ks[12], (L, CONV_K, CONV_W), CONV_K ** -0.5),
        'conv_w_out': nrm(ks[13], (L, CONV_W, D_MODEL), CONV_W ** -0.5),
        'gla_w_a2': nrm(ks[14], (L, 2, GLA_RANK, GLA_K_W), GLA_RANK ** -0.5),
        'gla_b_a': nrm(ks[15], (L, 2, GLA_K_W), 0.1),
        'gla_norm_g': 1.0 + nrm(ks[16], (L, GLA_DV), 0.05),
        'gla_w_out': nrm(ks[17], (L, GLA_V_W, D_MODEL), GLA_V_W ** -0.5),
        'w_o': nrm(ks[18], (L, D_MODEL, D_MODEL), D_MODEL ** -0.5),
        'router_w': nrm(ks[19], (L, D_MODEL, N_EXPERTS), D_MODEL ** -0.5),
        'router_b': nrm(ks[20], (L, N_EXPERTS), 0.01),
        'moe_w1': nrm(ks[21], (L, N_EXPERTS, D_MODEL, 2 * D_EXPERT), D_MODEL ** -0.5),
        'moe_b1': nrm(ks[22], (L, N_EXPERTS, 2 * D_EXPERT), 0.01),
        'moe_w2': nrm(ks[23], (L, N_EXPERTS, D_EXPERT, D_MODEL), D_EXPERT ** -0.5),
        'moe_b2': nrm(ks[24], (L, N_EXPERTS, D_MODEL), 0.01),
        'final_norm_g': 1.0 + nrm(ks[25], (D_MODEL,), 0.05),
    }


def reference(x, c, ctx, c_ctx, ada_w, ada_b, norm1_g, norm2_g, w_in, diff_lambda, diff_subln_g,
              diff_w_out, conv_w, conv_w_out, gla_w_a2, gla_b_a, gla_norm_g, gla_w_out, w_o,
              router_w, router_b, moe_w1, moe_b1, moe_w2, moe_b2, final_norm_g):
    b, s, d = x.shape
    rows = s // GRID_W
    row = jnp.repeat(jnp.arange(rows, dtype=jnp.int32), GRID_W)
    col = jnp.tile(jnp.arange(GRID_W, dtype=jnp.int32), rows)
    c_act = jax.nn.silu(c)
    c_ctx_act = jax.nn.silu(c_ctx)
    for layer in range(DEPTH):
        last = layer == DEPTH - 1
        lam_init = 0.8 - 0.6 * math.exp(-0.3 * layer)
        mod = c_act @ ada_w[layer] + ada_b[layer]
        mod_c = c_ctx_act @ ada_w[layer] + ada_b[layer]
        sh1, sc1, g1, sh2, sc2, g2 = jnp.split(mod[:, None, :], N_ADA, axis=-1)
        sh1c, sc1c, g1c, sh2c, sc2c, g2c = jnp.split(mod_c, N_ADA, axis=-1)
        h = modulate(rms_norm(x, norm1_g[layer]), sh1, sc1)
        hc = modulate(rms_norm(ctx, norm1_g[layer]), sh1c, sc1c)
        m, mc = token_mixer(h, hc, row, col, w_in[layer], diff_lambda[layer], lam_init,
                            diff_subln_g[layer], diff_w_out[layer], conv_w[layer], conv_w_out[layer],
                            gla_w_a2[layer], gla_b_a[layer], gla_norm_g[layer], gla_w_out[layer],
                            w_o[layer], not last)
        x = x + g1 * m
        h2 = modulate(rms_norm(x, norm2_g[layer]), sh2, sc2)
        moe_args = (router_w[layer], router_b[layer], moe_w1[layer], moe_b1[layer],
                    moe_w2[layer], moe_b2[layer])
        if last:
            f = moe_ffn(h2.reshape(b * s, d), *moe_args)
            x = x + g2 * f.reshape(b, s, d)
        else:
            ctx = ctx + g1c * mc
            h2c = modulate(rms_norm(ctx, norm2_g[layer]), sh2c, sc2c)
            tokens = jnp.concatenate([h2.reshape(b * s, d), h2c.reshape(-1, d)], axis=0)
            f = moe_ffn(tokens, *moe_args)
            x = x + g2 * f[:b * s].reshape(b, s, d)
            ctx = ctx + g2c * f[b * s:].reshape(ctx.shape)
    return rms_norm(x, final_norm_g)
```

```python
import functools
import math

import numpy as np
import jax
import jax.numpy as jnp
from jax import lax
from jax.experimental import pallas as pl
from jax.experimental.pallas import tpu as pltpu

F32 = jnp.float32
BF16 = jnp.bfloat16

D_MODEL = 1024
GRID_W = 64
RMS_EPS = 1e-6
N_ADA = 6
DIFF_HEADS = 8
DIFF_HEAD_DIM = 64
DIFF_V_DIM = 128
ROPE_BASE = 10000.0
ROT_AXIS_DIM = 32
CONV_W = 512
GLA_HEADS = 4
GLA_DK = 64
GLA_DV = 128
GLA_RANK = 16
GLA_TAU = 16.0
GLA_CHUNK = 64
N_EXPERTS = 32
TOP_K = 4
D_EXPERT = 1024
SWIGLU_LIMIT = 7.0
SWIGLU_ALPHA = 1.702

A16_Q, A16_K, A16_V, A16_GV, A16_W = 0, 1024, 2048, 3072, 3584
A32_GATES, A32_CVB, A32_CVC, A32_CVX = 0, 3072, 3584, 4096
A32_GLQ, A32_GLK, A32_GLR, A32_GLA, A32_W = 4608, 4864, 5120, 5632, 5760
W_ALL_COLS = A16_W + A32_W

Q_TILE = 256
KV_TILE = 512
MOE_BLOCK = 256
COMBINE_TILE = 128
VMEM_LIMIT = 56 * 1024 * 1024


def _cparams(sem):
    return pltpu.CompilerParams(dimension_semantics=sem, vmem_limit_bytes=VMEM_LIMIT)


def _row_tile(t, target):
    best = None
    for cand in range(16, target + 1, 16):
        if t % cand == 0:
            best = cand
    assert best is not None, t
    return best


def _sigmoid(v):
    return 1.0 / (1.0 + jnp.exp(-v))


def _ada_kernel(cc_ref, w_ref, b_ref, o_ref):
    a = cc_ref[...]
    a = a * _sigmoid(a)
    o_ref[0] = jnp.dot(a, w_ref[0], precision=lax.Precision.HIGHEST,
                       preferred_element_type=F32) + b_ref[0]


def _ada(cc, ada_w, ada_b):
    nl, d, n = ada_w.shape
    tn = 1024
    return pl.pallas_call(
        _ada_kernel,
        grid=(nl, n // tn),
        in_specs=[
            pl.BlockSpec((16, d), lambda l, j: (0, 0)),
            pl.BlockSpec((1, d, tn), lambda l, j: (l, 0, j)),
            pl.BlockSpec((1, 1, tn), lambda l, j: (l, 0, j)),
        ],
        out_specs=pl.BlockSpec((1, 16, tn), lambda l, j: (l, 0, j)),
        out_shape=jax.ShapeDtypeStruct((nl, 16, n), F32),
        compiler_params=_cparams(("parallel", "parallel")),
        name="ada_mod",
    )(cc, ada_w, ada_b.reshape(nl, 1, n))


def _norm_mod(x, ng, shift, scale):
    y = x * lax.rsqrt(jnp.mean(x * x, axis=-1, keepdims=True) + RMS_EPS) * ng
    return y * (1.0 + scale) + shift


def _k1_kernel(x_ref, modl_ref, modc_ref, ng_ref, cos_ref, sa_ref, sb_ref, w_ref,
               o16_ref, o32_ref, *, tm, ctx_len):
    i = pl.program_id(1)
    pos = i * tm + lax.broadcasted_iota(jnp.int32, (tm, 1), 0)
    is_ctx = pos < ctx_len
    shift = jnp.where(is_ctx, modc_ref[0:1, :], modl_ref[0, 0:1, :])
    scale = jnp.where(is_ctx, modc_ref[1:2, :], modl_ref[0, 1:2, :])
    h = _norm_mod(x_ref[0], ng_ref[...], shift, scale).astype(BF16)
    cos = cos_ref[...]
    sa = sa_ref[...]
    sb = sb_ref[...]
    ch = 512
    for c0 in range(0, A16_W, ch):
        acc = jnp.dot(h, w_ref[:, c0:c0 + ch], preferred_element_type=F32)
        if c0 < A16_V:
            parts = []
            for j in range(ch // 128):
                a = acc[:, j * 128:(j + 1) * 128]
                parts.append(a * cos + pltpu.roll(a, 16, 1) * sa + pltpu.roll(a, 112, 1) * sb)
            acc = jnp.concatenate(parts, axis=1)
        o16_ref[0, :, c0:c0 + ch] = acc.astype(BF16)
    for c0 in range(0, A32_W, ch):
        c1 = min(c0 + ch, A32_W)
        o32_ref[0, :, c0:c1] = jnp.dot(h, w_ref[:, A16_W + c0:A16_W + c1],
                                       preferred_element_type=F32)


def _k1(x, modl, modc, ng, cos, sa, sb, w_all, *, ctx_len):
    b, t, d = x.shape
    tm = _row_tile(t, 272)
    tbl = pl.BlockSpec((tm, 128), lambda bi, i: (i, 0))
    return pl.pallas_call(
        functools.partial(_k1_kernel, tm=tm, ctx_len=ctx_len),
        grid=(b, t // tm),
        in_specs=[
            pl.BlockSpec((1, tm, d), lambda bi, i: (bi, i, 0)),
            pl.BlockSpec((1, 8, d), lambda bi, i: (bi, 0, 0)),
            pl.BlockSpec((8, d), lambda bi, i: (0, 0)),
            pl.BlockSpec((1, d), lambda bi, i: (0, 0)),
            tbl, tbl, tbl,
            pl.BlockSpec((d, W_ALL_COLS), lambda bi, i: (0, 0), pipeline_mode=pl.Buffered(1)),
        ],
        out_specs=[
            pl.BlockSpec((1, tm, A16_W), lambda bi, i: (bi, i, 0)),
            pl.BlockSpec((1, tm, A32_W), lambda bi, i: (bi, i, 0)),
        ],
        out_shape=[
            jax.ShapeDtypeStruct((b, t, A16_W), BF16),
            jax.ShapeDtypeStruct((b, t, A32_W), F32),
        ],
        compiler_params=_cparams(("parallel", "parallel")),
        name="norm_inproj",
    )(x, modl, modc, ng, cos, sa, sb, w_all)


def _attn_kernel(lam_ref, g_ref, q_ref, k_ref, v_ref, o_ref, *, ctx_len, seq_len, lam_init):
    qi = pl.program_id(2)
    tq = Q_TILE
    q = q_ref[0]
    lane = lax.broadcasted_iota(jnp.int32, q.shape, 1)
    zero = jnp.zeros_like(q)
    qs = jnp.concatenate([jnp.where(lane < DIFF_HEAD_DIM, q, zero),
                          jnp.where(lane >= DIFF_HEAD_DIM, q, zero)], axis=0)

    def tile(carry, k_t, v_t):
        m, l, acc = carry
        s = lax.dot_general(qs, k_t, (((1,), (1,)), ((), ())), preferred_element_type=F32)
        m_new = jnp.maximum(m, jnp.max(s, axis=-1, keepdims=True))
        alpha = jnp.exp(m - m_new)
        p = jnp.exp(s - m_new)
        l = alpha * l + jnp.sum(p, axis=-1, keepdims=True)
        acc = alpha * acc + jnp.dot(p.astype(BF16), v_t, preferred_element_type=F32)
        return m_new, l, acc

    carry = (jnp.full((2 * tq, 1), -jnp.inf, F32), jnp.zeros((2 * tq, 1), F32),
             jnp.zeros((2 * tq, DIFF_V_DIM), F32))
    carry = tile(carry, k_ref[0, 0:ctx_len, :], v_ref[0, 0:ctx_len, :])

    def body(j, c):
        r0 = pl.multiple_of(ctx_len + j * KV_TILE, KV_TILE // 2)
        return tile(c, k_ref[0, pl.ds(r0, KV_TILE), :], v_ref[0, pl.ds(r0, KV_TILE), :])

    n_lat = jnp.where(qi * tq >= ctx_len, seq_len // KV_TILE, 0)
    _, l, acc = lax.fori_loop(0, n_lat, body, carry)

    lv = lam_ref[...]
    lam = (jnp.exp(jnp.sum(lv[0:1] * lv[1:2], axis=1, keepdims=True))
           - jnp.exp(jnp.sum(lv[2:3] * lv[3:4], axis=1, keepdims=True)) + lam_init)
    o = acc[:tq] / l[:tq] - lam * (acc[tq:] / l[tq:])
    o = o * lax.rsqrt(jnp.mean(o * o, axis=-1, keepdims=True) + RMS_EPS) * g_ref[...]
    o_ref[0] = (o * (1.0 - lam_init)).astype(BF16)


def _attn(a16, lam_p, subln_g, *, ctx_len, seq_len, lam_init):
    b, t, _ = a16.shape
    assert ctx_len % Q_TILE == 0 and seq_len % KV_TILE == 0 and t % Q_TILE == 0
    hb = DIFF_V_DIM
    return pl.pallas_call(
        functools.partial(_attn_kernel, ctx_len=ctx_len, seq_len=seq_len, lam_init=lam_init),
        grid=(b, DIFF_HEADS, t // Q_TILE),
        in_specs=[
            pl.BlockSpec((4, DIFF_HEAD_DIM), lambda bi, h, qi: (0, 0)),
            pl.BlockSpec((1, hb), lambda bi, h, qi: (0, 0)),
            pl.BlockSpec((1, Q_TILE, hb), lambda bi, h, qi: (bi, qi, A16_Q // hb + h)),
            pl.BlockSpec((1, t, hb), lambda bi, h, qi: (bi, 0, A16_K // hb + h)),
            pl.BlockSpec((1, t, hb), lambda bi, h, qi: (bi, 0, A16_V // hb + h)),
        ],
        out_specs=pl.BlockSpec((1, Q_TILE, hb), lambda bi, h, qi: (bi, qi, h)),
        out_shape=jax.ShapeDtypeStruct((b, t, DIFF_HEADS * hb), BF16),
        compiler_params=_cparams(("parallel", "parallel", "parallel")),
        name="diff_attn",
    )(lam_p, subln_g, a16, a16, a16)


def _gla_kernel(q_ref, k_ref, v_ref, a_ref, r_ref, wa_ref, ba_ref, ng_ref, o_ref,
                gf_s, gb_s, oacc_s, sf_s, sb_s, *, t, ctx_len):
    ck = GLA_CHUNK
    nc = t // ck
    ncc = ctx_len // ck
    rt = _row_tile(t, 544)
    hi = lax.Precision.HIGHEST

    for r0 in range(0, t, rt):
        a = a_ref[0, r0:r0 + rt, :].astype(BF16)
        logit = jnp.dot(a, wa_ref[0], preferred_element_type=F32) + ba_ref[0]
        g = (jnp.minimum(logit, 0.0) - jnp.log1p(jnp.exp(-jnp.abs(logit)))) * (1.0 / GLA_TAU)
        gf_s[r0:r0 + rt, :] = g[:, :128]
        gb_s[r0:r0 + rt, :] = g[:, 128:]
        oacc_s[r0:r0 + rt, :] = jnp.zeros((rt, 2 * GLA_DV), F32)
    sf_s[...] = jnp.zeros_like(sf_s)
    sb_s[...] = jnp.zeros_like(sb_s)

    ri = lax.broadcasted_iota(jnp.int32, (ck, ck), 0)
    ci = lax.broadcasted_iota(jnp.int32, (ck, ck), 1)
    tri_f = ci <= ri
    tri_b = ci >= ri
    lane = lax.broadcasted_iota(jnp.int32, (ck, 2 * GLA_DK), 1)
    ones = jnp.ones((ck, 128), F32)
    br = lax.broadcasted_iota(jnp.int32, (2 * GLA_DK, 2 * GLA_DV), 0)
    bc = lax.broadcasted_iota(jnp.int32, (2 * GLA_DK, 2 * GLA_DV), 1)
    blk = (br < GLA_DK) == (bc < GLA_DV)

    def chunk(r0, g_s, s_ref, tri, tot_row):
        q = q_ref[0, pl.ds(r0, ck), :]
        k = k_ref[0, pl.ds(r0, ck), :]
        v = v_ref[0, pl.ds(r0, ck), :]
        g = g_s[pl.ds(r0, ck), :]
        trif = tri.astype(F32)
        g_cum = jnp.dot(trif, g, precision=hi, preferred_element_type=F32)
        g_tot = g_cum[tot_row:tot_row + 1, :]
        q_dec = q * jnp.exp(g_cum)
        k_inv = (k * jnp.exp(-g_cum)).astype(BF16)
        k_end = (k * jnp.exp(g_tot - g_cum)).astype(BF16)
        zero = jnp.zeros_like(q_dec)
        qs = jnp.concatenate([jnp.where(lane < GLA_DK, q_dec, zero),
                              jnp.where(lane >= GLA_DK, q_dec, zero)], axis=0).astype(BF16)
        att = lax.dot_general(qs, k_inv, (((1,), (1,)), ((), ())), preferred_element_type=F32)
        tri2 = jnp.concatenate([tri, tri], axis=0)
        att = jnp.where(tri2, att, 0.0).astype(BF16)
        oi = jnp.dot(att, v, preferred_element_type=F32)
        o_intra = jnp.concatenate([oi[:ck, :GLA_DV], oi[ck:, GLA_DV:]], axis=1)
        s_prev = s_ref[...]
        o_inter = jnp.dot(q_dec.astype(BF16), s_prev.astype(BF16), preferred_element_type=F32)
        s_loc = lax.dot_general(k_end, v, (((0,), (0,)), ((), ())), preferred_element_type=F32)
        gt = lax.dot_general(g, ones, (((0,), (0,)), ((), ())), precision=hi,
                             preferred_element_type=F32)
        dec = jnp.exp(gt)
        dec2 = jnp.concatenate([dec, dec], axis=1)
        s_ref[...] = dec2 * s_prev + jnp.where(blk, s_loc, 0.0)
        oacc_s[pl.ds(r0, ck), :] = oacc_s[pl.ds(r0, ck), :] + o_intra + o_inter

    def body(n, _):
        rf = pl.multiple_of(n * ck, ck)
        chunk(rf, gf_s, sf_s, tri_f, ck - 1)
        nb = jnp.where(n < ncc, ncc - 1 - n, nc - 1 - (n - ncc))
        rb = pl.multiple_of(nb * ck, ck)
        chunk(rb, gb_s, sb_s, tri_b, 0)
        return 0

    lax.fori_loop(0, nc, body, 0)

    ng = ng_ref[...]
    for r0 in range(0, t, rt):
        o = oacc_s[r0:r0 + rt, :]
        r = r_ref[0, r0:r0 + rt, :]
        outs = []
        for hh in range(2):
            oh = o[:, hh * GLA_DV:(hh + 1) * GLA_DV]
            oh = oh * lax.rsqrt(jnp.mean(oh * oh, axis=-1, keepdims=True) + RMS_EPS) * ng
            outs.append(oh)
        on = jnp.concatenate(outs, axis=1)
        o_ref[0, r0:r0 + rt, :] = (on * (r * _sigmoid(r))).astype(BF16)


def _gla(a16, a32, wa, ba, ng, *, ctx_len):
    b, t, _ = a16.shape
    assert t % GLA_CHUNK == 0 and ctx_len % GLA_CHUNK == 0
    return pl.pallas_call(
        functools.partial(_gla_kernel, t=t, ctx_len=ctx_len),
        grid=(b, 2),
        in_specs=[
            pl.BlockSpec((1, t, 128), lambda bi, p: (bi, 0, A32_GLQ // 128 + p)),
            pl.BlockSpec((1, t, 128), lambda bi, p: (bi, 0, A32_GLK // 128 + p)),
            pl.BlockSpec((1, t, 256), lambda bi, p: (bi, 0, A16_GV // 256 + p)),
            pl.BlockSpec((1, t, 128), lambda bi, p: (bi, 0, A32_GLA // 128)),
            pl.BlockSpec((1, t, 256), lambda bi, p: (bi, 0, A32_GLR // 256 + p)),
            pl.BlockSpec((1, 128, 256), lambda bi, p: (p, 0, 0)),
            pl.BlockSpec((1, 1, 256), lambda bi, p: (p, 0, 0)),
            pl.BlockSpec((1, GLA_DV), lambda bi, p: (0, 0)),
        ],
        out_specs=pl.BlockSpec((1, t, 256), lambda bi, p: (bi, 0, p)),
        out_shape=jax.ShapeDtypeStruct((b, t, GLA_HEADS * GLA_DV), BF16),
        scratch_shapes=[
            pltpu.VMEM((t, 128), F32), pltpu.VMEM((t, 128), F32),
            pltpu.VMEM((t, 256), F32),
            pltpu.VMEM((128, 256), F32), pltpu.VMEM((128, 256), F32),
        ],
        compiler_params=_cparams(("parallel", "parallel")),
        name="gla_bidir",
    )(a32, a32, a16, a32, a32, wa, ba, ng)


def _merge_kernel(x_ref, od_ref, og_ref, gates_ref, cvb_ref, cvc_ref, cvx_ref,
                  cvc_p_ref, cvx_p_ref, cvc_n_ref, cvx_n_ref, cw_ref,
                  wdo_ref, wco_ref, wgo_ref, wo_ref, modl_ref, modc_ref, n2g_ref,
                  rw_ref, rb_ref, xo_ref, h2_ref, idx_ref, gate_ref, *, tm, ctx_len, t):
    i = pl.program_id(1)
    pos = i * tm + lax.broadcasted_iota(jnp.int32, (tm, 1), 0)
    is_ctx = pos < ctx_len
    row = lax.broadcasted_iota(jnp.int32, (tm, 1), 0)

    z = cvc_ref[0] * cvx_ref[0]
    z_before = cvc_p_ref[0, 7:8, :] * cvx_p_ref[0, 7:8, :]
    z_after = cvc_n_ref[0, 0:1, :] * cvx_n_ref[0, 0:1, :]
    z_prev = jnp.where(row == 0, z_before, pltpu.roll(z, 1, 0))
    z_prev = jnp.where((pos == 0) | (pos == ctx_len), 0.0, z_prev)
    z_next = jnp.where(row == tm - 1, z_after, pltpu.roll(z, tm - 1, 0))
    z_next = jnp.where((pos == ctx_len - 1) | (pos == t - 1), 0.0, z_next)
    conv = z_prev * cw_ref[0:1, :] + z * cw_ref[1:2, :] + z_next * cw_ref[2:3, :]
    zc = (cvb_ref[0] * conv).astype(BF16)

    y_diff = jnp.dot(od_ref[0], wdo_ref[...], preferred_element_type=F32)
    y_conv = jnp.dot(zc, wco_ref[...], preferred_element_type=F32)
    y_gla = jnp.dot(og_ref[0], wgo_ref[...], preferred_element_type=F32)
    d = D_MODEL
    mix = (_sigmoid(gates_ref[0, :, 0:d]) * y_diff
           + _sigmoid(gates_ref[0, :, d:2 * d]) * y_conv
           + _sigmoid(gates_ref[0, :, 2 * d:3 * d]) * y_gla)
    m = jnp.dot(mix.astype(BF16), wo_ref[...], preferred_element_type=F32)

    g1 = jnp.where(is_ctx, modc_ref[2:3, :], modl_ref[0, 2:3, :])
    x_new = x_ref[0] + g1 * m
    xo_ref[0] = x_new
    sh2 = jnp.where(is_ctx, modc_ref[3:4, :], modl_ref[0, 3:4, :])
    sc2 = jnp.where(is_ctx, modc_ref[4:5, :], modl_ref[0, 4:5, :])
    h2 = _norm_mod(x_new, n2g_ref[...], sh2, sc2)
    h2_ref[0] = h2

    logits = jnp.dot(h2.astype(BF16), rw_ref[...], preferred_element_type=F32) + rb_ref[...]
    lane = lax.broadcasted_iota(jnp.int32, logits.shape, 1)
    vals = logits
    tops = []
    idx_out = jnp.zeros(logits.shape, jnp.int32)
    for kk in range(TOP_K):
        mx = jnp.max(vals, axis=-1, keepdims=True)
        ix = jnp.min(jnp.where(vals == mx, lane, 128), axis=-1, keepdims=True)
        tops.append(mx)
        idx_out = jnp.where(lane == kk, ix, idx_out)
        vals = jnp.where(lane == ix, -jnp.inf, vals)
    es = [jnp.exp(v - tops[0]) for v in tops]
    den = es[0] + es[1] + es[2] + es[3]
    gate_out = jnp.zeros(logits.shape, F32)
    for kk in range(TOP_K):
        gate_out = jnp.where(lane == kk, es[kk] / den, gate_out)
    idx_ref[0] = idx_out
    gate_ref[0] = gate_out


def _merge(x, od, og, a32, cw, wdo, wco, wgo, wo, modl, modc, n2g, rw, rb, *, ctx_len):
    b, t, d = x.shape
    tm = _row_tile(t, 272)
    nt8 = t // 8
    row = lambda bi, i: (bi, i, 0)
    const2 = lambda bi, i: (0, 0)
    cw8 = pl.BlockSpec((1, 8, CONV_W), lambda bi, i: (bi, jnp.maximum(i * (tm // 8) - 1, 0),
                                                     A32_CVC // CONV_W))
    return pl.pallas_call(
        functools.partial(_merge_kernel, tm=tm, ctx_len=ctx_len, t=t),
        grid=(b, t // tm),
        in_specs=[
            pl.BlockSpec((1, tm, d), row),
            pl.BlockSpec((1, tm, d), row),
            pl.BlockSpec((1, tm, GLA_HEADS * GLA_DV), row),
            pl.BlockSpec((1, tm, 3 * d), lambda bi, i: (bi, i, 0)),
            pl.BlockSpec((1, tm, CONV_W), lambda bi, i: (bi, i, A32_CVB // CONV_W)),
            pl.BlockSpec((1, tm, CONV_W), lambda bi, i: (bi, i, A32_CVC // CONV_W)),
            pl.BlockSpec((1, tm, CONV_W), lambda bi, i: (bi, i, A32_CVX // CONV_W)),
            pl.BlockSpec((1, 8, CONV_W), lambda bi, i: (bi, jnp.maximum(i * (tm // 8) - 1, 0),
                                                   A32_CVC // CONV_W)),
            pl.BlockSpec((1, 8, CONV_W), lambda bi, i: (bi, jnp.maximum(i * (tm // 8) - 1, 0),
                                                   A32_CVX // CONV_W)),
            pl.BlockSpec((1, 8, CONV_W), lambda bi, i: (bi, jnp.minimum((i + 1) * (tm // 8), nt8 - 1),
                                                   A32_CVC // CONV_W)),
            pl.BlockSpec((1, 8, CONV_W), lambda bi, i: (bi, jnp.minimum((i + 1) * (tm // 8), nt8 - 1),
                                                   A32_CVX // CONV_W)),
            pl.BlockSpec((8, CONV_W), const2),
            pl.BlockSpec((d, d), const2),
            pl.BlockSpec((CONV_W, d), const2),
            pl.BlockSpec((GLA_HEADS * GLA_DV, d), const2),
            pl.BlockSpec((d, d), const2),
            pl.BlockSpec((1, 8, d), lambda bi, i: (bi, 0, 0)),
            pl.BlockSpec((8, d), const2),
            pl.BlockSpec((1, d), const2),
            pl.BlockSpec((d, 128), const2),
            pl.BlockSpec((1, 128), const2),
        ],
        out_specs=[
            pl.BlockSpec((1, tm, d), row),
            pl.BlockSpec((1, tm, d), row),
            pl.BlockSpec((1, tm, 128), row),
            pl.BlockSpec((1, tm, 128), row),
        ],
        out_shape=[
            jax.ShapeDtypeStruct((b, t, d), F32),
            jax.ShapeDtypeStruct((b, t, d), F32),
            jax.ShapeDtypeStruct((b, t, 128), jnp.int32),
            jax.ShapeDtypeStruct((b, t, 128), F32),
        ],
        compiler_params=_cparams(("parallel", "parallel")),
        name="merge_router",
    )(x, od, og, a32, a32, a32, a32, a32, a32, a32, a32, cw, wdo, wco, wgo, wo,
      modl, modc, n2g, rw, rb)


def _wait_rows(src_hbm, dst, sem, n):
    def wbody(j, _):
        pltpu.make_async_copy(src_hbm.at[pl.ds(0, 1)], dst.at[pl.ds(j, 1)], sem).wait()
        return 0
    lax.fori_loop(0, n, wbody, 0)


def _gather_rows(idx_vmem, idx_smem, src_hbm, dst, sem_i, sem_g, n):
    cp = pltpu.make_async_copy(idx_vmem, idx_smem, sem_i)
    cp.start()
    cp.wait()

    def body(j, _):
        pltpu.make_async_copy(src_hbm.at[pl.ds(idx_smem[j], 1)], dst.at[pl.ds(j, 1)], sem_g).start()
        return 0
    lax.fori_loop(0, n, body, 0)
    _wait_rows(src_hbm, dst, sem_g, n)


def _dispatch_kernel(idx_ref, src_hbm, o_ref, idx_smem, sem_i, sem_g):
    _gather_rows(idx_ref.at[0, 0], idx_smem, src_hbm, o_ref, sem_i, sem_g, MOE_BLOCK)


def _dispatch(row_tok, h2_flat):
    nb = row_tok.shape[0]
    r, d = h2_flat.shape
    return pl.pallas_call(
        _dispatch_kernel,
        grid=(nb,),
        in_specs=[
            pl.BlockSpec((1, 1, MOE_BLOCK), lambda i: (i, 0, 0)),
            pl.BlockSpec(memory_space=pl.ANY),
        ],
        out_specs=pl.BlockSpec((MOE_BLOCK, d), lambda i: (i, 0)),
        out_shape=jax.ShapeDtypeStruct((nb * MOE_BLOCK, d), F32),
        scratch_shapes=[pltpu.SMEM((MOE_BLOCK,), jnp.int32),
                        pltpu.SemaphoreType.DMA, pltpu.SemaphoreType.DMA],
        compiler_params=_cparams(("arbitrary",)),
        name="moe_dispatch",
    )(row_tok, h2_flat)


def _expert_kernel(be_ref, nv_ref, x_ref, g_ref, w1g_ref, w1l_ref, b1g_ref, b1l_ref,
                   w2_ref, b2_ref, o_ref):
    i = pl.program_id(0)

    @pl.when(i < nv_ref[0])
    def _():
        x = x_ref[...].astype(BF16)
        hg = jnp.dot(x, w1g_ref[0], preferred_element_type=F32) + b1g_ref[0]
        hl = jnp.dot(x, w1l_ref[0], preferred_element_type=F32) + b1l_ref[0]
        hg = jnp.minimum(hg, SWIGLU_LIMIT)
        hl = jnp.clip(hl, -SWIGLU_LIMIT, SWIGLU_LIMIT)
        act = hg * _sigmoid(SWIGLU_ALPHA * hg) * (hl + 1.0)
        y = jnp.dot(act.astype(BF16), w2_ref[0], preferred_element_type=F32) + b2_ref[0]
        o_ref[...] = y * g_ref[...]

    @pl.when(i >= nv_ref[0])
    def _():
        o_ref[...] = jnp.zeros_like(o_ref)


def _experts(block_exp, n_valid, xs, row_gate, w1g, w1l, b1g, b1l, w2, b2):
    nr, d = xs.shape
    nb = nr // MOE_BLOCK
    de = w1g.shape[-1]
    wmap = lambda i, be, nv: (be[i], 0, 0)
    rmap = lambda i, be, nv: (i, 0)
    return pl.pallas_call(
        _expert_kernel,
        grid_spec=pltpu.PrefetchScalarGridSpec(
            num_scalar_prefetch=2,
            grid=(nb,),
            in_specs=[
                pl.BlockSpec((MOE_BLOCK, d), rmap),
                pl.BlockSpec((MOE_BLOCK, 1), rmap),
                pl.BlockSpec((1, d, de), wmap),
                pl.BlockSpec((1, d, de), wmap),
                pl.BlockSpec((1, 1, de), wmap),
                pl.BlockSpec((1, 1, de), wmap),
                pl.BlockSpec((1, de, d), wmap),
                pl.BlockSpec((1, 1, d), wmap),
            ],
            out_specs=pl.BlockSpec((MOE_BLOCK, d), rmap),
        ),
        out_shape=jax.ShapeDtypeStruct((nr, d), F32),
        compiler_params=_cparams(("arbitrary",)),
        name="moe_experts",
    )(block_exp, n_valid, xs, row_gate, w1g, w1l, b1g, b1l, w2, b2)


def _combine_kernel(idx_ref, x_ref, modl_ref, modc_ref, fg_ref, ys_hbm, o_ref,
                    idx_smem, buf, sem_i, sem_g, *, tc, ctx_len, final):
    i = pl.program_id(1)
    _gather_rows(idx_ref.at[0, 0], idx_smem, ys_hbm, buf, sem_i, sem_g, TOP_K * tc)
    f = buf[0:tc, :] + buf[tc:2 * tc, :] + buf[2 * tc:3 * tc, :] + buf[3 * tc:4 * tc, :]
    pos = i * tc + lax.broadcasted_iota(jnp.int32, (tc, 1), 0)
    g2 = jnp.where(pos < ctx_len, modc_ref[5:6, :], modl_ref[0, 5:6, :])
    x_new = x_ref[0] + g2 * f
    if final:
        x_new = x_new * lax.rsqrt(jnp.mean(x_new * x_new, axis=-1, keepdims=True) + RMS_EPS) * fg_ref[...]
    o_ref[0] = x_new


def _combine(dest_t, x, modl, modc, fg, ys, *, ctx_len, final):
    b, t, d = x.shape
    tc = COMBINE_TILE
    npb = t // tc
    return pl.pallas_call(
        functools.partial(_combine_kernel, tc=tc, ctx_len=ctx_len, final=final),
        grid=(b, npb),
        in_specs=[
            pl.BlockSpec((1, 1, TOP_K * tc), lambda bi, i: (bi * npb + i, 0, 0)),
            pl.BlockSpec((1, tc, d), lambda bi, i: (bi, i, 0)),
            pl.BlockSpec((1, 8, d), lambda bi, i: (bi, 0, 0)),
            pl.BlockSpec((8, d), lambda bi, i: (0, 0)),
            pl.BlockSpec((1, d), lambda bi, i: (0, 0)),
            pl.BlockSpec(memory_space=pl.ANY),
        ],
        out_specs=pl.BlockSpec((1, tc, d), lambda bi, i: (bi, i, 0)),
        out_shape=jax.ShapeDtypeStruct((b, t, d), F32),
        scratch_shapes=[pltpu.SMEM((TOP_K * tc,), jnp.int32),
                        pltpu.VMEM((TOP_K * tc, d), F32),
                        pltpu.SemaphoreType.DMA, pltpu.SemaphoreType.DMA],
        compiler_params=_cparams(("arbitrary", "arbitrary")),
        name="moe_combine",
    )(dest_t, x, modl, modc, fg, ys)


def _routing(idx, gate):
    r = idx.shape[0]
    npair = r * TOP_K
    bm = MOE_BLOCK
    e_flat = idx.reshape(-1)
    onehot = (e_flat[:, None] == jnp.arange(N_EXPERTS, dtype=jnp.int32)[None, :]).astype(jnp.int32)
    csum = jnp.cumsum(onehot, axis=0)
    counts = csum[-1]
    rank = jnp.take_along_axis(csum, e_flat[:, None], axis=1)[:, 0] - 1
    padded = (counts + bm - 1) // bm * bm
    pad_end = jnp.cumsum(padded)
    pad_start = pad_end - padded
    dest = pad_start[e_flat] + rank
    nb = npair // bm + N_EXPERTS
    nr = nb * bm
    row_pair = jnp.full((nr,), -1, jnp.int32).at[dest].set(jnp.arange(npair, dtype=jnp.int32))
    valid = row_pair >= 0
    rp = jnp.maximum(row_pair, 0)
    row_tok = (rp // TOP_K).astype(jnp.int32)
    row_gate = jnp.where(valid, gate.reshape(-1)[rp], 0.0).astype(F32)
    block_exp = jnp.minimum(
        jnp.searchsorted(pad_end, jnp.arange(nb, dtype=jnp.int32) * bm, side='right'),
        N_EXPERTS - 1).astype(jnp.int32)
    n_valid = (pad_end[-1] // bm).astype(jnp.int32).reshape(1)
    return dest.astype(jnp.int32), row_tok, row_gate, block_exp, n_valid


def _rope_tables(ctx_len, seq_len):
    inv_freq = ROPE_BASE ** (-jnp.arange(0, ROT_AXIS_DIM, 2, dtype=F32) / ROT_AXIS_DIM)
    p = jnp.arange(seq_len, dtype=jnp.int32)
    row = (p // GRID_W).astype(F32)
    col = (p % GRID_W).astype(F32)
    lane = np.arange(128)
    dd = lane % DIFF_HEAD_DIM
    fidx = jnp.asarray((dd % ROT_AXIS_DIM) % (ROT_AXIS_DIM // 2))
    use_col = jnp.asarray(dd >= ROT_AXIS_DIM)
    second_half = jnp.asarray((dd % ROT_AXIS_DIM) >= ROT_AXIS_DIM // 2)
    posm = jnp.where(use_col[None, :], col[:, None], row[:, None])
    ang = posm * inv_freq[fidx][None, :]
    cos = jnp.cos(ang)
    sin = jnp.sin(ang)
    sa = jnp.where(second_half[None, :], sin, 0.0)
    sb = jnp.where(second_half[None, :], 0.0, -sin)
    pad = lambda a, v: jnp.concatenate([jnp.full((ctx_len, 128), v, F32), a], axis=0)
    return pad(cos, 1.0), pad(sa, 0.0), pad(sb, 0.0)


def kernel(x, c, ctx, c_ctx, ada_w, ada_b, norm1_g, norm2_g, w_in, diff_lambda, diff_subln_g,
           diff_w_out, conv_w, conv_w_out, gla_w_a2, gla_b_a, gla_norm_g, gla_w_out, w_o,
           router_w, router_b, moe_w1, moe_b1, moe_w2, moe_b2, final_norm_g):
    b, s, d = x.shape
    cl = ctx.shape[1]
    t = cl + s
    nl = w_in.shape[0]
    assert d == D_MODEL and b <= 15

    xs = jnp.concatenate([ctx, x], axis=1)
    cc = jnp.concatenate([c, c_ctx[None, :], jnp.zeros((15 - b, d), F32)], axis=0)
    mod = _ada(cc, ada_w, ada_b).reshape(nl, 16, N_ADA, d)
    cos, sa, sb = _rope_tables(cl, s)

    sp = np.cumsum([1024, 1024, 1024, 512, 512, 512, 256, 256, 512, 512, 32, 3072])
    (w_q, w_k, w_v, w_cb, w_cc, w_cx, w_gq, w_gk, w_gv, w_gr, w_ga, w_gt) = jnp.split(
        w_in, [int(v) for v in sp[:-1]], axis=-1)
    w_all = jnp.concatenate(
        [w_q * (DIFF_HEAD_DIM ** -0.5), w_k, w_v, w_gv,
         w_gt, w_cb, w_cc, w_cx, w_gq * (GLA_DK ** -0.5), w_gk, w_gr, w_ga,
         jnp.zeros((nl, d, A32_W - A32_GLA - 2 * GLA_RANK), F32)], axis=-1).astype(BF16)

    wa = jnp.zeros((nl, 2, 128, 256), F32)
    for p in range(2):
        wa = wa.at[:, p, 0:GLA_RANK, 0:128].set(gla_w_a2[:, 0, :, p * 128:(p + 1) * 128])
        wa = wa.at[:, p, GLA_RANK:2 * GLA_RANK, 128:256].set(gla_w_a2[:, 1, :, p * 128:(p + 1) * 128])
    wa = wa.astype(BF16)
    ba = jnp.stack([jnp.concatenate([gla_b_a[:, 0, p * 128:(p + 1) * 128],
                                     gla_b_a[:, 1, p * 128:(p + 1) * 128]], axis=-1)
                    for p in range(2)], axis=1)[:, :, None, :]

    cw = jnp.concatenate([conv_w, jnp.zeros((nl, 5, CONV_W), F32)], axis=1)
    rw = jnp.concatenate([router_w, jnp.zeros((nl, d, 128 - N_EXPERTS), F32)], axis=-1).astype(BF16)
    rb = jnp.concatenate([router_b, jnp.full((nl, 128 - N_EXPERTS), -1e30, F32)], axis=-1)[:, None, :]
    w1 = moe_w1.reshape(nl, N_EXPERTS, d, D_EXPERT, 2)
    w1g = w1[..., 0].astype(BF16)
    w1l = w1[..., 1].astype(BF16)
    b1 = moe_b1.reshape(nl, N_EXPERTS, 1, D_EXPERT, 2)
    b1g = b1[..., 0]
    b1l = b1[..., 1]
    w2 = moe_w2.astype(BF16)
    b2 = moe_b2[:, :, None, :]
    wdo = diff_w_out.astype(BF16)
    wco = conv_w_out.astype(BF16)
    wgo = gla_w_out.astype(BF16)
    wo = w_o.astype(BF16)

    for layer in range(nl):
        last = layer == nl - 1
        lam_init = 0.8 - 0.6 * math.exp(-0.3 * layer)
        ml = mod[layer]
        modl = jnp.concatenate([ml[:b], jnp.zeros((b, 2, d), F32)], axis=1)
        modc = jnp.concatenate([ml[b], jnp.zeros((2, d), F32)], axis=0)

        a16, a32 = _k1(xs, modl, modc, norm1_g[layer][None, :], cos, sa, sb, w_all[layer], ctx_len=cl)
        od = _attn(a16, diff_lambda[layer], diff_subln_g[layer][None, :],
                   ctx_len=cl, seq_len=s, lam_init=lam_init)
        og = _gla(a16, a32, wa[layer], ba[layer], gla_norm_g[layer][None, :], ctx_len=cl)
        xs, h2, idx, gate = _merge(xs, od, og, a32, cw[layer], wdo[layer], wco[layer], wgo[layer],
                                   wo[layer], modl, modc, norm2_g[layer][None, :], rw[layer],
                                   rb[layer], ctx_len=cl)

        r = b * t
        dest, row_tok, row_gate, block_exp, n_valid = _routing(
            idx.reshape(r, 128)[:, :TOP_K], gate.reshape(r, 128)[:, :TOP_K])
        xg = _dispatch(row_tok.reshape(-1, 1, MOE_BLOCK), h2.reshape(r, d))
        ys = _experts(block_exp, n_valid, xg, row_gate[:, None], w1g[layer], w1l[layer],
                      b1g[layer], b1l[layer], w2[layer], b2[layer])
        tc = COMBINE_TILE
        dest_t = dest.reshape(r // tc, tc, TOP_K).transpose(0, 2, 1).reshape(r // tc, 1, TOP_K * tc)
        xs = _combine(dest_t, xs, modl, modc, final_norm_g[None, :], ys, ctx_len=cl, final=last)

    return xs[:, cl:, :]
```

```python
import functools
import math

import numpy as np
import jax
import jax.numpy as jnp
from jax import lax
from jax.experimental import pallas as pl
from jax.experimental.pallas import tpu as pltpu

F32 = jnp.float32
BF16 = jnp.bfloat16

D_MODEL = 1024
GRID_W = 64
RMS_EPS = 1e-6
N_ADA = 6
DIFF_HEADS = 8
DIFF_HEAD_DIM = 64
DIFF_V_DIM = 128
ROPE_BASE = 10000.0
ROT_AXIS_DIM = 32
CONV_W = 512
GLA_HEADS = 4
GLA_DK = 64
GLA_DV = 128
GLA_RANK = 16
GLA_TAU = 16.0
GLA_CHUNK = 64
N_EXPERTS = 32
TOP_K = 4
D_EXPERT = 1024
SWIGLU_LIMIT = 7.0
SWIGLU_ALPHA = 1.702
LOG2_E = 1.4426950408889634

A16_Q, A16_K, A16_V, A16_GV, A16_W = 0, 1024, 2048, 3072, 3584
A32_GATES, A32_CVB, A32_CVC, A32_CVX = 0, 3072, 3584, 4096
A32_GLQ, A32_GLK, A32_GLR, A32_GLA, A32_W = 4608, 4864, 5120, 5632, 5760
W_ALL_COLS = A16_W + A32_W

Q_TILE = 256
KV_TILE = 1024
ATTN_HEADS = 2
MOE_BLOCK = 256
VMEM_LIMIT = 56 * 1024 * 1024


def _cparams(sem):
    return pltpu.CompilerParams(dimension_semantics=sem, vmem_limit_bytes=VMEM_LIMIT)


def _row_tile(t, target):
    best = None
    for cand in range(16, target + 1, 16):
        if t % cand == 0:
            best = cand
    assert best is not None, t
    return best


def _sigmoid(v):
    return 1.0 / (1.0 + jnp.exp(-v))


def _ada_kernel(cc_ref, w_ref, b_ref, o_ref):
    a = cc_ref[...]
    a = a * _sigmoid(a)
    o_ref[0] = jnp.dot(a, w_ref[0], precision=lax.Precision.HIGHEST,
                       preferred_element_type=F32) + b_ref[0]


def _ada(cc, ada_w, ada_b):
    nl, d, n = ada_w.shape
    tn = 1024
    return pl.pallas_call(
        _ada_kernel,
        grid=(nl, n // tn),
        in_specs=[
            pl.BlockSpec((16, d), lambda l, j: (0, 0)),
            pl.BlockSpec((1, d, tn), lambda l, j: (l, 0, j)),
            pl.BlockSpec((1, 1, tn), lambda l, j: (l, 0, j)),
        ],
        out_specs=pl.BlockSpec((1, 16, tn), lambda l, j: (l, 0, j)),
        out_shape=jax.ShapeDtypeStruct((nl, 16, n), F32),
        compiler_params=_cparams(("parallel", "parallel")),
        name="ada_mod",
    )(cc, ada_w, ada_b.reshape(nl, 1, n))


def _norm_mod(x, ng, shift, scale):
    y = x * lax.rsqrt(jnp.mean(x * x, axis=-1, keepdims=True) + RMS_EPS) * ng
    return y * (1.0 + scale) + shift


def _k1_kernel(x_ref, modl_ref, modc_ref, ng_ref, cos_ref, sa_ref, sb_ref, w_ref,
               o16_ref, o32_ref, *, tm, ctx_len):
    i = pl.program_id(1)
    pos = i * tm + lax.broadcasted_iota(jnp.int32, (tm, 1), 0)
    is_ctx = pos < ctx_len
    shift = jnp.where(is_ctx, modc_ref[0:1, :], modl_ref[0, 0:1, :])
    scale = jnp.where(is_ctx, modc_ref[1:2, :], modl_ref[0, 1:2, :])
    h = _norm_mod(x_ref[0], ng_ref[...], shift, scale).astype(BF16)
    cos = cos_ref[...]
    sa = sa_ref[...]
    sb = sb_ref[...]
    ch = 512
    for c0 in range(0, A16_W, ch):
        acc = jnp.dot(h, w_ref[:, c0:c0 + ch], preferred_element_type=F32)
        if c0 < A16_V:
            parts = []
            for j in range(ch // 128):
                a = acc[:, j * 128:(j + 1) * 128]
                parts.append(a * cos + pltpu.roll(a, 16, 1) * sa + pltpu.roll(a, 112, 1) * sb)
            acc = jnp.concatenate(parts, axis=1)
        if c0 < A16_K:
            acc = acc * LOG2_E
        o16_ref[0, :, c0:c0 + ch] = acc.astype(BF16)
    for c0 in range(0, A32_W, ch):
        c1 = min(c0 + ch, A32_W)
        o32_ref[0, :, c0:c1] = jnp.dot(h, w_ref[:, A16_W + c0:A16_W + c1],
                                       preferred_element_type=F32)


def _k1(x, modl, modc, ng, cos, sa, sb, w_all, *, ctx_len):
    b, t, d = x.shape
    tm = _row_tile(t, 272)
    tbl = pl.BlockSpec((tm, 128), lambda bi, i: (i, 0))
    return pl.pallas_call(
        functools.partial(_k1_kernel, tm=tm, ctx_len=ctx_len),
        grid=(b, t // tm),
        in_specs=[
            pl.BlockSpec((1, tm, d), lambda bi, i: (bi, i, 0)),
            pl.BlockSpec((1, 8, d), lambda bi, i: (bi, 0, 0)),
            pl.BlockSpec((8, d), lambda bi, i: (0, 0)),
            pl.BlockSpec((1, d), lambda bi, i: (0, 0)),
            tbl, tbl, tbl,
            pl.BlockSpec((d, W_ALL_COLS), lambda bi, i: (0, 0), pipeline_mode=pl.Buffered(1)),
        ],
        out_specs=[
            pl.BlockSpec((1, tm, A16_W), lambda bi, i: (bi, i, 0)),
            pl.BlockSpec((1, tm, A32_W), lambda bi, i: (bi, i, 0)),
        ],
        out_shape=[
            jax.ShapeDtypeStruct((b, t, A16_W), BF16),
            jax.ShapeDtypeStruct((b, t, A32_W), F32),
        ],
        compiler_params=_cparams(("parallel", "parallel")),
        name="norm_inproj",
    )(x, modl, modc, ng, cos, sa, sb, w_all)


def _attn_kernel(lam_ref, g_ref, q_ref, k_ref, v_ref, o_ref, *, ctx_len, seq_len, tk, lam_init):
    qi = pl.program_id(2)
    tq = Q_TILE
    hb = DIFF_V_DIM
    qss = []
    for g in range(ATTN_HEADS):
        q = q_ref[0, :, g * hb:(g + 1) * hb]
        lane = lax.broadcasted_iota(jnp.int32, q.shape, 1)
        zero = jnp.zeros_like(q)
        qss.append(jnp.concatenate([jnp.where(lane < DIFF_HEAD_DIM, q, zero),
                                    jnp.where(lane >= DIFF_HEAD_DIM, q, zero)], axis=0))

    def tile(g, carry, r0, rows):
        m, l, acc = carry
        k_t = k_ref[0, pl.ds(r0, rows), g * hb:(g + 1) * hb]
        v_t = v_ref[0, pl.ds(r0, rows), g * hb:(g + 1) * hb]
        s = lax.dot_general(qss[g], k_t, (((1,), (1,)), ((), ())), preferred_element_type=F32)
        m_new = jnp.maximum(m, jnp.max(s, axis=-1, keepdims=True))
        alpha = jnp.exp2(m - m_new)
        p = jnp.exp2(s - m_new)
        l = alpha * l + jnp.sum(p, axis=-1, keepdims=True)
        acc = alpha * acc + jnp.dot(p.astype(BF16), v_t, preferred_element_type=F32)
        return m_new, l, acc

    init = (jnp.full((2 * tq, 1), -jnp.inf, F32), jnp.zeros((2 * tq, 1), F32),
            jnp.zeros((2 * tq, hb), F32))
    carries = tuple(tile(g, init, 0, ctx_len) for g in range(ATTN_HEADS))

    def body(j, cs):
        r0 = pl.multiple_of(ctx_len + j * tk, 256)
        return tuple(tile(g, cs[g], r0, tk) for g in range(ATTN_HEADS))

    n_lat = jnp.where(qi * tq >= ctx_len, seq_len // tk, 0)
    carries = lax.fori_loop(0, n_lat, body, carries)

    lv = lam_ref[...]
    lam = (jnp.exp(jnp.sum(lv[0:1] * lv[1:2], axis=1, keepdims=True))
           - jnp.exp(jnp.sum(lv[2:3] * lv[3:4], axis=1, keepdims=True)) + lam_init)
    for g in range(ATTN_HEADS):
        _, l, acc = carries[g]
        o = acc[:tq] / l[:tq] - lam * (acc[tq:] / l[tq:])
        o = o * lax.rsqrt(jnp.mean(o * o, axis=-1, keepdims=True) + RMS_EPS) * g_ref[...]
        o_ref[0, :, g * hb:(g + 1) * hb] = (o * (1.0 - lam_init)).astype(BF16)


def _attn(a16, lam_p, subln_g, *, ctx_len, seq_len, lam_init):
    b, t, _ = a16.shape
    tk = min(KV_TILE, seq_len)
    assert ctx_len % Q_TILE == 0 and seq_len % tk == 0 and tk % 256 == 0
    hb = DIFF_V_DIM * ATTN_HEADS
    return pl.pallas_call(
        functools.partial(_attn_kernel, ctx_len=ctx_len, seq_len=seq_len, tk=tk, lam_init=lam_init),
        grid=(b, DIFF_HEADS // ATTN_HEADS, t // Q_TILE),
        in_specs=[
            pl.BlockSpec((4, DIFF_HEAD_DIM), lambda bi, h, qi: (0, 0)),
            pl.BlockSpec((1, DIFF_V_DIM), lambda bi, h, qi: (0, 0)),
            pl.BlockSpec((1, Q_TILE, hb), lambda bi, h, qi: (bi, qi, A16_Q // hb + h)),
            pl.BlockSpec((1, t, hb), lambda bi, h, qi: (bi, 0, A16_K // hb + h)),
            pl.BlockSpec((1, t, hb), lambda bi, h, qi: (bi, 0, A16_V // hb + h)),
        ],
        out_specs=pl.BlockSpec((1, Q_TILE, hb), lambda bi, h, qi: (bi, qi, h)),
        out_shape=jax.ShapeDtypeStruct((b, t, DIFF_HEADS * DIFF_V_DIM), BF16),
        compiler_params=_cparams(("parallel", "parallel", "parallel")),
        name="diff_attn",
    )(lam_p, subln_g, a16, a16, a16)


def _gla_kernel(q_ref, k_ref, v_ref, a_ref, r_ref, wa_ref, ba_ref, ng_ref, o_ref,
                gf_s, gb_s, oacc_s, sf_s, sb_s, *, t, ctx_len):
    ck = GLA_CHUNK
    nc = t // ck
    ncc = ctx_len // ck
    rt = _row_tile(t, 544)
    hi = lax.Precision.HIGHEST

    for r0 in range(0, t, rt):
        a = a_ref[0, r0:r0 + rt, :].astype(BF16)
        logit = jnp.dot(a, wa_ref[0], preferred_element_type=F32) + ba_ref[0]
        g = (jnp.minimum(logit, 0.0) - jnp.log1p(jnp.exp(-jnp.abs(logit)))) * (1.0 / GLA_TAU)
        gf_s[r0:r0 + rt, :] = g[:, :128]
        gb_s[r0:r0 + rt, :] = g[:, 128:]
        oacc_s[r0:r0 + rt, :] = jnp.zeros((rt, 2 * GLA_DV), F32)
    sf_s[...] = jnp.zeros_like(sf_s)
    sb_s[...] = jnp.zeros_like(sb_s)

    ri = lax.broadcasted_iota(jnp.int32, (ck, ck), 0)
    ci = lax.broadcasted_iota(jnp.int32, (ck, ck), 1)
    tri_f = ci <= ri
    tri_b = ci >= ri
    lane = lax.broadcasted_iota(jnp.int32, (ck, 2 * GLA_DK), 1)
    ones = jnp.ones((ck, 128), F32)
    br = lax.broadcasted_iota(jnp.int32, (2 * GLA_DK, 2 * GLA_DV), 0)
    bc = lax.broadcasted_iota(jnp.int32, (2 * GLA_DK, 2 * GLA_DV), 1)
    blk = (br < GLA_DK) == (bc < GLA_DV)

    def chunk(r0, g_s, s_ref, tri, tot_row):
        q = q_ref[0, pl.ds(r0, ck), :]
        k = k_ref[0, pl.ds(r0, ck), :]
        v = v_ref[0, pl.ds(r0, ck), :]
        g = g_s[pl.ds(r0, ck), :]
        trif = tri.astype(F32)
        g_cum = jnp.dot(trif, g, precision=hi, preferred_element_type=F32)
        g_tot = g_cum[tot_row:tot_row + 1, :]
        q_dec = q * jnp.exp(g_cum)
        k_inv = (k * jnp.exp(-g_cum)).astype(BF16)
        k_end = (k * jnp.exp(g_tot - g_cum)).astype(BF16)
        zero = jnp.zeros_like(q_dec)
        qs = jnp.concatenate([jnp.where(lane < GLA_DK, q_dec, zero),
                              jnp.where(lane >= GLA_DK, q_dec, zero)], axis=0).astype(BF16)
        att = lax.dot_general(qs, k_inv, (((1,), (1,)), ((), ())), preferred_element_type=F32)
        tri2 = jnp.concatenate([tri, tri], axis=0)
        att = jnp.where(tri2, att, 0.0).astype(BF16)
        oi = jnp.dot(att, v, preferred_element_type=F32)
        o_intra = jnp.concatenate([oi[:ck, :GLA_DV], oi[ck:, GLA_DV:]], axis=1)
        s_prev = s_ref[...]
        o_inter = jnp.dot(q_dec.astype(BF16), s_prev.astype(BF16), preferred_element_type=F32)
        s_loc = lax.dot_general(k_end, v, (((0,), (0,)), ((), ())), preferred_element_type=F32)
        gt = lax.dot_general(g, ones, (((0,), (0,)), ((), ())), precision=hi,
                             preferred_element_type=F32)
        dec = jnp.exp(gt)
        dec2 = jnp.concatenate([dec, dec], axis=1)
        s_ref[...] = dec2 * s_prev + jnp.where(blk, s_loc, 0.0)
        oacc_s[pl.ds(r0, ck), :] = oacc_s[pl.ds(r0, ck), :] + o_intra + o_inter

    def body(n, _):
        rf = pl.multiple_of(n * ck, ck)
        chunk(rf, gf_s, sf_s, tri_f, ck - 1)
        nb = jnp.where(n < ncc, ncc - 1 - n, nc - 1 - (n - ncc))
        rb = pl.multiple_of(nb * ck, ck)
        chunk(rb, gb_s, sb_s, tri_b, 0)
        return 0

    lax.fori_loop(0, nc, body, 0)

    ng = ng_ref[...]
    for r0 in range(0, t, rt):
        o = oacc_s[r0:r0 + rt, :]
        r = r_ref[0, r0:r0 + rt, :]
        outs = []
        for hh in range(2):
            oh = o[:, hh * GLA_DV:(hh + 1) * GLA_DV]
            oh = oh * lax.rsqrt(jnp.mean(oh * oh, axis=-1, keepdims=True) + RMS_EPS) * ng
            outs.append(oh)
        on = jnp.concatenate(outs, axis=1)
        o_ref[0, r0:r0 + rt, :] = (on * (r * _sigmoid(r))).astype(BF16)


def _gla(a16, a32, wa, ba, ng, *, ctx_len):
    b, t, _ = a16.shape
    assert t % GLA_CHUNK == 0 and ctx_len % GLA_CHUNK == 0
    return pl.pallas_call(
        functools.partial(_gla_kernel, t=t, ctx_len=ctx_len),
        grid=(b, 2),
        in_specs=[
            pl.BlockSpec((1, t, 128), lambda bi, p: (bi, 0, A32_GLQ // 128 + p)),
            pl.BlockSpec((1, t, 128), lambda bi, p: (bi, 0, A32_GLK // 128 + p)),
            pl.BlockSpec((1, t, 256), lambda bi, p: (bi, 0, A16_GV // 256 + p)),
            pl.BlockSpec((1, t, 128), lambda bi, p: (bi, 0, A32_GLA // 128)),
            pl.BlockSpec((1, t, 256), lambda bi, p: (bi, 0, A32_GLR // 256 + p)),
            pl.BlockSpec((1, 128, 256), lambda bi, p: (p, 0, 0)),
            pl.BlockSpec((1, 1, 256), lambda bi, p: (p, 0, 0)),
            pl.BlockSpec((1, GLA_DV), lambda bi, p: (0, 0)),
        ],
        out_specs=pl.BlockSpec((1, t, 256), lambda bi, p: (bi, 0, p)),
        out_shape=jax.ShapeDtypeStruct((b, t, GLA_HEADS * GLA_DV), BF16),
        scratch_shapes=[
            pltpu.VMEM((t, 128), F32), pltpu.VMEM((t, 128), F32),
            pltpu.VMEM((t, 256), F32),
            pltpu.VMEM((128, 256), F32), pltpu.VMEM((128, 256), F32),
        ],
        compiler_params=_cparams(("parallel", "parallel")),
        name="gla_bidir",
    )(a32, a32, a16, a32, a32, wa, ba, ng)


def _merge_kernel(x_ref, od_ref, og_ref, gates_ref, cvb_ref, cvc_ref, cvx_ref,
                  cvc_p_ref, cvx_p_ref, cvc_n_ref, cvx_n_ref, cw_ref,
                  wdo_ref, wco_ref, wgo_ref, wo_ref, modl_ref, modc_ref, n2g_ref,
                  rw_ref, rb_ref, xo_ref, h2_ref, idx_ref, gate_ref, *, tm, ctx_len, t):
    i = pl.program_id(1)
    pos = i * tm + lax.broadcasted_iota(jnp.int32, (tm, 1), 0)
    is_ctx = pos < ctx_len
    row = lax.broadcasted_iota(jnp.int32, (tm, 1), 0)

    z = cvc_ref[0] * cvx_ref[0]
    z_before = cvc_p_ref[0, 7:8, :] * cvx_p_ref[0, 7:8, :]
    z_after = cvc_n_ref[0, 0:1, :] * cvx_n_ref[0, 0:1, :]
    z_prev = jnp.where(row == 0, z_before, pltpu.roll(z, 1, 0))
    z_prev = jnp.where((pos == 0) | (pos == ctx_len), 0.0, z_prev)
    z_next = jnp.where(row == tm - 1, z_after, pltpu.roll(z, tm - 1, 0))
    z_next = jnp.where((pos == ctx_len - 1) | (pos == t - 1), 0.0, z_next)
    conv = z_prev * cw_ref[0:1, :] + z * cw_ref[1:2, :] + z_next * cw_ref[2:3, :]
    zc = (cvb_ref[0] * conv).astype(BF16)

    y_diff = jnp.dot(od_ref[0], wdo_ref[...], preferred_element_type=F32)
    y_conv = jnp.dot(zc, wco_ref[...], preferred_element_type=F32)
    y_gla = jnp.dot(og_ref[0], wgo_ref[...], preferred_element_type=F32)
    d = D_MODEL
    mix = (_sigmoid(gates_ref[0, :, 0:d]) * y_diff
           + _sigmoid(gates_ref[0, :, d:2 * d]) * y_conv
           + _sigmoid(gates_ref[0, :, 2 * d:3 * d]) * y_gla)
    m = jnp.dot(mix.astype(BF16), wo_ref[...], preferred_element_type=F32)

    g1 = jnp.where(is_ctx, modc_ref[2:3, :], modl_ref[0, 2:3, :])
    x_new = x_ref[0] + g1 * m
    xo_ref[0] = x_new
    sh2 = jnp.where(is_ctx, modc_ref[3:4, :], modl_ref[0, 3:4, :])
    sc2 = jnp.where(is_ctx, modc_ref[4:5, :], modl_ref[0, 4:5, :])
    h2 = _norm_mod(x_new, n2g_ref[...], sh2, sc2)
    h2_ref[0] = h2

    logits = jnp.dot(h2.astype(BF16), rw_ref[...], preferred_element_type=F32) + rb_ref[...]
    lane = lax.broadcasted_iota(jnp.int32, logits.shape, 1)
    vals = logits
    tops = []
    idx_out = jnp.zeros(logits.shape, jnp.int32)
    for kk in range(TOP_K):
        mx = jnp.max(vals, axis=-1, keepdims=True)
        ix = jnp.min(jnp.where(vals == mx, lane, 128), axis=-1, keepdims=True)
        tops.append(mx)
        idx_out = jnp.where(lane == kk, ix, idx_out)
        vals = jnp.where(lane == ix, -jnp.inf, vals)
    es = [jnp.exp(v - tops[0]) for v in tops]
    den = es[0] + es[1] + es[2] + es[3]
    gate_out = jnp.zeros(logits.shape, F32)
    for kk in range(TOP_K):
        gate_out = jnp.where(lane == kk, es[kk] / den, gate_out)
    idx_ref[0] = idx_out
    gate_ref[0] = gate_out


def _merge(x, od, og, a32, cw, wdo, wco, wgo, wo, modl, modc, n2g, rw, rb, *, ctx_len):
    b, t, d = x.shape
    tm = _row_tile(t, 272)
    nt8 = t // 8
    row = lambda bi, i: (bi, i, 0)
    const2 = lambda bi, i: (0, 0)
    cw8 = pl.BlockSpec((1, 8, CONV_W), lambda bi, i: (bi, jnp.maximum(i * (tm // 8) - 1, 0),
                                                     A32_CVC // CONV_W))
    return pl.pallas_call(
        functools.partial(_merge_kernel, tm=tm, ctx_len=ctx_len, t=t),
        grid=(b, t // tm),
        in_specs=[
            pl.BlockSpec((1, tm, d), row),
            pl.BlockSpec((1, tm, d), row),
            pl.BlockSpec((1, tm, GLA_HEADS * GLA_DV), row),
            pl.BlockSpec((1, tm, 3 * d), lambda bi, i: (bi, i, 0)),
            pl.BlockSpec((1, tm, CONV_W), lambda bi, i: (bi, i, A32_CVB // CONV_W)),
            pl.BlockSpec((1, tm, CONV_W), lambda bi, i: (bi, i, A32_CVC // CONV_W)),
            pl.BlockSpec((1, tm, CONV_W), lambda bi, i: (bi, i, A32_CVX // CONV_W)),
            pl.BlockSpec((1, 8, CONV_W), lambda bi, i: (bi, jnp.maximum(i * (tm // 8) - 1, 0),
                                                   A32_CVC // CONV_W)),
            pl.BlockSpec((1, 8, CONV_W), lambda bi, i: (bi, jnp.maximum(i * (tm // 8) - 1, 0),
                                                   A32_CVX // CONV_W)),
            pl.BlockSpec((1, 8, CONV_W), lambda bi, i: (bi, jnp.minimum((i + 1) * (tm // 8), nt8 - 1),
                                                   A32_CVC // CONV_W)),
            pl.BlockSpec((1, 8, CONV_W), lambda bi, i: (bi, jnp.minimum((i + 1) * (tm // 8), nt8 - 1),
                                                   A32_CVX // CONV_W)),
            pl.BlockSpec((8, CONV_W), const2),
            pl.BlockSpec((d, d), const2),
            pl.BlockSpec((CONV_W, d), const2),
            pl.BlockSpec((GLA_HEADS * GLA_DV, d), const2),
            pl.BlockSpec((d, d), const2),
            pl.BlockSpec((1, 8, d), lambda bi, i: (bi, 0, 0)),
            pl.BlockSpec((8, d), const2),
            pl.BlockSpec((1, d), const2),
            pl.BlockSpec((d, 128), const2),
            pl.BlockSpec((1, 128), const2),
        ],
        out_specs=[
            pl.BlockSpec((1, tm, d), row),
            pl.BlockSpec((1, tm, d), row),
            pl.BlockSpec((1, tm, 128), row),
            pl.BlockSpec((1, tm, 128), row),
        ],
        out_shape=[
            jax.ShapeDtypeStruct((b, t, d), F32),
            jax.ShapeDtypeStruct((b, t, d), F32),
            jax.ShapeDtypeStruct((b, t, 128), jnp.int32),
            jax.ShapeDtypeStruct((b, t, 128), F32),
        ],
        compiler_params=_cparams(("parallel", "parallel")),
        name="merge_router",
    )(x, od, og, a32, a32, a32, a32, a32, a32, a32, a32, cw, wdo, wco, wgo, wo,
      modl, modc, n2g, rw, rb)


def _moe_kernel(be_ref, g_ref, w1g_ref, w1l_ref, b1g_ref, b1l_ref, w2_ref, b2_ref,
                src_hbm, dst_hbm, h2_hbm, out_hbm, src_s, dst_s, xbuf, ybuf, sem_p, sem_g, sem_s,
                *, nb):
    i = pl.program_id(0)
    bm = MOE_BLOCK
    cur = lax.rem(i, 2)
    nxt = 1 - cur

    def idx_copies(blk_src, blk_dst, slot):
        return (pltpu.make_async_copy(src_hbm.at[blk_src, 0], src_s.at[pl.ds(slot * bm, bm)], sem_p),
                pltpu.make_async_copy(dst_hbm.at[blk_dst, 0], dst_s.at[pl.ds(slot * bm, bm)], sem_p))

    def start_gather(slot, buf):
        for j in range(bm):
            pltpu.make_async_copy(h2_hbm.at[pl.ds(src_s[slot * bm + j], 1)],
                                  xbuf.at[buf, pl.ds(j, 1)], sem_g.at[buf]).start()

    def wait_gather(buf):
        pltpu.make_async_copy(h2_hbm.at[pl.ds(0, bm)], xbuf.at[buf], sem_g.at[buf]).wait()

    def wait_scatter():
        pltpu.make_async_copy(ybuf.at[0], out_hbm.at[pl.ds(0, bm)], sem_s).wait()

    @pl.when(i == 0)
    def _():
        ybuf[...] = jnp.zeros_like(ybuf)
        for blk_src, blk_dst, slot in ((0, 0, 0), (1, 1, 1), (0, nb, 3)):
            for cp in idx_copies(blk_src, blk_dst, slot):
                cp.start()
                cp.wait()
        start_gather(0, 0)

    @pl.when(i >= 1)
    def _():
        for cp in idx_copies(0, 0, 0):
            cp.wait()
        wait_scatter()

    wait_gather(cur)

    start_gather(lax.rem(i + 1, 4), nxt)
    pslot = lax.rem(i + 3, 4)
    for j in range(bm):
        pltpu.make_async_copy(ybuf.at[nxt, pl.ds(j, 1)],
                              out_hbm.at[pl.ds(dst_s[pslot * bm + j], 1)], sem_s).start()
    x = xbuf[cur].astype(BF16)
    hg = jnp.dot(x, w1g_ref[0], preferred_element_type=F32) + b1g_ref[0]
    hl = jnp.dot(x, w1l_ref[0], preferred_element_type=F32) + b1l_ref[0]
    hg = jnp.minimum(hg, SWIGLU_LIMIT)
    hl = jnp.clip(hl, -SWIGLU_LIMIT, SWIGLU_LIMIT)
    act = hg * _sigmoid(SWIGLU_ALPHA * hg) * (hl + 1.0)
    y = jnp.dot(act.astype(BF16), w2_ref[0], preferred_element_type=F32) + b2_ref[0]
    ybuf[cur] = y * g_ref[...]

    @pl.when(i < nb)
    def _():
        blk = jnp.minimum(i + 2, nb - 1)
        for cp in idx_copies(blk, blk, lax.rem(i + 2, 4)):
            cp.start()

    @pl.when(i == nb)
    def _():
        wait_scatter()
        wait_gather(nxt)


def _moe(block_exp, row_src, row_dst, row_gate, h2_flat, w1g, w1l, b1g, b1l, w2, b2):
    r, d = h2_flat.shape
    nb = row_src.shape[0]
    bm = MOE_BLOCK
    de = w1g.shape[-1]
    blk = lambda i: jnp.minimum(i, nb - 1)
    wmap = lambda i, be: (be[blk(i)], 0, 0)
    return pl.pallas_call(
        functools.partial(_moe_kernel, nb=nb),
        grid_spec=pltpu.PrefetchScalarGridSpec(
            num_scalar_prefetch=1,
            grid=(nb + 1,),
            in_specs=[
                pl.BlockSpec((bm, 1), lambda i, be: (blk(i), 0)),
                pl.BlockSpec((1, d, de), wmap),
                pl.BlockSpec((1, d, de), wmap),
                pl.BlockSpec((1, 1, de), wmap),
                pl.BlockSpec((1, 1, de), wmap),
                pl.BlockSpec((1, de, d), wmap),
                pl.BlockSpec((1, 1, d), wmap),
                pl.BlockSpec(memory_space=pl.ANY),
                pl.BlockSpec(memory_space=pl.ANY),
                pl.BlockSpec(memory_space=pl.ANY),
            ],
            out_specs=pl.BlockSpec(memory_space=pl.ANY),
            scratch_shapes=[
                pltpu.SMEM((4 * bm,), jnp.int32),
                pltpu.SMEM((4 * bm,), jnp.int32),
                pltpu.VMEM((2, bm, d), F32),
                pltpu.VMEM((2, bm, d), F32),
                pltpu.SemaphoreType.DMA,
                pltpu.SemaphoreType.DMA((2,)),
                pltpu.SemaphoreType.DMA,
            ],
        ),
        out_shape=jax.ShapeDtypeStruct((r * TOP_K + 2 * bm, d), F32),
        compiler_params=_cparams(("arbitrary",)),
        name="moe_experts",
    )(block_exp, row_gate, w1g, w1l, b1g, b1l, w2, b2, row_src, row_dst, h2_flat)


def _combine_kernel(x_ref, y4_ref, modl_ref, modc_ref, fg_ref, o_ref, *, tm, ctx_len, final):
    i = pl.program_id(1)
    d = D_MODEL
    f = (y4_ref[:, 0:d] + y4_ref[:, d:2 * d]) + (y4_ref[:, 2 * d:3 * d] + y4_ref[:, 3 * d:4 * d])
    pos = i * tm + lax.broadcasted_iota(jnp.int32, (tm, 1), 0)
    g2 = jnp.where(pos < ctx_len, modc_ref[5:6, :], modl_ref[0, 5:6, :])
    x_new = x_ref[0] + g2 * f
    if final:
        x_new = x_new * lax.rsqrt(jnp.mean(x_new * x_new, axis=-1, keepdims=True) + RMS_EPS) * fg_ref[...]
    o_ref[0] = x_new


def _combine(x, y4, modl, modc, fg, *, ctx_len, final):
    b, t, d = x.shape
    tm = _row_tile(t, 272)
    npb = t // tm
    return pl.pallas_call(
        functools.partial(_combine_kernel, tm=tm, ctx_len=ctx_len, final=final),
        grid=(b, npb),
        in_specs=[
            pl.BlockSpec((1, tm, d), lambda bi, i: (bi, i, 0)),
            pl.BlockSpec((tm, TOP_K * d), lambda bi, i: (bi * npb + i, 0)),
            pl.BlockSpec((1, 8, d), lambda bi, i: (bi, 0, 0)),
            pl.BlockSpec((8, d), lambda bi, i: (0, 0)),
            pl.BlockSpec((1, d), lambda bi, i: (0, 0)),
        ],
        out_specs=pl.BlockSpec((1, tm, d), lambda bi, i: (bi, i, 0)),
        out_shape=jax.ShapeDtypeStruct((b, t, d), F32),
        compiler_params=_cparams(("parallel", "parallel")),
        name="moe_combine",
    )(x, y4, modl, modc, fg)


def _routing(idx, gate):
    r = idx.shape[0]
    npair = r * TOP_K
    bm = MOE_BLOCK
    e_flat = idx.reshape(-1)
    onehot = (e_flat[:, None] == jnp.arange(N_EXPERTS, dtype=jnp.int32)[None, :]).astype(jnp.int32)
    csum = jnp.cumsum(onehot, axis=0)
    counts = csum[-1]
    rank = jnp.take_along_axis(csum, e_flat[:, None], axis=1)[:, 0] - 1
    padded = (counts + bm - 1) // bm * bm
    pad_end = jnp.cumsum(padded)
    pad_start = pad_end - padded
    dest = pad_start[e_flat] + rank
    nb = npair // bm + N_EXPERTS
    nr = nb * bm
    row_pair = jnp.full((nr,), -1, jnp.int32).at[dest].set(jnp.arange(npair, dtype=jnp.int32))
    valid = row_pair >= 0
    rp = jnp.maximum(row_pair, 0)
    row_gate = jnp.where(valid, gate.reshape(-1)[rp], 0.0).astype(F32)
    row_src = (rp // TOP_K).reshape(nb, bm)
    j = jnp.arange(nr, dtype=jnp.int32)
    dump = npair + ((j // bm) % 2) * bm + j % bm
    row_dst = jnp.where(valid, row_pair, dump).reshape(nb, bm)
    row_dst = jnp.concatenate([row_dst, (npair + bm + jnp.arange(bm, dtype=jnp.int32))[None, :]], axis=0)
    block_exp = jnp.minimum(
        jnp.searchsorted(pad_end, jnp.arange(nb, dtype=jnp.int32) * bm, side='right'),
        N_EXPERTS - 1).astype(jnp.int32)
    return row_src[:, None, :], row_dst[:, None, :], row_gate[:, None], block_exp


def _rope_tables(ctx_len, seq_len):
    inv_freq = ROPE_BASE ** (-jnp.arange(0, ROT_AXIS_DIM, 2, dtype=F32) / ROT_AXIS_DIM)
    p = jnp.arange(seq_len, dtype=jnp.int32)
    row = (p // GRID_W).astype(F32)
    col = (p % GRID_W).astype(F32)
    lane = np.arange(128)
    dd = lane % DIFF_HEAD_DIM
    fidx = jnp.asarray((dd % ROT_AXIS_DIM) % (ROT_AXIS_DIM // 2))
    use_col = jnp.asarray(dd >= ROT_AXIS_DIM)
    second_half = jnp.asarray((dd % ROT_AXIS_DIM) >= ROT_AXIS_DIM // 2)
    posm = jnp.where(use_col[None, :], col[:, None], row[:, None])
    ang = posm * inv_freq[fidx][None, :]
    cos = jnp.cos(ang)
    sin = jnp.sin(ang)
    sa = jnp.where(second_half[None, :], sin, 0.0)
    sb = jnp.where(second_half[None, :], 0.0, -sin)
    pad = lambda a, v: jnp.concatenate([jnp.full((ctx_len, 128), v, F32), a], axis=0)
    return pad(cos, 1.0), pad(sa, 0.0), pad(sb, 0.0)


def kernel(x, c, ctx, c_ctx, ada_w, ada_b, norm1_g, norm2_g, w_in, diff_lambda, diff_subln_g,
           diff_w_out, conv_w, conv_w_out, gla_w_a2, gla_b_a, gla_norm_g, gla_w_out, w_o,
           router_w, router_b, moe_w1, moe_b1, moe_w2, moe_b2, final_norm_g):
    b, s, d = x.shape
    cl = ctx.shape[1]
    t = cl + s
    nl = w_in.shape[0]
    assert d == D_MODEL and b <= 15

    xs = jnp.concatenate([ctx, x], axis=1)
    cc = jnp.concatenate([c, c_ctx[None, :], jnp.zeros((15 - b, d), F32)], axis=0)
    mod = _ada(cc, ada_w, ada_b).reshape(nl, 16, N_ADA, d)
    cos, sa, sb = _rope_tables(cl, s)

    sp = np.cumsum([1024, 1024, 1024, 512, 512, 512, 256, 256, 512, 512, 32, 3072])
    (w_q, w_k, w_v, w_cb, w_cc, w_cx, w_gq, w_gk, w_gv, w_gr, w_ga, w_gt) = jnp.split(
        w_in, [int(v) for v in sp[:-1]], axis=-1)
    w_all = jnp.concatenate(
        [w_q * (DIFF_HEAD_DIM ** -0.5), w_k, w_v, w_gv,
         w_gt, w_cb, w_cc, w_cx, w_gq * (GLA_DK ** -0.5), w_gk, w_gr, w_ga,
         jnp.zeros((nl, d, A32_W - A32_GLA - 2 * GLA_RANK), F32)], axis=-1).astype(BF16)

    wa = jnp.zeros((nl, 2, 128, 256), F32)
    for p in range(2):
        wa = wa.at[:, p, 0:GLA_RANK, 0:128].set(gla_w_a2[:, 0, :, p * 128:(p + 1) * 128])
        wa = wa.at[:, p, GLA_RANK:2 * GLA_RANK, 128:256].set(gla_w_a2[:, 1, :, p * 128:(p + 1) * 128])
    wa = wa.astype(BF16)
    ba = jnp.stack([jnp.concatenate([gla_b_a[:, 0, p * 128:(p + 1) * 128],
                                     gla_b_a[:, 1, p * 128:(p + 1) * 128]], axis=-1)
                    for p in range(2)], axis=1)[:, :, None, :]

    cw = jnp.concatenate([conv_w, jnp.zeros((nl, 5, CONV_W), F32)], axis=1)
    rw = jnp.concatenate([router_w, jnp.zeros((nl, d, 128 - N_EXPERTS), F32)], axis=-1).astype(BF16)
    rb = jnp.concatenate([router_b, jnp.full((nl, 128 - N_EXPERTS), -1e30, F32)], axis=-1)[:, None, :]
    w1 = moe_w1.reshape(nl, N_EXPERTS, d, D_EXPERT, 2)
    w1g = w1[..., 0].astype(BF16)
    w1l = w1[..., 1].astype(BF16)
    b1 = moe_b1.reshape(nl, N_EXPERTS, 1, D_EXPERT, 2)
    b1g = b1[..., 0]
    b1l = b1[..., 1]
    w2 = moe_w2.astype(BF16)
    b2 = moe_b2[:, :, None, :]
    wdo = diff_w_out.astype(BF16)
    wco = conv_w_out.astype(BF16)
    wgo = gla_w_out.astype(BF16)
    wo = w_o.astype(BF16)

    for layer in range(nl):
        last = layer == nl - 1
        lam_init = 0.8 - 0.6 * math.exp(-0.3 * layer)
        ml = mod[layer]
        modl = jnp.concatenate([ml[:b], jnp.zeros((b, 2, d), F32)], axis=1)
        modc = jnp.concatenate([ml[b], jnp.zeros((2, d), F32)], axis=0)

        a16, a32 = _k1(xs, modl, modc, norm1_g[layer][None, :], cos, sa, sb, w_all[layer], ctx_len=cl)
        od = _attn(a16, diff_lambda[layer], diff_subln_g[layer][None, :],
                   ctx_len=cl, seq_len=s, lam_init=lam_init)
        og = _gla(a16, a32, wa[layer], ba[layer], gla_norm_g[layer][None, :], ctx_len=cl)
        xs, h2, idx, gate = _merge(xs, od, og, a32, cw[layer], wdo[layer], wco[layer], wgo[layer],
                                   wo[layer], modl, modc, norm2_g[layer][None, :], rw[layer],
                                   rb[layer], ctx_len=cl)

        r = b * t
        row_src, row_dst, row_gate, block_exp = _routing(
            idx.reshape(r, 128)[:, :TOP_K], gate.reshape(r, 128)[:, :TOP_K])
        y4 = _moe(block_exp, row_src, row_dst, row_gate, h2.reshape(r, d), w1g[layer], w1l[layer],
                  b1g[layer], b1l[layer], w2[layer], b2[layer])
        xs = _combine(xs, y4.reshape(-1, TOP_K * d), modl, modc, final_norm_g[None, :],
                      ctx_len=cl, final=last)

    return xs[:, cl:, :]
```

```python
import functools
import math

import numpy as np
import jax
import jax.numpy as jnp
from jax import lax
from jax.experimental import pallas as pl
from jax.experimental.pallas import tpu as pltpu

F32 = jnp.float32
BF16 = jnp.bfloat16

D_MODEL = 1024
GRID_W = 64
RMS_EPS = 1e-6
N_ADA = 6
DIFF_HEADS = 8
DIFF_HEAD_DIM = 64
DIFF_V_DIM = 128
ROPE_BASE = 10000.0
ROT_AXIS_DIM = 32
CONV_W = 512
GLA_HEADS = 4
GLA_DK = 64
GLA_DV = 128
GLA_RANK = 16
GLA_TAU = 16.0
GLA_CHUNK = 64
N_EXPERTS = 32
TOP_K = 4
D_EXPERT = 1024
SWIGLU_LIMIT = 7.0
SWIGLU_ALPHA = 1.702
LOG2_E = 1.4426950408889634

A16_Q, A16_K, A16_V, A16_GV, A16_W = 0, 1024, 2048, 3072, 3584
A32_GATES, A32_CVB, A32_CVC, A32_CVX = 0, 3072, 3584, 4096
A32_GLQ, A32_GLK, A32_GLR, A32_GLA, A32_W = 4608, 4864, 5120, 5632, 5760
W_ALL_COLS = A16_W + A32_W

Q_TILE = 256
KV_TILE = 2048
ATTN_HEADS = 2
MOE_BLOCK = 256
GLA_UNROLL = 2
TOK_ROWS = 8
VMEM_LIMIT = 56 * 1024 * 1024


def _cparams(sem):
    return pltpu.CompilerParams(dimension_semantics=sem, vmem_limit_bytes=VMEM_LIMIT)


def _row_tile(t, target):
    best = None
    for cand in range(16, target + 1, 16):
        if t % cand == 0:
            best = cand
    assert best is not None, t
    return best


def _sigmoid(v):
    return 1.0 / (1.0 + jnp.exp(-v))


def _ada_kernel(cc_ref, w_ref, b_ref, o_ref):
    a = cc_ref[...]
    a = a * _sigmoid(a)
    o_ref[0] = jnp.dot(a, w_ref[0], precision=lax.Precision.HIGHEST,
                       preferred_element_type=F32) + b_ref[0]


def _ada(cc, ada_w, ada_b):
    nl, d, n = ada_w.shape
    tn = 1024
    return pl.pallas_call(
        _ada_kernel,
        grid=(nl, n // tn),
        in_specs=[
            pl.BlockSpec((16, d), lambda l, j: (0, 0)),
            pl.BlockSpec((1, d, tn), lambda l, j: (l, 0, j)),
            pl.BlockSpec((1, 1, tn), lambda l, j: (l, 0, j)),
        ],
        out_specs=pl.BlockSpec((1, 16, tn), lambda l, j: (l, 0, j)),
        out_shape=jax.ShapeDtypeStruct((nl, 16, n), F32),
        compiler_params=_cparams(("parallel", "parallel")),
        name="ada_mod",
    )(cc, ada_w, ada_b.reshape(nl, 1, n))


def _norm_mod(x, ng, shift, scale):
    y = x * lax.rsqrt(jnp.mean(x * x, axis=-1, keepdims=True) + RMS_EPS) * ng
    return y * (1.0 + scale) + shift


def _k1_kernel(x_ref, modl_ref, modc_ref, ng_ref, cos_ref, sa_ref, sb_ref, w_ref,
               o16_ref, o32_ref, *, tm, ctx_len):
    i = pl.program_id(1)
    pos = i * tm + lax.broadcasted_iota(jnp.int32, (tm, 1), 0)
    is_ctx = pos < ctx_len
    shift = jnp.where(is_ctx, modc_ref[0:1, :], modl_ref[0, 0:1, :])
    scale = jnp.where(is_ctx, modc_ref[1:2, :], modl_ref[0, 1:2, :])
    h = _norm_mod(x_ref[0], ng_ref[...], shift, scale).astype(BF16)
    cos = cos_ref[...]
    sa = sa_ref[...]
    sb = sb_ref[...]
    ch = 512
    for c0 in range(0, A16_W, ch):
        acc = jnp.dot(h, w_ref[:, c0:c0 + ch], preferred_element_type=F32)
        if c0 < A16_V:
            parts = []
            for j in range(ch // 128):
                a = acc[:, j * 128:(j + 1) * 128]
                parts.append(a * cos + pltpu.roll(a, 16, 1) * sa + pltpu.roll(a, 112, 1) * sb)
            acc = jnp.concatenate(parts, axis=1)
        if c0 < A16_K:
            acc = acc * LOG2_E
        o16_ref[0, :, c0:c0 + ch] = acc.astype(BF16)
    for c0 in range(0, A32_W, ch):
        c1 = min(c0 + ch, A32_W)
        o32_ref[0, :, c0:c1] = jnp.dot(h, w_ref[:, A16_W + c0:A16_W + c1],
                                       preferred_element_type=F32)


def _k1(x, modl, modc, ng, cos, sa, sb, w_all, *, ctx_len):
    b, t, d = x.shape
    tm = _row_tile(t, 272)
    tbl = pl.BlockSpec((tm, 128), lambda bi, i: (i, 0))
    return pl.pallas_call(
        functools.partial(_k1_kernel, tm=tm, ctx_len=ctx_len),
        grid=(b, t // tm),
        in_specs=[
            pl.BlockSpec((1, tm, d), lambda bi, i: (bi, i, 0)),
            pl.BlockSpec((1, 8, d), lambda bi, i: (bi, 0, 0)),
            pl.BlockSpec((8, d), lambda bi, i: (0, 0)),
            pl.BlockSpec((1, d), lambda bi, i: (0, 0)),
            tbl, tbl, tbl,
            pl.BlockSpec((d, W_ALL_COLS), lambda bi, i: (0, 0), pipeline_mode=pl.Buffered(1)),
        ],
        out_specs=[
            pl.BlockSpec((1, tm, A16_W), lambda bi, i: (bi, i, 0)),
            pl.BlockSpec((1, tm, A32_W), lambda bi, i: (bi, i, 0)),
        ],
        out_shape=[
            jax.ShapeDtypeStruct((b, t, A16_W), BF16),
            jax.ShapeDtypeStruct((b, t, A32_W), F32),
        ],
        compiler_params=_cparams(("parallel", "parallel")),
        name="norm_inproj",
    )(x, modl, modc, ng, cos, sa, sb, w_all)


def _attn_kernel(lam_ref, g_ref, q_ref, k_ref, v_ref, o_ref, *, ctx_len, seq_len, tk, lam_init):
    qi = pl.program_id(2)
    tq = Q_TILE
    hb = DIFF_V_DIM
    qss = []
    for g in range(ATTN_HEADS):
        q = q_ref[0, :, g * hb:(g + 1) * hb]
        lane = lax.broadcasted_iota(jnp.int32, q.shape, 1)
        zero = jnp.zeros_like(q)
        qss.append(jnp.concatenate([jnp.where(lane < DIFF_HEAD_DIM, q, zero),
                                    jnp.where(lane >= DIFF_HEAD_DIM, q, zero)], axis=0))

    def tile(g, carry, r0, rows):
        m, l, acc = carry
        k_t = k_ref[0, pl.ds(r0, rows), g * hb:(g + 1) * hb]
        v_t = v_ref[0, pl.ds(r0, rows), g * hb:(g + 1) * hb]
        s = lax.dot_general(qss[g], k_t, (((1,), (1,)), ((), ())), preferred_element_type=F32)
        m_new = jnp.maximum(m, jnp.max(s, axis=-1, keepdims=True))
        alpha = jnp.exp2(m - m_new)
        p = jnp.exp2(s - m_new)
        l = alpha * l + jnp.sum(p, axis=-1, keepdims=True)
        acc = alpha * acc + jnp.dot(p.astype(BF16), v_t, preferred_element_type=F32)
        return m_new, l, acc

    init = (jnp.full((2 * tq, 1), -jnp.inf, F32), jnp.zeros((2 * tq, 1), F32),
            jnp.zeros((2 * tq, hb), F32))
    carries = tuple(tile(g, init, 0, ctx_len) for g in range(ATTN_HEADS))

    def body(j, cs):
        r0 = pl.multiple_of(ctx_len + j * tk, 256)
        return tuple(tile(g, cs[g], r0, tk) for g in range(ATTN_HEADS))

    n_lat = jnp.where(qi * tq >= ctx_len, seq_len // tk, 0)
    carries = lax.fori_loop(0, n_lat, body, carries)

    lv = lam_ref[...]
    lam = (jnp.exp(jnp.sum(lv[0:1] * lv[1:2], axis=1, keepdims=True))
           - jnp.exp(jnp.sum(lv[2:3] * lv[3:4], axis=1, keepdims=True)) + lam_init)
    for g in range(ATTN_HEADS):
        _, l, acc = carries[g]
        o = acc[:tq] / l[:tq] - lam * (acc[tq:] / l[tq:])
        o = o * lax.rsqrt(jnp.mean(o * o, axis=-1, keepdims=True) + RMS_EPS) * g_ref[...]
        o_ref[0, :, g * hb:(g + 1) * hb] = (o * (1.0 - lam_init)).astype(BF16)


def _attn(a16, lam_p, subln_g, *, ctx_len, seq_len, lam_init):
    b, t, _ = a16.shape
    tk = min(KV_TILE, seq_len)
    assert ctx_len % Q_TILE == 0 and seq_len % tk == 0 and tk % 256 == 0
    hb = DIFF_V_DIM * ATTN_HEADS
    return pl.pallas_call(
        functools.partial(_attn_kernel, ctx_len=ctx_len, seq_len=seq_len, tk=tk, lam_init=lam_init),
        grid=(b, DIFF_HEADS // ATTN_HEADS, t // Q_TILE),
        in_specs=[
            pl.BlockSpec((4, DIFF_HEAD_DIM), lambda bi, h, qi: (0, 0)),
            pl.BlockSpec((1, DIFF_V_DIM), lambda bi, h, qi: (0, 0)),
            pl.BlockSpec((1, Q_TILE, hb), lambda bi, h, qi: (bi, qi, A16_Q // hb + h)),
            pl.BlockSpec((1, t, hb), lambda bi, h, qi: (bi, 0, A16_K // hb + h)),
            pl.BlockSpec((1, t, hb), lambda bi, h, qi: (bi, 0, A16_V // hb + h)),
        ],
        out_specs=pl.BlockSpec((1, Q_TILE, hb), lambda bi, h, qi: (bi, qi, h)),
        out_shape=jax.ShapeDtypeStruct((b, t, DIFF_HEADS * DIFF_V_DIM), BF16),
        compiler_params=_cparams(("parallel", "parallel", "parallel")),
        name="diff_attn",
    )(lam_p, subln_g, a16, a16, a16)


def _split3(x):
    hi = x.astype(BF16)
    r1 = x - hi.astype(F32)
    mid = r1.astype(BF16)
    lo = (r1 - mid.astype(F32)).astype(BF16)
    return jnp.concatenate([hi, mid, lo], axis=1)


def _sum3(y, w):
    return y[:, 0:w] + y[:, w:2 * w] + y[:, 2 * w:3 * w]


def _gla_kernel(q_ref, k_ref, v_ref, a_ref, r_ref, wa_ref, ba_ref, ng_ref, o_ref,
                qd_s, ki_s, ke_s, dec_s, oacc_s, st_s, *, t, ctx_len):
    ck = GLA_CHUNK
    nc = t // ck
    ncc = ctx_len // ck
    pr = 256
    w = 2 * GLA_DK

    ri = lax.broadcasted_iota(jnp.int32, (pr, pr), 0)
    ci = lax.broadcasted_iota(jnp.int32, (pr, pr), 1)
    same = (ri // ck) == (ci // ck)
    m_tot = jnp.where(same, 1.0, 0.0).astype(BF16)
    m_cum = (jnp.where(same & (ci <= ri), 1.0, 0.0).astype(BF16),
             jnp.where(same & (ci >= ri), 1.0, 0.0).astype(BF16))
    for r0 in range(0, t, pr):
        a = a_ref[0, r0:r0 + pr, :].astype(BF16)
        logit = jnp.dot(a, wa_ref[0], preferred_element_type=F32) + ba_ref[0]
        g = (jnp.minimum(logit, 0.0) - jnp.log1p(jnp.exp(-jnp.abs(logit)))) * (1.0 / GLA_TAU)
        q = q_ref[0, r0:r0 + pr, :]
        k = k_ref[0, r0:r0 + pr, :]
        for d in range(2):
            g3 = _split3(g[:, d * w:(d + 1) * w])
            g_cum = _sum3(jnp.dot(m_cum[d], g3, preferred_element_type=F32), w)
            g_tot = _sum3(jnp.dot(m_tot, g3, preferred_element_type=F32), w)
            qd_s[d, r0:r0 + pr, :] = (q * jnp.exp(g_cum)).astype(BF16)
            ki_s[d, r0:r0 + pr, :] = (k * jnp.exp(-g_cum)).astype(BF16)
            ke_s[d, r0:r0 + pr, :] = (k * jnp.exp(g_tot - g_cum)).astype(BF16)
            dec_s[d, r0:r0 + pr, :] = jnp.exp(g_tot)
        oacc_s[r0:r0 + pr, :] = jnp.zeros((pr, 2 * GLA_DV), F32)
    st_s[...] = jnp.zeros_like(st_s)

    ri = lax.broadcasted_iota(jnp.int32, (2 * ck, ck), 0)
    ci = lax.broadcasted_iota(jnp.int32, (2 * ck, ck), 1)
    rr = jnp.where(ri >= ck, ri - ck, ri)
    tri = (ci <= rr, ci >= rr)
    lane = lax.broadcasted_iota(jnp.int32, (ck, w), 1)
    br = lax.broadcasted_iota(jnp.int32, (2 * GLA_DV, w), 0)
    bc = lax.broadcasted_iota(jnp.int32, (2 * GLA_DV, w), 1)
    blk = (br < GLA_DV) == (bc < GLA_DK)
    nt = (((1,), (1,)), ((), ()))
    tn = (((0,), (0,)), ((), ()))

    def chunk(d, r0):
        qd = qd_s[d, pl.ds(r0, ck), :]
        ki = ki_s[d, pl.ds(r0, ck), :]
        ke = ke_s[d, pl.ds(r0, ck), :]
        v = v_ref[0, pl.ds(r0, ck), :]
        dec = dec_s[d, pl.ds(r0, 8), :][0:1, :]
        zq = jnp.zeros_like(qd)
        qs = jnp.concatenate([jnp.where(lane < GLA_DK, qd, zq), jnp.where(lane >= GLA_DK, qd, zq)], axis=0)
        att = lax.dot_general(qs, ki, nt, preferred_element_type=F32)
        att = jnp.where(tri[d], att, 0.0).astype(BF16)
        oi = jnp.dot(att, v, preferred_element_type=F32)
        o_intra = jnp.concatenate([oi[:ck, :GLA_DV], oi[ck:, GLA_DV:]], axis=1)
        st = st_s[d]
        o_inter = lax.dot_general(qd, st.astype(BF16), nt, preferred_element_type=F32)
        s_loc = lax.dot_general(v, ke, tn, preferred_element_type=F32)
        st_s[d] = st * dec + jnp.where(blk, s_loc, 0.0)
        oacc_s[pl.ds(r0, ck), :] = oacc_s[pl.ds(r0, ck), :] + o_intra + o_inter

    def body(n, _):
        chunk(0, pl.multiple_of(n * ck, ck))
        nb = jnp.where(n < ncc, ncc - 1 - n, nc - 1 - (n - ncc))
        chunk(1, pl.multiple_of(nb * ck, ck))
        return 0

    lax.fori_loop(0, nc, body, 0, unroll=GLA_UNROLL)

    ng = ng_ref[...]
    rt = _row_tile(t, 544)
    for r0 in range(0, t, rt):
        o = oacc_s[r0:r0 + rt, :]
        r = r_ref[0, r0:r0 + rt, :]
        outs = []
        for hh in range(2):
            oh = o[:, hh * GLA_DV:(hh + 1) * GLA_DV]
            outs.append(oh * lax.rsqrt(jnp.mean(oh * oh, axis=-1, keepdims=True) + RMS_EPS) * ng)
        o_ref[0, r0:r0 + rt, :] = (jnp.concatenate(outs, axis=1) * (r * _sigmoid(r))).astype(BF16)


def _gla(a16, a32, wa, ba, ng, *, ctx_len):
    b, t, _ = a16.shape
    assert t % 256 == 0 and ctx_len % GLA_CHUNK == 0 and (t // GLA_CHUNK) % GLA_UNROLL == 0
    return pl.pallas_call(
        functools.partial(_gla_kernel, t=t, ctx_len=ctx_len),
        grid=(b, 2),
        in_specs=[
            pl.BlockSpec((1, t, 128), lambda bi, p: (bi, 0, A32_GLQ // 128 + p)),
            pl.BlockSpec((1, t, 128), lambda bi, p: (bi, 0, A32_GLK // 128 + p)),
            pl.BlockSpec((1, t, 256), lambda bi, p: (bi, 0, A16_GV // 256 + p)),
            pl.BlockSpec((1, t, 128), lambda bi, p: (bi, 0, A32_GLA // 128)),
            pl.BlockSpec((1, t, 256), lambda bi, p: (bi, 0, A32_GLR // 256 + p)),
            pl.BlockSpec((1, 128, 256), lambda bi, p: (p, 0, 0)),
            pl.BlockSpec((1, 1, 256), lambda bi, p: (p, 0, 0)),
            pl.BlockSpec((1, GLA_DV), lambda bi, p: (0, 0)),
        ],
        out_specs=pl.BlockSpec((1, t, 256), lambda bi, p: (bi, 0, p)),
        out_shape=jax.ShapeDtypeStruct((b, t, GLA_HEADS * GLA_DV), BF16),
        scratch_shapes=[
            pltpu.VMEM((2, t, 128), BF16), pltpu.VMEM((2, t, 128), BF16), pltpu.VMEM((2, t, 128), BF16),
            pltpu.VMEM((2, t, 128), F32),
            pltpu.VMEM((t, 256), F32),
            pltpu.VMEM((2, 256, 128), F32),
        ],
        compiler_params=_cparams(("parallel", "parallel")),
        name="gla_bidir",
    )(a32, a32, a16, a32, a32, wa, ba, ng)


def _merge_kernel(x_ref, od_ref, og_ref, gates_ref, cvb_ref, cvc_ref, cvx_ref,
                  cvc_p_ref, cvx_p_ref, cvc_n_ref, cvx_n_ref, cw_ref,
                  wdo_ref, wco_ref, wgo_ref, wo_ref, modl_ref, modc_ref, n2g_ref,
                  rw_ref, rb_ref, xo_ref, h2_ref, idx_ref, gate_ref, *, tm, ctx_len, t):
    i = pl.program_id(1)
    pos = i * tm + lax.broadcasted_iota(jnp.int32, (tm, 1), 0)
    is_ctx = pos < ctx_len
    row = lax.broadcasted_iota(jnp.int32, (tm, 1), 0)

    z = cvc_ref[0] * cvx_ref[0]
    z_before = cvc_p_ref[0, 7:8, :] * cvx_p_ref[0, 7:8, :]
    z_after = cvc_n_ref[0, 0:1, :] * cvx_n_ref[0, 0:1, :]
    z_prev = jnp.where(row == 0, z_before, pltpu.roll(z, 1, 0))
    z_prev = jnp.where((pos == 0) | (pos == ctx_len), 0.0, z_prev)
    z_next = jnp.where(row == tm - 1, z_after, pltpu.roll(z, tm - 1, 0))
    z_next = jnp.where((pos == ctx_len - 1) | (pos == t - 1), 0.0, z_next)
    conv = z_prev * cw_ref[0:1, :] + z * cw_ref[1:2, :] + z_next * cw_ref[2:3, :]
    zc = (cvb_ref[0] * conv).astype(BF16)

    y_diff = jnp.dot(od_ref[0], wdo_ref[...], preferred_element_type=F32)
    y_conv = jnp.dot(zc, wco_ref[...], preferred_element_type=F32)
    y_gla = jnp.dot(og_ref[0], wgo_ref[...], preferred_element_type=F32)
    d = D_MODEL
    mix = (_sigmoid(gates_ref[0, :, 0:d]) * y_diff
           + _sigmoid(gates_ref[0, :, d:2 * d]) * y_conv
           + _sigmoid(gates_ref[0, :, 2 * d:3 * d]) * y_gla)
    m = jnp.dot(mix.astype(BF16), wo_ref[...], preferred_element_type=F32)

    g1 = jnp.where(is_ctx, modc_ref[2:3, :], modl_ref[0, 2:3, :])
    x_new = x_ref[0] + g1 * m
    xo_ref[0] = x_new
    sh2 = jnp.where(is_ctx, modc_ref[3:4, :], modl_ref[0, 3:4, :])
    sc2 = jnp.where(is_ctx, modc_ref[4:5, :], modl_ref[0, 4:5, :])
    h2 = _norm_mod(x_new, n2g_ref[...], sh2, sc2)
    for cc in range(d // 128):
        h2_ref[0, pl.ds(cc, tm, stride=8), :] = h2[:, cc * 128:(cc + 1) * 128]

    logits = jnp.dot(h2.astype(BF16), rw_ref[...], preferred_element_type=F32) + rb_ref[...]
    lane = lax.broadcasted_iota(jnp.int32, logits.shape, 1)
    vals = logits
    tops = []
    idx_out = jnp.zeros(logits.shape, jnp.int32)
    for kk in range(TOP_K):
        mx = jnp.max(vals, axis=-1, keepdims=True)
        ix = jnp.min(jnp.where(vals == mx, lane, 128), axis=-1, keepdims=True)
        tops.append(mx)
        idx_out = jnp.where(lane == kk, ix, idx_out)
        vals = jnp.where(lane == ix, -jnp.inf, vals)
    es = [jnp.exp(v - tops[0]) for v in tops]
    den = es[0] + es[1] + es[2] + es[3]
    gate_out = jnp.zeros(logits.shape, F32)
    for kk in range(TOP_K):
        gate_out = jnp.where(lane == kk, es[kk] / den, gate_out)
    idx_ref[0] = idx_out
    gate_ref[0] = gate_out


def _merge(x, od, og, a32, cw, wdo, wco, wgo, wo, modl, modc, n2g, rw, rb, *, ctx_len):
    b, t, d = x.shape
    tm = _row_tile(t, 272)
    nt8 = t // 8
    row = lambda bi, i: (bi, i, 0)
    const2 = lambda bi, i: (0, 0)
    return pl.pallas_call(
        functools.partial(_merge_kernel, tm=tm, ctx_len=ctx_len, t=t),
        grid=(b, t // tm),
        in_specs=[
            pl.BlockSpec((1, tm, d), row),
            pl.BlockSpec((1, tm, d), row),
            pl.BlockSpec((1, tm, GLA_HEADS * GLA_DV), row),
            pl.BlockSpec((1, tm, 3 * d), lambda bi, i: (bi, i, 0)),
            pl.BlockSpec((1, tm, CONV_W), lambda bi, i: (bi, i, A32_CVB // CONV_W)),
            pl.BlockSpec((1, tm, CONV_W), lambda bi, i: (bi, i, A32_CVC // CONV_W)),
            pl.BlockSpec((1, tm, CONV_W), lambda bi, i: (bi, i, A32_CVX // CONV_W)),
            pl.BlockSpec((1, 8, CONV_W), lambda bi, i: (bi, jnp.maximum(i * (tm // 8) - 1, 0),
                                                   A32_CVC // CONV_W)),
            pl.BlockSpec((1, 8, CONV_W), lambda bi, i: (bi, jnp.maximum(i * (tm // 8) - 1, 0),
                                                   A32_CVX // CONV_W)),
            pl.BlockSpec((1, 8, CONV_W), lambda bi, i: (bi, jnp.minimum((i + 1) * (tm // 8), nt8 - 1),
                                                   A32_CVC // CONV_W)),
            pl.BlockSpec((1, 8, CONV_W), lambda bi, i: (bi, jnp.minimum((i + 1) * (tm // 8), nt8 - 1),
                                                   A32_CVX // CONV_W)),
            pl.BlockSpec((8, CONV_W), const2),
            pl.BlockSpec((d, d), const2),
            pl.BlockSpec((CONV_W, d), const2),
            pl.BlockSpec((GLA_HEADS * GLA_DV, d), const2),
            pl.BlockSpec((d, d), const2),
            pl.BlockSpec((1, 8, d), lambda bi, i: (bi, 0, 0)),
            pl.BlockSpec((8, d), const2),
            pl.BlockSpec((1, d), const2),
            pl.BlockSpec((d, 128), const2),
            pl.BlockSpec((1, 128), const2),
        ],
        out_specs=[
            pl.BlockSpec((1, tm, d), row),
            pl.BlockSpec((1, tm * (d // 128), 128), row),
            pl.BlockSpec((1, tm, 128), row),
            pl.BlockSpec((1, tm, 128), row),
        ],
        out_shape=[
            jax.ShapeDtypeStruct((b, t, d), F32),
            jax.ShapeDtypeStruct((b, t * (d // 128), 128), F32),
            jax.ShapeDtypeStruct((b, t, 128), jnp.int32),
            jax.ShapeDtypeStruct((b, t, 128), F32),
        ],
        compiler_params=_cparams(("parallel", "parallel")),
        name="merge_router",
    )(x, od, og, a32, a32, a32, a32, a32, a32, a32, a32, cw, wdo, wco, wgo, wo,
      modl, modc, n2g, rw, rb)


def _moe_kernel(be_ref, nv_ref, g_ref, w1g_ref, w1l_ref, b1g_ref, b1l_ref, w2_ref, b2_ref,
                src_hbm, dst_hbm, h2_hbm, out_hbm, src_s, dst_s, xbuf, ybuf, sem_p, sem_g, sem_s,
                *, nb, n_slots):
    i = pl.program_id(0)
    nv = nv_ref[0]
    bm = MOE_BLOCK
    cur = lax.rem(i, 2)
    nxt = 1 - cur

    def idx_copies(blk_src, blk_dst, slot):
        return (pltpu.make_async_copy(src_hbm.at[blk_src, 0], src_s.at[pl.ds(slot * bm, bm)], sem_p),
                pltpu.make_async_copy(dst_hbm.at[blk_dst, 0], dst_s.at[pl.ds(slot * bm, bm)], sem_p))

    def start_gather(slot, buf):
        for j in range(bm):
            r0 = pl.multiple_of(src_s[slot * bm + j], TOK_ROWS)
            pltpu.make_async_copy(h2_hbm.at[pl.ds(r0, TOK_ROWS)],
                                  xbuf.at[buf, pl.ds(j * TOK_ROWS, TOK_ROWS)], sem_g.at[buf]).start()

    def wait_gather(buf):
        pltpu.make_async_copy(h2_hbm.at[pl.ds(0, bm * TOK_ROWS)], xbuf.at[buf], sem_g.at[buf]).wait()

    def wait_scatter():
        pltpu.make_async_copy(ybuf.at[0], out_hbm.at[pl.ds(0, bm * TOK_ROWS)], sem_s).wait()

    @pl.when(i <= nv)
    def _():
        @pl.when(i == 0)
        def _():
            ybuf[...] = jnp.zeros_like(ybuf)
            for blk_src, blk_dst, slot in ((0, 0, 0), (1, 1, 1), (0, nb, 3)):
                for cp in idx_copies(blk_src, blk_dst, slot):
                    cp.start()
                    cp.wait()
            start_gather(0, 0)

        @pl.when(i >= 1)
        def _():
            for cp in idx_copies(0, 0, 0):
                cp.wait()
            wait_scatter()

        wait_gather(cur)

        start_gather(lax.rem(i + 1, 4), nxt)
        pslot = lax.rem(i + 3, 4)
        for j in range(bm):
            r0 = pl.multiple_of(dst_s[pslot * bm + j], TOK_ROWS)
            pltpu.make_async_copy(ybuf.at[nxt, pl.ds(j * TOK_ROWS, TOK_ROWS)],
                                  out_hbm.at[pl.ds(r0, TOK_ROWS)], sem_s).start()
        x = jnp.concatenate([xbuf[cur, pl.ds(cc, bm, stride=TOK_ROWS), :] for cc in range(TOK_ROWS)],
                            axis=1).astype(BF16)
        hg = jnp.dot(x, w1g_ref[0, 0], preferred_element_type=F32) + b1g_ref[0]
        hl = jnp.dot(x, w1l_ref[0, 0], preferred_element_type=F32) + b1l_ref[0]
        hg = jnp.minimum(hg, SWIGLU_LIMIT)
        hl = jnp.clip(hl, -SWIGLU_LIMIT, SWIGLU_LIMIT)
        act = hg * _sigmoid(SWIGLU_ALPHA * hg) * (hl + 1.0)
        y = jnp.dot(act.astype(BF16), w2_ref[0], preferred_element_type=F32) + b2_ref[0]
        y = y * g_ref[...]
        for cc in range(TOK_ROWS):
            ybuf[cur, pl.ds(cc, bm, stride=TOK_ROWS), :] = y[:, cc * 128:(cc + 1) * 128]

        @pl.when(i < nv)
        def _():
            blk = jnp.minimum(i + 2, nv - 1)
            for cp in idx_copies(blk, blk, lax.rem(i + 2, 4)):
                cp.start()

        @pl.when(i == nv)
        def _():
            wait_scatter()
            wait_gather(nxt)
            ybuf[...] = jnp.zeros_like(ybuf)
            fills = [pltpu.make_async_copy(
                ybuf.at[sl], out_hbm.at[pl.ds((n_slots + sl * bm) * TOK_ROWS, bm * TOK_ROWS)], sem_s)
                for sl in range(2)]
            for cp in fills:
                cp.start()
            for cp in fills:
                cp.wait()


def _moe(block_exp, n_valid, row_src, row_dst, row_gate, h2_flat, w1, b1g, b1l, w2, b2, *, n_slots):
    d = D_MODEL
    nb = row_src.shape[0]
    bm = MOE_BLOCK
    de = w1.shape[-1]
    blk = lambda i, nv: jnp.minimum(i, nv[0] - 1)
    wmap = lambda i, be, nv: (be[blk(i, nv)], 0, 0)
    return pl.pallas_call(
        functools.partial(_moe_kernel, nb=nb, n_slots=n_slots),
        grid_spec=pltpu.PrefetchScalarGridSpec(
            num_scalar_prefetch=2,
            grid=(nb + 1,),
            in_specs=[
                pl.BlockSpec((bm, 1), lambda i, be, nv: (blk(i, nv), 0)),
                pl.BlockSpec((1, 1, d, de), lambda i, be, nv: (be[blk(i, nv)], 0, 0, 0)),
                pl.BlockSpec((1, 1, d, de), lambda i, be, nv: (be[blk(i, nv)], 1, 0, 0)),
                pl.BlockSpec((1, 1, de), wmap),
                pl.BlockSpec((1, 1, de), wmap),
                pl.BlockSpec((1, de, d), wmap),
                pl.BlockSpec((1, 1, d), wmap),
                pl.BlockSpec(memory_space=pl.ANY),
                pl.BlockSpec(memory_space=pl.ANY),
                pl.BlockSpec(memory_space=pl.ANY),
            ],
            out_specs=pl.BlockSpec(memory_space=pl.ANY),
            scratch_shapes=[
                pltpu.SMEM((4 * bm,), jnp.int32),
                pltpu.SMEM((4 * bm,), jnp.int32),
                pltpu.VMEM((2, bm * TOK_ROWS, 128), F32),
                pltpu.VMEM((2, bm * TOK_ROWS, 128), F32),
                pltpu.SemaphoreType.DMA,
                pltpu.SemaphoreType.DMA((2,)),
                pltpu.SemaphoreType.DMA,
            ],
        ),
        out_shape=jax.ShapeDtypeStruct(((n_slots + 2 * bm) * TOK_ROWS, 128), F32),
        compiler_params=_cparams(("arbitrary",)),
        name="moe_experts",
    )(block_exp, n_valid, row_gate, w1, w1, b1g, b1l, w2, b2, row_src, row_dst, h2_flat)


def _combine_kernel(x_ref, y4_ref, modl_ref, modc_ref, fg_ref, o_ref, *, tm, ctx_len, final, skip):
    i = pl.program_id(1) + skip
    d = D_MODEL
    ld = lambda k, cc: y4_ref[pl.ds(k * TOK_ROWS + cc, tm, stride=TOP_K * TOK_ROWS), :]
    f = jnp.concatenate([(ld(0, cc) + ld(1, cc)) + (ld(2, cc) + ld(3, cc)) for cc in range(TOK_ROWS)],
                        axis=1)
    pos = i * tm + lax.broadcasted_iota(jnp.int32, (tm, 1), 0)
    g2 = jnp.where(pos < ctx_len, modc_ref[5:6, :], modl_ref[0, 5:6, :])
    x_new = x_ref[0] + g2 * f
    if final:
        x_new = x_new * lax.rsqrt(jnp.mean(x_new * x_new, axis=-1, keepdims=True) + RMS_EPS) * fg_ref[...]
    o_ref[0] = x_new


def _combine(x, y4, modl, modc, fg, *, ctx_len, final):
    b, t, d = x.shape
    if final:
        tm = Q_TILE
        assert ctx_len % tm == 0
        skip = ctx_len // tm
    else:
        tm = _row_tile(t, 272)
        skip = 0
    npb = t // tm
    return pl.pallas_call(
        functools.partial(_combine_kernel, tm=tm, ctx_len=ctx_len, final=final, skip=skip),
        grid=(b, npb - skip),
        in_specs=[
            pl.BlockSpec((1, tm, d), lambda bi, i: (bi, i + skip, 0)),
            pl.BlockSpec((tm * TOP_K * TOK_ROWS, 128), lambda bi, i: (bi * (npb - skip) + i, 0)),
            pl.BlockSpec((1, 8, d), lambda bi, i: (bi, 0, 0)),
            pl.BlockSpec((8, d), lambda bi, i: (0, 0)),
            pl.BlockSpec((1, d), lambda bi, i: (0, 0)),
        ],
        out_specs=pl.BlockSpec((1, tm, d), lambda bi, i: (bi, i, 0)),
        out_shape=jax.ShapeDtypeStruct((b, t - skip * tm, d), F32),
        compiler_params=_cparams(("parallel", "parallel")),
        name="moe_combine",
    )(x, y4, modl, modc, fg)


def _routing(idx, gate, tok_slot, n_slots):
    r = idx.shape[0]
    npair = r * TOP_K
    bm = MOE_BLOCK
    e_flat = jnp.where(tok_slot[:, None] < 0, N_EXPERTS, idx).reshape(-1)
    skey = jnp.sort(e_flat * npair + jnp.arange(npair, dtype=jnp.int32))
    starts = jnp.searchsorted(skey, jnp.arange(N_EXPERTS + 1, dtype=jnp.int32) * npair).astype(jnp.int32)
    counts = starts[1:] - starts[:-1]
    padded = (counts + bm - 1) // bm * bm
    pad_end = jnp.cumsum(padded)
    pad_start = pad_end - padded
    nb = npair // bm + N_EXPERTS
    nr = nb * bm
    block_exp = jnp.minimum(
        jnp.searchsorted(pad_end, jnp.arange(nb, dtype=jnp.int32) * bm, side='right'),
        N_EXPERTS - 1).astype(jnp.int32)
    j = jnp.arange(nr, dtype=jnp.int32)
    e_row = block_exp[j // bm]
    rank = j - pad_start[e_row]
    valid = rank < counts[e_row]
    rp = skey[jnp.minimum(starts[e_row] + rank, npair - 1)] - e_row * npair
    rp = jnp.where(valid, rp, 0)
    row_gate = jnp.where(valid, gate.reshape(-1)[rp], 0.0).astype(F32)
    row_src = (rp // TOP_K * TOK_ROWS).reshape(nb, bm)
    dump = n_slots + ((j // bm) % 2) * bm + j % bm
    row_dst = jnp.where(valid, tok_slot[rp // TOP_K] * TOP_K + rp % TOP_K, dump).reshape(nb, bm)
    row_dst = jnp.concatenate([row_dst, (n_slots + bm + jnp.arange(bm, dtype=jnp.int32))[None, :]],
                              axis=0) * TOK_ROWS
    n_valid = (pad_end[-1] // bm).astype(jnp.int32).reshape(1)
    return row_src[:, None, :], row_dst[:, None, :], row_gate[:, None], block_exp, n_valid


def _rope_tables(ctx_len, seq_len):
    inv_freq = ROPE_BASE ** (-jnp.arange(0, ROT_AXIS_DIM, 2, dtype=F32) / ROT_AXIS_DIM)
    p = jnp.arange(seq_len, dtype=jnp.int32)
    row = (p // GRID_W).astype(F32)
    col = (p % GRID_W).astype(F32)
    lane = np.arange(128)
    dd = lane % DIFF_HEAD_DIM
    fidx = jnp.asarray((dd % ROT_AXIS_DIM) % (ROT_AXIS_DIM // 2))
    use_col = jnp.asarray(dd >= ROT_AXIS_DIM)
    second_half = jnp.asarray((dd % ROT_AXIS_DIM) >= ROT_AXIS_DIM // 2)
    posm = jnp.where(use_col[None, :], col[:, None], row[:, None])
    ang = posm * inv_freq[fidx][None, :]
    cos = jnp.cos(ang)
    sin = jnp.sin(ang)
    sa = jnp.where(second_half[None, :], sin, 0.0)
    sb = jnp.where(second_half[None, :], 0.0, -sin)
    pad = lambda a, v: jnp.concatenate([jnp.full((ctx_len, 128), v, F32), a], axis=0)
    return pad(cos, 1.0), pad(sa, 0.0), pad(sb, 0.0)


def kernel(x, c, ctx, c_ctx, ada_w, ada_b, norm1_g, norm2_g, w_in, diff_lambda, diff_subln_g,
           diff_w_out, conv_w, conv_w_out, gla_w_a2, gla_b_a, gla_norm_g, gla_w_out, w_o,
           router_w, router_b, moe_w1, moe_b1, moe_w2, moe_b2, final_norm_g):
    b, s, d = x.shape
    cl = ctx.shape[1]
    t = cl + s
    nl = w_in.shape[0]
    assert d == D_MODEL and b <= 15

    xs = jnp.concatenate([ctx, x], axis=1)
    cc = jnp.concatenate([c, c_ctx[None, :], jnp.zeros((15 - b, d), F32)], axis=0)
    mod = _ada(cc, ada_w, ada_b).reshape(nl, 16, N_ADA, d)
    cos, sa, sb = _rope_tables(cl, s)

    sp = np.cumsum([1024, 1024, 1024, 512, 512, 512, 256, 256, 512, 512, 32, 3072])
    (w_q, w_k, w_v, w_cb, w_cc, w_cx, w_gq, w_gk, w_gv, w_gr, w_ga, w_gt) = jnp.split(
        w_in, [int(v) for v in sp[:-1]], axis=-1)
    w_all = jnp.concatenate(
        [w_q * (DIFF_HEAD_DIM ** -0.5), w_k, w_v, w_gv,
         w_gt, w_cb, w_cc, w_cx, w_gq * (GLA_DK ** -0.5), w_gk, w_gr, w_ga,
         jnp.zeros((nl, d, A32_W - A32_GLA - 2 * GLA_RANK), F32)], axis=-1).astype(BF16)

    wa = jnp.zeros((nl, 2, 128, 256), F32)
    for p in range(2):
        wa = wa.at[:, p, 0:GLA_RANK, 0:128].set(gla_w_a2[:, 0, :, p * 128:(p + 1) * 128])
        wa = wa.at[:, p, GLA_RANK:2 * GLA_RANK, 128:256].set(gla_w_a2[:, 1, :, p * 128:(p + 1) * 128])
    wa = wa.astype(BF16)
    ba = jnp.stack([jnp.concatenate([gla_b_a[:, 0, p * 128:(p + 1) * 128],
                                     gla_b_a[:, 1, p * 128:(p + 1) * 128]], axis=-1)
                    for p in range(2)], axis=1)[:, :, None, :]

    cw = jnp.concatenate([conv_w, jnp.zeros((nl, 5, CONV_W), F32)], axis=1)
    rw = jnp.concatenate([router_w, jnp.zeros((nl, d, 128 - N_EXPERTS), F32)], axis=-1).astype(BF16)
    rb = jnp.concatenate([router_b, jnp.full((nl, 128 - N_EXPERTS), -1e30, F32)], axis=-1)[:, None, :]
    w1 = jnp.moveaxis(moe_w1.reshape(nl, N_EXPERTS, d, D_EXPERT, 2), -1, 2).astype(BF16)
    b1 = moe_b1.reshape(nl, N_EXPERTS, 1, D_EXPERT, 2)
    b1g = b1[..., 0]
    b1l = b1[..., 1]
    w2 = moe_w2.astype(BF16)
    b2 = moe_b2[:, :, None, :]
    wdo = diff_w_out.astype(BF16)
    wco = conv_w_out.astype(BF16)
    wgo = gla_w_out.astype(BF16)
    wo = w_o.astype(BF16)

    for layer in range(nl):
        last = layer == nl - 1
        lam_init = 0.8 - 0.6 * math.exp(-0.3 * layer)
        ml = mod[layer]
        modl = jnp.concatenate([ml[:b], jnp.zeros((b, 2, d), F32)], axis=1)
        modc = jnp.concatenate([ml[b], jnp.zeros((2, d), F32)], axis=0)

        a16, a32 = _k1(xs, modl, modc, norm1_g[layer][None, :], cos, sa, sb, w_all[layer], ctx_len=cl)
        od = _attn(a16, diff_lambda[layer], diff_subln_g[layer][None, :],
                   ctx_len=cl, seq_len=s, lam_init=lam_init)
        og = _gla(a16, a32, wa[layer], ba[layer], gla_norm_g[layer][None, :], ctx_len=cl)
        xs, h2, idx, gate = _merge(xs, od, og, a32, cw[layer], wdo[layer], wco[layer], wgo[layer],
                                   wo[layer], modl, modc, norm2_g[layer][None, :], rw[layer],
                                   rb[layer], ctx_len=cl)

        r = b * t
        if last:
            pos = jnp.arange(t, dtype=jnp.int32)[None, :] - cl
            tok_slot = jnp.where(pos >= 0, jnp.arange(b, dtype=jnp.int32)[:, None] * s + pos, -1).reshape(r)
            n_tok = b * s
        else:
            tok_slot = jnp.arange(r, dtype=jnp.int32)
            n_tok = r
        row_src, row_dst, row_gate, block_exp, n_valid = _routing(
            idx.reshape(r, 128)[:, :TOP_K], gate.reshape(r, 128)[:, :TOP_K], tok_slot, n_tok * TOP_K)
        y4 = _moe(block_exp, n_valid, row_src, row_dst, row_gate, h2.reshape(r * TOK_ROWS, 128), w1[layer],
                  b1g[layer], b1l[layer], w2[layer], b2[layer], n_slots=n_tok * TOP_K)
        xs = _combine(xs, y4, modl, modc, final_norm_g[None, :], ctx_len=cl, final=last)

    return xs
```

```python
import functools
import math

import numpy as np
import jax
import jax.numpy as jnp
from jax import lax
from jax.experimental import pallas as pl
from jax.experimental.pallas import tpu as pltpu

F32 = jnp.float32
BF16 = jnp.bfloat16

D_MODEL = 1024
GRID_W = 64
RMS_EPS = 1e-6
N_ADA = 6
DIFF_HEADS = 8
DIFF_HEAD_DIM = 64
DIFF_V_DIM = 128
ROPE_BASE = 10000.0
ROT_AXIS_DIM = 32
CONV_W = 512
GLA_HEADS = 4
GLA_DK = 64
GLA_DV = 128
GLA_RANK = 16
GLA_TAU = 16.0
GLA_CHUNK = 64
N_EXPERTS = 32
TOP_K = 4
D_EXPERT = 1024
SWIGLU_LIMIT = 7.0
SWIGLU_ALPHA = 1.702
LOG2_E = 1.4426950408889634

A16_Q, A16_K, A16_V, A16_GV, A16_W = 0, 1024, 2048, 3072, 3584
A32_GATES, A32_CVB, A32_CVC, A32_CVX = 0, 3072, 3584, 4096
A32_GLQ, A32_GLK, A32_GLR, A32_GLA, A32_W = 4608, 4864, 5120, 5632, 5760
W_ALL_COLS = A16_W + A32_W

Q_TILE = 256
KV_TILE = 2048
ATTN_HEADS = 2
MOE_BLOCK = 256
GLA_UNROLL = 2
TOK_ROWS = 8
VMEM_LIMIT = 56 * 1024 * 1024


def _cparams(sem):
    return pltpu.CompilerParams(dimension_semantics=sem, vmem_limit_bytes=VMEM_LIMIT)


def _row_tile(t, target):
    best = None
    for cand in range(16, target + 1, 16):
        if t % cand == 0:
            best = cand
    assert best is not None, t
    return best


def _sigmoid(v):
    return 1.0 / (1.0 + jnp.exp(-v))


def _ada_kernel(cc_ref, w_ref, b_ref, o_ref):
    a = cc_ref[...]
    a = a * _sigmoid(a)
    o_ref[0] = jnp.dot(a, w_ref[0], precision=lax.Precision.HIGHEST,
                       preferred_element_type=F32) + b_ref[0]


def _ada(cc, ada_w, ada_b):
    nl, d, n = ada_w.shape
    tn = 1024
    return pl.pallas_call(
        _ada_kernel,
        grid=(nl, n // tn),
        in_specs=[
            pl.BlockSpec((16, d), lambda l, j: (0, 0)),
            pl.BlockSpec((1, d, tn), lambda l, j: (l, 0, j)),
            pl.BlockSpec((1, 1, tn), lambda l, j: (l, 0, j)),
        ],
        out_specs=pl.BlockSpec((1, 16, tn), lambda l, j: (l, 0, j)),
        out_shape=jax.ShapeDtypeStruct((nl, 16, n), F32),
        compiler_params=_cparams(("parallel", "parallel")),
        name="ada_mod",
    )(cc, ada_w, ada_b.reshape(nl, 1, n))


def _norm_mod(x, ng, shift, scale):
    y = x * lax.rsqrt(jnp.mean(x * x, axis=-1, keepdims=True) + RMS_EPS) * ng
    return y * (1.0 + scale) + shift


def _k1_kernel(x_ref, modl_ref, modc_ref, ng_ref, cos_ref, sa_ref, sb_ref, w_ref,
               o16_ref, o32_ref, *, tm, ctx_len):
    i = pl.program_id(1)
    pos = i * tm + lax.broadcasted_iota(jnp.int32, (tm, 1), 0)
    is_ctx = pos < ctx_len
    shift = jnp.where(is_ctx, modc_ref[0:1, :], modl_ref[0, 0:1, :])
    scale = jnp.where(is_ctx, modc_ref[1:2, :], modl_ref[0, 1:2, :])
    h = _norm_mod(x_ref[0], ng_ref[...], shift, scale).astype(BF16)
    cos = cos_ref[...]
    sa = sa_ref[...]
    sb = sb_ref[...]
    ch = 512
    for c0 in range(0, A16_W, ch):
        acc = jnp.dot(h, w_ref[:, c0:c0 + ch], preferred_element_type=F32)
        if c0 < A16_V:
            parts = []
            for j in range(ch // 128):
                a = acc[:, j * 128:(j + 1) * 128]
                parts.append(a * cos + pltpu.roll(a, 16, 1) * sa + pltpu.roll(a, 112, 1) * sb)
            acc = jnp.concatenate(parts, axis=1)
        if c0 < A16_K:
            acc = acc * LOG2_E
        o16_ref[0, :, c0:c0 + ch] = acc.astype(BF16)
    for c0 in range(0, A32_W, ch):
        c1 = min(c0 + ch, A32_W)
        o32_ref[0, :, c0:c1] = jnp.dot(h, w_ref[:, A16_W + c0:A16_W + c1],
                                       preferred_element_type=F32)


def _k1(x, modl, modc, ng, cos, sa, sb, w_all, *, ctx_len):
    b, t, d = x.shape
    tm = _row_tile(t, 272)
    tbl = pl.BlockSpec((tm, 128), lambda bi, i: (i, 0))
    return pl.pallas_call(
        functools.partial(_k1_kernel, tm=tm, ctx_len=ctx_len),
        grid=(b, t // tm),
        in_specs=[
            pl.BlockSpec((1, tm, d), lambda bi, i: (bi, i, 0)),
            pl.BlockSpec((1, 8, d), lambda bi, i: (bi, 0, 0)),
            pl.BlockSpec((8, d), lambda bi, i: (0, 0)),
            pl.BlockSpec((1, d), lambda bi, i: (0, 0)),
            tbl, tbl, tbl,
            pl.BlockSpec((d, W_ALL_COLS), lambda bi, i: (0, 0), pipeline_mode=pl.Buffered(1)),
        ],
        out_specs=[
            pl.BlockSpec((1, tm, A16_W), lambda bi, i: (bi, i, 0)),
            pl.BlockSpec((1, tm, A32_W), lambda bi, i: (bi, i, 0)),
        ],
        out_shape=[
            jax.ShapeDtypeStruct((b, t, A16_W), BF16),
            jax.ShapeDtypeStruct((b, t, A32_W), F32),
        ],
        compiler_params=_cparams(("parallel", "parallel")),
        name="norm_inproj",
    )(x, modl, modc, ng, cos, sa, sb, w_all)


def _attn_kernel(lam_ref, g_ref, q_ref, k_ref, v_ref, o_ref, *, ctx_len, seq_len, tk, lam_init):
    qi = pl.program_id(2)
    tq = Q_TILE
    hb = DIFF_V_DIM
    qss = []
    for g in range(ATTN_HEADS):
        q = q_ref[0, :, g * hb:(g + 1) * hb]
        lane = lax.broadcasted_iota(jnp.int32, q.shape, 1)
        zero = jnp.zeros_like(q)
        qss.append(jnp.concatenate([jnp.where(lane < DIFF_HEAD_DIM, q, zero),
                                    jnp.where(lane >= DIFF_HEAD_DIM, q, zero)], axis=0))

    def tile(g, carry, r0, rows):
        m, l, acc = carry
        k_t = k_ref[0, pl.ds(r0, rows), g * hb:(g + 1) * hb]
        v_t = v_ref[0, pl.ds(r0, rows), g * hb:(g + 1) * hb]
        s = lax.dot_general(qss[g], k_t, (((1,), (1,)), ((), ())), preferred_element_type=F32)
        m_new = jnp.maximum(m, jnp.max(s, axis=-1, keepdims=True))
        alpha = jnp.exp2(m - m_new)
        p = jnp.exp2(s - m_new)
        l = alpha * l + jnp.sum(p, axis=-1, keepdims=True)
        acc = alpha * acc + jnp.dot(p.astype(BF16), v_t, preferred_element_type=F32)
        return m_new, l, acc

    init = (jnp.full((2 * tq, 1), -jnp.inf, F32), jnp.zeros((2 * tq, 1), F32),
            jnp.zeros((2 * tq, hb), F32))
    carries = tuple(tile(g, init, 0, ctx_len) for g in range(ATTN_HEADS))

    def body(j, cs):
        r0 = pl.multiple_of(ctx_len + j * tk, 256)
        return tuple(tile(g, cs[g], r0, tk) for g in range(ATTN_HEADS))

    n_lat = jnp.where(qi * tq >= ctx_len, seq_len // tk, 0)
    carries = lax.fori_loop(0, n_lat, body, carries)

    lv = lam_ref[...]
    lam = (jnp.exp(jnp.sum(lv[0:1] * lv[1:2], axis=1, keepdims=True))
           - jnp.exp(jnp.sum(lv[2:3] * lv[3:4], axis=1, keepdims=True)) + lam_init)
    for g in range(ATTN_HEADS):
        _, l, acc = carries[g]
        o = acc[:tq] / l[:tq] - lam * (acc[tq:] / l[tq:])
        o = o * lax.rsqrt(jnp.mean(o * o, axis=-1, keepdims=True) + RMS_EPS) * g_ref[...]
        o_ref[0, :, g * hb:(g + 1) * hb] = (o * (1.0 - lam_init)).astype(BF16)


def _attn(a16, lam_p, subln_g, *, ctx_len, seq_len, lam_init):
    b, t, _ = a16.shape
    tk = min(KV_TILE, seq_len)
    assert ctx_len % Q_TILE == 0 and seq_len % tk == 0 and tk % 256 == 0
    hb = DIFF_V_DIM * ATTN_HEADS
    return pl.pallas_call(
        functools.partial(_attn_kernel, ctx_len=ctx_len, seq_len=seq_len, tk=tk, lam_init=lam_init),
        grid=(b, DIFF_HEADS // ATTN_HEADS, t // Q_TILE),
        in_specs=[
            pl.BlockSpec((4, DIFF_HEAD_DIM), lambda bi, h, qi: (0, 0)),
            pl.BlockSpec((1, DIFF_V_DIM), lambda bi, h, qi: (0, 0)),
            pl.BlockSpec((1, Q_TILE, hb), lambda bi, h, qi: (bi, qi, A16_Q // hb + h)),
            pl.BlockSpec((1, t, hb), lambda bi, h, qi: (bi, 0, A16_K // hb + h)),
            pl.BlockSpec((1, t, hb), lambda bi, h, qi: (bi, 0, A16_V // hb + h)),
        ],
        out_specs=pl.BlockSpec((1, Q_TILE, hb), lambda bi, h, qi: (bi, qi, h)),
        out_shape=jax.ShapeDtypeStruct((b, t, DIFF_HEADS * DIFF_V_DIM), BF16),
        compiler_params=_cparams(("parallel", "parallel", "parallel")),
        name="diff_attn",
    )(lam_p, subln_g, a16, a16, a16)


def _split3(x):
    hi = x.astype(BF16)
    r1 = x - hi.astype(F32)
    mid = r1.astype(BF16)
    lo = (r1 - mid.astype(F32)).astype(BF16)
    return jnp.concatenate([hi, mid, lo], axis=1)


def _sum3(y, w):
    return y[:, 0:w] + y[:, w:2 * w] + y[:, 2 * w:3 * w]


def _gla_kernel(q_ref, k_ref, v_ref, a_ref, r_ref, wa_ref, ba_ref, ng_ref, o_ref,
                qd_s, ki_s, ke_s, dec_s, oacc_s, st_s, *, t, ctx_len):
    ck = GLA_CHUNK
    nc = t // ck
    ncc = ctx_len // ck
    pr = 256
    w = 2 * GLA_DK

    ri = lax.broadcasted_iota(jnp.int32, (pr, pr), 0)
    ci = lax.broadcasted_iota(jnp.int32, (pr, pr), 1)
    same = (ri // ck) == (ci // ck)
    m_tot = jnp.where(same, 1.0, 0.0).astype(BF16)
    m_cum = (jnp.where(same & (ci <= ri), 1.0, 0.0).astype(BF16),
             jnp.where(same & (ci >= ri), 1.0, 0.0).astype(BF16))
    for r0 in range(0, t, pr):
        a = a_ref[0, r0:r0 + pr, :].astype(BF16)
        logit = jnp.dot(a, wa_ref[0], preferred_element_type=F32) + ba_ref[0]
        g = (jnp.minimum(logit, 0.0) - jnp.log1p(jnp.exp(-jnp.abs(logit)))) * (1.0 / GLA_TAU)
        q = q_ref[0, r0:r0 + pr, :]
        k = k_ref[0, r0:r0 + pr, :]
        for d in range(2):
            g3 = _split3(g[:, d * w:(d + 1) * w])
            g_cum = _sum3(jnp.dot(m_cum[d], g3, preferred_element_type=F32), w)
            g_tot = _sum3(jnp.dot(m_tot, g3, preferred_element_type=F32), w)
            qd_s[d, r0:r0 + pr, :] = (q * jnp.exp(g_cum)).astype(BF16)
            ki_s[d, r0:r0 + pr, :] = (k * jnp.exp(-g_cum)).astype(BF16)
            ke_s[d, r0:r0 + pr, :] = (k * jnp.exp(g_tot - g_cum)).astype(BF16)
            dec_s[d, r0:r0 + pr, :] = jnp.exp(g_tot)
        oacc_s[r0:r0 + pr, :] = jnp.zeros((pr, 2 * GLA_DV), F32)
    st_s[...] = jnp.zeros_like(st_s)

    ri = lax.broadcasted_iota(jnp.int32, (2 * ck, ck), 0)
    ci = lax.broadcasted_iota(jnp.int32, (2 * ck, ck), 1)
    rr = jnp.where(ri >= ck, ri - ck, ri)
    tri = (ci <= rr, ci >= rr)
    lane = lax.broadcasted_iota(jnp.int32, (ck, w), 1)
    br = lax.broadcasted_iota(jnp.int32, (2 * GLA_DV, w), 0)
    bc = lax.broadcasted_iota(jnp.int32, (2 * GLA_DV, w), 1)
    blk = (br < GLA_DV) == (bc < GLA_DK)
    nt = (((1,), (1,)), ((), ()))
    tn = (((0,), (0,)), ((), ()))

    def chunk(d, r0):
        qd = qd_s[d, pl.ds(r0, ck), :]
        ki = ki_s[d, pl.ds(r0, ck), :]
        ke = ke_s[d, pl.ds(r0, ck), :]
        v = v_ref[0, pl.ds(r0, ck), :]
        dec = dec_s[d, pl.ds(r0, 8), :][0:1, :]
        zq = jnp.zeros_like(qd)
        qs = jnp.concatenate([jnp.where(lane < GLA_DK, qd, zq), jnp.where(lane >= GLA_DK, qd, zq)], axis=0)
        att = lax.dot_general(qs, ki, nt, preferred_element_type=F32)
        att = jnp.where(tri[d], att, 0.0).astype(BF16)
        oi = jnp.dot(att, v, preferred_element_type=F32)
        o_intra = jnp.concatenate([oi[:ck, :GLA_DV], oi[ck:, GLA_DV:]], axis=1)
        st = st_s[d]
        o_inter = lax.dot_general(qd, st.astype(BF16), nt, preferred_element_type=F32)
        s_loc = lax.dot_general(v, ke, tn, preferred_element_type=F32)
        st_s[d] = st * dec + jnp.where(blk, s_loc, 0.0)
        oacc_s[pl.ds(r0, ck), :] = oacc_s[pl.ds(r0, ck), :] + o_intra + o_inter

    def body(n, _):
        chunk(0, pl.multiple_of(n * ck, ck))
        nb = jnp.where(n < ncc, ncc - 1 - n, nc - 1 - (n - ncc))
        chunk(1, pl.multiple_of(nb * ck, ck))
        return 0

    lax.fori_loop(0, nc, body, 0, unroll=GLA_UNROLL)

    ng = ng_ref[...]
    rt = _row_tile(t, 544)
    for r0 in range(0, t, rt):
        o = oacc_s[r0:r0 + rt, :]
        r = r_ref[0, r0:r0 + rt, :]
        outs = []
        for hh in range(2):
            oh = o[:, hh * GLA_DV:(hh + 1) * GLA_DV]
            outs.append(oh * lax.rsqrt(jnp.mean(oh * oh, axis=-1, keepdims=True) + RMS_EPS) * ng)
        o_ref[0, r0:r0 + rt, :] = (jnp.concatenate(outs, axis=1) * (r * _sigmoid(r))).astype(BF16)


def _gla(a16, a32, wa, ba, ng, *, ctx_len):
    b, t, _ = a16.shape
    assert t % 256 == 0 and ctx_len % GLA_CHUNK == 0 and (t // GLA_CHUNK) % GLA_UNROLL == 0
    return pl.pallas_call(
        functools.partial(_gla_kernel, t=t, ctx_len=ctx_len),
        grid=(b, 2),
        in_specs=[
            pl.BlockSpec((1, t, 128), lambda bi, p: (bi, 0, A32_GLQ // 128 + p)),
            pl.BlockSpec((1, t, 128), lambda bi, p: (bi, 0, A32_GLK // 128 + p)),
            pl.BlockSpec((1, t, 256), lambda bi, p: (bi, 0, A16_GV // 256 + p)),
            pl.BlockSpec((1, t, 128), lambda bi, p: (bi, 0, A32_GLA // 128)),
            pl.BlockSpec((1, t, 256), lambda bi, p: (bi, 0, A32_GLR // 256 + p)),
            pl.BlockSpec((1, 128, 256), lambda bi, p: (p, 0, 0)),
            pl.BlockSpec((1, 1, 256), lambda bi, p: (p, 0, 0)),
            pl.BlockSpec((1, GLA_DV), lambda bi, p: (0, 0)),
        ],
        out_specs=pl.BlockSpec((1, t, 256), lambda bi, p: (bi, 0, p)),
        out_shape=jax.ShapeDtypeStruct((b, t, GLA_HEADS * GLA_DV), BF16),
        scratch_shapes=[
            pltpu.VMEM((2, t, 128), BF16), pltpu.VMEM((2, t, 128), BF16), pltpu.VMEM((2, t, 128), BF16),
            pltpu.VMEM((2, t, 128), F32),
            pltpu.VMEM((t, 256), F32),
            pltpu.VMEM((2, 256, 128), F32),
        ],
        compiler_params=_cparams(("parallel", "parallel")),
        name="gla_bidir",
    )(a32, a32, a16, a32, a32, wa, ba, ng)


def _merge_kernel(x_ref, od_ref, og_ref, gates_ref, cvb_ref, cvc_ref, cvx_ref,
                  cvc_p_ref, cvx_p_ref, cvc_n_ref, cvx_n_ref, cw_ref,
                  wdo_ref, wco_ref, wgo_ref, wo_ref, modl_ref, modc_ref, n2g_ref,
                  rw_ref, rb_ref, xo_ref, h2_ref, rt_ref, *, tm, ctx_len, t):
    i = pl.program_id(1)
    pos = i * tm + lax.broadcasted_iota(jnp.int32, (tm, 1), 0)
    is_ctx = pos < ctx_len
    row = lax.broadcasted_iota(jnp.int32, (tm, 1), 0)

    z = cvc_ref[0] * cvx_ref[0]
    z_before = cvc_p_ref[0, 7:8, :] * cvx_p_ref[0, 7:8, :]
    z_after = cvc_n_ref[0, 0:1, :] * cvx_n_ref[0, 0:1, :]
    z_prev = jnp.where(row == 0, z_before, pltpu.roll(z, 1, 0))
    z_prev = jnp.where((pos == 0) | (pos == ctx_len), 0.0, z_prev)
    z_next = jnp.where(row == tm - 1, z_after, pltpu.roll(z, tm - 1, 0))
    z_next = jnp.where((pos == ctx_len - 1) | (pos == t - 1), 0.0, z_next)
    conv = z_prev * cw_ref[0:1, :] + z * cw_ref[1:2, :] + z_next * cw_ref[2:3, :]
    zc = (cvb_ref[0] * conv).astype(BF16)

    y_diff = jnp.dot(od_ref[0], wdo_ref[...], preferred_element_type=F32)
    y_conv = jnp.dot(zc, wco_ref[...], preferred_element_type=F32)
    y_gla = jnp.dot(og_ref[0], wgo_ref[...], preferred_element_type=F32)
    d = D_MODEL
    mix = (_sigmoid(gates_ref[0, :, 0:d]) * y_diff
           + _sigmoid(gates_ref[0, :, d:2 * d]) * y_conv
           + _sigmoid(gates_ref[0, :, 2 * d:3 * d]) * y_gla)
    m = jnp.dot(mix.astype(BF16), wo_ref[...], preferred_element_type=F32)

    g1 = jnp.where(is_ctx, modc_ref[2:3, :], modl_ref[0, 2:3, :])
    x_new = x_ref[0] + g1 * m
    xo_ref[0] = x_new
    sh2 = jnp.where(is_ctx, modc_ref[3:4, :], modl_ref[0, 3:4, :])
    sc2 = jnp.where(is_ctx, modc_ref[4:5, :], modl_ref[0, 4:5, :])
    h2 = _norm_mod(x_new, n2g_ref[...], sh2, sc2)
    for cc in range(d // 128):
        h2_ref[0, pl.ds(cc, tm, stride=8), :] = h2[:, cc * 128:(cc + 1) * 128]

    logits = jnp.dot(h2.astype(BF16), rw_ref[...], preferred_element_type=F32) + rb_ref[...]
    lane = lax.broadcasted_iota(jnp.int32, logits.shape, 1)
    vals = logits
    tops = []
    idx_out = jnp.zeros(logits.shape, jnp.int32)
    for kk in range(TOP_K):
        mx = jnp.max(vals, axis=-1, keepdims=True)
        ix = jnp.min(jnp.where(vals == mx, lane, 128), axis=-1, keepdims=True)
        tops.append(mx)
        idx_out = jnp.where(lane == kk, ix, idx_out)
        vals = jnp.where(lane == ix, -jnp.inf, vals)
    es = [jnp.exp(v - tops[0]) for v in tops]
    den = es[0] + es[1] + es[2] + es[3]
    packed = idx_out.astype(F32)
    for kk in range(TOP_K):
        gk = es[kk] / den
        hi = gk.astype(BF16).astype(F32)
        mid = (gk - hi).astype(BF16).astype(F32)
        lo = gk - hi - mid
        for piece, val in enumerate((hi, mid, lo)):
            packed = jnp.where(lane == TOP_K * (piece + 1) + kk, val, packed)
    ri = lax.broadcasted_iota(jnp.int32, (tm, tm), 0)
    ci = lax.broadcasted_iota(jnp.int32, (tm, tm), 1)
    eye = jnp.where(ri == ci, 1.0, 0.0).astype(BF16)
    rows = lax.dot_general(packed.astype(BF16), eye, (((0,), (0,)), ((), ())), preferred_element_type=F32)
    rt_ref[0, 0] = rows[0:4 * TOP_K, :]


def _merge(x, od, og, a32, cw, wdo, wco, wgo, wo, modl, modc, n2g, rw, rb, *, ctx_len):
    b, t, d = x.shape
    tm = _row_tile(t, 272)
    nt8 = t // 8
    row = lambda bi, i: (bi, i, 0)
    const2 = lambda bi, i: (0, 0)
    return pl.pallas_call(
        functools.partial(_merge_kernel, tm=tm, ctx_len=ctx_len, t=t),
        grid=(b, t // tm),
        in_specs=[
            pl.BlockSpec((1, tm, d), row),
            pl.BlockSpec((1, tm, d), row),
            pl.BlockSpec((1, tm, GLA_HEADS * GLA_DV), row),
            pl.BlockSpec((1, tm, 3 * d), lambda bi, i: (bi, i, 0)),
            pl.BlockSpec((1, tm, CONV_W), lambda bi, i: (bi, i, A32_CVB // CONV_W)),
            pl.BlockSpec((1, tm, CONV_W), lambda bi, i: (bi, i, A32_CVC // CONV_W)),
            pl.BlockSpec((1, tm, CONV_W), lambda bi, i: (bi, i, A32_CVX // CONV_W)),
            pl.BlockSpec((1, 8, CONV_W), lambda bi, i: (bi, jnp.maximum(i * (tm // 8) - 1, 0),
                                                   A32_CVC // CONV_W)),
            pl.BlockSpec((1, 8, CONV_W), lambda bi, i: (bi, jnp.maximum(i * (tm // 8) - 1, 0),
                                                   A32_CVX // CONV_W)),
            pl.BlockSpec((1, 8, CONV_W), lambda bi, i: (bi, jnp.minimum((i + 1) * (tm // 8), nt8 - 1),
                                                   A32_CVC // CONV_W)),
            pl.BlockSpec((1, 8, CONV_W), lambda bi, i: (bi, jnp.minimum((i + 1) * (tm // 8), nt8 - 1),
                                                   A32_CVX // CONV_W)),
            pl.BlockSpec((8, CONV_W), const2),
            pl.BlockSpec((d, d), const2),
            pl.BlockSpec((CONV_W, d), const2),
            pl.BlockSpec((GLA_HEADS * GLA_DV, d), const2),
            pl.BlockSpec((d, d), const2),
            pl.BlockSpec((1, 8, d), lambda bi, i: (bi, 0, 0)),
            pl.BlockSpec((8, d), const2),
            pl.BlockSpec((1, d), const2),
            pl.BlockSpec((d, 128), const2),
            pl.BlockSpec((1, 128), const2),
        ],
        out_specs=[
            pl.BlockSpec((1, tm, d), row),
            pl.BlockSpec((1, tm * (d // 128), 128), row),
            pl.BlockSpec((1, 1, 4 * TOP_K, tm), lambda bi, i: (bi, i, 0, 0)),
        ],
        out_shape=[
            jax.ShapeDtypeStruct((b, t, d), F32),
            jax.ShapeDtypeStruct((b, t * (d // 128), 128), F32),
            jax.ShapeDtypeStruct((b, t // tm, 4 * TOP_K, tm), F32),
        ],
        compiler_params=_cparams(("parallel", "parallel")),
        name="merge_router",
    )(x, od, og, a32, a32, a32, a32, a32, a32, a32, a32, cw, wdo, wco, wgo, wo,
      modl, modc, n2g, rw, rb)


def _moe_kernel(be_ref, nv_ref, g_ref, w1g_ref, w1l_ref, b1g_ref, b1l_ref, w2_ref, b2_ref,
                src_hbm, dst_hbm, h2_hbm, out_hbm, src_s, dst_s, xbuf, ybuf, sem_p, sem_g, sem_s,
                *, nb, n_slots):
    i = pl.program_id(0)
    nv = nv_ref[0]
    bm = MOE_BLOCK
    cur = lax.rem(i, 2)
    nxt = 1 - cur

    def idx_copies(blk_src, blk_dst, slot):
        return (pltpu.make_async_copy(src_hbm.at[blk_src, 0], src_s.at[pl.ds(slot * bm, bm)], sem_p),
                pltpu.make_async_copy(dst_hbm.at[blk_dst, 0], dst_s.at[pl.ds(slot * bm, bm)], sem_p))

    def start_gather(slot, buf):
        for j in range(bm):
            r0 = pl.multiple_of(src_s[slot * bm + j], TOK_ROWS)
            pltpu.make_async_copy(h2_hbm.at[pl.ds(r0, TOK_ROWS)],
                                  xbuf.at[buf, pl.ds(j * TOK_ROWS, TOK_ROWS)], sem_g.at[buf]).start()

    def wait_gather(buf):
        pltpu.make_async_copy(h2_hbm.at[pl.ds(0, bm * TOK_ROWS)], xbuf.at[buf], sem_g.at[buf]).wait()

    def wait_scatter():
        pltpu.make_async_copy(ybuf.at[0], out_hbm.at[pl.ds(0, bm * TOK_ROWS)], sem_s).wait()

    @pl.when(i <= nv)
    def _():
        @pl.when(i == 0)
        def _():
            ybuf[...] = jnp.zeros_like(ybuf)
            for blk_src, blk_dst, slot in ((0, 0, 0), (1, 1, 1), (0, nb, 3)):
                for cp in idx_copies(blk_src, blk_dst, slot):
                    cp.start()
                    cp.wait()
            start_gather(0, 0)

        @pl.when(i >= 1)
        def _():
            for cp in idx_copies(0, 0, 0):
                cp.wait()
            wait_scatter()

        wait_gather(cur)

        start_gather(lax.rem(i + 1, 4), nxt)
        pslot = lax.rem(i + 3, 4)
        for j in range(bm):
            r0 = pl.multiple_of(dst_s[pslot * bm + j], TOK_ROWS)
            pltpu.make_async_copy(ybuf.at[nxt, pl.ds(j * TOK_ROWS, TOK_ROWS)],
                                  out_hbm.at[pl.ds(r0, TOK_ROWS)], sem_s).start()
        x = jnp.concatenate([xbuf[cur, pl.ds(cc, bm, stride=TOK_ROWS), :] for cc in range(TOK_ROWS)],
                            axis=1).astype(BF16)
        hg = jnp.dot(x, w1g_ref[0, 0], preferred_element_type=F32) + b1g_ref[0]
        hl = jnp.dot(x, w1l_ref[0, 0], preferred_element_type=F32) + b1l_ref[0]
        hg = jnp.minimum(hg, SWIGLU_LIMIT)
        hl = jnp.clip(hl, -SWIGLU_LIMIT, SWIGLU_LIMIT)
        act = hg * _sigmoid(SWIGLU_ALPHA * hg) * (hl + 1.0)
        y = jnp.dot(act.astype(BF16), w2_ref[0], preferred_element_type=F32) + b2_ref[0]
        gi = lax.broadcasted_iota(jnp.int32, (bm, bm), 0)
        gj = lax.broadcasted_iota(jnp.int32, (bm, bm), 1)
        y = y * jnp.sum(jnp.where(gi == gj, g_ref[0], 0.0), axis=1, keepdims=True)
        for cc in range(TOK_ROWS):
            ybuf[cur, pl.ds(cc, bm, stride=TOK_ROWS), :] = y[:, cc * 128:(cc + 1) * 128]

        @pl.when(i < nv)
        def _():
            blk = jnp.minimum(i + 2, nv - 1)
            for cp in idx_copies(blk, blk, lax.rem(i + 2, 4)):
                cp.start()

        @pl.when(i == nv)
        def _():
            wait_scatter()
            wait_gather(nxt)
            ybuf[...] = jnp.zeros_like(ybuf)
            fills = [pltpu.make_async_copy(
                ybuf.at[sl], out_hbm.at[pl.ds((n_slots + sl * bm) * TOK_ROWS, bm * TOK_ROWS)], sem_s)
                for sl in range(2)]
            for cp in fills:
                cp.start()
            for cp in fills:
                cp.wait()


def _moe(block_exp, n_valid, row_src, row_dst, row_gate, h2_flat, w1, b1g, b1l, w2, b2, *, n_slots):
    d = D_MODEL
    nb = row_src.shape[0]
    bm = MOE_BLOCK
    de = w1.shape[-1]
    blk = lambda i, nv: jnp.minimum(i, nv[0] - 1)
    wmap = lambda i, be, nv: (be[blk(i, nv)], 0, 0)
    return pl.pallas_call(
        functools.partial(_moe_kernel, nb=nb, n_slots=n_slots),
        grid_spec=pltpu.PrefetchScalarGridSpec(
            num_scalar_prefetch=2,
            grid=(nb + 1,),
            in_specs=[
                pl.BlockSpec((1, 1, bm), lambda i, be, nv: (blk(i, nv), 0, 0)),
                pl.BlockSpec((1, 1, d, de), lambda i, be, nv: (be[blk(i, nv)], 0, 0, 0)),
                pl.BlockSpec((1, 1, d, de), lambda i, be, nv: (be[blk(i, nv)], 1, 0, 0)),
                pl.BlockSpec((1, 1, de), wmap),
                pl.BlockSpec((1, 1, de), wmap),
                pl.BlockSpec((1, de, d), wmap),
                pl.BlockSpec((1, 1, d), wmap),
                pl.BlockSpec(memory_space=pl.ANY),
                pl.BlockSpec(memory_space=pl.ANY),
                pl.BlockSpec(memory_space=pl.ANY),
            ],
            out_specs=pl.BlockSpec(memory_space=pl.ANY),
            scratch_shapes=[
                pltpu.SMEM((4 * bm,), jnp.int32),
                pltpu.SMEM((4 * bm,), jnp.int32),
                pltpu.VMEM((2, bm * TOK_ROWS, 128), F32),
                pltpu.VMEM((2, bm * TOK_ROWS, 128), F32),
                pltpu.SemaphoreType.DMA,
                pltpu.SemaphoreType.DMA((2,)),
                pltpu.SemaphoreType.DMA,
            ],
        ),
        out_shape=jax.ShapeDtypeStruct(((n_slots + 2 * bm) * TOK_ROWS, 128), F32),
        compiler_params=_cparams(("arbitrary",)),
        name="moe_experts",
    )(block_exp, n_valid, row_gate, w1, w1, b1g, b1l, w2, b2, row_src, row_dst, h2_flat)


def _combine_kernel(x_ref, y4_ref, modl_ref, modc_ref, fg_ref, o_ref, *, tm, ctx_len, final, skip):
    i = pl.program_id(1) + skip
    d = D_MODEL
    ld = lambda k, cc: y4_ref[pl.ds(k * TOK_ROWS + cc, tm, stride=TOP_K * TOK_ROWS), :]
    f = jnp.concatenate([(ld(0, cc) + ld(1, cc)) + (ld(2, cc) + ld(3, cc)) for cc in range(TOK_ROWS)],
                        axis=1)
    pos = i * tm + lax.broadcasted_iota(jnp.int32, (tm, 1), 0)
    g2 = jnp.where(pos < ctx_len, modc_ref[5:6, :], modl_ref[0, 5:6, :])
    x_new = x_ref[0] + g2 * f
    if final:
        x_new = x_new * lax.rsqrt(jnp.mean(x_new * x_new, axis=-1, keepdims=True) + RMS_EPS) * fg_ref[...]
    o_ref[0] = x_new


def _combine(x, y4, modl, modc, fg, *, ctx_len, final):
    b, t, d = x.shape
    if final:
        tm = Q_TILE
        assert ctx_len % tm == 0
        skip = ctx_len // tm
    else:
        tm = _row_tile(t, 272)
        skip = 0
    npb = t // tm
    return pl.pallas_call(
        functools.partial(_combine_kernel, tm=tm, ctx_len=ctx_len, final=final, skip=skip),
        grid=(b, npb - skip),
        in_specs=[
            pl.BlockSpec((1, tm, d), lambda bi, i: (bi, i + skip, 0)),
            pl.BlockSpec((tm * TOP_K * TOK_ROWS, 128), lambda bi, i: (bi * (npb - skip) + i, 0)),
            pl.BlockSpec((1, 8, d), lambda bi, i: (bi, 0, 0)),
            pl.BlockSpec((8, d), lambda bi, i: (0, 0)),
            pl.BlockSpec((1, d), lambda bi, i: (0, 0)),
        ],
        out_specs=pl.BlockSpec((1, tm, d), lambda bi, i: (bi, i, 0)),
        out_shape=jax.ShapeDtypeStruct((b, t - skip * tm, d), F32),
        compiler_params=_cparams(("parallel", "parallel")),
        name="moe_combine",
    )(x, y4, modl, modc, fg)


def _routing(e_flat, gate_flat, *, tm, seq, ctx_len, latent_only):
    npair = e_flat.shape[0]
    t = seq + ctx_len
    bm = MOE_BLOCK
    nb = npair // bm + N_EXPERTS
    nr = nb * bm
    npad = nr - npair
    experts = jnp.arange(N_EXPERTS, dtype=jnp.int32)
    ids = jnp.arange(npair, dtype=jnp.int32)
    tok_of = lambda i: (i // (TOP_K * tm)) * tm + i % tm
    if latent_only:
        e_flat = jnp.where(tok_of(ids) % t >= ctx_len, e_flat, N_EXPERTS)
    counts = jnp.sum((e_flat[:, None] == experts[None, :]).astype(jnp.int32), axis=0)
    padded = (counts + bm - 1) // bm * bm
    pad_end = jnp.cumsum(padded)
    fill_end = jnp.cumsum(padded - counts)
    dummy = jnp.arange(npad, dtype=jnp.int32)
    e_dummy = jnp.sum((dummy[:, None] >= fill_end[None, :]).astype(jnp.int32), axis=1)
    kb = nr
    keys = jnp.concatenate([e_flat * kb + ids, e_dummy * kb + npair + dummy])
    gates = jnp.concatenate([gate_flat, jnp.zeros((npad,), F32)])
    skey, sgate = lax.sort((keys, gates), num_keys=1)
    e_row = skey // kb
    ident = skey - e_row * kb
    valid = (ident < npair) & (e_row < N_EXPERTS)
    pair = jnp.where(valid, ident, 0)
    tok = tok_of(pair)
    kk = (pair // tm) % TOP_K
    row_gate = jnp.where(valid, sgate, 0.0).reshape(nb, 1, bm)
    row_src = (tok * TOK_ROWS).reshape(nb, 1, bm)
    if latent_only:
        slot = (tok // t) * seq + tok % t - ctx_len
        n_slots = (npair // TOP_K // t) * seq * TOP_K
    else:
        slot = tok
        n_slots = npair
    j = jnp.arange(nr, dtype=jnp.int32)
    dump = n_slots + ((j // bm) % 2) * bm + j % bm
    row_dst = jnp.where(valid, slot * TOP_K + kk, dump).reshape(nb, bm)
    row_dst = jnp.concatenate([row_dst, (n_slots + bm + jnp.arange(bm, dtype=jnp.int32))[None, :]],
                              axis=0)[:, None, :] * TOK_ROWS
    blocks = jnp.arange(nb, dtype=jnp.int32) * bm
    block_exp = jnp.minimum(jnp.sum((blocks[:, None] >= pad_end[None, :]).astype(jnp.int32), axis=1),
                            N_EXPERTS - 1)
    n_valid = (pad_end[-1] // bm).astype(jnp.int32).reshape(1)
    return row_src, row_dst, row_gate, block_exp, n_valid, n_slots


def _rope_tables(ctx_len, seq_len):
    inv_freq = ROPE_BASE ** (-jnp.arange(0, ROT_AXIS_DIM, 2, dtype=F32) / ROT_AXIS_DIM)
    p = jnp.arange(seq_len, dtype=jnp.int32)
    row = (p // GRID_W).astype(F32)
    col = (p % GRID_W).astype(F32)
    lane = np.arange(128)
    dd = lane % DIFF_HEAD_DIM
    fidx = jnp.asarray((dd % ROT_AXIS_DIM) % (ROT_AXIS_DIM // 2))
    use_col = jnp.asarray(dd >= ROT_AXIS_DIM)
    second_half = jnp.asarray((dd % ROT_AXIS_DIM) >= ROT_AXIS_DIM // 2)
    posm = jnp.where(use_col[None, :], col[:, None], row[:, None])
    ang = posm * inv_freq[fidx][None, :]
    cos = jnp.cos(ang)
    sin = jnp.sin(ang)
    sa = jnp.where(second_half[None, :], sin, 0.0)
    sb = jnp.where(second_half[None, :], 0.0, -sin)
    pad = lambda a, v: jnp.concatenate([jnp.full((ctx_len, 128), v, F32), a], axis=0)
    return pad(cos, 1.0), pad(sa, 0.0), pad(sb, 0.0)


def kernel(x, c, ctx, c_ctx, ada_w, ada_b, norm1_g, norm2_g, w_in, diff_lambda, diff_subln_g,
           diff_w_out, conv_w, conv_w_out, gla_w_a2, gla_b_a, gla_norm_g, gla_w_out, w_o,
           router_w, router_b, moe_w1, moe_b1, moe_w2, moe_b2, final_norm_g):
    b, s, d = x.shape
    cl = ctx.shape[1]
    t = cl + s
    nl = w_in.shape[0]
    assert d == D_MODEL and b <= 15

    xs = jnp.concatenate([ctx, x], axis=1)
    cc = jnp.concatenate([c, c_ctx[None, :], jnp.zeros((15 - b, d), F32)], axis=0)
    mod = _ada(cc, ada_w, ada_b).reshape(nl, 16, N_ADA, d)
    cos, sa, sb = _rope_tables(cl, s)

    sp = np.cumsum([1024, 1024, 1024, 512, 512, 512, 256, 256, 512, 512, 32, 3072])
    (w_q, w_k, w_v, w_cb, w_cc, w_cx, w_gq, w_gk, w_gv, w_gr, w_ga, w_gt) = jnp.split(
        w_in, [int(v) for v in sp[:-1]], axis=-1)
    w_all = jnp.concatenate(
        [w_q * (DIFF_HEAD_DIM ** -0.5), w_k, w_v, w_gv,
         w_gt, w_cb, w_cc, w_cx, w_gq * (GLA_DK ** -0.5), w_gk, w_gr, w_ga,
         jnp.zeros((nl, d, A32_W - A32_GLA - 2 * GLA_RANK), F32)], axis=-1).astype(BF16)

    wa = jnp.zeros((nl, 2, 128, 256), F32)
    for p in range(2):
        wa = wa.at[:, p, 0:GLA_RANK, 0:128].set(gla_w_a2[:, 0, :, p * 128:(p + 1) * 128])
        wa = wa.at[:, p, GLA_RANK:2 * GLA_RANK, 128:256].set(gla_w_a2[:, 1, :, p * 128:(p + 1) * 128])
    wa = wa.astype(BF16)
    ba = jnp.stack([jnp.concatenate([gla_b_a[:, 0, p * 128:(p + 1) * 128],
                                     gla_b_a[:, 1, p * 128:(p + 1) * 128]], axis=-1)
                    for p in range(2)], axis=1)[:, :, None, :]

    cw = jnp.concatenate([conv_w, jnp.zeros((nl, 5, CONV_W), F32)], axis=1)
    rw = jnp.concatenate([router_w, jnp.zeros((nl, d, 128 - N_EXPERTS), F32)], axis=-1).astype(BF16)
    rb = jnp.concatenate([router_b, jnp.full((nl, 128 - N_EXPERTS), -1e30, F32)], axis=-1)[:, None, :]
    w1 = jnp.moveaxis(moe_w1.reshape(nl, N_EXPERTS, d, D_EXPERT, 2), -1, 2).astype(BF16)
    b1 = moe_b1.reshape(nl, N_EXPERTS, 1, D_EXPERT, 2)
    b1g = b1[..., 0]
    b1l = b1[..., 1]
    w2 = moe_w2.astype(BF16)
    b2 = moe_b2[:, :, None, :]
    wdo = diff_w_out.astype(BF16)
    wco = conv_w_out.astype(BF16)
    wgo = gla_w_out.astype(BF16)
    wo = w_o.astype(BF16)

    for layer in range(nl):
        last = layer == nl - 1
        lam_init = 0.8 - 0.6 * math.exp(-0.3 * layer)
        ml = mod[layer]
        modl = jnp.concatenate([ml[:b], jnp.zeros((b, 2, d), F32)], axis=1)
        modc = jnp.concatenate([ml[b], jnp.zeros((2, d), F32)], axis=0)

        a16, a32 = _k1(xs, modl, modc, norm1_g[layer][None, :], cos, sa, sb, w_all[layer], ctx_len=cl)
        od = _attn(a16, diff_lambda[layer], diff_subln_g[layer][None, :],
                   ctx_len=cl, seq_len=s, lam_init=lam_init)
        og = _gla(a16, a32, wa[layer], ba[layer], gla_norm_g[layer][None, :], ctx_len=cl)
        xs, h2, rt = _merge(xs, od, og, a32, cw[layer], wdo[layer], wco[layer], wgo[layer],
                                   wo[layer], modl, modc, norm2_g[layer][None, :], rw[layer],
                                   rb[layer], ctx_len=cl)

        r = b * t
        tm_r = rt.shape[-1]
        rt = rt.reshape(-1, 4, TOP_K, tm_r)
        row_src, row_dst, row_gate, block_exp, n_valid, n_slots = _routing(
            rt[:, 0].astype(jnp.int32).reshape(-1), (rt[:, 1] + rt[:, 2] + rt[:, 3]).reshape(-1),
            tm=tm_r, seq=s, ctx_len=cl, latent_only=last)
        y4 = _moe(block_exp, n_valid, row_src, row_dst, row_gate, h2.reshape(r * TOK_ROWS, 128), w1[layer],
                  b1g[layer], b1l[layer], w2[layer], b2[layer], n_slots=n_slots)
        xs = _combine(xs, y4, modl, modc, final_norm_g[None, :], ctx_len=cl, final=last)

    return xs
```

```python
import functools
import math

import numpy as np
import jax
import jax.numpy as jnp
from jax import lax
from jax.experimental import pallas as pl
from jax.experimental.pallas import tpu as pltpu

F32 = jnp.float32
BF16 = jnp.bfloat16

D_MODEL = 1024
GRID_W = 64
RMS_EPS = 1e-6
N_ADA = 6
DIFF_HEADS = 8
DIFF_HEAD_DIM = 64
DIFF_V_DIM = 128
ROPE_BASE = 10000.0
ROT_AXIS_DIM = 32
CONV_W = 512
GLA_HEADS = 4
GLA_DK = 64
GLA_DV = 128
GLA_RANK = 16
GLA_TAU = 16.0
GLA_CHUNK = 64
N_EXPERTS = 32
TOP_K = 4
D_EXPERT = 1024
SWIGLU_LIMIT = 7.0
SWIGLU_ALPHA = 1.702
LOG2_E = 1.4426950408889634

A16_Q, A16_K, A16_V, A16_GV, A16_W = 0, 1024, 2048, 3072, 3584
A32_GATES, A32_CVB, A32_CVC, A32_CVX = 0, 3072, 3584, 4096
A32_GLQ, A32_GLK, A32_GLR, A32_GLA, A32_W = 4608, 4864, 5120, 5632, 5760
W_ALL_COLS = A16_W + A32_W

Q_TILE = 256
KV_SEGMENTS = 2
ATTN_HEADS = 4
MOE_BLOCK = 256
GLA_UNROLL = 2
TOK_ROWS = 8
VMEM_LIMIT = 56 * 1024 * 1024


def _cparams(sem):
    return pltpu.CompilerParams(dimension_semantics=sem, vmem_limit_bytes=VMEM_LIMIT)


def _row_tile(t, target):
    best = None
    for cand in range(16, target + 1, 16):
        if t % cand == 0:
            best = cand
    assert best is not None, t
    return best


def _sigmoid(v):
    return 1.0 / (1.0 + jnp.exp(-v))


def _ada_kernel(cc_ref, w_ref, b_ref, o_ref):
    a = cc_ref[...]
    a = a * _sigmoid(a)
    o_ref[0] = jnp.dot(a, w_ref[0], precision=lax.Precision.HIGHEST,
                       preferred_element_type=F32) + b_ref[0]


def _ada(cc, ada_w, ada_b):
    nl, d, n = ada_w.shape
    tn = 1024
    return pl.pallas_call(
        _ada_kernel,
        grid=(nl, n // tn),
        in_specs=[
            pl.BlockSpec((16, d), lambda l, j: (0, 0)),
            pl.BlockSpec((1, d, tn), lambda l, j: (l, 0, j)),
            pl.BlockSpec((1, 1, tn), lambda l, j: (l, 0, j)),
        ],
        out_specs=pl.BlockSpec((1, 16, tn), lambda l, j: (l, 0, j)),
        out_shape=jax.ShapeDtypeStruct((nl, 16, n), F32),
        compiler_params=_cparams(("parallel", "parallel")),
        name="ada_mod",
    )(cc, ada_w, ada_b.reshape(nl, 1, n))


def _norm_mod(x, ng, shift, scale):
    y = x * lax.rsqrt(jnp.mean(x * x, axis=-1, keepdims=True) + RMS_EPS) * ng
    return y * (1.0 + scale) + shift


def _k1_kernel(x_ref, modl_ref, modc_ref, ng_ref, cos_ref, sa_ref, sb_ref, w_ref,
               o16_ref, o32_ref, *, tm, ctx_len):
    i = pl.program_id(1)
    pos = i * tm + lax.broadcasted_iota(jnp.int32, (tm, 1), 0)
    is_ctx = pos < ctx_len
    shift = jnp.where(is_ctx, modc_ref[0:1, :], modl_ref[0, 0:1, :])
    scale = jnp.where(is_ctx, modc_ref[1:2, :], modl_ref[0, 1:2, :])
    h = _norm_mod(x_ref[0], ng_ref[...], shift, scale).astype(BF16)
    cos = cos_ref[...]
    sa = sa_ref[...]
    sb = sb_ref[...]
    ch = 512
    for c0 in range(0, A16_W, ch):
        acc = jnp.dot(h, w_ref[:, c0:c0 + ch], preferred_element_type=F32)
        if c0 < A16_V:
            parts = []
            for j in range(ch // 128):
                a = acc[:, j * 128:(j + 1) * 128]
                parts.append(a * cos + pltpu.roll(a, 16, 1) * sa + pltpu.roll(a, 112, 1) * sb)
            acc = jnp.concatenate(parts, axis=1)
        if c0 < A16_K:
            acc = acc * LOG2_E
        o16_ref[0, :, c0:c0 + ch] = acc.astype(BF16)
    for c0 in range(0, A32_W, ch):
        c1 = min(c0 + ch, A32_W)
        o32_ref[0, :, c0:c1] = jnp.dot(h, w_ref[:, A16_W + c0:A16_W + c1],
                                       preferred_element_type=F32)


def _k1(x, modl, modc, ng, cos, sa, sb, w_all, *, ctx_len):
    b, t, d = x.shape
    tm = _row_tile(t, 272)
    tbl = pl.BlockSpec((tm, 128), lambda bi, i: (i, 0))
    return pl.pallas_call(
        functools.partial(_k1_kernel, tm=tm, ctx_len=ctx_len),
        grid=(b, t // tm),
        in_specs=[
            pl.BlockSpec((1, tm, d), lambda bi, i: (bi, i, 0)),
            pl.BlockSpec((1, 8, d), lambda bi, i: (bi, 0, 0)),
            pl.BlockSpec((8, d), lambda bi, i: (0, 0)),
            pl.BlockSpec((1, d), lambda bi, i: (0, 0)),
            tbl, tbl, tbl,
            pl.BlockSpec((d, W_ALL_COLS), lambda bi, i: (0, 0), pipeline_mode=pl.Buffered(1)),
        ],
        out_specs=[
            pl.BlockSpec((1, tm, A16_W), lambda bi, i: (bi, i, 0)),
            pl.BlockSpec((1, tm, A32_W), lambda bi, i: (bi, i, 0)),
        ],
        out_shape=[
            jax.ShapeDtypeStruct((b, t, A16_W), BF16),
            jax.ShapeDtypeStruct((b, t, A32_W), F32),
        ],
        compiler_params=_cparams(("parallel", "parallel")),
        name="norm_inproj",
    )(x, modl, modc, ng, cos, sa, sb, w_all)


def _attn_kernel(lam_ref, g_ref, q_ref, k_ref, v_ref, o_ref, *, ctx_len, segments, lam_init):
    qi = pl.program_id(2)
    tq = Q_TILE
    hb = DIFF_V_DIM
    lv = lam_ref[...]
    lam = (jnp.exp(jnp.sum(lv[0:1] * lv[1:2], axis=1, keepdims=True))
           - jnp.exp(jnp.sum(lv[2:3] * lv[3:4], axis=1, keepdims=True)) + lam_init)

    def run(segs):
        qss = []
        for g in range(ATTN_HEADS):
            q = q_ref[0, :, g * hb:(g + 1) * hb]
            lane = lax.broadcasted_iota(jnp.int32, q.shape, 1)
            zero = jnp.zeros_like(q)
            qss.append(jnp.concatenate([jnp.where(lane < DIFF_HEAD_DIM, q, zero),
                                        jnp.where(lane >= DIFF_HEAD_DIM, q, zero)], axis=0))
        carries = [None] * ATTN_HEADS
        for r0, rows in segs:
            for g in range(ATTN_HEADS):
                k_t = k_ref[0, r0:r0 + rows, g * hb:(g + 1) * hb]
                v_t = v_ref[0, r0:r0 + rows, g * hb:(g + 1) * hb]
                s = lax.dot_general(qss[g], k_t, (((1,), (1,)), ((), ())), preferred_element_type=F32)
                smax = jnp.max(s, axis=-1, keepdims=True)
                if carries[g] is None:
                    p = jnp.exp2(s - smax)
                    carries[g] = (smax, jnp.sum(p, axis=-1, keepdims=True),
                                  jnp.dot(p.astype(BF16), v_t, preferred_element_type=F32))
                else:
                    m, l, acc = carries[g]
                    m_new = jnp.maximum(m, smax)
                    alpha = jnp.exp2(m - m_new)
                    p = jnp.exp2(s - m_new)
                    carries[g] = (m_new, alpha * l + jnp.sum(p, axis=-1, keepdims=True),
                                  alpha * acc + jnp.dot(p.astype(BF16), v_t, preferred_element_type=F32))
        for g in range(ATTN_HEADS):
            _, l, acc = carries[g]
            o = acc[:tq] / l[:tq] - lam * (acc[tq:] / l[tq:])
            o = o * lax.rsqrt(jnp.mean(o * o, axis=-1, keepdims=True) + RMS_EPS) * g_ref[...]
            o_ref[0, :, g * hb:(g + 1) * hb] = (o * (1.0 - lam_init)).astype(BF16)

    @pl.when(qi * tq >= ctx_len)
    def _():
        run(segments)

    @pl.when(qi * tq < ctx_len)
    def _():
        run([(0, ctx_len)])


def _attn(a16, lam_p, subln_g, *, ctx_len, seq_len, lam_init):
    b, t, _ = a16.shape
    assert ctx_len % Q_TILE == 0 and t % 256 == 0
    first = (t // 256 + KV_SEGMENTS - 1) // KV_SEGMENTS * 256
    bounds = [min(first * n, t) for n in range(KV_SEGMENTS + 1)]
    segments = [(lo, hi - lo) for lo, hi in zip(bounds[:-1], bounds[1:]) if hi > lo]
    hb = DIFF_V_DIM * ATTN_HEADS
    return pl.pallas_call(
        functools.partial(_attn_kernel, ctx_len=ctx_len, segments=segments, lam_init=lam_init),
        grid=(b, DIFF_HEADS // ATTN_HEADS, t // Q_TILE),
        in_specs=[
            pl.BlockSpec((4, DIFF_HEAD_DIM), lambda bi, h, qi: (0, 0)),
            pl.BlockSpec((1, DIFF_V_DIM), lambda bi, h, qi: (0, 0)),
            pl.BlockSpec((1, Q_TILE, hb), lambda bi, h, qi: (bi, qi, A16_Q // hb + h)),
            pl.BlockSpec((1, t, hb), lambda bi, h, qi: (bi, 0, A16_K // hb + h)),
            pl.BlockSpec((1, t, hb), lambda bi, h, qi: (bi, 0, A16_V // hb + h)),
        ],
        out_specs=pl.BlockSpec((1, Q_TILE, hb), lambda bi, h, qi: (bi, qi, h)),
        out_shape=jax.ShapeDtypeStruct((b, t, DIFF_HEADS * DIFF_V_DIM), BF16),
        compiler_params=_cparams(("parallel", "parallel", "parallel")),
        name="diff_attn",
    )(lam_p, subln_g, a16, a16, a16)


def _split3(x):
    hi = x.astype(BF16)
    r1 = x - hi.astype(F32)
    mid = r1.astype(BF16)
    lo = (r1 - mid.astype(F32)).astype(BF16)
    return jnp.concatenate([hi, mid, lo], axis=1)


def _sum3(y, w):
    return y[:, 0:w] + y[:, w:2 * w] + y[:, 2 * w:3 * w]


def _gla_kernel(q_ref, k_ref, v_ref, a_ref, r_ref, wa_ref, ba_ref, ng_ref, o_ref,
                qd_s, ki_s, ke_s, dec_s, oacc_s, st_s, *, t, ctx_len):
    ck = GLA_CHUNK
    nc = t // ck
    ncc = ctx_len // ck
    pr = 256
    w = 2 * GLA_DK

    ri = lax.broadcasted_iota(jnp.int32, (pr, pr), 0)
    ci = lax.broadcasted_iota(jnp.int32, (pr, pr), 1)
    same = (ri // ck) == (ci // ck)
    m_tot = jnp.where(same, 1.0, 0.0).astype(BF16)
    m_cum = (jnp.where(same & (ci <= ri), 1.0, 0.0).astype(BF16),
             jnp.where(same & (ci >= ri), 1.0, 0.0).astype(BF16))
    for r0 in range(0, t, pr):
        a = a_ref[0, r0:r0 + pr, :].astype(BF16)
        logit = jnp.dot(a, wa_ref[0], preferred_element_type=F32) + ba_ref[0]
        g = (jnp.minimum(logit, 0.0) - jnp.log1p(jnp.exp(-jnp.abs(logit)))) * (1.0 / GLA_TAU)
        q = q_ref[0, r0:r0 + pr, :]
        k = k_ref[0, r0:r0 + pr, :]
        for d in range(2):
            g3 = _split3(g[:, d * w:(d + 1) * w])
            g_cum = _sum3(jnp.dot(m_cum[d], g3, preferred_element_type=F32), w)
            g_tot = _sum3(jnp.dot(m_tot, g3, preferred_element_type=F32), w)
            qd_s[d, r0:r0 + pr, :] = (q * jnp.exp(g_cum)).astype(BF16)
            ki_s[d, r0:r0 + pr, :] = (k * jnp.exp(-g_cum)).astype(BF16)
            ke_s[d, r0:r0 + pr, :] = (k * jnp.exp(g_tot - g_cum)).astype(BF16)
            dec_s[d, r0:r0 + pr, :] = jnp.exp(g_tot)
        oacc_s[r0:r0 + pr, :] = jnp.zeros((pr, 2 * GLA_DV), F32)
    st_s[...] = jnp.zeros_like(st_s)

    ri = lax.broadcasted_iota(jnp.int32, (2 * ck, ck), 0)
    ci = lax.broadcasted_iota(jnp.int32, (2 * ck, ck), 1)
    rr = jnp.where(ri >= ck, ri - ck, ri)
    tri = (ci <= rr, ci >= rr)
    lane = lax.broadcasted_iota(jnp.int32, (ck, w), 1)
    br = lax.broadcasted_iota(jnp.int32, (2 * GLA_DV, w), 0)
    bc = lax.broadcasted_iota(jnp.int32, (2 * GLA_DV, w), 1)
    blk = (br < GLA_DV) == (bc < GLA_DK)
    nt = (((1,), (1,)), ((), ()))
    tn = (((0,), (0,)), ((), ()))

    def chunk(d, r0):
        qd = qd_s[d, pl.ds(r0, ck), :]
        ki = ki_s[d, pl.ds(r0, ck), :]
        ke = ke_s[d, pl.ds(r0, ck), :]
        v = v_ref[0, pl.ds(r0, ck), :]
        dec = dec_s[d, pl.ds(r0, 8), :][0:1, :]
        zq = jnp.zeros_like(qd)
        qs = jnp.concatenate([jnp.where(lane < GLA_DK, qd, zq), jnp.where(lane >= GLA_DK, qd, zq)], axis=0)
        att = lax.dot_general(qs, ki, nt, preferred_element_type=F32)
        att = jnp.where(tri[d], att, 0.0).astype(BF16)
        oi = jnp.dot(att, v, preferred_element_type=F32)
        o_intra = jnp.concatenate([oi[:ck, :GLA_DV], oi[ck:, GLA_DV:]], axis=1)
        st = st_s[d]
        o_inter = lax.dot_general(qd, st.astype(BF16), nt, preferred_element_type=F32)
        s_loc = lax.dot_general(v, ke, tn, preferred_element_type=F32)
        st_s[d] = st * dec + jnp.where(blk, s_loc, 0.0)
        oacc_s[pl.ds(r0, ck), :] = oacc_s[pl.ds(r0, ck), :] + o_intra + o_inter

    def body(n, _):
        chunk(0, pl.multiple_of(n * ck, ck))
        nb = jnp.where(n < ncc, ncc - 1 - n, nc - 1 - (n - ncc))
        chunk(1, pl.multiple_of(nb * ck, ck))
        return 0

    lax.fori_loop(0, nc, body, 0, unroll=GLA_UNROLL)

    ng = ng_ref[...]
    rt = _row_tile(t, 544)
    for r0 in range(0, t, rt):
        o = oacc_s[r0:r0 + rt, :]
        r = r_ref[0, r0:r0 + rt, :]
        outs = []
        for hh in range(2):
            oh = o[:, hh * GLA_DV:(hh + 1) * GLA_DV]
            outs.append(oh * lax.rsqrt(jnp.mean(oh * oh, axis=-1, keepdims=True) + RMS_EPS) * ng)
        o_ref[0, r0:r0 + rt, :] = (jnp.concatenate(outs, axis=1) * (r * _sigmoid(r))).astype(BF16)


def _gla(a16, a32, wa, ba, ng, *, ctx_len):
    b, t, _ = a16.shape
    assert t % 256 == 0 and ctx_len % GLA_CHUNK == 0 and (t // GLA_CHUNK) % GLA_UNROLL == 0
    return pl.pallas_call(
        functools.partial(_gla_kernel, t=t, ctx_len=ctx_len),
        grid=(b, 2),
        in_specs=[
            pl.BlockSpec((1, t, 128), lambda bi, p: (bi, 0, A32_GLQ // 128 + p)),
            pl.BlockSpec((1, t, 128), lambda bi, p: (bi, 0, A32_GLK // 128 + p)),
            pl.BlockSpec((1, t, 256), lambda bi, p: (bi, 0, A16_GV // 256 + p)),
            pl.BlockSpec((1, t, 128), lambda bi, p: (bi, 0, A32_GLA // 128)),
            pl.BlockSpec((1, t, 256), lambda bi, p: (bi, 0, A32_GLR // 256 + p)),
            pl.BlockSpec((1, 128, 256), lambda bi, p: (p, 0, 0)),
            pl.BlockSpec((1, 1, 256), lambda bi, p: (p, 0, 0)),
            pl.BlockSpec((1, GLA_DV), lambda bi, p: (0, 0)),
        ],
        out_specs=pl.BlockSpec((1, t, 256), lambda bi, p: (bi, 0, p)),
        out_shape=jax.ShapeDtypeStruct((b, t, GLA_HEADS * GLA_DV), BF16),
        scratch_shapes=[
            pltpu.VMEM((2, t, 128), BF16), pltpu.VMEM((2, t, 128), BF16), pltpu.VMEM((2, t, 128), BF16),
            pltpu.VMEM((2, t, 128), F32),
            pltpu.VMEM((t, 256), F32),
            pltpu.VMEM((2, 256, 128), F32),
        ],
        compiler_params=_cparams(("parallel", "parallel")),
        name="gla_bidir",
    )(a32, a32, a16, a32, a32, wa, ba, ng)


def _merge_kernel(x_ref, od_ref, og_ref, gates_ref, cvb_ref, cvc_ref, cvx_ref,
                  cvc_p_ref, cvx_p_ref, cvc_n_ref, cvx_n_ref, cw_ref,
                  wdo_ref, wco_ref, wgo_ref, wo_ref, modl_ref, modc_ref, n2g_ref,
                  rw_ref, rb_ref, xo_ref, h2_ref, rt_ref, *, tm, ctx_len, t):
    i = pl.program_id(1)
    pos = i * tm + lax.broadcasted_iota(jnp.int32, (tm, 1), 0)
    is_ctx = pos < ctx_len
    row = lax.broadcasted_iota(jnp.int32, (tm, 1), 0)

    z = cvc_ref[0] * cvx_ref[0]
    z_before = cvc_p_ref[0, 7:8, :] * cvx_p_ref[0, 7:8, :]
    z_after = cvc_n_ref[0, 0:1, :] * cvx_n_ref[0, 0:1, :]
    z_prev = jnp.where(row == 0, z_before, pltpu.roll(z, 1, 0))
    z_prev = jnp.where((pos == 0) | (pos == ctx_len), 0.0, z_prev)
    z_next = jnp.where(row == tm - 1, z_after, pltpu.roll(z, tm - 1, 0))
    z_next = jnp.where((pos == ctx_len - 1) | (pos == t - 1), 0.0, z_next)
    conv = z_prev * cw_ref[0:1, :] + z * cw_ref[1:2, :] + z_next * cw_ref[2:3, :]
    zc = (cvb_ref[0] * conv).astype(BF16)

    y_diff = jnp.dot(od_ref[0], wdo_ref[...], preferred_element_type=F32)
    y_conv = jnp.dot(zc, wco_ref[...], preferred_element_type=F32)
    y_gla = jnp.dot(og_ref[0], wgo_ref[...], preferred_element_type=F32)
    d = D_MODEL
    mix = (_sigmoid(gates_ref[0, :, 0:d]) * y_diff
           + _sigmoid(gates_ref[0, :, d:2 * d]) * y_conv
           + _sigmoid(gates_ref[0, :, 2 * d:3 * d]) * y_gla)
    m = jnp.dot(mix.astype(BF16), wo_ref[...], preferred_element_type=F32)

    g1 = jnp.where(is_ctx, modc_ref[2:3, :], modl_ref[0, 2:3, :])
    x_new = x_ref[0] + g1 * m
    xo_ref[0] = x_new
    sh2 = jnp.where(is_ctx, modc_ref[3:4, :], modl_ref[0, 3:4, :])
    sc2 = jnp.where(is_ctx, modc_ref[4:5, :], modl_ref[0, 4:5, :])
    h2 = _norm_mod(x_new, n2g_ref[...], sh2, sc2)
    for cc in range(d // 128):
        h2_ref[0, pl.ds(cc, tm, stride=8), :] = h2[:, cc * 128:(cc + 1) * 128]

    logits = jnp.dot(h2.astype(BF16), rw_ref[...], preferred_element_type=F32) + rb_ref[...]
    lane = lax.broadcasted_iota(jnp.int32, logits.shape, 1)
    vals = logits
    tops = []
    idx_out = jnp.zeros(logits.shape, jnp.int32)
    for kk in range(TOP_K):
        mx = jnp.max(vals, axis=-1, keepdims=True)
        ix = jnp.min(jnp.where(vals == mx, lane, 128), axis=-1, keepdims=True)
        tops.append(mx)
        idx_out = jnp.where(lane == kk, ix, idx_out)
        vals = jnp.where(lane == ix, -jnp.inf, vals)
    es = [jnp.exp(v - tops[0]) for v in tops]
    den = es[0] + es[1] + es[2] + es[3]
    packed = idx_out.astype(F32)
    for kk in range(TOP_K):
        gk = es[kk] / den
        hi = gk.astype(BF16).astype(F32)
        mid = (gk - hi).astype(BF16).astype(F32)
        lo = gk - hi - mid
        for piece, val in enumerate((hi, mid, lo)):
            packed = jnp.where(lane == TOP_K * (piece + 1) + kk, val, packed)
    ri = lax.broadcasted_iota(jnp.int32, (tm, tm), 0)
    ci = lax.broadcasted_iota(jnp.int32, (tm, tm), 1)
    eye = jnp.where(ri == ci, 1.0, 0.0).astype(BF16)
    rows = lax.dot_general(packed.astype(BF16), eye, (((0,), (0,)), ((), ())), preferred_element_type=F32)
    rt_ref[0, 0] = rows[0:4 * TOP_K, :]


def _merge(x, od, og, a32, cw, wdo, wco, wgo, wo, modl, modc, n2g, rw, rb, *, ctx_len):
    b, t, d = x.shape
    tm = _row_tile(t, 272)
    nt8 = t // 8
    row = lambda bi, i: (bi, i, 0)
    const2 = lambda bi, i: (0, 0)
    return pl.pallas_call(
        functools.partial(_merge_kernel, tm=tm, ctx_len=ctx_len, t=t),
        grid=(b, t // tm),
        in_specs=[
            pl.BlockSpec((1, tm, d), row),
            pl.BlockSpec((1, tm, d), row),
            pl.BlockSpec((1, tm, GLA_HEADS * GLA_DV), row),
            pl.BlockSpec((1, tm, 3 * d), lambda bi, i: (bi, i, 0)),
            pl.BlockSpec((1, tm, CONV_W), lambda bi, i: (bi, i, A32_CVB // CONV_W)),
            pl.BlockSpec((1, tm, CONV_W), lambda bi, i: (bi, i, A32_CVC // CONV_W)),
            pl.BlockSpec((1, tm, CONV_W), lambda bi, i: (bi, i, A32_CVX // CONV_W)),
            pl.BlockSpec((1, 8, CONV_W), lambda bi, i: (bi, jnp.maximum(i * (tm // 8) - 1, 0),
                                                   A32_CVC // CONV_W)),
            pl.BlockSpec((1, 8, CONV_W), lambda bi, i: (bi, jnp.maximum(i * (tm // 8) - 1, 0),
                                                   A32_CVX // CONV_W)),
            pl.BlockSpec((1, 8, CONV_W), lambda bi, i: (bi, jnp.minimum((i + 1) * (tm // 8), nt8 - 1),
                                                   A32_CVC // CONV_W)),
            pl.BlockSpec((1, 8, CONV_W), lambda bi, i: (bi, jnp.minimum((i + 1) * (tm // 8), nt8 - 1),
                                                   A32_CVX // CONV_W)),
            pl.BlockSpec((8, CONV_W), const2),
            pl.BlockSpec((d, d), const2),
            pl.BlockSpec((CONV_W, d), const2),
            pl.BlockSpec((GLA_HEADS * GLA_DV, d), const2),
            pl.BlockSpec((d, d), const2),
            pl.BlockSpec((1, 8, d), lambda bi, i: (bi, 0, 0)),
            pl.BlockSpec((8, d), const2),
            pl.BlockSpec((1, d), const2),
            pl.BlockSpec((d, 128), const2),
            pl.BlockSpec((1, 128), const2),
        ],
        out_specs=[
            pl.BlockSpec((1, tm, d), row),
            pl.BlockSpec((1, tm * (d // 128), 128), row),
            pl.BlockSpec((1, 1, 4 * TOP_K, tm), lambda bi, i: (bi, i, 0, 0)),
        ],
        out_shape=[
            jax.ShapeDtypeStruct((b, t, d), F32),
            jax.ShapeDtypeStruct((b, t * (d // 128), 128), F32),
            jax.ShapeDtypeStruct((b, t // tm, 4 * TOP_K, tm), F32),
        ],
        compiler_params=_cparams(("parallel", "parallel")),
        name="merge_router",
    )(x, od, og, a32, a32, a32, a32, a32, a32, a32, a32, cw, wdo, wco, wgo, wo,
      modl, modc, n2g, rw, rb)


def _moe_kernel(be_ref, nv_ref, g_ref, w1g_ref, w1l_ref, b1g_ref, b1l_ref, w2_ref, b2_ref,
                src_hbm, dst_hbm, h2_hbm, out_hbm, src_s, dst_s, xbuf, ybuf, sem_p, sem_g, sem_s,
                *, nb, n_slots):
    i = pl.program_id(0)
    nv = nv_ref[0]
    bm = MOE_BLOCK
    cur = lax.rem(i, 2)
    nxt = 1 - cur

    def idx_copies(blk_src, blk_dst, slot):
        return (pltpu.make_async_copy(src_hbm.at[blk_src, 0], src_s.at[pl.ds(slot * bm, bm)], sem_p),
                pltpu.make_async_copy(dst_hbm.at[blk_dst, 0], dst_s.at[pl.ds(slot * bm, bm)], sem_p))

    def start_gather(slot, buf):
        for j in range(bm):
            r0 = pl.multiple_of(src_s[slot * bm + j], TOK_ROWS)
            pltpu.make_async_copy(h2_hbm.at[pl.ds(r0, TOK_ROWS)],
                                  xbuf.at[buf, pl.ds(j * TOK_ROWS, TOK_ROWS)], sem_g.at[buf]).start()

    def wait_gather(buf):
        pltpu.make_async_copy(h2_hbm.at[pl.ds(0, bm * TOK_ROWS)], xbuf.at[buf], sem_g.at[buf]).wait()

    def wait_scatter():
        pltpu.make_async_copy(ybuf.at[0], out_hbm.at[pl.ds(0, bm * TOK_ROWS)], sem_s).wait()

    @pl.when(i <= nv)
    def _():
        @pl.when(i == 0)
        def _():
            ybuf[...] = jnp.zeros_like(ybuf)
            for blk_src, blk_dst, slot in ((0, 0, 0), (1, 1, 1), (0, nb, 3)):
                for cp in idx_copies(blk_src, blk_dst, slot):
                    cp.start()
                    cp.wait()
            start_gather(0, 0)

        @pl.when(i >= 1)
        def _():
            for cp in idx_copies(0, 0, 0):
                cp.wait()
            wait_scatter()

        wait_gather(cur)

        start_gather(lax.rem(i + 1, 4), nxt)
        pslot = lax.rem(i + 3, 4)
        for j in range(bm):
            r0 = pl.multiple_of(dst_s[pslot * bm + j], TOK_ROWS)
            pltpu.make_async_copy(ybuf.at[nxt, pl.ds(j * TOK_ROWS, TOK_ROWS)],
                                  out_hbm.at[pl.ds(r0, TOK_ROWS)], sem_s).start()
        x = jnp.concatenate([xbuf[cur, pl.ds(cc, bm, stride=TOK_ROWS), :] for cc in range(TOK_ROWS)],
                            axis=1).astype(BF16)
        hg = jnp.dot(x, w1g_ref[0, 0], preferred_element_type=F32) + b1g_ref[0]
        hl = jnp.dot(x, w1l_ref[0, 0], preferred_element_type=F32) + b1l_ref[0]
        hg = jnp.minimum(hg, SWIGLU_LIMIT)
        hl = jnp.clip(hl, -SWIGLU_LIMIT, SWIGLU_LIMIT)
        act = hg * _sigmoid(SWIGLU_ALPHA * hg) * (hl + 1.0)
        y = jnp.dot(act.astype(BF16), w2_ref[0], preferred_element_type=F32) + b2_ref[0]
        gi = lax.broadcasted_iota(jnp.int32, (bm, bm), 0)
        gj = lax.broadcasted_iota(jnp.int32, (bm, bm), 1)
        y = y * jnp.sum(jnp.where(gi == gj, g_ref[0], 0.0), axis=1, keepdims=True)
        for cc in range(TOK_ROWS):
            ybuf[cur, pl.ds(cc, bm, stride=TOK_ROWS), :] = y[:, cc * 128:(cc + 1) * 128]

        @pl.when(i < nv)
        def _():
            blk = jnp.minimum(i + 2, nv - 1)
            for cp in idx_copies(blk, blk, lax.rem(i + 2, 4)):
                cp.start()

        @pl.when(i == nv)
        def _():
            wait_scatter()
            wait_gather(nxt)
            ybuf[...] = jnp.zeros_like(ybuf)
            fills = [pltpu.make_async_copy(
                ybuf.at[sl], out_hbm.at[pl.ds((n_slots + sl * bm) * TOK_ROWS, bm * TOK_ROWS)], sem_s)
                for sl in range(2)]
            for cp in fills:
                cp.start()
            for cp in fills:
                cp.wait()


def _moe(block_exp, n_valid, row_src, row_dst, row_gate, h2_flat, w1, b1g, b1l, w2, b2, *, n_slots):
    d = D_MODEL
    nb = row_src.shape[0]
    bm = MOE_BLOCK
    de = w1.shape[-1]
    blk = lambda i, nv: jnp.minimum(i, nv[0] - 1)
    wmap = lambda i, be, nv: (be[blk(i, nv)], 0, 0)
    return pl.pallas_call(
        functools.partial(_moe_kernel, nb=nb, n_slots=n_slots),
        grid_spec=pltpu.PrefetchScalarGridSpec(
            num_scalar_prefetch=2,
            grid=(nb + 1,),
            in_specs=[
                pl.BlockSpec((1, 1, bm), lambda i, be, nv: (blk(i, nv), 0, 0)),
                pl.BlockSpec((1, 1, d, de), lambda i, be, nv: (be[blk(i, nv)], 0, 0, 0)),
                pl.BlockSpec((1, 1, d, de), lambda i, be, nv: (be[blk(i, nv)], 1, 0, 0)),
                pl.BlockSpec((1, 1, de), wmap),
                pl.BlockSpec((1, 1, de), wmap),
                pl.BlockSpec((1, de, d), wmap),
                pl.BlockSpec((1, 1, d), wmap),
                pl.BlockSpec(memory_space=pl.ANY),
                pl.BlockSpec(memory_space=pl.ANY),
                pl.BlockSpec(memory_space=pl.ANY),
            ],
            out_specs=pl.BlockSpec(memory_space=pl.ANY),
            scratch_shapes=[
                pltpu.SMEM((4 * bm,), jnp.int32),
                pltpu.SMEM((4 * bm,), jnp.int32),
                pltpu.VMEM((2, bm * TOK_ROWS, 128), F32),
                pltpu.VMEM((2, bm * TOK_ROWS, 128), F32),
                pltpu.SemaphoreType.DMA,
                pltpu.SemaphoreType.DMA((2,)),
                pltpu.SemaphoreType.DMA,
            ],
        ),
        out_shape=jax.ShapeDtypeStruct(((n_slots + 2 * bm) * TOK_ROWS, 128), F32),
        compiler_params=_cparams(("arbitrary",)),
        name="moe_experts",
    )(block_exp, n_valid, row_gate, w1, w1, b1g, b1l, w2, b2, row_src, row_dst, h2_flat)


def _combine_kernel(x_ref, y4_ref, modl_ref, modc_ref, fg_ref, o_ref, *, tm, ctx_len, final, skip):
    i = pl.program_id(1) + skip
    d = D_MODEL
    ld = lambda k, cc: y4_ref[pl.ds(k * TOK_ROWS + cc, tm, stride=TOP_K * TOK_ROWS), :]
    f = jnp.concatenate([(ld(0, cc) + ld(1, cc)) + (ld(2, cc) + ld(3, cc)) for cc in range(TOK_ROWS)],
                        axis=1)
    pos = i * tm + lax.broadcasted_iota(jnp.int32, (tm, 1), 0)
    g2 = jnp.where(pos < ctx_len, modc_ref[5:6, :], modl_ref[0, 5:6, :])
    x_new = x_ref[0] + g2 * f
    if final:
        x_new = x_new * lax.rsqrt(jnp.mean(x_new * x_new, axis=-1, keepdims=True) + RMS_EPS) * fg_ref[...]
    o_ref[0] = x_new


def _combine(x, y4, modl, modc, fg, *, ctx_len, final):
    b, t, d = x.shape
    if final:
        tm = Q_TILE
        assert ctx_len % tm == 0
        skip = ctx_len // tm
    else:
        tm = _row_tile(t, 272)
        skip = 0
    npb = t // tm
    return pl.pallas_call(
        functools.partial(_combine_kernel, tm=tm, ctx_len=ctx_len, final=final, skip=skip),
        grid=(b, npb - skip),
        in_specs=[
            pl.BlockSpec((1, tm, d), lambda bi, i: (bi, i + skip, 0)),
            pl.BlockSpec((tm * TOP_K * TOK_ROWS, 128), lambda bi, i: (bi * (npb - skip) + i, 0)),
            pl.BlockSpec((1, 8, d), lambda bi, i: (bi, 0, 0)),
            pl.BlockSpec((8, d), lambda bi, i: (0, 0)),
            pl.BlockSpec((1, d), lambda bi, i: (0, 0)),
        ],
        out_specs=pl.BlockSpec((1, tm, d), lambda bi, i: (bi, i, 0)),
        out_shape=jax.ShapeDtypeStruct((b, t - skip * tm, d), F32),
        compiler_params=_cparams(("parallel", "parallel")),
        name="moe_combine",
    )(x, y4, modl, modc, fg)


def _routing(e_flat, gate_flat, *, tm, seq, ctx_len, latent_only):
    npair = e_flat.shape[0]
    t = seq + ctx_len
    bm = MOE_BLOCK
    nb = npair // bm + N_EXPERTS
    nr = nb * bm
    npad = nr - npair
    experts = jnp.arange(N_EXPERTS, dtype=jnp.int32)
    ids = jnp.arange(npair, dtype=jnp.int32)
    tok_of = lambda i: (i // (TOP_K * tm)) * tm + i % tm
    if latent_only:
        e_flat = jnp.where(tok_of(ids) % t >= ctx_len, e_flat, N_EXPERTS)
    counts = jnp.sum((e_flat[:, None] == experts[None, :]).astype(jnp.int32), axis=0)
    padded = (counts + bm - 1) // bm * bm
    pad_end = jnp.cumsum(padded)
    fill_end = jnp.cumsum(padded - counts)
    dummy = jnp.arange(npad, dtype=jnp.int32)
    e_dummy = jnp.sum((dummy[:, None] >= fill_end[None, :]).astype(jnp.int32), axis=1)
    kb = nr
    keys = jnp.concatenate([e_flat * kb + ids, e_dummy * kb + npair + dummy])
    gates = jnp.concatenate([gate_flat, jnp.zeros((npad,), F32)])
    skey, sgate = lax.sort((keys, gates), num_keys=1)
    e_row = skey // kb
    ident = skey - e_row * kb
    valid = (ident < npair) & (e_row < N_EXPERTS)
    pair = jnp.where(valid, ident, 0)
    tok = tok_of(pair)
    kk = (pair // tm) % TOP_K
    row_gate = jnp.where(valid, sgate, 0.0).reshape(nb, 1, bm)
    row_src = (tok * TOK_ROWS).reshape(nb, 1, bm)
    if latent_only:
        slot = (tok // t) * seq + tok % t - ctx_len
        n_slots = (npair // TOP_K // t) * seq * TOP_K
    else:
        slot = tok
        n_slots = npair
    j = jnp.arange(nr, dtype=jnp.int32)
    dump = n_slots + ((j // bm) % 2) * bm + j % bm
    row_dst = jnp.where(valid, slot * TOP_K + kk, dump).reshape(nb, bm)
    row_dst = jnp.concatenate([row_dst, (n_slots + bm + jnp.arange(bm, dtype=jnp.int32))[None, :]],
                              axis=0)[:, None, :] * TOK_ROWS
    blocks = jnp.arange(nb, dtype=jnp.int32) * bm
    block_exp = jnp.minimum(jnp.sum((blocks[:, None] >= pad_end[None, :]).astype(jnp.int32), axis=1),
                            N_EXPERTS - 1)
    n_valid = (pad_end[-1] // bm).astype(jnp.int32).reshape(1)
    return row_src, row_dst, row_gate, block_exp, n_valid, n_slots


def _rope_tables(ctx_len, seq_len):
    inv_freq = ROPE_BASE ** (-jnp.arange(0, ROT_AXIS_DIM, 2, dtype=F32) / ROT_AXIS_DIM)
    p = jnp.arange(seq_len, dtype=jnp.int32)
    row = (p // GRID_W).astype(F32)
    col = (p % GRID_W).astype(F32)
    lane = np.arange(128)
    dd = lane % DIFF_HEAD_DIM
    fidx = jnp.asarray((dd % ROT_AXIS_DIM) % (ROT_AXIS_DIM // 2))
    use_col = jnp.asarray(dd >= ROT_AXIS_DIM)
    second_half = jnp.asarray((dd % ROT_AXIS_DIM) >= ROT_AXIS_DIM // 2)
    posm = jnp.where(use_col[None, :], col[:, None], row[:, None])
    ang = posm * inv_freq[fidx][None, :]
    cos = jnp.cos(ang)
    sin = jnp.sin(ang)
    sa = jnp.where(second_half[None, :], sin, 0.0)
    sb = jnp.where(second_half[None, :], 0.0, -sin)
    pad = lambda a, v: jnp.concatenate([jnp.full((ctx_len, 128), v, F32), a], axis=0)
    return pad(cos, 1.0), pad(sa, 0.0), pad(sb, 0.0)


def kernel(x, c, ctx, c_ctx, ada_w, ada_b, norm1_g, norm2_g, w_in, diff_lambda, diff_subln_g,
           diff_w_out, conv_w, conv_w_out, gla_w_a2, gla_b_a, gla_norm_g, gla_w_out, w_o,
           router_w, router_b, moe_w1, moe_b1, moe_w2, moe_b2, final_norm_g):
    b, s, d = x.shape
    cl = ctx.shape[1]
    t = cl + s
    nl = w_in.shape[0]
    assert d == D_MODEL and b <= 15

    xs = jnp.concatenate([ctx, x], axis=1)
    cc = jnp.concatenate([c, c_ctx[None, :], jnp.zeros((15 - b, d), F32)], axis=0)
    mod = _ada(cc, ada_w, ada_b).reshape(nl, 16, N_ADA, d)
    cos, sa, sb = _rope_tables(cl, s)

    sp = np.cumsum([1024, 1024, 1024, 512, 512, 512, 256, 256, 512, 512, 32, 3072])
    (w_q, w_k, w_v, w_cb, w_cc, w_cx, w_gq, w_gk, w_gv, w_gr, w_ga, w_gt) = jnp.split(
        w_in, [int(v) for v in sp[:-1]], axis=-1)
    w_all = jnp.concatenate(
        [w_q * (DIFF_HEAD_DIM ** -0.5), w_k, w_v, w_gv,
         w_gt, w_cb, w_cc, w_cx, w_gq * (GLA_DK ** -0.5), w_gk, w_gr, w_ga,
         jnp.zeros((nl, d, A32_W - A32_GLA - 2 * GLA_RANK), F32)], axis=-1).astype(BF16)

    wa = jnp.zeros((nl, 2, 128, 256), F32)
    for p in range(2):
        wa = wa.at[:, p, 0:GLA_RANK, 0:128].set(gla_w_a2[:, 0, :, p * 128:(p + 1) * 128])
        wa = wa.at[:, p, GLA_RANK:2 * GLA_RANK, 128:256].set(gla_w_a2[:, 1, :, p * 128:(p + 1) * 128])
    wa = wa.astype(BF16)
    ba = jnp.stack([jnp.concatenate([gla_b_a[:, 0, p * 128:(p + 1) * 128],
                                     gla_b_a[:, 1, p * 128:(p + 1) * 128]], axis=-1)
                    for p in range(2)], axis=1)[:, :, None, :]

    cw = jnp.concatenate([conv_w, jnp.zeros((nl, 5, CONV_W), F32)], axis=1)
    rw = jnp.concatenate([router_w, jnp.zeros((nl, d, 128 - N_EXPERTS), F32)], axis=-1).astype(BF16)
    rb = jnp.concatenate([router_b, jnp.full((nl, 128 - N_EXPERTS), -1e30, F32)], axis=-1)[:, None, :]
    w1 = jnp.moveaxis(moe_w1.reshape(nl, N_EXPERTS, d, D_EXPERT, 2), -1, 2).astype(BF16)
    b1 = moe_b1.reshape(nl, N_EXPERTS, 1, D_EXPERT, 2)
    b1g = b1[..., 0]
    b1l = b1[..., 1]
    w2 = moe_w2.astype(BF16)
    b2 = moe_b2[:, :, None, :]
    wdo = diff_w_out.astype(BF16)
    wco = conv_w_out.astype(BF16)
    wgo = gla_w_out.astype(BF16)
    wo = w_o.astype(BF16)

    for layer in range(nl):
        last = layer == nl - 1
        lam_init = 0.8 - 0.6 * math.exp(-0.3 * layer)
        ml = mod[layer]
        modl = jnp.concatenate([ml[:b], jnp.zeros((b, 2, d), F32)], axis=1)
        modc = jnp.concatenate([ml[b], jnp.zeros((2, d), F32)], axis=0)

        a16, a32 = _k1(xs, modl, modc, norm1_g[layer][None, :], cos, sa, sb, w_all[layer], ctx_len=cl)
        od = _attn(a16, diff_lambda[layer], diff_subln_g[layer][None, :],
                   ctx_len=cl, seq_len=s, lam_init=lam_init)
        og = _gla(a16, a32, wa[layer], ba[layer], gla_norm_g[layer][None, :], ctx_len=cl)
        xs, h2, rt = _merge(xs, od, og, a32, cw[layer], wdo[layer], wco[layer], wgo[layer],
                                   wo[layer], modl, modc, norm2_g[layer][None, :], rw[layer],
                                   rb[layer], ctx_len=cl)

        r = b * t
        tm_r = rt.shape[-1]
        rt = rt.reshape(-1, 4, TOP_K, tm_r)
        row_src, row_dst, row_gate, block_exp, n_valid, n_slots = _routing(
            rt[:, 0].astype(jnp.int32).reshape(-1), (rt[:, 1] + rt[:, 2] + rt[:, 3]).reshape(-1),
            tm=tm_r, seq=s, ctx_len=cl, latent_only=last)
        y4 = _moe(block_exp, n_valid, row_src, row_dst, row_gate, h2.reshape(r * TOK_ROWS, 128), w1[layer],
                  b1g[layer], b1l[layer], w2[layer], b2[layer], n_slots=n_slots)
        xs = _combine(xs, y4, modl, modc, final_norm_g[None, :], ctx_len=cl, final=last)

    return xs
```

```python
import functools
import math

import numpy as np
import jax
import jax.numpy as jnp
from jax import lax
from jax.experimental import pallas as pl
from jax.experimental.pallas import tpu as pltpu

F32 = jnp.float32
BF16 = jnp.bfloat16

D_MODEL = 1024
GRID_W = 64
RMS_EPS = 1e-6
N_ADA = 6
DIFF_HEADS = 8
DIFF_HEAD_DIM = 64
DIFF_V_DIM = 128
ROPE_BASE = 10000.0
ROT_AXIS_DIM = 32
CONV_W = 512
GLA_HEADS = 4
GLA_DK = 64
GLA_DV = 128
GLA_RANK = 16
GLA_TAU = 16.0
GLA_CHUNK = 64
N_EXPERTS = 32
TOP_K = 4
D_EXPERT = 1024
SWIGLU_LIMIT = 7.0
SWIGLU_ALPHA = 1.702
LOG2_E = 1.4426950408889634

A16_Q, A16_K, A16_V, A16_GV, A16_W = 0, 1024, 2048, 3072, 3584
A32_GATES, A32_CVB, A32_CVC, A32_CVX = 0, 3072, 3584, 4096
A32_GLQ, A32_GLK, A32_GLR, A32_GLA, A32_W = 4608, 4864, 5120, 5632, 5760
W_ALL_COLS = A16_W + A32_W

Q_TILE = 256
KV_SEGMENTS = 2
ATTN_HEADS = 4
MOE_BLOCK = 256
GLA_UNROLL = 2
TOK_ROWS = 8
VMEM_LIMIT = 56 * 1024 * 1024


def _cparams(sem):
    return pltpu.CompilerParams(dimension_semantics=sem, vmem_limit_bytes=VMEM_LIMIT)


def _row_tile(t, target):
    best = None
    for cand in range(16, target + 1, 16):
        if t % cand == 0:
            best = cand
    assert best is not None, t
    return best


def _sigmoid(v):
    return 1.0 / (1.0 + jnp.exp(-v))


def _ada_kernel(cc_ref, w_ref, b_ref, o_ref):
    a = cc_ref[...]
    a = a * _sigmoid(a)
    o_ref[0] = jnp.dot(a, w_ref[0], precision=lax.Precision.HIGHEST,
                       preferred_element_type=F32) + b_ref[0]


def _ada(cc, ada_w, ada_b):
    nl, d, n = ada_w.shape
    tn = 1024
    return pl.pallas_call(
        _ada_kernel,
        grid=(nl, n // tn),
        in_specs=[
            pl.BlockSpec((16, d), lambda l, j: (0, 0)),
            pl.BlockSpec((1, d, tn), lambda l, j: (l, 0, j)),
            pl.BlockSpec((1, 1, tn), lambda l, j: (l, 0, j)),
        ],
        out_specs=pl.BlockSpec((1, 16, tn), lambda l, j: (l, 0, j)),
        out_shape=jax.ShapeDtypeStruct((nl, 16, n), F32),
        compiler_params=_cparams(("parallel", "parallel")),
        name="ada_mod",
    )(cc, ada_w, ada_b.reshape(nl, 1, n))


def _norm_mod(x, ng, shift, scale):
    y = x * lax.rsqrt(jnp.mean(x * x, axis=-1, keepdims=True) + RMS_EPS) * ng
    return y * (1.0 + scale) + shift


def _k1_kernel(x_ref, modl_ref, modc_ref, ng_ref, cos_ref, sa_ref, sb_ref, w_ref,
               o16_ref, o32_ref, *, tm, ctx_len):
    i = pl.program_id(1)
    pos = i * tm + lax.broadcasted_iota(jnp.int32, (tm, 1), 0)
    is_ctx = pos < ctx_len
    shift = jnp.where(is_ctx, modc_ref[0:1, :], modl_ref[0, 0:1, :])
    scale = jnp.where(is_ctx, modc_ref[1:2, :], modl_ref[0, 1:2, :])
    h = _norm_mod(x_ref[0], ng_ref[...], shift, scale).astype(BF16)
    cos = cos_ref[...]
    sa = sa_ref[...]
    sb = sb_ref[...]
    ch = 512
    for c0 in range(0, A16_W, ch):
        acc = jnp.dot(h, w_ref[:, c0:c0 + ch], preferred_element_type=F32)
        if c0 < A16_V:
            parts = []
            for j in range(ch // 128):
                a = acc[:, j * 128:(j + 1) * 128]
                parts.append(a * cos + pltpu.roll(a, 16, 1) * sa + pltpu.roll(a, 112, 1) * sb)
            acc = jnp.concatenate(parts, axis=1)
        if c0 < A16_K:
            acc = acc * LOG2_E
        o16_ref[0, :, c0:c0 + ch] = acc.astype(BF16)
    for c0 in range(0, A32_W, ch):
        c1 = min(c0 + ch, A32_W)
        o32_ref[0, :, c0:c1] = jnp.dot(h, w_ref[:, A16_W + c0:A16_W + c1],
                                       preferred_element_type=F32)


def _k1(x, modl, modc, ng, cos, sa, sb, w_all, *, ctx_len):
    b, t, d = x.shape
    tm = _row_tile(t, 272)
    tbl = pl.BlockSpec((tm, 128), lambda bi, i: (i, 0))
    return pl.pallas_call(
        functools.partial(_k1_kernel, tm=tm, ctx_len=ctx_len),
        grid=(b, t // tm),
        in_specs=[
            pl.BlockSpec((1, tm, d), lambda bi, i: (bi, i, 0)),
            pl.BlockSpec((1, 8, d), lambda bi, i: (bi, 0, 0)),
            pl.BlockSpec((8, d), lambda bi, i: (0, 0)),
            pl.BlockSpec((1, d), lambda bi, i: (0, 0)),
            tbl, tbl, tbl,
            pl.BlockSpec((d, W_ALL_COLS), lambda bi, i: (0, 0), pipeline_mode=pl.Buffered(1)),
        ],
        out_specs=[
            pl.BlockSpec((1, tm, A16_W), lambda bi, i: (bi, i, 0)),
            pl.BlockSpec((1, tm, A32_W), lambda bi, i: (bi, i, 0)),
        ],
        out_shape=[
            jax.ShapeDtypeStruct((b, t, A16_W), BF16),
            jax.ShapeDtypeStruct((b, t, A32_W), F32),
        ],
        compiler_params=_cparams(("parallel", "parallel")),
        name="norm_inproj",
    )(x, modl, modc, ng, cos, sa, sb, w_all)


def _attn_kernel(lam_ref, g_ref, q_ref, k_ref, v_ref, o_ref, *, ctx_len, segments, lam_init):
    qi = pl.program_id(2)
    tq = Q_TILE
    hb = DIFF_V_DIM
    lv = lam_ref[...]
    lam = (jnp.exp(jnp.sum(lv[0:1] * lv[1:2], axis=1, keepdims=True))
           - jnp.exp(jnp.sum(lv[2:3] * lv[3:4], axis=1, keepdims=True)) + lam_init)

    def run(segs):
        qss = []
        for g in range(ATTN_HEADS):
            q = q_ref[0, :, g * hb:(g + 1) * hb]
            lane = lax.broadcasted_iota(jnp.int32, q.shape, 1)
            zero = jnp.zeros_like(q)
            qss.append(jnp.concatenate([jnp.where(lane < DIFF_HEAD_DIM, q, zero),
                                        jnp.where(lane >= DIFF_HEAD_DIM, q, zero)], axis=0))
        carries = [None] * ATTN_HEADS
        for r0, rows in segs:
            for g in range(ATTN_HEADS):
                k_t = k_ref[0, r0:r0 + rows, g * hb:(g + 1) * hb]
                v_t = v_ref[0, r0:r0 + rows, g * hb:(g + 1) * hb]
                s = lax.dot_general(qss[g], k_t, (((1,), (1,)), ((), ())), preferred_element_type=F32)
                smax = jnp.max(s, axis=-1, keepdims=True)
                if carries[g] is None:
                    p = jnp.exp2(s - smax)
                    carries[g] = (smax, jnp.sum(p, axis=-1, keepdims=True),
                                  jnp.dot(p.astype(BF16), v_t, preferred_element_type=F32))
                else:
                    m, l, acc = carries[g]
                    m_new = jnp.maximum(m, smax)
                    alpha = jnp.exp2(m - m_new)
                    p = jnp.exp2(s - m_new)
                    carries[g] = (m_new, alpha * l + jnp.sum(p, axis=-1, keepdims=True),
                                  alpha * acc + jnp.dot(p.astype(BF16), v_t, preferred_element_type=F32))
        for g in range(ATTN_HEADS):
            _, l, acc = carries[g]
            o = acc[:tq] / l[:tq] - lam * (acc[tq:] / l[tq:])
            o = o * lax.rsqrt(jnp.mean(o * o, axis=-1, keepdims=True) + RMS_EPS) * g_ref[...]
            o_ref[0, :, g * hb:(g + 1) * hb] = (o * (1.0 - lam_init)).astype(BF16)

    @pl.when(qi * tq >= ctx_len)
    def _():
        run(segments)

    @pl.when(qi * tq < ctx_len)
    def _():
        run([(0, ctx_len)])


def _attn(a16, lam_p, subln_g, *, ctx_len, seq_len, lam_init):
    b, t, _ = a16.shape
    assert ctx_len % Q_TILE == 0 and t % 256 == 0
    first = (t // 256 + KV_SEGMENTS - 1) // KV_SEGMENTS * 256
    bounds = [min(first * n, t) for n in range(KV_SEGMENTS + 1)]
    segments = [(lo, hi - lo) for lo, hi in zip(bounds[:-1], bounds[1:]) if hi > lo]
    hb = DIFF_V_DIM * ATTN_HEADS
    return pl.pallas_call(
        functools.partial(_attn_kernel, ctx_len=ctx_len, segments=segments, lam_init=lam_init),
        grid=(b, DIFF_HEADS // ATTN_HEADS, t // Q_TILE),
        in_specs=[
            pl.BlockSpec((4, DIFF_HEAD_DIM), lambda bi, h, qi: (0, 0)),
            pl.BlockSpec((1, DIFF_V_DIM), lambda bi, h, qi: (0, 0)),
            pl.BlockSpec((1, Q_TILE, hb), lambda bi, h, qi: (bi, qi, A16_Q // hb + h)),
            pl.BlockSpec((1, t, hb), lambda bi, h, qi: (bi, 0, A16_K // hb + h)),
            pl.BlockSpec((1, t, hb), lambda bi, h, qi: (bi, 0, A16_V // hb + h)),
        ],
        out_specs=pl.BlockSpec((1, Q_TILE, hb), lambda bi, h, qi: (bi, qi, h)),
        out_shape=jax.ShapeDtypeStruct((b, t, DIFF_HEADS * DIFF_V_DIM), BF16),
        compiler_params=_cparams(("parallel", "parallel", "parallel")),
        name="diff_attn",
    )(lam_p, subln_g, a16, a16, a16)


def _split3(x):
    hi = x.astype(BF16)
    r1 = x - hi.astype(F32)
    mid = r1.astype(BF16)
    lo = (r1 - mid.astype(F32)).astype(BF16)
    return jnp.concatenate([hi, mid, lo], axis=1)


def _sum3(y, w):
    return y[:, 0:w] + y[:, w:2 * w] + y[:, 2 * w:3 * w]


def _gla_kernel(q_ref, k_ref, v_ref, a_ref, r_ref, wa_ref, ba_ref, ng_ref, o_ref,
                qd_s, ki_s, ke_s, dec_s, oacc_s, st_s, *, t, ctx_len):
    ck = GLA_CHUNK
    nc = t // ck
    ncc = ctx_len // ck
    pr = 256
    w = 2 * GLA_DK

    ri = lax.broadcasted_iota(jnp.int32, (pr, pr), 0)
    ci = lax.broadcasted_iota(jnp.int32, (pr, pr), 1)
    same = (ri // ck) == (ci // ck)
    m_tot = jnp.where(same, 1.0, 0.0).astype(BF16)
    m_cum = (jnp.where(same & (ci <= ri), 1.0, 0.0).astype(BF16),
             jnp.where(same & (ci >= ri), 1.0, 0.0).astype(BF16))
    for r0 in range(0, t, pr):
        a = a_ref[0, r0:r0 + pr, :].astype(BF16)
        logit = jnp.dot(a, wa_ref[0], preferred_element_type=F32) + ba_ref[0]
        g = (jnp.minimum(logit, 0.0) - jnp.log1p(jnp.exp(-jnp.abs(logit)))) * (1.0 / GLA_TAU)
        q = q_ref[0, r0:r0 + pr, :]
        k = k_ref[0, r0:r0 + pr, :]
        for d in range(2):
            g3 = _split3(g[:, d * w:(d + 1) * w])
            g_cum = _sum3(jnp.dot(m_cum[d], g3, preferred_element_type=F32), w)
            g_tot = _sum3(jnp.dot(m_tot, g3, preferred_element_type=F32), w)
            qd_s[d, r0:r0 + pr, :] = (q * jnp.exp(g_cum)).astype(BF16)
            ki_s[d, r0:r0 + pr, :] = (k * jnp.exp(-g_cum)).astype(BF16)
            ke_s[d, r0:r0 + pr, :] = (k * jnp.exp(g_tot - g_cum)).astype(BF16)
            dec_s[d, r0:r0 + pr, :] = jnp.exp(g_tot)
        oacc_s[r0:r0 + pr, :] = jnp.zeros((pr, 2 * GLA_DV), F32)
    st_s[...] = jnp.zeros_like(st_s)

    ri = lax.broadcasted_iota(jnp.int32, (2 * ck, ck), 0)
    ci = lax.broadcasted_iota(jnp.int32, (2 * ck, ck), 1)
    rr = jnp.where(ri >= ck, ri - ck, ri)
    tri = (ci <= rr, ci >= rr)
    lane = lax.broadcasted_iota(jnp.int32, (ck, w), 1)
    br = lax.broadcasted_iota(jnp.int32, (2 * GLA_DV, w), 0)
    bc = lax.broadcasted_iota(jnp.int32, (2 * GLA_DV, w), 1)
    blk = (br < GLA_DV) == (bc < GLA_DK)
    nt = (((1,), (1,)), ((), ()))
    tn = (((0,), (0,)), ((), ()))

    def chunk(d, r0):
        qd = qd_s[d, pl.ds(r0, ck), :]
        ki = ki_s[d, pl.ds(r0, ck), :]
        ke = ke_s[d, pl.ds(r0, ck), :]
        v = v_ref[0, pl.ds(r0, ck), :]
        dec = dec_s[d, pl.ds(r0, 8), :][0:1, :]
        zq = jnp.zeros_like(qd)
        qs = jnp.concatenate([jnp.where(lane < GLA_DK, qd, zq), jnp.where(lane >= GLA_DK, qd, zq)], axis=0)
        att = lax.dot_general(qs, ki, nt, preferred_element_type=F32)
        att = jnp.where(tri[d], att, 0.0).astype(BF16)
        oi = jnp.dot(att, v, preferred_element_type=F32)
        o_intra = jnp.concatenate([oi[:ck, :GLA_DV], oi[ck:, GLA_DV:]], axis=1)
        st = st_s[d]
        o_inter = lax.dot_general(qd, st.astype(BF16), nt, preferred_element_type=F32)
        s_loc = lax.dot_general(v, ke, tn, preferred_element_type=F32)
        st_s[d] = st * dec + jnp.where(blk, s_loc, 0.0)
        oacc_s[pl.ds(r0, ck), :] = oacc_s[pl.ds(r0, ck), :] + o_intra + o_inter

    def body(n, _):
        chunk(0, pl.multiple_of(n * ck, ck))
        nb = jnp.where(n < ncc, ncc - 1 - n, nc - 1 - (n - ncc))
        chunk(1, pl.multiple_of(nb * ck, ck))
        return 0

    lax.fori_loop(0, nc, body, 0, unroll=GLA_UNROLL)

    ng = ng_ref[...]
    rt = _row_tile(t, 544)
    for r0 in range(0, t, rt):
        o = oacc_s[r0:r0 + rt, :]
        r = r_ref[0, r0:r0 + rt, :]
        outs = []
        for hh in range(2):
            oh = o[:, hh * GLA_DV:(hh + 1) * GLA_DV]
            outs.append(oh * lax.rsqrt(jnp.mean(oh * oh, axis=-1, keepdims=True) + RMS_EPS) * ng)
        o_ref[0, r0:r0 + rt, :] = (jnp.concatenate(outs, axis=1) * (r * _sigmoid(r))).astype(BF16)


def _gla(a16, a32, wa, ba, ng, *, ctx_len):
    b, t, _ = a16.shape
    assert t % 256 == 0 and ctx_len % GLA_CHUNK == 0 and (t // GLA_CHUNK) % GLA_UNROLL == 0
    return pl.pallas_call(
        functools.partial(_gla_kernel, t=t, ctx_len=ctx_len),
        grid=(b, 2),
        in_specs=[
            pl.BlockSpec((1, t, 128), lambda bi, p: (bi, 0, A32_GLQ // 128 + p)),
            pl.BlockSpec((1, t, 128), lambda bi, p: (bi, 0, A32_GLK // 128 + p)),
            pl.BlockSpec((1, t, 256), lambda bi, p: (bi, 0, A16_GV // 256 + p)),
            pl.BlockSpec((1, t, 128), lambda bi, p: (bi, 0, A32_GLA // 128)),
            pl.BlockSpec((1, t, 256), lambda bi, p: (bi, 0, A32_GLR // 256 + p)),
            pl.BlockSpec((1, 128, 256), lambda bi, p: (p, 0, 0)),
            pl.BlockSpec((1, 1, 256), lambda bi, p: (p, 0, 0)),
            pl.BlockSpec((1, GLA_DV), lambda bi, p: (0, 0)),
        ],
        out_specs=pl.BlockSpec((1, t, 256), lambda bi, p: (bi, 0, p)),
        out_shape=jax.ShapeDtypeStruct((b, t, GLA_HEADS * GLA_DV), BF16),
        scratch_shapes=[
            pltpu.VMEM((2, t, 128), BF16), pltpu.VMEM((2, t, 128), BF16), pltpu.VMEM((2, t, 128), BF16),
            pltpu.VMEM((2, t, 128), F32),
            pltpu.VMEM((t, 256), F32),
            pltpu.VMEM((2, 256, 128), F32),
        ],
        compiler_params=_cparams(("parallel", "parallel")),
        name="gla_bidir",
    )(a32, a32, a16, a32, a32, wa, ba, ng)


def _merge_kernel(x_ref, od_ref, og_ref, gates_ref, cvb_ref, cvc_ref, cvx_ref,
                  cvc_p_ref, cvx_p_ref, cvc_n_ref, cvx_n_ref, cw_ref,
                  wdo_ref, wco_ref, wgo_ref, wo_ref, modl_ref, modc_ref, n2g_ref,
                  rw_ref, rb_ref, xo_ref, h2_ref, rt_ref, *, tm, ctx_len, t):
    i = pl.program_id(1)
    pos = i * tm + lax.broadcasted_iota(jnp.int32, (tm, 1), 0)
    is_ctx = pos < ctx_len
    row = lax.broadcasted_iota(jnp.int32, (tm, 1), 0)

    z = cvc_ref[0] * cvx_ref[0]
    z_before = cvc_p_ref[0, 7:8, :] * cvx_p_ref[0, 7:8, :]
    z_after = cvc_n_ref[0, 0:1, :] * cvx_n_ref[0, 0:1, :]
    z_prev = jnp.where(row == 0, z_before, pltpu.roll(z, 1, 0))
    z_prev = jnp.where((pos == 0) | (pos == ctx_len), 0.0, z_prev)
    z_next = jnp.where(row == tm - 1, z_after, pltpu.roll(z, tm - 1, 0))
    z_next = jnp.where((pos == ctx_len - 1) | (pos == t - 1), 0.0, z_next)
    conv = z_prev * cw_ref[0:1, :] + z * cw_ref[1:2, :] + z_next * cw_ref[2:3, :]
    zc = (cvb_ref[0] * conv).astype(BF16)

    y_diff = jnp.dot(od_ref[0], wdo_ref[...], preferred_element_type=F32)
    y_conv = jnp.dot(zc, wco_ref[...], preferred_element_type=F32)
    y_gla = jnp.dot(og_ref[0], wgo_ref[...], preferred_element_type=F32)
    d = D_MODEL
    mix = (_sigmoid(gates_ref[0, :, 0:d]) * y_diff
           + _sigmoid(gates_ref[0, :, d:2 * d]) * y_conv
           + _sigmoid(gates_ref[0, :, 2 * d:3 * d]) * y_gla)
    m = jnp.dot(mix.astype(BF16), wo_ref[...], preferred_element_type=F32)

    g1 = jnp.where(is_ctx, modc_ref[2:3, :], modl_ref[0, 2:3, :])
    x_new = x_ref[0] + g1 * m
    xo_ref[0] = x_new
    sh2 = jnp.where(is_ctx, modc_ref[3:4, :], modl_ref[0, 3:4, :])
    sc2 = jnp.where(is_ctx, modc_ref[4:5, :], modl_ref[0, 4:5, :])
    h2 = _norm_mod(x_new, n2g_ref[...], sh2, sc2)
    for cc in range(d // 128):
        h2_ref[0, pl.ds(cc, tm, stride=8), :] = h2[:, cc * 128:(cc + 1) * 128]

    logits = jnp.dot(h2.astype(BF16), rw_ref[...], preferred_element_type=F32) + rb_ref[...]
    lane = lax.broadcasted_iota(jnp.int32, logits.shape, 1)
    vals = logits
    tops = []
    idx_out = jnp.zeros(logits.shape, jnp.int32)
    for kk in range(TOP_K):
        mx = jnp.max(vals, axis=-1, keepdims=True)
        ix = jnp.min(jnp.where(vals == mx, lane, 128), axis=-1, keepdims=True)
        tops.append(mx)
        idx_out = jnp.where(lane == kk, ix, idx_out)
        vals = jnp.where(lane == ix, -jnp.inf, vals)
    es = [jnp.exp(v - tops[0]) for v in tops]
    den = es[0] + es[1] + es[2] + es[3]
    packed = idx_out.astype(F32)
    for kk in range(TOP_K):
        gk = es[kk] / den
        hi = gk.astype(BF16).astype(F32)
        mid = (gk - hi).astype(BF16).astype(F32)
        lo = gk - hi - mid
        for piece, val in enumerate((hi, mid, lo)):
            packed = jnp.where(lane == TOP_K * (piece + 1) + kk, val, packed)
    ri = lax.broadcasted_iota(jnp.int32, (tm, tm), 0)
    ci = lax.broadcasted_iota(jnp.int32, (tm, tm), 1)
    eye = jnp.where(ri == ci, 1.0, 0.0).astype(BF16)
    rows = lax.dot_general(packed.astype(BF16), eye, (((0,), (0,)), ((), ())), preferred_element_type=F32)
    rt_ref[0, 0] = rows[0:4 * TOP_K, :]


def _merge(x, od, og, a32, cw, wdo, wco, wgo, wo, modl, modc, n2g, rw, rb, *, ctx_len):
    b, t, d = x.shape
    tm = _row_tile(t, 272)
    nt8 = t // 8
    row = lambda bi, i: (bi, i, 0)
    const2 = lambda bi, i: (0, 0)
    return pl.pallas_call(
        functools.partial(_merge_kernel, tm=tm, ctx_len=ctx_len, t=t),
        grid=(b, t // tm),
        in_specs=[
            pl.BlockSpec((1, tm, d), row),
            pl.BlockSpec((1, tm, d), row),
            pl.BlockSpec((1, tm, GLA_HEADS * GLA_DV), row),
            pl.BlockSpec((1, tm, 3 * d), lambda bi, i: (bi, i, 0)),
            pl.BlockSpec((1, tm, CONV_W), lambda bi, i: (bi, i, A32_CVB // CONV_W)),
            pl.BlockSpec((1, tm, CONV_W), lambda bi, i: (bi, i, A32_CVC // CONV_W)),
            pl.BlockSpec((1, tm, CONV_W), lambda bi, i: (bi, i, A32_CVX // CONV_W)),
            pl.BlockSpec((1, 8, CONV_W), lambda bi, i: (bi, jnp.maximum(i * (tm // 8) - 1, 0),
                                                   A32_CVC // CONV_W)),
            pl.BlockSpec((1, 8, CONV_W), lambda bi, i: (bi, jnp.maximum(i * (tm // 8) - 1, 0),
                                                   A32_CVX // CONV_W)),
            pl.BlockSpec((1, 8, CONV_W), lambda bi, i: (bi, jnp.minimum((i + 1) * (tm // 8), nt8 - 1),
                                                   A32_CVC // CONV_W)),
            pl.BlockSpec((1, 8, CONV_W), lambda bi, i: (bi, jnp.minimum((i + 1) * (tm // 8), nt8 - 1),
                                                   A32_CVX // CONV_W)),
            pl.BlockSpec((8, CONV_W), const2),
            pl.BlockSpec((d, d), const2),
            pl.BlockSpec((CONV_W, d), const2),
            pl.BlockSpec((GLA_HEADS * GLA_DV, d), const2),
            pl.BlockSpec((d, d), const2),
            pl.BlockSpec((1, 8, d), lambda bi, i: (bi, 0, 0)),
            pl.BlockSpec((8, d), const2),
            pl.BlockSpec((1, d), const2),
            pl.BlockSpec((d, 128), const2),
            pl.BlockSpec((1, 128), const2),
        ],
        out_specs=[
            pl.BlockSpec((1, tm, d), row),
            pl.BlockSpec((1, tm * (d // 128), 128), row),
            pl.BlockSpec((1, 1, 4 * TOP_K, tm), lambda bi, i: (bi, i, 0, 0)),
        ],
        out_shape=[
            jax.ShapeDtypeStruct((b, t, d), F32),
            jax.ShapeDtypeStruct((b, t * (d // 128), 128), F32),
            jax.ShapeDtypeStruct((b, t // tm, 4 * TOP_K, tm), F32),
        ],
        compiler_params=_cparams(("parallel", "parallel")),
        name="merge_router",
    )(x, od, og, a32, a32, a32, a32, a32, a32, a32, a32, cw, wdo, wco, wgo, wo,
      modl, modc, n2g, rw, rb)


def _moe_kernel(be_ref, nv_ref, g_ref, w1g_ref, w1l_ref, b1g_ref, b1l_ref, w2_ref, b2_ref,
                src_hbm, dst_hbm, h2_hbm, out_hbm, src_s, dst_s, xbuf, ybuf, sem_p, sem_g, sem_s,
                *, nb, n_slots):
    i = pl.program_id(0)
    nv = nv_ref[0]
    bm = MOE_BLOCK
    cur = lax.rem(i, 2)
    nxt = 1 - cur

    def idx_copies(blk_src, blk_dst, slot):
        return (pltpu.make_async_copy(src_hbm.at[blk_src, 0], src_s.at[pl.ds(slot * bm, bm)], sem_p),
                pltpu.make_async_copy(dst_hbm.at[blk_dst, 0], dst_s.at[pl.ds(slot * bm, bm)], sem_p))

    def start_gather(slot, buf):
        for j in range(bm):
            r0 = pl.multiple_of(src_s[slot * bm + j], TOK_ROWS)
            pltpu.make_async_copy(h2_hbm.at[pl.ds(r0, TOK_ROWS)],
                                  xbuf.at[buf, pl.ds(j * TOK_ROWS, TOK_ROWS)],
                                  sem_g.at[buf]).start(priority=j % 2)

    def wait_gather(buf):
        pltpu.make_async_copy(h2_hbm.at[pl.ds(0, bm * TOK_ROWS)], xbuf.at[buf], sem_g.at[buf]).wait()

    def wait_scatter():
        pltpu.make_async_copy(ybuf.at[0], out_hbm.at[pl.ds(0, bm * TOK_ROWS)], sem_s).wait()

    @pl.when(i <= nv)
    def _():
        @pl.when(i == 0)
        def _():
            ybuf[...] = jnp.zeros_like(ybuf)
            for blk_src, blk_dst, slot in ((0, 0, 0), (1, 1, 1), (0, nb, 3)):
                for cp in idx_copies(blk_src, blk_dst, slot):
                    cp.start()
                    cp.wait()
            start_gather(0, 0)

        @pl.when(i >= 1)
        def _():
            for cp in idx_copies(0, 0, 0):
                cp.wait()
            wait_scatter()

        wait_gather(cur)

        start_gather(lax.rem(i + 1, 4), nxt)
        pslot = lax.rem(i + 3, 4)
        for j in range(bm):
            r0 = pl.multiple_of(dst_s[pslot * bm + j], TOK_ROWS)
            pltpu.make_async_copy(ybuf.at[nxt, pl.ds(j * TOK_ROWS, TOK_ROWS)],
                                  out_hbm.at[pl.ds(r0, TOK_ROWS)], sem_s).start(priority=j % 2)
        x = jnp.concatenate([xbuf[cur, pl.ds(cc, bm, stride=TOK_ROWS), :] for cc in range(TOK_ROWS)],
                            axis=1).astype(BF16)
        hg = jnp.dot(x, w1g_ref[0, 0], preferred_element_type=F32) + b1g_ref[0]
        hl = jnp.dot(x, w1l_ref[0, 0], preferred_element_type=F32) + b1l_ref[0]
        hg = jnp.minimum(hg, SWIGLU_LIMIT)
        hl = jnp.clip(hl, -SWIGLU_LIMIT, SWIGLU_LIMIT)
        act = hg * _sigmoid(SWIGLU_ALPHA * hg) * (hl + 1.0)
        y = jnp.dot(act.astype(BF16), w2_ref[0], preferred_element_type=F32) + b2_ref[0]
        gi = lax.broadcasted_iota(jnp.int32, (bm, bm), 0)
        gj = lax.broadcasted_iota(jnp.int32, (bm, bm), 1)
        y = y * jnp.sum(jnp.where(gi == gj, g_ref[0], 0.0), axis=1, keepdims=True)
        for cc in range(TOK_ROWS):
            ybuf[cur, pl.ds(cc, bm, stride=TOK_ROWS), :] = y[:, cc * 128:(cc + 1) * 128]

        @pl.when(i < nv)
        def _():
            blk = jnp.minimum(i + 2, nv - 1)
            for cp in idx_copies(blk, blk, lax.rem(i + 2, 4)):
                cp.start()

        @pl.when(i == nv)
        def _():
            wait_scatter()
            wait_gather(nxt)
            ybuf[...] = jnp.zeros_like(ybuf)
            fills = [pltpu.make_async_copy(
                ybuf.at[sl], out_hbm.at[pl.ds((n_slots + sl * bm) * TOK_ROWS, bm * TOK_ROWS)], sem_s)
                for sl in range(2)]
            for cp in fills:
                cp.start()
            for cp in fills:
                cp.wait()


def _moe(block_exp, n_valid, row_src, row_dst, row_gate, h2_flat, w1, b1g, b1l, w2, b2, *, n_slots):
    d = D_MODEL
    nb = row_src.shape[0]
    bm = MOE_BLOCK
    de = w1.shape[-1]
    blk = lambda i, nv: jnp.minimum(i, nv[0] - 1)
    wmap = lambda i, be, nv: (be[blk(i, nv)], 0, 0)
    return pl.pallas_call(
        functools.partial(_moe_kernel, nb=nb, n_slots=n_slots),
        grid_spec=pltpu.PrefetchScalarGridSpec(
            num_scalar_prefetch=2,
            grid=(nb + 1,),
            in_specs=[
                pl.BlockSpec((1, 1, bm), lambda i, be, nv: (blk(i, nv), 0, 0)),
                pl.BlockSpec((1, 1, d, de), lambda i, be, nv: (be[blk(i, nv)], 0, 0, 0)),
                pl.BlockSpec((1, 1, d, de), lambda i, be, nv: (be[blk(i, nv)], 1, 0, 0)),
                pl.BlockSpec((1, 1, de), wmap),
                pl.BlockSpec((1, 1, de), wmap),
                pl.BlockSpec((1, de, d), wmap),
                pl.BlockSpec((1, 1, d), wmap),
                pl.BlockSpec(memory_space=pl.ANY),
                pl.BlockSpec(memory_space=pl.ANY),
                pl.BlockSpec(memory_space=pl.ANY),
            ],
            out_specs=pl.BlockSpec(memory_space=pl.ANY),
            scratch_shapes=[
                pltpu.SMEM((4 * bm,), jnp.int32),
                pltpu.SMEM((4 * bm,), jnp.int32),
                pltpu.VMEM((2, bm * TOK_ROWS, 128), F32),
                pltpu.VMEM((2, bm * TOK_ROWS, 128), F32),
                pltpu.SemaphoreType.DMA,
                pltpu.SemaphoreType.DMA((2,)),
                pltpu.SemaphoreType.DMA,
            ],
        ),
        out_shape=jax.ShapeDtypeStruct(((n_slots + 2 * bm) * TOK_ROWS, 128), F32),
        compiler_params=_cparams(("arbitrary",)),
        name="moe_experts",
    )(block_exp, n_valid, row_gate, w1, w1, b1g, b1l, w2, b2, row_src, row_dst, h2_flat)


def _combine_kernel(x_ref, y4_ref, modl_ref, modc_ref, fg_ref, o_ref, *, tm, ctx_len, final, skip):
    i = pl.program_id(1) + skip
    d = D_MODEL
    ld = lambda k, cc: y4_ref[pl.ds(k * TOK_ROWS + cc, tm, stride=TOP_K * TOK_ROWS), :]
    f = jnp.concatenate([(ld(0, cc) + ld(1, cc)) + (ld(2, cc) + ld(3, cc)) for cc in range(TOK_ROWS)],
                        axis=1)
    pos = i * tm + lax.broadcasted_iota(jnp.int32, (tm, 1), 0)
    g2 = jnp.where(pos < ctx_len, modc_ref[5:6, :], modl_ref[0, 5:6, :])
    x_new = x_ref[0] + g2 * f
    if final:
        x_new = x_new * lax.rsqrt(jnp.mean(x_new * x_new, axis=-1, keepdims=True) + RMS_EPS) * fg_ref[...]
    o_ref[0] = x_new


def _combine(x, y4, modl, modc, fg, *, ctx_len, final):
    b, t, d = x.shape
    if final:
        tm = Q_TILE
        assert ctx_len % tm == 0
        skip = ctx_len // tm
    else:
        tm = _row_tile(t, 272)
        skip = 0
    npb = t // tm
    return pl.pallas_call(
        functools.partial(_combine_kernel, tm=tm, ctx_len=ctx_len, final=final, skip=skip),
        grid=(b, npb - skip),
        in_specs=[
            pl.BlockSpec((1, tm, d), lambda bi, i: (bi, i + skip, 0)),
            pl.BlockSpec((tm * TOP_K * TOK_ROWS, 128), lambda bi, i: (bi * (npb - skip) + i, 0)),
            pl.BlockSpec((1, 8, d), lambda bi, i: (bi, 0, 0)),
            pl.BlockSpec((8, d), lambda bi, i: (0, 0)),
            pl.BlockSpec((1, d), lambda bi, i: (0, 0)),
        ],
        out_specs=pl.BlockSpec((1, tm, d), lambda bi, i: (bi, i, 0)),
        out_shape=jax.ShapeDtypeStruct((b, t - skip * tm, d), F32),
        compiler_params=_cparams(("parallel", "parallel")),
        name="moe_combine",
    )(x, y4, modl, modc, fg)


def _routing(e_flat, gate_flat, *, tm, seq, ctx_len, latent_only):
    npair = e_flat.shape[0]
    t = seq + ctx_len
    bm = MOE_BLOCK
    nb = npair // bm + N_EXPERTS
    nr = nb * bm
    npad = nr - npair
    experts = jnp.arange(N_EXPERTS, dtype=jnp.int32)
    ids = jnp.arange(npair, dtype=jnp.int32)
    tok_of = lambda i: (i // (TOP_K * tm)) * tm + i % tm
    if latent_only:
        e_flat = jnp.where(tok_of(ids) % t >= ctx_len, e_flat, N_EXPERTS)
    counts = jnp.sum((e_flat[:, None] == experts[None, :]).astype(jnp.int32), axis=0)
    padded = (counts + bm - 1) // bm * bm
    pad_end = jnp.cumsum(padded)
    fill_end = jnp.cumsum(padded - counts)
    dummy = jnp.arange(npad, dtype=jnp.int32)
    e_dummy = jnp.sum((dummy[:, None] >= fill_end[None, :]).astype(jnp.int32), axis=1)
    kb = nr
    keys = jnp.concatenate([e_flat * kb + ids, e_dummy * kb + npair + dummy])
    gates = jnp.concatenate([gate_flat, jnp.zeros((npad,), F32)])
    skey, sgate = lax.sort((keys, gates), num_keys=1)
    e_row = skey // kb
    ident = skey - e_row * kb
    valid = (ident < npair) & (e_row < N_EXPERTS)
    pair = jnp.where(valid, ident, 0)
    tok = tok_of(pair)
    kk = (pair // tm) % TOP_K
    row_gate = jnp.where(valid, sgate, 0.0).reshape(nb, 1, bm)
    row_src = (tok * TOK_ROWS).reshape(nb, 1, bm)
    if latent_only:
        slot = (tok // t) * seq + tok % t - ctx_len
        n_slots = (npair // TOP_K // t) * seq * TOP_K
    else:
        slot = tok
        n_slots = npair
    j = jnp.arange(nr, dtype=jnp.int32)
    dump = n_slots + ((j // bm) % 2) * bm + j % bm
    row_dst = jnp.where(valid, slot * TOP_K + kk, dump).reshape(nb, bm)
    row_dst = jnp.concatenate([row_dst, (n_slots + bm + jnp.arange(bm, dtype=jnp.int32))[None, :]],
                              axis=0)[:, None, :] * TOK_ROWS
    blocks = jnp.arange(nb, dtype=jnp.int32) * bm
    block_exp = jnp.minimum(jnp.sum((blocks[:, None] >= pad_end[None, :]).astype(jnp.int32), axis=1),
                            N_EXPERTS - 1)
    n_valid = (pad_end[-1] // bm).astype(jnp.int32).reshape(1)
    return row_src, row_dst, row_gate, block_exp, n_valid, n_slots


def _rope_tables(ctx_len, seq_len):
    inv_freq = ROPE_BASE ** (-jnp.arange(0, ROT_AXIS_DIM, 2, dtype=F32) / ROT_AXIS_DIM)
    p = jnp.arange(seq_len, dtype=jnp.int32)
    row = (p // GRID_W).astype(F32)
    col = (p % GRID_W).astype(F32)
    lane = np.arange(128)
    dd = lane % DIFF_HEAD_DIM
    fidx = jnp.asarray((dd % ROT_AXIS_DIM) % (ROT_AXIS_DIM // 2))
    use_col = jnp.asarray(dd >= ROT_AXIS_DIM)
    second_half = jnp.asarray((dd % ROT_AXIS_DIM) >= ROT_AXIS_DIM // 2)
    posm = jnp.where(use_col[None, :], col[:, None], row[:, None])
    ang = posm * inv_freq[fidx][None, :]
    cos = jnp.cos(ang)
    sin = jnp.sin(ang)
    sa = jnp.where(second_half[None, :], sin, 0.0)
    sb = jnp.where(second_half[None, :], 0.0, -sin)
    pad = lambda a, v: jnp.concatenate([jnp.full((ctx_len, 128), v, F32), a], axis=0)
    return pad(cos, 1.0), pad(sa, 0.0), pad(sb, 0.0)


def kernel(x, c, ctx, c_ctx, ada_w, ada_b, norm1_g, norm2_g, w_in, diff_lambda, diff_subln_g,
           diff_w_out, conv_w, conv_w_out, gla_w_a2, gla_b_a, gla_norm_g, gla_w_out, w_o,
           router_w, router_b, moe_w1, moe_b1, moe_w2, moe_b2, final_norm_g):
    b, s, d = x.shape
    cl = ctx.shape[1]
    t = cl + s
    nl = w_in.shape[0]
    assert d == D_MODEL and b <= 15

    xs = jnp.concatenate([ctx, x], axis=1)
    cc = jnp.concatenate([c, c_ctx[None, :], jnp.zeros((15 - b, d), F32)], axis=0)
    mod = _ada(cc, ada_w, ada_b).reshape(nl, 16, N_ADA, d)
    cos, sa, sb = _rope_tables(cl, s)

    sp = np.cumsum([1024, 1024, 1024, 512, 512, 512, 256, 256, 512, 512, 32, 3072])
    (w_q, w_k, w_v, w_cb, w_cc, w_cx, w_gq, w_gk, w_gv, w_gr, w_ga, w_gt) = jnp.split(
        w_in, [int(v) for v in sp[:-1]], axis=-1)
    w_all = jnp.concatenate(
        [w_q * (DIFF_HEAD_DIM ** -0.5), w_k, w_v, w_gv,
         w_gt, w_cb, w_cc, w_cx, w_gq * (GLA_DK ** -0.5), w_gk, w_gr, w_ga,
         jnp.zeros((nl, d, A32_W - A32_GLA - 2 * GLA_RANK), F32)], axis=-1).astype(BF16)

    wa = jnp.zeros((nl, 2, 128, 256), F32)
    for p in range(2):
        wa = wa.at[:, p, 0:GLA_RANK, 0:128].set(gla_w_a2[:, 0, :, p * 128:(p + 1) * 128])
        wa = wa.at[:, p, GLA_RANK:2 * GLA_RANK, 128:256].set(gla_w_a2[:, 1, :, p * 128:(p + 1) * 128])
    wa = wa.astype(BF16)
    ba = jnp.stack([jnp.concatenate([gla_b_a[:, 0, p * 128:(p + 1) * 128],
                                     gla_b_a[:, 1, p * 128:(p + 1) * 128]], axis=-1)
                    for p in range(2)], axis=1)[:, :, None, :]

    cw = jnp.concatenate([conv_w, jnp.zeros((nl, 5, CONV_W), F32)], axis=1)
    rw = jnp.concatenate([router_w, jnp.zeros((nl, d, 128 - N_EXPERTS), F32)], axis=-1).astype(BF16)
    rb = jnp.concatenate([router_b, jnp.full((nl, 128 - N_EXPERTS), -1e30, F32)], axis=-1)[:, None, :]
    w1 = jnp.moveaxis(moe_w1.reshape(nl, N_EXPERTS, d, D_EXPERT, 2), -1, 2).astype(BF16)
    b1 = moe_b1.reshape(nl, N_EXPERTS, 1, D_EXPERT, 2)
    b1g = b1[..., 0]
    b1l = b1[..., 1]
    w2 = moe_w2.astype(BF16)
    b2 = moe_b2[:, :, None, :]
    wdo = diff_w_out.astype(BF16)
    wco = conv_w_out.astype(BF16)
    wgo = gla_w_out.astype(BF16)
    wo = w_o.astype(BF16)

    for layer in range(nl):
        last = layer == nl - 1
        lam_init = 0.8 - 0.6 * math.exp(-0.3 * layer)
        ml = mod[layer]
        modl = jnp.concatenate([ml[:b], jnp.zeros((b, 2, d), F32)], axis=1)
        modc = jnp.concatenate([ml[b], jnp.zeros((2, d), F32)], axis=0)

        a16, a32 = _k1(xs, modl, modc, norm1_g[layer][None, :], cos, sa, sb, w_all[layer], ctx_len=cl)
        od = _attn(a16, diff_lambda[layer], diff_subln_g[layer][None, :],
                   ctx_len=cl, seq_len=s, lam_init=lam_init)
        og = _gla(a16, a32, wa[layer], ba[layer], gla_norm_g[layer][None, :], ctx_len=cl)
        xs, h2, rt = _merge(xs, od, og, a32, cw[layer], wdo[layer], wco[layer], wgo[layer],
                                   wo[layer], modl, modc, norm2_g[layer][None, :], rw[layer],
                                   rb[layer], ctx_len=cl)

        r = b * t
        tm_r = rt.shape[-1]
        rt = rt.reshape(-1, 4, TOP_K, tm_r)
        row_src, row_dst, row_gate, block_exp, n_valid, n_slots = _routing(
            rt[:, 0].astype(jnp.int32).reshape(-1), (rt[:, 1] + rt[:, 2] + rt[:, 3]).reshape(-1),
            tm=tm_r, seq=s, ctx_len=cl, latent_only=last)
        y4 = _moe(block_exp, n_valid, row_src, row_dst, row_gate, h2.reshape(r * TOK_ROWS, 128), w1[layer],
                  b1g[layer], b1l[layer], w2[layer], b2[layer], n_slots=n_slots)
        xs = _combine(xs, y4, modl, modc, final_norm_g[None, :], ctx_len=cl, final=last)

    return xs
```

```python
import functools
import math

import numpy as np
import jax
import jax.numpy as jnp
from jax import lax
from jax.experimental import pallas as pl
from jax.experimental.pallas import tpu as pltpu

F32 = jnp.float32
BF16 = jnp.bfloat16

D_MODEL = 1024
GRID_W = 64
RMS_EPS = 1e-6
N_ADA = 6
DIFF_HEADS = 8
DIFF_HEAD_DIM = 64
DIFF_V_DIM = 128
ROPE_BASE = 10000.0
ROT_AXIS_DIM = 32
CONV_W = 512
GLA_HEADS = 4
GLA_DK = 64
GLA_DV = 128
GLA_RANK = 16
GLA_TAU = 16.0
GLA_CHUNK = 64
N_EXPERTS = 32
TOP_K = 4
D_EXPERT = 1024
SWIGLU_LIMIT = 7.0
SWIGLU_ALPHA = 1.702
LOG2_E = 1.4426950408889634

A16_Q, A16_K, A16_V, A16_GV, A16_W = 0, 1024, 2048, 3072, 3584
A32_GATES, A32_CVB, A32_CVC, A32_CVX = 0, 3072, 3584, 4096
A32_GLQ, A32_GLK, A32_GLR, A32_GLA, A32_W = 4608, 4864, 5120, 5632, 5760
W_ALL_COLS = A16_W + A32_W

Q_TILE = 256
KV_SEGMENTS = 2
ATTN_HEADS = 4
MOE_BLOCK = 256
GLA_UNROLL = 2
TOK_ROWS = 8
VMEM_LIMIT = 56 * 1024 * 1024


def _cparams(sem):
    return pltpu.CompilerParams(dimension_semantics=sem, vmem_limit_bytes=VMEM_LIMIT)


def _row_tile(t, target):
    best = None
    for cand in range(16, target + 1, 16):
        if t % cand == 0:
            best = cand
    assert best is not None, t
    return best


def _sigmoid(v):
    return 1.0 / (1.0 + jnp.exp(-v))


def _ada_kernel(cc_ref, w_ref, b_ref, o_ref):
    a = cc_ref[...]
    a = a * _sigmoid(a)
    o_ref[0] = jnp.dot(a, w_ref[0], precision=lax.Precision.HIGHEST,
                       preferred_element_type=F32) + b_ref[0]


def _ada(cc, ada_w, ada_b):
    nl, d, n = ada_w.shape
    tn = 1024
    return pl.pallas_call(
        _ada_kernel,
        grid=(nl, n // tn),
        in_specs=[
            pl.BlockSpec((16, d), lambda l, j: (0, 0)),
            pl.BlockSpec((1, d, tn), lambda l, j: (l, 0, j)),
            pl.BlockSpec((1, 1, tn), lambda l, j: (l, 0, j)),
        ],
        out_specs=pl.BlockSpec((1, 16, tn), lambda l, j: (l, 0, j)),
        out_shape=jax.ShapeDtypeStruct((nl, 16, n), F32),
        compiler_params=_cparams(("parallel", "parallel")),
        name="ada_mod",
    )(cc, ada_w, ada_b.reshape(nl, 1, n))


def _norm_mod(x, ng, shift, scale):
    y = x * lax.rsqrt(jnp.mean(x * x, axis=-1, keepdims=True) + RMS_EPS) * ng
    return y * (1.0 + scale) + shift


def _k1_kernel(x_ref, modl_ref, modc_ref, ng_ref, cos_ref, sa_ref, sb_ref, w_ref,
               o16_ref, o32_ref, *, tm, ctx_len):
    i = pl.program_id(1)
    pos = i * tm + lax.broadcasted_iota(jnp.int32, (tm, 1), 0)
    is_ctx = pos < ctx_len
    shift = jnp.where(is_ctx, modc_ref[0:1, :], modl_ref[0, 0:1, :])
    scale = jnp.where(is_ctx, modc_ref[1:2, :], modl_ref[0, 1:2, :])
    h = _norm_mod(x_ref[0], ng_ref[...], shift, scale).astype(BF16)
    cos = cos_ref[...]
    sa = sa_ref[...]
    sb = sb_ref[...]
    ch = 512
    for c0 in range(0, A16_W, ch):
        acc = jnp.dot(h, w_ref[:, c0:c0 + ch], preferred_element_type=F32)
        if c0 < A16_V:
            parts = []
            for j in range(ch // 128):
                a = acc[:, j * 128:(j + 1) * 128]
                parts.append(a * cos + pltpu.roll(a, 16, 1) * sa + pltpu.roll(a, 112, 1) * sb)
            acc = jnp.concatenate(parts, axis=1)
        if c0 < A16_K:
            acc = acc * LOG2_E
        o16_ref[0, :, c0:c0 + ch] = acc.astype(BF16)
    for c0 in range(0, A32_W, ch):
        c1 = min(c0 + ch, A32_W)
        o32_ref[0, :, c0:c1] = jnp.dot(h, w_ref[:, A16_W + c0:A16_W + c1],
                                       preferred_element_type=F32)


def _k1(x, modl, modc, ng, cos, sa, sb, w_all, *, ctx_len):
    b, t, d = x.shape
    tm = _row_tile(t, 272)
    tbl = pl.BlockSpec((tm, 128), lambda bi, i: (i, 0))
    return pl.pallas_call(
        functools.partial(_k1_kernel, tm=tm, ctx_len=ctx_len),
        grid=(b, t // tm),
        in_specs=[
            pl.BlockSpec((1, tm, d), lambda bi, i: (bi, i, 0)),
            pl.BlockSpec((1, 8, d), lambda bi, i: (bi, 0, 0)),
            pl.BlockSpec((8, d), lambda bi, i: (0, 0)),
            pl.BlockSpec((1, d), lambda bi, i: (0, 0)),
            tbl, tbl, tbl,
            pl.BlockSpec((d, W_ALL_COLS), lambda bi, i: (0, 0), pipeline_mode=pl.Buffered(1)),
        ],
        out_specs=[
            pl.BlockSpec((1, tm, A16_W), lambda bi, i: (bi, i, 0)),
            pl.BlockSpec((1, tm, A32_W), lambda bi, i: (bi, i, 0)),
        ],
        out_shape=[
            jax.ShapeDtypeStruct((b, t, A16_W), BF16),
            jax.ShapeDtypeStruct((b, t, A32_W), F32),
        ],
        compiler_params=_cparams(("parallel", "parallel")),
        name="norm_inproj",
    )(x, modl, modc, ng, cos, sa, sb, w_all)


def _attn_kernel(lam_ref, g_ref, q_ref, k_ref, v_ref, o_ref, *, ctx_len, segments, lam_init):
    qi = pl.program_id(2)
    tq = Q_TILE
    hb = DIFF_V_DIM
    lv = lam_ref[...]
    lam = (jnp.exp(jnp.sum(lv[0:1] * lv[1:2], axis=1, keepdims=True))
           - jnp.exp(jnp.sum(lv[2:3] * lv[3:4], axis=1, keepdims=True)) + lam_init)

    def run(segs):
        qss = []
        for g in range(ATTN_HEADS):
            q = q_ref[0, :, g * hb:(g + 1) * hb]
            lane = lax.broadcasted_iota(jnp.int32, q.shape, 1)
            zero = jnp.zeros_like(q)
            qss.append(jnp.concatenate([jnp.where(lane < DIFF_HEAD_DIM, q, zero),
                                        jnp.where(lane >= DIFF_HEAD_DIM, q, zero)], axis=0))
        carries = [None] * ATTN_HEADS
        for r0, rows in segs:
            for g in range(ATTN_HEADS):
                k_t = k_ref[0, r0:r0 + rows, g * hb:(g + 1) * hb]
                v_t = v_ref[0, r0:r0 + rows, g * hb:(g + 1) * hb]
                s = lax.dot_general(qss[g], k_t, (((1,), (1,)), ((), ())), preferred_element_type=F32)
                smax = jnp.max(s, axis=-1, keepdims=True)
                if carries[g] is None:
                    p = jnp.exp2(s - smax)
                    carries[g] = (smax, jnp.sum(p, axis=-1, keepdims=True),
                                  jnp.dot(p.astype(BF16), v_t, preferred_element_type=F32))
                else:
                    m, l, acc = carries[g]
                    m_new = jnp.maximum(m, smax)
                    alpha = jnp.exp2(m - m_new)
                    p = jnp.exp2(s - m_new)
                    carries[g] = (m_new, alpha * l + jnp.sum(p, axis=-1, keepdims=True),
                                  alpha * acc + jnp.dot(p.astype(BF16), v_t, preferred_element_type=F32))
        for g in range(ATTN_HEADS):
            _, l, acc = carries[g]
            o = acc[:tq] / l[:tq] - lam * (acc[tq:] / l[tq:])
            o = o * lax.rsqrt(jnp.mean(o * o, axis=-1, keepdims=True) + RMS_EPS) * g_ref[...]
            o_ref[0, :, g * hb:(g + 1) * hb] = (o * (1.0 - lam_init)).astype(BF16)

    @pl.when(qi * tq >= ctx_len)
    def _():
        run(segments)

    @pl.when(qi * tq < ctx_len)
    def _():
        run([(0, ctx_len)])


def _attn(a16, lam_p, subln_g, *, ctx_len, seq_len, lam_init):
    b, t, _ = a16.shape
    assert ctx_len % Q_TILE == 0 and t % 256 == 0
    first = (t // 256 + KV_SEGMENTS - 1) // KV_SEGMENTS * 256
    bounds = [min(first * n, t) for n in range(KV_SEGMENTS + 1)]
    segments = [(lo, hi - lo) for lo, hi in zip(bounds[:-1], bounds[1:]) if hi > lo]
    hb = DIFF_V_DIM * ATTN_HEADS
    return pl.pallas_call(
        functools.partial(_attn_kernel, ctx_len=ctx_len, segments=segments, lam_init=lam_init),
        grid=(b, DIFF_HEADS // ATTN_HEADS, t // Q_TILE),
        in_specs=[
            pl.BlockSpec((4, DIFF_HEAD_DIM), lambda bi, h, qi: (0, 0)),
            pl.BlockSpec((1, DIFF_V_DIM), lambda bi, h, qi: (0, 0)),
            pl.BlockSpec((1, Q_TILE, hb), lambda bi, h, qi: (bi, qi, A16_Q // hb + h)),
            pl.BlockSpec((1, t, hb), lambda bi, h, qi: (bi, 0, A16_K // hb + h)),
            pl.BlockSpec((1, t, hb), lambda bi, h, qi: (bi, 0, A16_V // hb + h)),
        ],
        out_specs=pl.BlockSpec((1, Q_TILE, hb), lambda bi, h, qi: (bi, qi, h)),
        out_shape=jax.ShapeDtypeStruct((b, t, DIFF_HEADS * DIFF_V_DIM), BF16),
        compiler_params=_cparams(("parallel", "parallel", "parallel")),
        name="diff_attn",
    )(lam_p, subln_g, a16, a16, a16)


def _split3(x):
    hi = x.astype(BF16)
    r1 = x - hi.astype(F32)
    mid = r1.astype(BF16)
    lo = (r1 - mid.astype(F32)).astype(BF16)
    return jnp.concatenate([hi, mid, lo], axis=1)


def _sum3(y, w):
    return y[:, 0:w] + y[:, w:2 * w] + y[:, 2 * w:3 * w]


def _gla_kernel(q_ref, k_ref, v_ref, a_ref, r_ref, wa_ref, ba_ref, ng_ref, o_ref,
                qd_s, ki_s, ke_s, dec_s, oacc_s, st_s, *, t, ctx_len):
    ck = GLA_CHUNK
    nc = t // ck
    ncc = ctx_len // ck
    pr = 256
    w = 2 * GLA_DK

    ri = lax.broadcasted_iota(jnp.int32, (pr, pr), 0)
    ci = lax.broadcasted_iota(jnp.int32, (pr, pr), 1)
    same = (ri // ck) == (ci // ck)
    m_tot = jnp.where(same, 1.0, 0.0).astype(BF16)
    m_cum = (jnp.where(same & (ci <= ri), 1.0, 0.0).astype(BF16),
             jnp.where(same & (ci >= ri), 1.0, 0.0).astype(BF16))
    for r0 in range(0, t, pr):
        a = a_ref[0, r0:r0 + pr, :].astype(BF16)
        logit = jnp.dot(a, wa_ref[0], preferred_element_type=F32) + ba_ref[0]
        g = (jnp.minimum(logit, 0.0) - jnp.log1p(jnp.exp(-jnp.abs(logit)))) * (1.0 / GLA_TAU)
        q = q_ref[0, r0:r0 + pr, :]
        k = k_ref[0, r0:r0 + pr, :]
        for d in range(2):
            g3 = _split3(g[:, d * w:(d + 1) * w])
            g_cum = _sum3(jnp.dot(m_cum[d], g3, preferred_element_type=F32), w)
            g_tot = _sum3(jnp.dot(m_tot, g3, preferred_element_type=F32), w)
            qd_s[d, r0:r0 + pr, :] = (q * jnp.exp(g_cum)).astype(BF16)
            ki_s[d, r0:r0 + pr, :] = (k * jnp.exp(-g_cum)).astype(BF16)
            ke_s[d, r0:r0 + pr, :] = (k * jnp.exp(g_tot - g_cum)).astype(BF16)
            dec_s[d, r0:r0 + pr, :] = jnp.exp(g_tot)
        oacc_s[r0:r0 + pr, :] = jnp.zeros((pr, 2 * GLA_DV), F32)
    st_s[...] = jnp.zeros_like(st_s)

    ri = lax.broadcasted_iota(jnp.int32, (2 * ck, ck), 0)
    ci = lax.broadcasted_iota(jnp.int32, (2 * ck, ck), 1)
    rr = jnp.where(ri >= ck, ri - ck, ri)
    tri = (ci <= rr, ci >= rr)
    lane = lax.broadcasted_iota(jnp.int32, (ck, w), 1)
    br = lax.broadcasted_iota(jnp.int32, (2 * GLA_DV, w), 0)
    bc = lax.broadcasted_iota(jnp.int32, (2 * GLA_DV, w), 1)
    blk = (br < GLA_DV) == (bc < GLA_DK)
    nt = (((1,), (1,)), ((), ()))
    tn = (((0,), (0,)), ((), ()))

    def chunk(d, r0):
        qd = qd_s[d, pl.ds(r0, ck), :]
        ki = ki_s[d, pl.ds(r0, ck), :]
        ke = ke_s[d, pl.ds(r0, ck), :]
        v = v_ref[0, pl.ds(r0, ck), :]
        dec = dec_s[d, pl.ds(r0, 8), :][0:1, :]
        zq = jnp.zeros_like(qd)
        qs = jnp.concatenate([jnp.where(lane < GLA_DK, qd, zq), jnp.where(lane >= GLA_DK, qd, zq)], axis=0)
        att = lax.dot_general(qs, ki, nt, preferred_element_type=F32)
        att = jnp.where(tri[d], att, 0.0).astype(BF16)
        oi = jnp.dot(att, v, preferred_element_type=F32)
        o_intra = jnp.concatenate([oi[:ck, :GLA_DV], oi[ck:, GLA_DV:]], axis=1)
        st = st_s[d]
        o_inter = lax.dot_general(qd, st.astype(BF16), nt, preferred_element_type=F32)
        s_loc = lax.dot_general(v, ke, tn, preferred_element_type=F32)
        st_s[d] = st * dec + jnp.where(blk, s_loc, 0.0)
        oacc_s[pl.ds(r0, ck), :] = oacc_s[pl.ds(r0, ck), :] + o_intra + o_inter

    def body(n, _):
        chunk(0, pl.multiple_of(n * ck, ck))
        nb = jnp.where(n < ncc, ncc - 1 - n, nc - 1 - (n - ncc))
        chunk(1, pl.multiple_of(nb * ck, ck))
        return 0

    lax.fori_loop(0, nc, body, 0, unroll=GLA_UNROLL)

    ng = ng_ref[...]
    rt = _row_tile(t, 544)
    for r0 in range(0, t, rt):
        o = oacc_s[r0:r0 + rt, :]
        r = r_ref[0, r0:r0 + rt, :]
        outs = []
        for hh in range(2):
            oh = o[:, hh * GLA_DV:(hh + 1) * GLA_DV]
            outs.append(oh * lax.rsqrt(jnp.mean(oh * oh, axis=-1, keepdims=True) + RMS_EPS) * ng)
        o_ref[0, r0:r0 + rt, :] = (jnp.concatenate(outs, axis=1) * (r * _sigmoid(r))).astype(BF16)


def _gla(a16, a32, wa, ba, ng, *, ctx_len):
    b, t, _ = a16.shape
    assert t % 256 == 0 and ctx_len % GLA_CHUNK == 0 and (t // GLA_CHUNK) % GLA_UNROLL == 0
    return pl.pallas_call(
        functools.partial(_gla_kernel, t=t, ctx_len=ctx_len),
        grid=(b, 2),
        in_specs=[
            pl.BlockSpec((1, t, 128), lambda bi, p: (bi, 0, A32_GLQ // 128 + p)),
            pl.BlockSpec((1, t, 128), lambda bi, p: (bi, 0, A32_GLK // 128 + p)),
            pl.BlockSpec((1, t, 256), lambda bi, p: (bi, 0, A16_GV // 256 + p)),
            pl.BlockSpec((1, t, 128), lambda bi, p: (bi, 0, A32_GLA // 128)),
            pl.BlockSpec((1, t, 256), lambda bi, p: (bi, 0, A32_GLR // 256 + p)),
            pl.BlockSpec((1, 128, 256), lambda bi, p: (p, 0, 0)),
            pl.BlockSpec((1, 1, 256), lambda bi, p: (p, 0, 0)),
            pl.BlockSpec((1, GLA_DV), lambda bi, p: (0, 0)),
        ],
        out_specs=pl.BlockSpec((1, t, 256), lambda bi, p: (bi, 0, p)),
        out_shape=jax.ShapeDtypeStruct((b, t, GLA_HEADS * GLA_DV), BF16),
        scratch_shapes=[
            pltpu.VMEM((2, t, 128), BF16), pltpu.VMEM((2, t, 128), BF16), pltpu.VMEM((2, t, 128), BF16),
            pltpu.VMEM((2, t, 128), F32),
            pltpu.VMEM((t, 256), F32),
            pltpu.VMEM((2, 256, 128), F32),
        ],
        compiler_params=_cparams(("parallel", "parallel")),
        name="gla_bidir",
    )(a32, a32, a16, a32, a32, wa, ba, ng)


def _merge_kernel(x_ref, od_ref, og_ref, gates_ref, cvb_ref, cvc_ref, cvx_ref,
                  cvc_p_ref, cvx_p_ref, cvc_n_ref, cvx_n_ref, cw_ref,
                  wdo_ref, wco_ref, wgo_ref, wo_ref, modl_ref, modc_ref, n2g_ref,
                  rw_ref, rb_ref, xo_ref, h2_ref, rt_ref, *, tm, ctx_len, t):
    i = pl.program_id(1)
    pos = i * tm + lax.broadcasted_iota(jnp.int32, (tm, 1), 0)
    is_ctx = pos < ctx_len
    row = lax.broadcasted_iota(jnp.int32, (tm, 1), 0)

    z = cvc_ref[0] * cvx_ref[0]
    z_before = cvc_p_ref[0, 7:8, :] * cvx_p_ref[0, 7:8, :]
    z_after = cvc_n_ref[0, 0:1, :] * cvx_n_ref[0, 0:1, :]
    z_prev = jnp.where(row == 0, z_before, pltpu.roll(z, 1, 0))
    z_prev = jnp.where((pos == 0) | (pos == ctx_len), 0.0, z_prev)
    z_next = jnp.where(row == tm - 1, z_after, pltpu.roll(z, tm - 1, 0))
    z_next = jnp.where((pos == ctx_len - 1) | (pos == t - 1), 0.0, z_next)
    conv = z_prev * cw_ref[0:1, :] + z * cw_ref[1:2, :] + z_next * cw_ref[2:3, :]
    zc = (cvb_ref[0] * conv).astype(BF16)

    y_diff = jnp.dot(od_ref[0], wdo_ref[...], preferred_element_type=F32)
    y_conv = jnp.dot(zc, wco_ref[...], preferred_element_type=F32)
    y_gla = jnp.dot(og_ref[0], wgo_ref[...], preferred_element_type=F32)
    d = D_MODEL
    mix = (_sigmoid(gates_ref[0, :, 0:d]) * y_diff
           + _sigmoid(gates_ref[0, :, d:2 * d]) * y_conv
           + _sigmoid(gates_ref[0, :, 2 * d:3 * d]) * y_gla)
    m = jnp.dot(mix.astype(BF16), wo_ref[...], preferred_element_type=F32)

    g1 = jnp.where(is_ctx, modc_ref[2:3, :], modl_ref[0, 2:3, :])
    x_new = x_ref[0] + g1 * m
    xo_ref[0] = x_new
    sh2 = jnp.where(is_ctx, modc_ref[3:4, :], modl_ref[0, 3:4, :])
    sc2 = jnp.where(is_ctx, modc_ref[4:5, :], modl_ref[0, 4:5, :])
    h2 = _norm_mod(x_new, n2g_ref[...], sh2, sc2)
    for cc in range(d // 128):
        h2_ref[0, pl.ds(cc, tm, stride=8), :] = h2[:, cc * 128:(cc + 1) * 128]

    logits = jnp.dot(h2.astype(BF16), rw_ref[...], preferred_element_type=F32) + rb_ref[...]
    lane = lax.broadcasted_iota(jnp.int32, logits.shape, 1)
    vals = logits
    tops = []
    idx_out = jnp.zeros(logits.shape, jnp.int32)
    for kk in range(TOP_K):
        mx = jnp.max(vals, axis=-1, keepdims=True)
        ix = jnp.min(jnp.where(vals == mx, lane, 128), axis=-1, keepdims=True)
        tops.append(mx)
        idx_out = jnp.where(lane == kk, ix, idx_out)
        vals = jnp.where(lane == ix, -jnp.inf, vals)
    es = [jnp.exp(v - tops[0]) for v in tops]
    den = es[0] + es[1] + es[2] + es[3]
    packed = idx_out.astype(F32)
    for kk in range(TOP_K):
        gk = es[kk] / den
        hi = gk.astype(BF16).astype(F32)
        mid = (gk - hi).astype(BF16).astype(F32)
        lo = gk - hi - mid
        for piece, val in enumerate((hi, mid, lo)):
            packed = jnp.where(lane == TOP_K * (piece + 1) + kk, val, packed)
    ri = lax.broadcasted_iota(jnp.int32, (tm, tm), 0)
    ci = lax.broadcasted_iota(jnp.int32, (tm, tm), 1)
    eye = jnp.where(ri == ci, 1.0, 0.0).astype(BF16)
    rows = lax.dot_general(packed.astype(BF16), eye, (((0,), (0,)), ((), ())), preferred_element_type=F32)
    rt_ref[0, 0] = rows[0:4 * TOP_K, :]


def _merge(x, od, og, a32, cw, wdo, wco, wgo, wo, modl, modc, n2g, rw, rb, *, ctx_len):
    b, t, d = x.shape
    tm = _row_tile(t, 272)
    nt8 = t // 8
    row = lambda bi, i: (bi, i, 0)
    const2 = lambda bi, i: (0, 0)
    return pl.pallas_call(
        functools.partial(_merge_kernel, tm=tm, ctx_len=ctx_len, t=t),
        grid=(b, t // tm),
        in_specs=[
            pl.BlockSpec((1, tm, d), row),
            pl.BlockSpec((1, tm, d), row),
            pl.BlockSpec((1, tm, GLA_HEADS * GLA_DV), row),
            pl.BlockSpec((1, tm, 3 * d), lambda bi, i: (bi, i, 0)),
            pl.BlockSpec((1, tm, CONV_W), lambda bi, i: (bi, i, A32_CVB // CONV_W)),
            pl.BlockSpec((1, tm, CONV_W), lambda bi, i: (bi, i, A32_CVC // CONV_W)),
            pl.BlockSpec((1, tm, CONV_W), lambda bi, i: (bi, i, A32_CVX // CONV_W)),
            pl.BlockSpec((1, 8, CONV_W), lambda bi, i: (bi, jnp.maximum(i * (tm // 8) - 1, 0),
                                                   A32_CVC // CONV_W)),
            pl.BlockSpec((1, 8, CONV_W), lambda bi, i: (bi, jnp.maximum(i * (tm // 8) - 1, 0),
                                                   A32_CVX // CONV_W)),
            pl.BlockSpec((1, 8, CONV_W), lambda bi, i: (bi, jnp.minimum((i + 1) * (tm // 8), nt8 - 1),
                                                   A32_CVC // CONV_W)),
            pl.BlockSpec((1, 8, CONV_W), lambda bi, i: (bi, jnp.minimum((i + 1) * (tm // 8), nt8 - 1),
                                                   A32_CVX // CONV_W)),
            pl.BlockSpec((8, CONV_W), const2),
            pl.BlockSpec((d, d), const2),
            pl.BlockSpec((CONV_W, d), const2),
            pl.BlockSpec((GLA_HEADS * GLA_DV, d), const2),
            pl.BlockSpec((d, d), const2),
            pl.BlockSpec((1, 8, d), lambda bi, i: (bi, 0, 0)),
            pl.BlockSpec((8, d), const2),
            pl.BlockSpec((1, d), const2),
            pl.BlockSpec((d, 128), const2),
            pl.BlockSpec((1, 128), const2),
        ],
        out_specs=[
            pl.BlockSpec((1, tm, d), row),
            pl.BlockSpec((1, tm * (d // 128), 128), row),
            pl.BlockSpec((1, 1, 4 * TOP_K, tm), lambda bi, i: (bi, i, 0, 0)),
        ],
        out_shape=[
            jax.ShapeDtypeStruct((b, t, d), F32),
            jax.ShapeDtypeStruct((b, t * (d // 128), 128), F32),
            jax.ShapeDtypeStruct((b, t // tm, 4 * TOP_K, tm), F32),
        ],
        compiler_params=_cparams(("parallel", "parallel")),
        name="merge_router",
    )(x, od, og, a32, a32, a32, a32, a32, a32, a32, a32, cw, wdo, wco, wgo, wo,
      modl, modc, n2g, rw, rb)


def _moe_kernel(be_ref, nv_ref, g_ref, w1g_ref, w1l_ref, b1g_ref, b1l_ref, w2_ref, b2_ref,
                src_hbm, dst_hbm, h2_hbm, out_hbm, src_s, dst_s, xbuf, ybuf, sem_p, sem_g, sem_s,
                *, nb, n_slots):
    i = pl.program_id(0)
    nv = nv_ref[0]
    bm = MOE_BLOCK
    cur = lax.rem(i, 2)
    nxt = 1 - cur

    def idx_copies(blk_src, blk_dst, slot):
        return (pltpu.make_async_copy(src_hbm.at[blk_src, 0], src_s.at[pl.ds(slot * bm, bm)], sem_p),
                pltpu.make_async_copy(dst_hbm.at[blk_dst, 0], dst_s.at[pl.ds(slot * bm, bm)], sem_p))

    def start_gather(slot, buf):
        for j in range(bm):
            r0 = pl.multiple_of(src_s[slot * bm + j], TOK_ROWS)
            pltpu.make_async_copy(h2_hbm.at[pl.ds(r0, TOK_ROWS)],
                                  xbuf.at[buf, pl.ds(j * TOK_ROWS, TOK_ROWS)],
                                  sem_g.at[buf]).start(priority=j % 2)

    def wait_gather(buf):
        pltpu.make_async_copy(h2_hbm.at[pl.ds(0, bm * TOK_ROWS)], xbuf.at[buf], sem_g.at[buf]).wait()

    def wait_scatter():
        pltpu.make_async_copy(ybuf.at[0], out_hbm.at[pl.ds(0, bm * TOK_ROWS)], sem_s).wait()

    @pl.when(i <= nv)
    def _():
        @pl.when(i == 0)
        def _():
            ybuf[...] = jnp.zeros_like(ybuf)
            for blk_src, blk_dst, slot in ((0, 0, 0), (1, 1, 1), (0, nb, 3)):
                for cp in idx_copies(blk_src, blk_dst, slot):
                    cp.start()
                    cp.wait()
            start_gather(0, 0)

        @pl.when(i >= 1)
        def _():
            for cp in idx_copies(0, 0, 0):
                cp.wait()
            wait_scatter()

        wait_gather(cur)

        @pl.when(i < nv)
        def _():
            blk = jnp.minimum(i + 2, nv - 1)
            for cp in idx_copies(blk, blk, lax.rem(i + 2, 4)):
                cp.start()

        start_gather(lax.rem(i + 1, 4), nxt)
        pslot = lax.rem(i + 3, 4)
        for j in range(bm):
            r0 = pl.multiple_of(dst_s[pslot * bm + j], TOK_ROWS)
            pltpu.make_async_copy(ybuf.at[nxt, pl.ds(j * TOK_ROWS, TOK_ROWS)],
                                  out_hbm.at[pl.ds(r0, TOK_ROWS)], sem_s).start(priority=j % 2)
        x = jnp.concatenate([xbuf[cur, pl.ds(cc, bm, stride=TOK_ROWS), :] for cc in range(TOK_ROWS)],
                            axis=1).astype(BF16)
        hg = jnp.dot(x, w1g_ref[0, 0], preferred_element_type=F32) + b1g_ref[0]
        hl = jnp.dot(x, w1l_ref[0, 0], preferred_element_type=F32) + b1l_ref[0]
        hg = jnp.minimum(hg, SWIGLU_LIMIT)
        hl = jnp.clip(hl, -SWIGLU_LIMIT, SWIGLU_LIMIT)
        act = hg * _sigmoid(SWIGLU_ALPHA * hg) * (hl + 1.0)
        y = jnp.dot(act.astype(BF16), w2_ref[0], preferred_element_type=F32) + b2_ref[0]
        gi = lax.broadcasted_iota(jnp.int32, (bm, bm), 0)
        gj = lax.broadcasted_iota(jnp.int32, (bm, bm), 1)
        y = y * jnp.sum(jnp.where(gi == gj, g_ref[0], 0.0), axis=1, keepdims=True)
        for cc in range(TOK_ROWS):
            ybuf[cur, pl.ds(cc, bm, stride=TOK_ROWS), :] = y[:, cc * 128:(cc + 1) * 128]

        @pl.when(i == nv)
        def _():
            wait_scatter()
            wait_gather(nxt)
            ybuf[...] = jnp.zeros_like(ybuf)
            fills = [pltpu.make_async_copy(
                ybuf.at[sl], out_hbm.at[pl.ds((n_slots + sl * bm) * TOK_ROWS, bm * TOK_ROWS)], sem_s)
                for sl in range(2)]
            for cp in fills:
                cp.start()
            for cp in fills:
                cp.wait()


def _moe(block_exp, n_valid, row_src, row_dst, row_gate, h2_flat, w1, b1g, b1l, w2, b2, *, n_slots):
    d = D_MODEL
    nb = row_src.shape[0]
    bm = MOE_BLOCK
    de = w1.shape[-1]
    blk = lambda i, nv: jnp.minimum(i, nv[0] - 1)
    wmap = lambda i, be, nv: (be[blk(i, nv)], 0, 0)
    return pl.pallas_call(
        functools.partial(_moe_kernel, nb=nb, n_slots=n_slots),
        grid_spec=pltpu.PrefetchScalarGridSpec(
            num_scalar_prefetch=2,
            grid=(nb + 1,),
            in_specs=[
                pl.BlockSpec((1, 1, bm), lambda i, be, nv: (blk(i, nv), 0, 0)),
                pl.BlockSpec((1, 1, d, de), lambda i, be, nv: (be[blk(i, nv)], 0, 0, 0)),
                pl.BlockSpec((1, 1, d, de), lambda i, be, nv: (be[blk(i, nv)], 1, 0, 0)),
                pl.BlockSpec((1, 1, de), wmap),
                pl.BlockSpec((1, 1, de), wmap),
                pl.BlockSpec((1, de, d), wmap),
                pl.BlockSpec((1, 1, d), wmap),
                pl.BlockSpec(memory_space=pl.ANY),
                pl.BlockSpec(memory_space=pl.ANY),
                pl.BlockSpec(memory_space=pl.ANY),
            ],
            out_specs=pl.BlockSpec(memory_space=pl.ANY),
            scratch_shapes=[
                pltpu.SMEM((4 * bm,), jnp.int32),
                pltpu.SMEM((4 * bm,), jnp.int32),
                pltpu.VMEM((2, bm * TOK_ROWS, 128), F32),
                pltpu.VMEM((2, bm * TOK_ROWS, 128), F32),
                pltpu.SemaphoreType.DMA,
                pltpu.SemaphoreType.DMA((2,)),
                pltpu.SemaphoreType.DMA,
            ],
        ),
        out_shape=jax.ShapeDtypeStruct(((n_slots + 2 * bm) * TOK_ROWS, 128), F32),
        compiler_params=_cparams(("arbitrary",)),
        name="moe_experts",
    )(block_exp, n_valid, row_gate, w1, w1, b1g, b1l, w2, b2, row_src, row_dst, h2_flat)


def _combine_kernel(x_ref, y4_ref, modl_ref, modc_ref, fg_ref, o_ref, *, tm, ctx_len, final, skip):
    i = pl.program_id(1) + skip
    d = D_MODEL
    ld = lambda k, cc: y4_ref[pl.ds(k * TOK_ROWS + cc, tm, stride=TOP_K * TOK_ROWS), :]
    f = jnp.concatenate([(ld(0, cc) + ld(1, cc)) + (ld(2, cc) + ld(3, cc)) for cc in range(TOK_ROWS)],
                        axis=1)
    pos = i * tm + lax.broadcasted_iota(jnp.int32, (tm, 1), 0)
    g2 = jnp.where(pos < ctx_len, modc_ref[5:6, :], modl_ref[0, 5:6, :])
    x_new = x_ref[0] + g2 * f
    if final:
        x_new = x_new * lax.rsqrt(jnp.mean(x_new * x_new, axis=-1, keepdims=True) + RMS_EPS) * fg_ref[...]
    o_ref[0] = x_new


def _combine(x, y4, modl, modc, fg, *, ctx_len, final):
    b, t, d = x.shape
    if final:
        tm = Q_TILE
        assert ctx_len % tm == 0
        skip = ctx_len // tm
    else:
        tm = _row_tile(t, 272)
        skip = 0
    npb = t // tm
    return pl.pallas_call(
        functools.partial(_combine_kernel, tm=tm, ctx_len=ctx_len, final=final, skip=skip),
        grid=(b, npb - skip),
        in_specs=[
            pl.BlockSpec((1, tm, d), lambda bi, i: (bi, i + skip, 0)),
            pl.BlockSpec((tm * TOP_K * TOK_ROWS, 128), lambda bi, i: (bi * (npb - skip) + i, 0)),
            pl.BlockSpec((1, 8, d), lambda bi, i: (bi, 0, 0)),
            pl.BlockSpec((8, d), lambda bi, i: (0, 0)),
            pl.BlockSpec((1, d), lambda bi, i: (0, 0)),
        ],
        out_specs=pl.BlockSpec((1, tm, d), lambda bi, i: (bi, i, 0)),
        out_shape=jax.ShapeDtypeStruct((b, t - skip * tm, d), F32),
        compiler_params=_cparams(("parallel", "parallel")),
        name="moe_combine",
    )(x, y4, modl, modc, fg)


def _routing(e_flat, gate_flat, *, tm, seq, ctx_len, latent_only):
    npair = e_flat.shape[0]
    t = seq + ctx_len
    bm = MOE_BLOCK
    nb = npair // bm + N_EXPERTS
    nr = nb * bm
    npad = nr - npair
    experts = jnp.arange(N_EXPERTS, dtype=jnp.int32)
    ids = jnp.arange(npair, dtype=jnp.int32)
    tok_of = lambda i: (i // (TOP_K * tm)) * tm + i % tm
    if latent_only:
        e_flat = jnp.where(tok_of(ids) % t >= ctx_len, e_flat, N_EXPERTS)
    counts = jnp.sum((e_flat[:, None] == experts[None, :]).astype(jnp.int32), axis=0)
    padded = (counts + bm - 1) // bm * bm
    pad_end = jnp.cumsum(padded)
    fill_end = jnp.cumsum(padded - counts)
    dummy = jnp.arange(npad, dtype=jnp.int32)
    e_dummy = jnp.sum((dummy[:, None] >= fill_end[None, :]).astype(jnp.int32), axis=1)
    kb = nr
    keys = jnp.concatenate([e_flat * kb + ids, e_dummy * kb + npair + dummy])
    gates = jnp.concatenate([gate_flat, jnp.zeros((npad,), F32)])
    skey, sgate = lax.sort((keys, gates), num_keys=1)
    e_row = skey // kb
    ident = skey - e_row * kb
    valid = (ident < npair) & (e_row < N_EXPERTS)
    pair = jnp.where(valid, ident, 0)
    tok = tok_of(pair)
    kk = (pair // tm) % TOP_K
    row_gate = jnp.where(valid, sgate, 0.0).reshape(nb, 1, bm)
    row_src = (tok * TOK_ROWS).reshape(nb, 1, bm)
    if latent_only:
        slot = (tok // t) * seq + tok % t - ctx_len
        n_slots = (npair // TOP_K // t) * seq * TOP_K
    else:
        slot = tok
        n_slots = npair
    j = jnp.arange(nr, dtype=jnp.int32)
    dump = n_slots + ((j // bm) % 2) * bm + j % bm
    row_dst = jnp.where(valid, slot * TOP_K + kk, dump).reshape(nb, bm)
    row_dst = jnp.concatenate([row_dst, (n_slots + bm + jnp.arange(bm, dtype=jnp.int32))[None, :]],
                              axis=0)[:, None, :] * TOK_ROWS
    blocks = jnp.arange(nb, dtype=jnp.int32) * bm
    block_exp = jnp.minimum(jnp.sum((blocks[:, None] >= pad_end[None, :]).astype(jnp.int32), axis=1),
                            N_EXPERTS - 1)
    n_valid = (pad_end[-1] // bm).astype(jnp.int32).reshape(1)
    return row_src, row_dst, row_gate, block_exp, n_valid, n_slots


def _rope_tables(ctx_len, seq_len):
    inv_freq = ROPE_BASE ** (-jnp.arange(0, ROT_AXIS_DIM, 2, dtype=F32) / ROT_AXIS_DIM)
    p = jnp.arange(seq_len, dtype=jnp.int32)
    row = (p // GRID_W).astype(F32)
    col = (p % GRID_W).astype(F32)
    lane = np.arange(128)
    dd = lane % DIFF_HEAD_DIM
    fidx = jnp.asarray((dd % ROT_AXIS_DIM) % (ROT_AXIS_DIM // 2))
    use_col = jnp.asarray(dd >= ROT_AXIS_DIM)
    second_half = jnp.asarray((dd % ROT_AXIS_DIM) >= ROT_AXIS_DIM // 2)
    posm = jnp.where(use_col[None, :], col[:, None], row[:, None])
    ang = posm * inv_freq[fidx][None, :]
    cos = jnp.cos(ang)
    sin = jnp.sin(ang)
    sa = jnp.where(second_half[None, :], sin, 0.0)
    sb = jnp.where(second_half[None, :], 0.0, -sin)
    pad = lambda a, v: jnp.concatenate([jnp.full((ctx_len, 128), v, F32), a], axis=0)
    return pad(cos, 1.0), pad(sa, 0.0), pad(sb, 0.0)


def kernel(x, c, ctx, c_ctx, ada_w, ada_b, norm1_g, norm2_g, w_in, diff_lambda, diff_subln_g,
           diff_w_out, conv_w, conv_w_out, gla_w_a2, gla_b_a, gla_norm_g, gla_w_out, w_o,
           router_w, router_b, moe_w1, moe_b1, moe_w2, moe_b2, final_norm_g):
    b, s, d = x.shape
    cl = ctx.shape[1]
    t = cl + s
    nl = w_in.shape[0]
    assert d == D_MODEL and b <= 15

    xs = jnp.concatenate([ctx, x], axis=1)
    cc = jnp.concatenate([c, c_ctx[None, :], jnp.zeros((15 - b, d), F32)], axis=0)
    mod = _ada(cc, ada_w, ada_b).reshape(nl, 16, N_ADA, d)
    cos, sa, sb = _rope_tables(cl, s)

    sp = np.cumsum([1024, 1024, 1024, 512, 512, 512, 256, 256, 512, 512, 32, 3072])
    (w_q, w_k, w_v, w_cb, w_cc, w_cx, w_gq, w_gk, w_gv, w_gr, w_ga, w_gt) = jnp.split(
        w_in, [int(v) for v in sp[:-1]], axis=-1)
    w_all = jnp.concatenate(
        [w_q * (DIFF_HEAD_DIM ** -0.5), w_k, w_v, w_gv,
         w_gt, w_cb, w_cc, w_cx, w_gq * (GLA_DK ** -0.5), w_gk, w_gr, w_ga,
         jnp.zeros((nl, d, A32_W - A32_GLA - 2 * GLA_RANK), F32)], axis=-1).astype(BF16)

    wa = jnp.zeros((nl, 2, 128, 256), F32)
    for p in range(2):
        wa = wa.at[:, p, 0:GLA_RANK, 0:128].set(gla_w_a2[:, 0, :, p * 128:(p + 1) * 128])
        wa = wa.at[:, p, GLA_RANK:2 * GLA_RANK, 128:256].set(gla_w_a2[:, 1, :, p * 128:(p + 1) * 128])
    wa = wa.astype(BF16)
    ba = jnp.stack([jnp.concatenate([gla_b_a[:, 0, p * 128:(p + 1) * 128],
                                     gla_b_a[:, 1, p * 128:(p + 1) * 128]], axis=-1)
                    for p in range(2)], axis=1)[:, :, None, :]

    cw = jnp.concatenate([conv_w, jnp.zeros((nl, 5, CONV_W), F32)], axis=1)
    rw = jnp.concatenate([router_w, jnp.zeros((nl, d, 128 - N_EXPERTS), F32)], axis=-1).astype(BF16)
    rb = jnp.concatenate([router_b, jnp.full((nl, 128 - N_EXPERTS), -1e30, F32)], axis=-1)[:, None, :]
    w1 = jnp.moveaxis(moe_w1.reshape(nl, N_EXPERTS, d, D_EXPERT, 2), -1, 2).astype(BF16)
    b1 = moe_b1.reshape(nl, N_EXPERTS, 1, D_EXPERT, 2)
    b1g = b1[..., 0]
    b1l = b1[..., 1]
    w2 = moe_w2.astype(BF16)
    b2 = moe_b2[:, :, None, :]
    wdo = diff_w_out.astype(BF16)
    wco = conv_w_out.astype(BF16)
    wgo = gla_w_out.astype(BF16)
    wo = w_o.astype(BF16)

    for layer in range(nl):
        last = layer == nl - 1
        lam_init = 0.8 - 0.6 * math.exp(-0.3 * layer)
        ml = mod[layer]
        modl = jnp.concatenate([ml[:b], jnp.zeros((b, 2, d), F32)], axis=1)
        modc = jnp.concatenate([ml[b], jnp.zeros((2, d), F32)], axis=0)

        a16, a32 = _k1(xs, modl, modc, norm1_g[layer][None, :], cos, sa, sb, w_all[layer], ctx_len=cl)
        od = _attn(a16, diff_lambda[layer], diff_subln_g[layer][None, :],
                   ctx_len=cl, seq_len=s, lam_init=lam_init)
        og = _gla(a16, a32, wa[layer], ba[layer], gla_norm_g[layer][None, :], ctx_len=cl)
        xs, h2, rt = _merge(xs, od, og, a32, cw[layer], wdo[layer], wco[layer], wgo[layer],
                                   wo[layer], modl, modc, norm2_g[layer][None, :], rw[layer],
                                   rb[layer], ctx_len=cl)

        r = b * t
        tm_r = rt.shape[-1]
        rt = rt.reshape(-1, 4, TOP_K, tm_r)
        row_src, row_dst, row_gate, block_exp, n_valid, n_slots = _routing(
            rt[:, 0].astype(jnp.int32).reshape(-1), (rt[:, 1] + rt[:, 2] + rt[:, 3]).reshape(-1),
            tm=tm_r, seq=s, ctx_len=cl, latent_only=last)
        y4 = _moe(block_exp, n_valid, row_src, row_dst, row_gate, h2.reshape(r * TOK_ROWS, 128), w1[layer],
                  b1g[layer], b1l[layer], w2[layer], b2[layer], n_slots=n_slots)
        xs = _combine(xs, y4, modl, modc, final_norm_g[None, :], ctx_len=cl, final=last)

    return xs
```

```python
import functools
import math

import numpy as np
import jax
import jax.numpy as jnp
from jax import lax
from jax.experimental import pallas as pl
from jax.experimental.pallas import tpu as pltpu

F32 = jnp.float32
BF16 = jnp.bfloat16

D_MODEL = 1024
GRID_W = 64
RMS_EPS = 1e-6
N_ADA = 6
DIFF_HEADS = 8
DIFF_HEAD_DIM = 64
DIFF_V_DIM = 128
ROPE_BASE = 10000.0
ROT_AXIS_DIM = 32
CONV_W = 512
GLA_HEADS = 4
GLA_DK = 64
GLA_DV = 128
GLA_RANK = 16
GLA_TAU = 16.0
GLA_CHUNK = 64
N_EXPERTS = 32
TOP_K = 4
D_EXPERT = 1024
SWIGLU_LIMIT = 7.0
SWIGLU_ALPHA = 1.702
LOG2_E = 1.4426950408889634

A16_Q, A16_K, A16_V, A16_GV, A16_W = 0, 1024, 2048, 3072, 3584
A32_GATES, A32_CVB, A32_CVC, A32_CVX = 0, 3072, 3584, 4096
A32_GLQ, A32_GLK, A32_GLR, A32_GLA, A32_W = 4608, 4864, 5120, 5632, 5760
W_ALL_COLS = A16_W + A32_W

Q_TILE = 256
KV_SEGMENTS = 2
ATTN_HEADS = 4
MOE_BLOCK = 256
GLA_UNROLL = 2
MOE_DMA_GROUPS = 4
TOK_ROWS = 8
VMEM_LIMIT = 56 * 1024 * 1024


def _cparams(sem):
    return pltpu.CompilerParams(dimension_semantics=sem, vmem_limit_bytes=VMEM_LIMIT)


def _row_tile(t, target):
    best = None
    for cand in range(16, target + 1, 16):
        if t % cand == 0:
            best = cand
    assert best is not None, t
    return best


def _sigmoid(v):
    return 1.0 / (1.0 + jnp.exp(-v))


def _ada_kernel(cc_ref, w_ref, b_ref, o_ref):
    a = cc_ref[...]
    a = a * _sigmoid(a)
    o_ref[0] = jnp.dot(a, w_ref[0], precision=lax.Precision.HIGHEST,
                       preferred_element_type=F32) + b_ref[0]


def _ada(cc, ada_w, ada_b):
    nl, d, n = ada_w.shape
    tn = 1024
    return pl.pallas_call(
        _ada_kernel,
        grid=(nl, n // tn),
        in_specs=[
            pl.BlockSpec((16, d), lambda l, j: (0, 0)),
            pl.BlockSpec((1, d, tn), lambda l, j: (l, 0, j)),
            pl.BlockSpec((1, 1, tn), lambda l, j: (l, 0, j)),
        ],
        out_specs=pl.BlockSpec((1, 16, tn), lambda l, j: (l, 0, j)),
        out_shape=jax.ShapeDtypeStruct((nl, 16, n), F32),
        compiler_params=_cparams(("parallel", "parallel")),
        name="ada_mod",
    )(cc, ada_w, ada_b.reshape(nl, 1, n))


def _norm_mod(x, ng, shift, scale):
    y = x * lax.rsqrt(jnp.mean(x * x, axis=-1, keepdims=True) + RMS_EPS) * ng
    return y * (1.0 + scale) + shift


def _k1_kernel(x_ref, modl_ref, modc_ref, ng_ref, cos_ref, sa_ref, sb_ref, w_ref,
               o16_ref, o32_ref, *, tm, ctx_len):
    i = pl.program_id(1)
    pos = i * tm + lax.broadcasted_iota(jnp.int32, (tm, 1), 0)
    is_ctx = pos < ctx_len
    shift = jnp.where(is_ctx, modc_ref[0:1, :], modl_ref[0, 0:1, :])
    scale = jnp.where(is_ctx, modc_ref[1:2, :], modl_ref[0, 1:2, :])
    h = _norm_mod(x_ref[0], ng_ref[...], shift, scale).astype(BF16)
    cos = cos_ref[...]
    sa = sa_ref[...]
    sb = sb_ref[...]
    ch = 512
    for c0 in range(0, A16_W, ch):
        acc = jnp.dot(h, w_ref[:, c0:c0 + ch], preferred_element_type=F32)
        if c0 < A16_V:
            parts = []
            for j in range(ch // 128):
                a = acc[:, j * 128:(j + 1) * 128]
                parts.append(a * cos + pltpu.roll(a, 16, 1) * sa + pltpu.roll(a, 112, 1) * sb)
            acc = jnp.concatenate(parts, axis=1)
        if c0 < A16_K:
            acc = acc * LOG2_E
        o16_ref[0, :, c0:c0 + ch] = acc.astype(BF16)
    for c0 in range(0, A32_W, ch):
        c1 = min(c0 + ch, A32_W)
        o32_ref[0, :, c0:c1] = jnp.dot(h, w_ref[:, A16_W + c0:A16_W + c1],
                                       preferred_element_type=F32)


def _k1(x, modl, modc, ng, cos, sa, sb, w_all, *, ctx_len):
    b, t, d = x.shape
    tm = _row_tile(t, 272)
    tbl = pl.BlockSpec((tm, 128), lambda bi, i: (i, 0))
    return pl.pallas_call(
        functools.partial(_k1_kernel, tm=tm, ctx_len=ctx_len),
        grid=(b, t // tm),
        in_specs=[
            pl.BlockSpec((1, tm, d), lambda bi, i: (bi, i, 0)),
            pl.BlockSpec((1, 8, d), lambda bi, i: (bi, 0, 0)),
            pl.BlockSpec((8, d), lambda bi, i: (0, 0)),
            pl.BlockSpec((1, d), lambda bi, i: (0, 0)),
            tbl, tbl, tbl,
            pl.BlockSpec((d, W_ALL_COLS), lambda bi, i: (0, 0), pipeline_mode=pl.Buffered(1)),
        ],
        out_specs=[
            pl.BlockSpec((1, tm, A16_W), lambda bi, i: (bi, i, 0)),
            pl.BlockSpec((1, tm, A32_W), lambda bi, i: (bi, i, 0)),
        ],
        out_shape=[
            jax.ShapeDtypeStruct((b, t, A16_W), BF16),
            jax.ShapeDtypeStruct((b, t, A32_W), F32),
        ],
        compiler_params=_cparams(("parallel", "parallel")),
        name="norm_inproj",
    )(x, modl, modc, ng, cos, sa, sb, w_all)


def _attn_kernel(lam_ref, g_ref, q_ref, k_ref, v_ref, o_ref, *, ctx_len, segments, lam_init):
    qi = pl.program_id(2)
    tq = Q_TILE
    hb = DIFF_V_DIM
    lv = lam_ref[...]
    lam = (jnp.exp(jnp.sum(lv[0:1] * lv[1:2], axis=1, keepdims=True))
           - jnp.exp(jnp.sum(lv[2:3] * lv[3:4], axis=1, keepdims=True)) + lam_init)

    def run(segs):
        qss = []
        for g in range(ATTN_HEADS):
            q = q_ref[0, :, g * hb:(g + 1) * hb]
            lane = lax.broadcasted_iota(jnp.int32, q.shape, 1)
            zero = jnp.zeros_like(q)
            qss.append(jnp.concatenate([jnp.where(lane < DIFF_HEAD_DIM, q, zero),
                                        jnp.where(lane >= DIFF_HEAD_DIM, q, zero)], axis=0))
        carries = [None] * ATTN_HEADS
        for r0, rows in segs:
            for g in range(ATTN_HEADS):
                k_t = k_ref[0, r0:r0 + rows, g * hb:(g + 1) * hb]
                v_t = v_ref[0, r0:r0 + rows, g * hb:(g + 1) * hb]
                s = lax.dot_general(qss[g], k_t, (((1,), (1,)), ((), ())), preferred_element_type=F32)
                smax = jnp.max(s, axis=-1, keepdims=True)
                if carries[g] is None:
                    p = jnp.exp2(s - smax)
                    carries[g] = (smax, jnp.sum(p, axis=-1, keepdims=True),
                                  jnp.dot(p.astype(BF16), v_t, preferred_element_type=F32))
                else:
                    m, l, acc = carries[g]
                    m_new = jnp.maximum(m, smax)
                    alpha = jnp.exp2(m - m_new)
                    p = jnp.exp2(s - m_new)
                    carries[g] = (m_new, alpha * l + jnp.sum(p, axis=-1, keepdims=True),
                                  alpha * acc + jnp.dot(p.astype(BF16), v_t, preferred_element_type=F32))
        for g in range(ATTN_HEADS):
            _, l, acc = carries[g]
            o = acc[:tq] / l[:tq] - lam * (acc[tq:] / l[tq:])
            o = o * lax.rsqrt(jnp.mean(o * o, axis=-1, keepdims=True) + RMS_EPS) * g_ref[...]
            o_ref[0, :, g * hb:(g + 1) * hb] = (o * (1.0 - lam_init)).astype(BF16)

    @pl.when(qi * tq >= ctx_len)
    def _():
        run(segments)

    @pl.when(qi * tq < ctx_len)
    def _():
        run([(0, ctx_len)])


def _attn(a16, lam_p, subln_g, *, ctx_len, seq_len, lam_init):
    b, t, _ = a16.shape
    assert ctx_len % Q_TILE == 0 and t % 256 == 0
    first = (t // 256 + KV_SEGMENTS - 1) // KV_SEGMENTS * 256
    bounds = [min(first * n, t) for n in range(KV_SEGMENTS + 1)]
    segments = [(lo, hi - lo) for lo, hi in zip(bounds[:-1], bounds[1:]) if hi > lo]
    hb = DIFF_V_DIM * ATTN_HEADS
    return pl.pallas_call(
        functools.partial(_attn_kernel, ctx_len=ctx_len, segments=segments, lam_init=lam_init),
        grid=(b, DIFF_HEADS // ATTN_HEADS, t // Q_TILE),
        in_specs=[
            pl.BlockSpec((4, DIFF_HEAD_DIM), lambda bi, h, qi: (0, 0)),
            pl.BlockSpec((1, DIFF_V_DIM), lambda bi, h, qi: (0, 0)),
            pl.BlockSpec((1, Q_TILE, hb), lambda bi, h, qi: (bi, qi, A16_Q // hb + h)),
            pl.BlockSpec((1, t, hb), lambda bi, h, qi: (bi, 0, A16_K // hb + h)),
            pl.BlockSpec((1, t, hb), lambda bi, h, qi: (bi, 0, A16_V // hb + h)),
        ],
        out_specs=pl.BlockSpec((1, Q_TILE, hb), lambda bi, h, qi: (bi, qi, h)),
        out_shape=jax.ShapeDtypeStruct((b, t, DIFF_HEADS * DIFF_V_DIM), BF16),
        compiler_params=_cparams(("parallel", "parallel", "parallel")),
        name="diff_attn",
    )(lam_p, subln_g, a16, a16, a16)


def _split3(x):
    hi = x.astype(BF16)
    r1 = x - hi.astype(F32)
    mid = r1.astype(BF16)
    lo = (r1 - mid.astype(F32)).astype(BF16)
    return jnp.concatenate([hi, mid, lo], axis=1)


def _sum3(y, w):
    return y[:, 0:w] + y[:, w:2 * w] + y[:, 2 * w:3 * w]


def _gla_kernel(q_ref, k_ref, v_ref, a_ref, r_ref, wa_ref, ba_ref, ng_ref, o_ref,
                qd_s, ki_s, ke_s, dec_s, oacc_s, st_s, *, t, ctx_len):
    ck = GLA_CHUNK
    nc = t // ck
    ncc = ctx_len // ck
    pr = 256
    w = 2 * GLA_DK

    ri = lax.broadcasted_iota(jnp.int32, (pr, pr), 0)
    ci = lax.broadcasted_iota(jnp.int32, (pr, pr), 1)
    same = (ri // ck) == (ci // ck)
    m_tot = jnp.where(same, 1.0, 0.0).astype(BF16)
    m_cum = (jnp.where(same & (ci <= ri), 1.0, 0.0).astype(BF16),
             jnp.where(same & (ci >= ri), 1.0, 0.0).astype(BF16))
    for r0 in range(0, t, pr):
        a = a_ref[0, r0:r0 + pr, :].astype(BF16)
        logit = jnp.dot(a, wa_ref[0], preferred_element_type=F32) + ba_ref[0]
        g = (jnp.minimum(logit, 0.0) - jnp.log1p(jnp.exp(-jnp.abs(logit)))) * (1.0 / GLA_TAU)
        q = q_ref[0, r0:r0 + pr, :]
        k = k_ref[0, r0:r0 + pr, :]
        for d in range(2):
            g3 = _split3(g[:, d * w:(d + 1) * w])
            g_cum = _sum3(jnp.dot(m_cum[d], g3, preferred_element_type=F32), w)
            g_tot = _sum3(jnp.dot(m_tot, g3, preferred_element_type=F32), w)
            qd_s[d, r0:r0 + pr, :] = (q * jnp.exp(g_cum)).astype(BF16)
            ki_s[d, r0:r0 + pr, :] = (k * jnp.exp(-g_cum)).astype(BF16)
            ke_s[d, r0:r0 + pr, :] = (k * jnp.exp(g_tot - g_cum)).astype(BF16)
            dec_s[d, r0:r0 + pr, :] = jnp.exp(g_tot)
        oacc_s[r0:r0 + pr, :] = jnp.zeros((pr, 2 * GLA_DV), F32)
    st_s[...] = jnp.zeros_like(st_s)

    ri = lax.broadcasted_iota(jnp.int32, (2 * ck, ck), 0)
    ci = lax.broadcasted_iota(jnp.int32, (2 * ck, ck), 1)
    rr = jnp.where(ri >= ck, ri - ck, ri)
    tri = (ci <= rr, ci >= rr)
    lane = lax.broadcasted_iota(jnp.int32, (ck, w), 1)
    br = lax.broadcasted_iota(jnp.int32, (2 * GLA_DV, w), 0)
    bc = lax.broadcasted_iota(jnp.int32, (2 * GLA_DV, w), 1)
    blk = (br < GLA_DV) == (bc < GLA_DK)
    nt = (((1,), (1,)), ((), ()))
    tn = (((0,), (0,)), ((), ()))

    def chunk(d, r0):
        qd = qd_s[d, pl.ds(r0, ck), :]
        ki = ki_s[d, pl.ds(r0, ck), :]
        ke = ke_s[d, pl.ds(r0, ck), :]
        v = v_ref[0, pl.ds(r0, ck), :]
        dec = dec_s[d, pl.ds(r0, 8), :][0:1, :]
        zq = jnp.zeros_like(qd)
        qs = jnp.concatenate([jnp.where(lane < GLA_DK, qd, zq), jnp.where(lane >= GLA_DK, qd, zq)], axis=0)
        att = lax.dot_general(qs, ki, nt, preferred_element_type=F32)
        att = jnp.where(tri[d], att, 0.0).astype(BF16)
        oi = jnp.dot(att, v, preferred_element_type=F32)
        o_intra = jnp.concatenate([oi[:ck, :GLA_DV], oi[ck:, GLA_DV:]], axis=1)
        st = st_s[d]
        o_inter = lax.dot_general(qd, st.astype(BF16), nt, preferred_element_type=F32)
        s_loc = lax.dot_general(v, ke, tn, preferred_element_type=F32)
        st_s[d] = st * dec + jnp.where(blk, s_loc, 0.0)
        oacc_s[pl.ds(r0, ck), :] = oacc_s[pl.ds(r0, ck), :] + o_intra + o_inter

    def body(n, _):
        chunk(0, pl.multiple_of(n * ck, ck))
        nb = jnp.where(n < ncc, ncc - 1 - n, nc - 1 - (n - ncc))
        chunk(1, pl.multiple_of(nb * ck, ck))
        return 0

    lax.fori_loop(0, nc, body, 0, unroll=GLA_UNROLL)

    ng = ng_ref[...]
    rt = _row_tile(t, 544)
    for r0 in range(0, t, rt):
        o = oacc_s[r0:r0 + rt, :]
        r = r_ref[0, r0:r0 + rt, :]
        outs = []
        for hh in range(2):
            oh = o[:, hh * GLA_DV:(hh + 1) * GLA_DV]
            outs.append(oh * lax.rsqrt(jnp.mean(oh * oh, axis=-1, keepdims=True) + RMS_EPS) * ng)
        o_ref[0, r0:r0 + rt, :] = (jnp.concatenate(outs, axis=1) * (r * _sigmoid(r))).astype(BF16)


def _gla(a16, a32, wa, ba, ng, *, ctx_len):
    b, t, _ = a16.shape
    assert t % 256 == 0 and ctx_len % GLA_CHUNK == 0 and (t // GLA_CHUNK) % GLA_UNROLL == 0
    return pl.pallas_call(
        functools.partial(_gla_kernel, t=t, ctx_len=ctx_len),
        grid=(b, 2),
        in_specs=[
            pl.BlockSpec((1, t, 128), lambda bi, p: (bi, 0, A32_GLQ // 128 + p)),
            pl.BlockSpec((1, t, 128), lambda bi, p: (bi, 0, A32_GLK // 128 + p)),
            pl.BlockSpec((1, t, 256), lambda bi, p: (bi, 0, A16_GV // 256 + p)),
            pl.BlockSpec((1, t, 128), lambda bi, p: (bi, 0, A32_GLA // 128)),
            pl.BlockSpec((1, t, 256), lambda bi, p: (bi, 0, A32_GLR // 256 + p)),
            pl.BlockSpec((1, 128, 256), lambda bi, p: (p, 0, 0)),
            pl.BlockSpec((1, 1, 256), lambda bi, p: (p, 0, 0)),
            pl.BlockSpec((1, GLA_DV), lambda bi, p: (0, 0)),
        ],
        out_specs=pl.BlockSpec((1, t, 256), lambda bi, p: (bi, 0, p)),
        out_shape=jax.ShapeDtypeStruct((b, t, GLA_HEADS * GLA_DV), BF16),
        scratch_shapes=[
            pltpu.VMEM((2, t, 128), BF16), pltpu.VMEM((2, t, 128), BF16), pltpu.VMEM((2, t, 128), BF16),
            pltpu.VMEM((2, t, 128), F32),
            pltpu.VMEM((t, 256), F32),
            pltpu.VMEM((2, 256, 128), F32),
        ],
        compiler_params=_cparams(("parallel", "parallel")),
        name="gla_bidir",
    )(a32, a32, a16, a32, a32, wa, ba, ng)


def _merge_kernel(x_ref, od_ref, og_ref, gates_ref, cvb_ref, cvc_ref, cvx_ref,
                  cvc_p_ref, cvx_p_ref, cvc_n_ref, cvx_n_ref, cw_ref,
                  wdo_ref, wco_ref, wgo_ref, wo_ref, modl_ref, modc_ref, n2g_ref,
                  rw_ref, rb_ref, xo_ref, h2_ref, rt_ref, *, tm, ctx_len, t):
    i = pl.program_id(1)
    pos = i * tm + lax.broadcasted_iota(jnp.int32, (tm, 1), 0)
    is_ctx = pos < ctx_len
    row = lax.broadcasted_iota(jnp.int32, (tm, 1), 0)

    z = cvc_ref[0] * cvx_ref[0]
    z_before = cvc_p_ref[0, 7:8, :] * cvx_p_ref[0, 7:8, :]
    z_after = cvc_n_ref[0, 0:1, :] * cvx_n_ref[0, 0:1, :]
    z_prev = jnp.where(row == 0, z_before, pltpu.roll(z, 1, 0))
    z_prev = jnp.where((pos == 0) | (pos == ctx_len), 0.0, z_prev)
    z_next = jnp.where(row == tm - 1, z_after, pltpu.roll(z, tm - 1, 0))
    z_next = jnp.where((pos == ctx_len - 1) | (pos == t - 1), 0.0, z_next)
    conv = z_prev * cw_ref[0:1, :] + z * cw_ref[1:2, :] + z_next * cw_ref[2:3, :]
    zc = (cvb_ref[0] * conv).astype(BF16)

    y_diff = jnp.dot(od_ref[0], wdo_ref[...], preferred_element_type=F32)
    y_conv = jnp.dot(zc, wco_ref[...], preferred_element_type=F32)
    y_gla = jnp.dot(og_ref[0], wgo_ref[...], preferred_element_type=F32)
    d = D_MODEL
    mix = (_sigmoid(gates_ref[0, :, 0:d]) * y_diff
           + _sigmoid(gates_ref[0, :, d:2 * d]) * y_conv
           + _sigmoid(gates_ref[0, :, 2 * d:3 * d]) * y_gla)
    m = jnp.dot(mix.astype(BF16), wo_ref[...], preferred_element_type=F32)

    g1 = jnp.where(is_ctx, modc_ref[2:3, :], modl_ref[0, 2:3, :])
    x_new = x_ref[0] + g1 * m
    xo_ref[0] = x_new
    sh2 = jnp.where(is_ctx, modc_ref[3:4, :], modl_ref[0, 3:4, :])
    sc2 = jnp.where(is_ctx, modc_ref[4:5, :], modl_ref[0, 4:5, :])
    h2 = _norm_mod(x_new, n2g_ref[...], sh2, sc2)
    for cc in range(d // 128):
        h2_ref[0, pl.ds(cc, tm, stride=8), :] = h2[:, cc * 128:(cc + 1) * 128]

    logits = jnp.dot(h2.astype(BF16), rw_ref[...], preferred_element_type=F32) + rb_ref[...]
    lane = lax.broadcasted_iota(jnp.int32, logits.shape, 1)
    vals = logits
    tops = []
    idx_out = jnp.zeros(logits.shape, jnp.int32)
    for kk in range(TOP_K):
        mx = jnp.max(vals, axis=-1, keepdims=True)
        ix = jnp.min(jnp.where(vals == mx, lane, 128), axis=-1, keepdims=True)
        tops.append(mx)
        idx_out = jnp.where(lane == kk, ix, idx_out)
        vals = jnp.where(lane == ix, -jnp.inf, vals)
    es = [jnp.exp(v - tops[0]) for v in tops]
    den = es[0] + es[1] + es[2] + es[3]
    packed = idx_out.astype(F32)
    for kk in range(TOP_K):
        gk = es[kk] / den
        hi = gk.astype(BF16).astype(F32)
        mid = (gk - hi).astype(BF16).astype(F32)
        lo = gk - hi - mid
        for piece, val in enumerate((hi, mid, lo)):
            packed = jnp.where(lane == TOP_K * (piece + 1) + kk, val, packed)
    ri = lax.broadcasted_iota(jnp.int32, (tm, tm), 0)
    ci = lax.broadcasted_iota(jnp.int32, (tm, tm), 1)
    eye = jnp.where(ri == ci, 1.0, 0.0).astype(BF16)
    rows = lax.dot_general(packed.astype(BF16), eye, (((0,), (0,)), ((), ())), preferred_element_type=F32)
    rt_ref[0, 0] = rows[0:4 * TOP_K, :]


def _merge(x, od, og, a32, cw, wdo, wco, wgo, wo, modl, modc, n2g, rw, rb, *, ctx_len):
    b, t, d = x.shape
    tm = _row_tile(t, 272)
    nt8 = t // 8
    row = lambda bi, i: (bi, i, 0)
    const2 = lambda bi, i: (0, 0)
    return pl.pallas_call(
        functools.partial(_merge_kernel, tm=tm, ctx_len=ctx_len, t=t),
        grid=(b, t // tm),
        in_specs=[
            pl.BlockSpec((1, tm, d), row),
            pl.BlockSpec((1, tm, d), row),
            pl.BlockSpec((1, tm, GLA_HEADS * GLA_DV), row),
            pl.BlockSpec((1, tm, 3 * d), lambda bi, i: (bi, i, 0)),
            pl.BlockSpec((1, tm, CONV_W), lambda bi, i: (bi, i, A32_CVB // CONV_W)),
            pl.BlockSpec((1, tm, CONV_W), lambda bi, i: (bi, i, A32_CVC // CONV_W)),
            pl.BlockSpec((1, tm, CONV_W), lambda bi, i: (bi, i, A32_CVX // CONV_W)),
            pl.BlockSpec((1, 8, CONV_W), lambda bi, i: (bi, jnp.maximum(i * (tm // 8) - 1, 0),
                                                   A32_CVC // CONV_W)),
            pl.BlockSpec((1, 8, CONV_W), lambda bi, i: (bi, jnp.maximum(i * (tm // 8) - 1, 0),
                                                   A32_CVX // CONV_W)),
            pl.BlockSpec((1, 8, CONV_W), lambda bi, i: (bi, jnp.minimum((i + 1) * (tm // 8), nt8 - 1),
                                                   A32_CVC // CONV_W)),
            pl.BlockSpec((1, 8, CONV_W), lambda bi, i: (bi, jnp.minimum((i + 1) * (tm // 8), nt8 - 1),
                                                   A32_CVX // CONV_W)),
            pl.BlockSpec((8, CONV_W), const2),
            pl.BlockSpec((d, d), const2),
            pl.BlockSpec((CONV_W, d), const2),
            pl.BlockSpec((GLA_HEADS * GLA_DV, d), const2),
            pl.BlockSpec((d, d), const2),
            pl.BlockSpec((1, 8, d), lambda bi, i: (bi, 0, 0)),
            pl.BlockSpec((8, d), const2),
            pl.BlockSpec((1, d), const2),
            pl.BlockSpec((d, 128), const2),
            pl.BlockSpec((1, 128), const2),
        ],
        out_specs=[
            pl.BlockSpec((1, tm, d), row),
            pl.BlockSpec((1, tm * (d // 128), 128), row),
            pl.BlockSpec((1, 1, 4 * TOP_K, tm), lambda bi, i: (bi, i, 0, 0)),
        ],
        out_shape=[
            jax.ShapeDtypeStruct((b, t, d), F32),
            jax.ShapeDtypeStruct((b, t * (d // 128), 128), F32),
            jax.ShapeDtypeStruct((b, t // tm, 4 * TOP_K, tm), F32),
        ],
        compiler_params=_cparams(("parallel", "parallel")),
        name="merge_router",
    )(x, od, og, a32, a32, a32, a32, a32, a32, a32, a32, cw, wdo, wco, wgo, wo,
      modl, modc, n2g, rw, rb)


def _moe_kernel(be_ref, nv_ref, g_ref, w1g_ref, w1l_ref, b1g_ref, b1l_ref, w2_ref, b2_ref,
                src_hbm, dst_hbm, h2_hbm, out_hbm, src_s, dst_s, xbuf, ybuf, sem_p, sem_g, sem_s,
                *, nb, n_slots):
    i = pl.program_id(0)
    nv = nv_ref[0]
    bm = MOE_BLOCK
    cur = lax.rem(i, 2)
    nxt = 1 - cur

    def idx_copies(blk_src, blk_dst, slot):
        return (pltpu.make_async_copy(src_hbm.at[blk_src, 0], src_s.at[pl.ds(slot * bm, bm)], sem_p),
                pltpu.make_async_copy(dst_hbm.at[blk_dst, 0], dst_s.at[pl.ds(slot * bm, bm)], sem_p))

    def start_gather(slot, buf):
        for j in range(bm):
            r0 = pl.multiple_of(src_s[slot * bm + j], TOK_ROWS)
            pltpu.make_async_copy(h2_hbm.at[pl.ds(r0, TOK_ROWS)],
                                  xbuf.at[buf, pl.ds(j * TOK_ROWS, TOK_ROWS)],
                                  sem_g.at[buf]).start(priority=j % 2)

    def wait_gather(buf):
        pltpu.make_async_copy(h2_hbm.at[pl.ds(0, bm * TOK_ROWS)], xbuf.at[buf], sem_g.at[buf]).wait()

    def wait_scatter():
        pltpu.make_async_copy(ybuf.at[0], out_hbm.at[pl.ds(0, bm * TOK_ROWS)], sem_s).wait()

    @pl.when(i <= nv)
    def _():
        @pl.when(i == 0)
        def _():
            ybuf[...] = jnp.zeros_like(ybuf)
            for blk_src, blk_dst, slot in ((0, 0, 0), (1, 1, 1), (0, nb, 3)):
                for cp in idx_copies(blk_src, blk_dst, slot):
                    cp.start()
                    cp.wait()
            start_gather(0, 0)

        @pl.when(i >= 1)
        def _():
            for cp in idx_copies(0, 0, 0):
                cp.wait()
            wait_scatter()

        wait_gather(cur)

        @pl.when(i < nv)
        def _():
            blk = jnp.minimum(i + 2, nv - 1)
            for cp in idx_copies(blk, blk, lax.rem(i + 2, 4)):
                cp.start()

        x = jnp.concatenate([xbuf[cur, pl.ds(cc, bm, stride=TOK_ROWS), :] for cc in range(TOK_ROWS)],
                            axis=1).astype(BF16)
        gslot = lax.rem(i + 1, 4)
        pslot = lax.rem(i + 3, 4)
        ng = MOE_DMA_GROUPS
        cw = w1g_ref.shape[-1] // ng
        y = None
        for c in range(ng):
            for j in range(c * bm // ng, (c + 1) * bm // ng):
                r0 = pl.multiple_of(src_s[gslot * bm + j], TOK_ROWS)
                pltpu.make_async_copy(h2_hbm.at[pl.ds(r0, TOK_ROWS)],
                                      xbuf.at[nxt, pl.ds(j * TOK_ROWS, TOK_ROWS)],
                                      sem_g.at[nxt]).start(priority=j % 2)
                r1 = pl.multiple_of(dst_s[pslot * bm + j], TOK_ROWS)
                pltpu.make_async_copy(ybuf.at[nxt, pl.ds(j * TOK_ROWS, TOK_ROWS)],
                                      out_hbm.at[pl.ds(r1, TOK_ROWS)], sem_s).start(priority=(j + 1) % 2)
            cs = slice(c * cw, (c + 1) * cw)
            hg = jnp.dot(x, w1g_ref[0, 0, :, cs], preferred_element_type=F32) + b1g_ref[0, :, cs]
            hl = jnp.dot(x, w1l_ref[0, 0, :, cs], preferred_element_type=F32) + b1l_ref[0, :, cs]
            hg = jnp.minimum(hg, SWIGLU_LIMIT)
            hl = jnp.clip(hl, -SWIGLU_LIMIT, SWIGLU_LIMIT)
            act = (hg * _sigmoid(SWIGLU_ALPHA * hg) * (hl + 1.0)).astype(BF16)
            part = jnp.dot(act, w2_ref[0, cs, :], preferred_element_type=F32)
            y = part if y is None else y + part
        y = y + b2_ref[0]
        gi = lax.broadcasted_iota(jnp.int32, (bm, bm), 0)
        gj = lax.broadcasted_iota(jnp.int32, (bm, bm), 1)
        y = y * jnp.sum(jnp.where(gi == gj, g_ref[0], 0.0), axis=1, keepdims=True)
        for cc in range(TOK_ROWS):
            ybuf[cur, pl.ds(cc, bm, stride=TOK_ROWS), :] = y[:, cc * 128:(cc + 1) * 128]

        @pl.when(i == nv)
        def _():
            wait_scatter()
            wait_gather(nxt)
            ybuf[...] = jnp.zeros_like(ybuf)
            fills = [pltpu.make_async_copy(
                ybuf.at[sl], out_hbm.at[pl.ds((n_slots + sl * bm) * TOK_ROWS, bm * TOK_ROWS)], sem_s)
                for sl in range(2)]
            for cp in fills:
                cp.start()
            for cp in fills:
                cp.wait()


def _moe(block_exp, n_valid, row_src, row_dst, row_gate, h2_flat, w1, b1g, b1l, w2, b2, *, n_slots):
    d = D_MODEL
    nb = row_src.shape[0]
    bm = MOE_BLOCK
    de = w1.shape[-1]
    blk = lambda i, nv: jnp.minimum(i, nv[0] - 1)
    wmap = lambda i, be, nv: (be[blk(i, nv)], 0, 0)
    return pl.pallas_call(
        functools.partial(_moe_kernel, nb=nb, n_slots=n_slots),
        grid_spec=pltpu.PrefetchScalarGridSpec(
            num_scalar_prefetch=2,
            grid=(nb + 1,),
            in_specs=[
                pl.BlockSpec((1, 1, bm), lambda i, be, nv: (blk(i, nv), 0, 0)),
                pl.BlockSpec((1, 1, d, de), lambda i, be, nv: (be[blk(i, nv)], 0, 0, 0)),
                pl.BlockSpec((1, 1, d, de), lambda i, be, nv: (be[blk(i, nv)], 1, 0, 0)),
                pl.BlockSpec((1, 1, de), wmap),
                pl.BlockSpec((1, 1, de), wmap),
                pl.BlockSpec((1, de, d), wmap),
                pl.BlockSpec((1, 1, d), wmap),
                pl.BlockSpec(memory_space=pl.ANY),
                pl.BlockSpec(memory_space=pl.ANY),
                pl.BlockSpec(memory_space=pl.ANY),
            ],
            out_specs=pl.BlockSpec(memory_space=pl.ANY),
            scratch_shapes=[
                pltpu.SMEM((4 * bm,), jnp.int32),
                pltpu.SMEM((4 * bm,), jnp.int32),
                pltpu.VMEM((2, bm * TOK_ROWS, 128), F32),
                pltpu.VMEM((2, bm * TOK_ROWS, 128), F32),
                pltpu.SemaphoreType.DMA,
                pltpu.SemaphoreType.DMA((2,)),
                pltpu.SemaphoreType.DMA,
            ],
        ),
        out_shape=jax.ShapeDtypeStruct(((n_slots + 2 * bm) * TOK_ROWS, 128), F32),
        compiler_params=_cparams(("arbitrary",)),
        name="moe_experts",
    )(block_exp, n_valid, row_gate, w1, w1, b1g, b1l, w2, b2, row_src, row_dst, h2_flat)


def _combine_kernel(x_ref, y4_ref, modl_ref, modc_ref, fg_ref, o_ref, *, tm, ctx_len, final, skip):
    i = pl.program_id(1) + skip
    d = D_MODEL
    ld = lambda k, cc: y4_ref[pl.ds(k * TOK_ROWS + cc, tm, stride=TOP_K * TOK_ROWS), :]
    f = jnp.concatenate([(ld(0, cc) + ld(1, cc)) + (ld(2, cc) + ld(3, cc)) for cc in range(TOK_ROWS)],
                        axis=1)
    pos = i * tm + lax.broadcasted_iota(jnp.int32, (tm, 1), 0)
    g2 = jnp.where(pos < ctx_len, modc_ref[5:6, :], modl_ref[0, 5:6, :])
    x_new = x_ref[0] + g2 * f
    if final:
        x_new = x_new * lax.rsqrt(jnp.mean(x_new * x_new, axis=-1, keepdims=True) + RMS_EPS) * fg_ref[...]
    o_ref[0] = x_new


def _combine(x, y4, modl, modc, fg, *, ctx_len, final):
    b, t, d = x.shape
    if final:
        tm = Q_TILE
        assert ctx_len % tm == 0
        skip = ctx_len // tm
    else:
        tm = _row_tile(t, 272)
        skip = 0
    npb = t // tm
    return pl.pallas_call(
        functools.partial(_combine_kernel, tm=tm, ctx_len=ctx_len, final=final, skip=skip),
        grid=(b, npb - skip),
        in_specs=[
            pl.BlockSpec((1, tm, d), lambda bi, i: (bi, i + skip, 0)),
            pl.BlockSpec((tm * TOP_K * TOK_ROWS, 128), lambda bi, i: (bi * (npb - skip) + i, 0)),
            pl.BlockSpec((1, 8, d), lambda bi, i: (bi, 0, 0)),
            pl.BlockSpec((8, d), lambda bi, i: (0, 0)),
            pl.BlockSpec((1, d), lambda bi, i: (0, 0)),
        ],
        out_specs=pl.BlockSpec((1, tm, d), lambda bi, i: (bi, i, 0)),
        out_shape=jax.ShapeDtypeStruct((b, t - skip * tm, d), F32),
        compiler_params=_cparams(("parallel", "parallel")),
        name="moe_combine",
    )(x, y4, modl, modc, fg)


def _routing(e_flat, gate_flat, *, tm, seq, ctx_len, latent_only):
    npair = e_flat.shape[0]
    t = seq + ctx_len
    bm = MOE_BLOCK
    nb = npair // bm + N_EXPERTS
    nr = nb * bm
    npad = nr - npair
    experts = jnp.arange(N_EXPERTS, dtype=jnp.int32)
    ids = jnp.arange(npair, dtype=jnp.int32)
    tok_of = lambda i: (i // (TOP_K * tm)) * tm + i % tm
    if latent_only:
        e_flat = jnp.where(tok_of(ids) % t >= ctx_len, e_flat, N_EXPERTS)
    counts = jnp.sum((e_flat[:, None] == experts[None, :]).astype(jnp.int32), axis=0)
    padded = (counts + bm - 1) // bm * bm
    pad_end = jnp.cumsum(padded)
    fill_end = jnp.cumsum(padded - counts)
    dummy = jnp.arange(npad, dtype=jnp.int32)
    e_dummy = jnp.sum((dummy[:, None] >= fill_end[None, :]).astype(jnp.int32), axis=1)
    kb = nr
    keys = jnp.concatenate([e_flat * kb + ids, e_dummy * kb + npair + dummy])
    gates = jnp.concatenate([gate_flat, jnp.zeros((npad,), F32)])
    skey, sgate = lax.sort((keys, gates), num_keys=1)
    e_row = skey // kb
    ident = skey - e_row * kb
    valid = (ident < npair) & (e_row < N_EXPERTS)
    pair = jnp.where(valid, ident, 0)
    tok = tok_of(pair)
    kk = (pair // tm) % TOP_K
    row_gate = jnp.where(valid, sgate, 0.0).reshape(nb, 1, bm)
    row_src = (tok * TOK_ROWS).reshape(nb, 1, bm)
    if latent_only:
        slot = (tok // t) * seq + tok % t - ctx_len
        n_slots = (npair // TOP_K // t) * seq * TOP_K
    else:
        slot = tok
        n_slots = npair
    j = jnp.arange(nr, dtype=jnp.int32)
    dump = n_slots + ((j // bm) % 2) * bm + j % bm
    row_dst = jnp.where(valid, slot * TOP_K + kk, dump).reshape(nb, bm)
    row_dst = jnp.concatenate([row_dst, (n_slots + bm + jnp.arange(bm, dtype=jnp.int32))[None, :]],
                              axis=0)[:, None, :] * TOK_ROWS
    blocks = jnp.arange(nb, dtype=jnp.int32) * bm
    block_exp = jnp.minimum(jnp.sum((blocks[:, None] >= pad_end[None, :]).astype(jnp.int32), axis=1),
                            N_EXPERTS - 1)
    n_valid = (pad_end[-1] // bm).astype(jnp.int32).reshape(1)
    return row_src, row_dst, row_gate, block_exp, n_valid, n_slots


def _rope_tables(ctx_len, seq_len):
    inv_freq = ROPE_BASE ** (-jnp.arange(0, ROT_AXIS_DIM, 2, dtype=F32) / ROT_AXIS_DIM)
    p = jnp.arange(seq_len, dtype=jnp.int32)
    row = (p // GRID_W).astype(F32)
    col = (p % GRID_W).astype(F32)
    lane = np.arange(128)
    dd = lane % DIFF_HEAD_DIM
    fidx = jnp.asarray((dd % ROT_AXIS_DIM) % (ROT_AXIS_DIM // 2))
    use_col = jnp.asarray(dd >= ROT_AXIS_DIM)
    second_half = jnp.asarray((dd % ROT_AXIS_DIM) >= ROT_AXIS_DIM // 2)
    posm = jnp.where(use_col[None, :], col[:, None], row[:, None])
    ang = posm * inv_freq[fidx][None, :]
    cos = jnp.cos(ang)
    sin = jnp.sin(ang)
    sa = jnp.where(second_half[None, :], sin, 0.0)
    sb = jnp.where(second_half[None, :], 0.0, -sin)
    pad = lambda a, v: jnp.concatenate([jnp.full((ctx_len, 128), v, F32), a], axis=0)
    return pad(cos, 1.0), pad(sa, 0.0), pad(sb, 0.0)


def kernel(x, c, ctx, c_ctx, ada_w, ada_b, norm1_g, norm2_g, w_in, diff_lambda, diff_subln_g,
           diff_w_out, conv_w, conv_w_out, gla_w_a2, gla_b_a, gla_norm_g, gla_w_out, w_o,
           router_w, router_b, moe_w1, moe_b1, moe_w2, moe_b2, final_norm_g):
    b, s, d = x.shape
    cl = ctx.shape[1]
    t = cl + s
    nl = w_in.shape[0]
    assert d == D_MODEL and b <= 15

    xs = jnp.concatenate([ctx, x], axis=1)
    cc = jnp.concatenate([c, c_ctx[None, :], jnp.zeros((15 - b, d), F32)], axis=0)
    mod = _ada(cc, ada_w, ada_b).reshape(nl, 16, N_ADA, d)
    cos, sa, sb = _rope_tables(cl, s)

    sp = np.cumsum([1024, 1024, 1024, 512, 512, 512, 256, 256, 512, 512, 32, 3072])
    (w_q, w_k, w_v, w_cb, w_cc, w_cx, w_gq, w_gk, w_gv, w_gr, w_ga, w_gt) = jnp.split(
        w_in, [int(v) for v in sp[:-1]], axis=-1)
    w_all = jnp.concatenate(
        [w_q * (DIFF_HEAD_DIM ** -0.5), w_k, w_v, w_gv,
         w_gt, w_cb, w_cc, w_cx, w_gq * (GLA_DK ** -0.5), w_gk, w_gr, w_ga,
         jnp.zeros((nl, d, A32_W - A32_GLA - 2 * GLA_RANK), F32)], axis=-1).astype(BF16)

    wa = jnp.zeros((nl, 2, 128, 256), F32)
    for p in range(2):
        wa = wa.at[:, p, 0:GLA_RANK, 0:128].set(gla_w_a2[:, 0, :, p * 128:(p + 1) * 128])
        wa = wa.at[:, p, GLA_RANK:2 * GLA_RANK, 128:256].set(gla_w_a2[:, 1, :, p * 128:(p + 1) * 128])
    wa = wa.astype(BF16)
    ba = jnp.stack([jnp.concatenate([gla_b_a[:, 0, p * 128:(p + 1) * 128],
                                     gla_b_a[:, 1, p * 128:(p + 1) * 128]], axis=-1)
                    for p in range(2)], axis=1)[:, :, None, :]

    cw = jnp.concatenate([conv_w, jnp.zeros((nl, 5, CONV_W), F32)], axis=1)
    rw = jnp.concatenate([router_w, jnp.zeros((nl, d, 128 - N_EXPERTS), F32)], axis=-1).astype(BF16)
    rb = jnp.concatenate([router_b, jnp.full((nl, 128 - N_EXPERTS), -1e30, F32)], axis=-1)[:, None, :]
    w1 = jnp.moveaxis(moe_w1.reshape(nl, N_EXPERTS, d, D_EXPERT, 2), -1, 2).astype(BF16)
    b1 = moe_b1.reshape(nl, N_EXPERTS, 1, D_EXPERT, 2)
    b1g = b1[..., 0]
    b1l = b1[..., 1]
    w2 = moe_w2.astype(BF16)
    b2 = moe_b2[:, :, None, :]
    wdo = diff_w_out.astype(BF16)
    wco = conv_w_out.astype(BF16)
    wgo = gla_w_out.astype(BF16)
    wo = w_o.astype(BF16)

    for layer in range(nl):
        last = layer == nl - 1
        lam_init = 0.8 - 0.6 * math.exp(-0.3 * layer)
        ml = mod[layer]
        modl = jnp.concatenate([ml[:b], jnp.zeros((b, 2, d), F32)], axis=1)
        modc = jnp.concatenate([ml[b], jnp.zeros((2, d), F32)], axis=0)

        a16, a32 = _k1(xs, modl, modc, norm1_g[layer][None, :], cos, sa, sb, w_all[layer], ctx_len=cl)
        od = _attn(a16, diff_lambda[layer], diff_subln_g[layer][None, :],
                   ctx_len=cl, seq_len=s, lam_init=lam_init)
        og = _gla(a16, a32, wa[layer], ba[layer], gla_norm_g[layer][None, :], ctx_len=cl)
        xs, h2, rt = _merge(xs, od, og, a32, cw[layer], wdo[layer], wco[layer], wgo[layer],
                                   wo[layer], modl, modc, norm2_g[layer][None, :], rw[layer],
                                   rb[layer], ctx_len=cl)

        r = b * t
        tm_r = rt.shape[-1]
        rt = rt.reshape(-1, 4, TOP_K, tm_r)
        row_src, row_dst, row_gate, block_exp, n_valid, n_slots = _routing(
            rt[:, 0].astype(jnp.int32).reshape(-1), (rt[:, 1] + rt[:, 2] + rt[:, 3]).reshape(-1),
            tm=tm_r, seq=s, ctx_len=cl, latent_only=last)
        y4 = _moe(block_exp, n_valid, row_src, row_dst, row_gate, h2.reshape(r * TOK_ROWS, 128), w1[layer],
                  b1g[layer], b1l[layer], w2[layer], b2[layer], n_slots=n_slots)
        xs = _combine(xs, y4, modl, modc, final_norm_g[None, :], ctx_len=cl, final=last)

    return xs
```

```python
import functools
import math

import numpy as np
import jax
import jax.numpy as jnp
from jax import lax
from jax.experimental import pallas as pl
from jax.experimental.pallas import tpu as pltpu

F32 = jnp.float32
BF16 = jnp.bfloat16

D_MODEL = 1024
GRID_W = 64
RMS_EPS = 1e-6
N_ADA = 6
DIFF_HEADS = 8
DIFF_HEAD_DIM = 64
DIFF_V_DIM = 128
ROPE_BASE = 10000.0
ROT_AXIS_DIM = 32
CONV_W = 512
GLA_HEADS = 4
GLA_DK = 64
GLA_DV = 128
GLA_RANK = 16
GLA_TAU = 16.0
GLA_CHUNK = 64
N_EXPERTS = 32
TOP_K = 4
D_EXPERT = 1024
SWIGLU_LIMIT = 7.0
SWIGLU_ALPHA = 1.702
LOG2_E = 1.4426950408889634

A16_Q, A16_K, A16_V, A16_GV, A16_W = 0, 1024, 2048, 3072, 3584
A32_GATES, A32_CVB, A32_CVC, A32_CVX = 0, 3072, 3584, 4096
A32_GLQ, A32_GLK, A32_GLR, A32_GLA, A32_W = 4608, 4864, 5120, 5632, 5760
W_ALL_COLS = A16_W + A32_W
W_GATES, W_GLA = 6144, 9216


def _k1_chunks():
    out = []
    def add(w0, width, to16, o0, rope=False, ch=512):
        for c in range(0, width, ch):
            out.append((w0 + c, min(ch, width - c), to16, o0 + c, rope))
    add(0, 1024, True, A16_Q, rope=True)
    add(1024, 1024, True, A16_K, rope=True)
    add(2048, 1024, True, A16_V)
    add(3072, 512, False, A32_CVB)
    add(3584, 512, False, A32_CVC)
    add(4096, 512, False, A32_CVX)
    add(4608, 256, False, A32_GLQ)
    add(4864, 256, False, A32_GLK)
    add(5120, 512, True, A16_GV)
    add(5632, 512, False, A32_GLR)
    add(W_GATES, 3072, False, A32_GATES)
    add(W_GLA, 128, False, A32_GLA)
    return tuple(out)


K1_CHUNKS = _k1_chunks()

Q_TILE = 256
KV_SEGMENTS = 2
ATTN_HEADS = 4
MOE_BLOCK = 256
GLA_UNROLL = 2
TOK_ROWS = 8
VMEM_LIMIT = 56 * 1024 * 1024


def _cparams(sem):
    return pltpu.CompilerParams(dimension_semantics=sem, vmem_limit_bytes=VMEM_LIMIT)


def _row_tile(t, target):
    best = None
    for cand in range(16, target + 1, 16):
        if t % cand == 0:
            best = cand
    assert best is not None, t
    return best


def _sigmoid(v):
    return 1.0 / (1.0 + jnp.exp(-v))


def _ada_kernel(cc_ref, w_ref, b_ref, o_ref):
    a = cc_ref[...]
    a = a * _sigmoid(a)
    o_ref[0] = jnp.dot(a, w_ref[0], precision=lax.Precision.HIGHEST,
                       preferred_element_type=F32) + b_ref[0]


def _ada(cc, ada_w, ada_b):
    nl, d, n = ada_w.shape
    tn = 1024
    return pl.pallas_call(
        _ada_kernel,
        grid=(nl, n // tn),
        in_specs=[
            pl.BlockSpec((16, d), lambda l, j: (0, 0)),
            pl.BlockSpec((1, d, tn), lambda l, j: (l, 0, j)),
            pl.BlockSpec((1, 1, tn), lambda l, j: (l, 0, j)),
        ],
        out_specs=pl.BlockSpec((1, 16, tn), lambda l, j: (l, 0, j)),
        out_shape=jax.ShapeDtypeStruct((nl, 16, n), F32),
        compiler_params=_cparams(("parallel", "parallel")),
        name="ada_mod",
    )(cc, ada_w, ada_b.reshape(nl, 1, n))


def _norm_mod(x, ng, shift, scale):
    y = x * lax.rsqrt(jnp.mean(x * x, axis=-1, keepdims=True) + RMS_EPS) * ng
    return y * (1.0 + scale) + shift


def _k1_kernel(x_ref, modl_ref, modc_ref, ng_ref, cos_ref, sa_ref, sb_ref, w_ref,
               o16_ref, o32_ref, *, tm, ctx_len):
    i = pl.program_id(1)
    pos = i * tm + lax.broadcasted_iota(jnp.int32, (tm, 1), 0)
    is_ctx = pos < ctx_len
    shift = jnp.where(is_ctx, modc_ref[0:1, :], modl_ref[0, 0:1, :])
    scale = jnp.where(is_ctx, modc_ref[1:2, :], modl_ref[0, 1:2, :])
    h = _norm_mod(x_ref[0], ng_ref[...], shift, scale).astype(BF16)
    cos = cos_ref[...]
    sa = sa_ref[...]
    sb = sb_ref[...]
    for w0, width, to16, o0, rope in K1_CHUNKS:
        acc = jnp.dot(h, w_ref[:, w0:w0 + width], preferred_element_type=F32)
        if rope:
            parts = []
            for j in range(width // 128):
                a = acc[:, j * 128:(j + 1) * 128]
                parts.append(a * cos + pltpu.roll(a, 16, 1) * sa + pltpu.roll(a, 112, 1) * sb)
            acc = jnp.concatenate(parts, axis=1)
        if to16 and o0 < A16_K:
            acc = acc * LOG2_E
        if to16:
            o16_ref[0, :, o0:o0 + width] = acc.astype(BF16)
        else:
            o32_ref[0, :, o0:o0 + width] = acc


def _k1(x, modl, modc, ng, cos, sa, sb, w_all, *, ctx_len):
    b, t, d = x.shape
    tm = _row_tile(t, 272)
    tbl = pl.BlockSpec((tm, 128), lambda bi, i: (i, 0))
    return pl.pallas_call(
        functools.partial(_k1_kernel, tm=tm, ctx_len=ctx_len),
        grid=(b, t // tm),
        in_specs=[
            pl.BlockSpec((1, tm, d), lambda bi, i: (bi, i, 0)),
            pl.BlockSpec((1, 8, d), lambda bi, i: (bi, 0, 0)),
            pl.BlockSpec((8, d), lambda bi, i: (0, 0)),
            pl.BlockSpec((1, d), lambda bi, i: (0, 0)),
            tbl, tbl, tbl,
            pl.BlockSpec((d, W_ALL_COLS), lambda bi, i: (0, 0), pipeline_mode=pl.Buffered(1)),
        ],
        out_specs=[
            pl.BlockSpec((1, tm, A16_W), lambda bi, i: (bi, i, 0)),
            pl.BlockSpec((1, tm, A32_W), lambda bi, i: (bi, i, 0)),
        ],
        out_shape=[
            jax.ShapeDtypeStruct((b, t, A16_W), BF16),
            jax.ShapeDtypeStruct((b, t, A32_W), F32),
        ],
        compiler_params=_cparams(("parallel", "parallel")),
        name="norm_inproj",
    )(x, modl, modc, ng, cos, sa, sb, w_all)


def _attn_kernel(lam_ref, g_ref, q_ref, k_ref, v_ref, o_ref, *, ctx_len, segments, lam_init):
    qi = pl.program_id(2)
    tq = Q_TILE
    hb = DIFF_V_DIM
    lv = lam_ref[...]
    lam = (jnp.exp(jnp.sum(lv[0:1] * lv[1:2], axis=1, keepdims=True))
           - jnp.exp(jnp.sum(lv[2:3] * lv[3:4], axis=1, keepdims=True)) + lam_init)

    def run(segs):
        qss = []
        for g in range(ATTN_HEADS):
            q = q_ref[0, :, g * hb:(g + 1) * hb]
            lane = lax.broadcasted_iota(jnp.int32, q.shape, 1)
            zero = jnp.zeros_like(q)
            qss.append(jnp.concatenate([jnp.where(lane < DIFF_HEAD_DIM, q, zero),
                                        jnp.where(lane >= DIFF_HEAD_DIM, q, zero)], axis=0))
        carries = [None] * ATTN_HEADS
        for r0, rows in segs:
            for g in range(ATTN_HEADS):
                k_t = k_ref[0, r0:r0 + rows, g * hb:(g + 1) * hb]
                v_t = v_ref[0, r0:r0 + rows, g * hb:(g + 1) * hb]
                s = lax.dot_general(qss[g], k_t, (((1,), (1,)), ((), ())), preferred_element_type=F32)
                smax = jnp.max(s, axis=-1, keepdims=True)
                if carries[g] is None:
                    p = jnp.exp2(s - smax)
                    carries[g] = (smax, jnp.sum(p, axis=-1, keepdims=True),
                                  jnp.dot(p.astype(BF16), v_t, preferred_element_type=F32))
                else:
                    m, l, acc = carries[g]
                    m_new = jnp.maximum(m, smax)
                    alpha = jnp.exp2(m - m_new)
                    p = jnp.exp2(s - m_new)
                    carries[g] = (m_new, alpha * l + jnp.sum(p, axis=-1, keepdims=True),
                                  alpha * acc + jnp.dot(p.astype(BF16), v_t, preferred_element_type=F32))
        for g in range(ATTN_HEADS):
            _, l, acc = carries[g]
            o = acc[:tq] / l[:tq] - lam * (acc[tq:] / l[tq:])
            o = o * lax.rsqrt(jnp.mean(o * o, axis=-1, keepdims=True) + RMS_EPS) * g_ref[...]
            o_ref[0, :, g * hb:(g + 1) * hb] = (o * (1.0 - lam_init)).astype(BF16)

    @pl.when(qi * tq >= ctx_len)
    def _():
        run(segments)

    @pl.when(qi * tq < ctx_len)
    def _():
        run([(0, ctx_len)])


def _attn(a16, lam_p, subln_g, *, ctx_len, seq_len, lam_init):
    b, t, _ = a16.shape
    assert ctx_len % Q_TILE == 0 and t % 256 == 0
    first = (t // 256 + KV_SEGMENTS - 1) // KV_SEGMENTS * 256
    bounds = [min(first * n, t) for n in range(KV_SEGMENTS + 1)]
    segments = [(lo, hi - lo) for lo, hi in zip(bounds[:-1], bounds[1:]) if hi > lo]
    hb = DIFF_V_DIM * ATTN_HEADS
    return pl.pallas_call(
        functools.partial(_attn_kernel, ctx_len=ctx_len, segments=segments, lam_init=lam_init),
        grid=(b, DIFF_HEADS // ATTN_HEADS, t // Q_TILE),
        in_specs=[
            pl.BlockSpec((4, DIFF_HEAD_DIM), lambda bi, h, qi: (0, 0)),
            pl.BlockSpec((1, DIFF_V_DIM), lambda bi, h, qi: (0, 0)),
            pl.BlockSpec((1, Q_TILE, hb), lambda bi, h, qi: (bi, qi, A16_Q // hb + h)),
            pl.BlockSpec((1, t, hb), lambda bi, h, qi: (bi, 0, A16_K // hb + h)),
            pl.BlockSpec((1, t, hb), lambda bi, h, qi: (bi, 0, A16_V // hb + h)),
        ],
        out_specs=pl.BlockSpec((1, Q_TILE, hb), lambda bi, h, qi: (bi, qi, h)),
        out_shape=jax.ShapeDtypeStruct((b, t, DIFF_HEADS * DIFF_V_DIM), BF16),
        compiler_params=_cparams(("parallel", "parallel", "parallel")),
        name="diff_attn",
    )(lam_p, subln_g, a16, a16, a16)


def _split3(x):
    hi = x.astype(BF16)
    r1 = x - hi.astype(F32)
    mid = r1.astype(BF16)
    lo = (r1 - mid.astype(F32)).astype(BF16)
    return jnp.concatenate([hi, mid, lo], axis=1)


def _sum3(y, w):
    return y[:, 0:w] + y[:, w:2 * w] + y[:, 2 * w:3 * w]


def _gla_kernel(q_ref, k_ref, v_ref, a_ref, r_ref, wa_ref, ba_ref, ng_ref, o_ref,
                qd_s, ki_s, ke_s, dec_s, oacc_s, st_s, *, t, ctx_len):
    ck = GLA_CHUNK
    nc = t // ck
    ncc = ctx_len // ck
    pr = 256
    w = 2 * GLA_DK

    ri = lax.broadcasted_iota(jnp.int32, (pr, pr), 0)
    ci = lax.broadcasted_iota(jnp.int32, (pr, pr), 1)
    same = (ri // ck) == (ci // ck)
    m_tot = jnp.where(same, 1.0, 0.0).astype(BF16)
    m_cum = (jnp.where(same & (ci <= ri), 1.0, 0.0).astype(BF16),
             jnp.where(same & (ci >= ri), 1.0, 0.0).astype(BF16))
    for r0 in range(0, t, pr):
        a = a_ref[0, r0:r0 + pr, :].astype(BF16)
        logit = jnp.dot(a, wa_ref[0], preferred_element_type=F32) + ba_ref[0]
        g = (jnp.minimum(logit, 0.0) - jnp.log1p(jnp.exp(-jnp.abs(logit)))) * (1.0 / GLA_TAU)
        q = q_ref[0, r0:r0 + pr, :]
        k = k_ref[0, r0:r0 + pr, :]
        for d in range(2):
            g3 = _split3(g[:, d * w:(d + 1) * w])
            g_cum = _sum3(jnp.dot(m_cum[d], g3, preferred_element_type=F32), w)
            g_tot = _sum3(jnp.dot(m_tot, g3, preferred_element_type=F32), w)
            qd_s[d, r0:r0 + pr, :] = (q * jnp.exp(g_cum)).astype(BF16)
            ki_s[d, r0:r0 + pr, :] = (k * jnp.exp(-g_cum)).astype(BF16)
            ke_s[d, r0:r0 + pr, :] = (k * jnp.exp(g_tot - g_cum)).astype(BF16)
            dec_s[d, r0:r0 + pr, :] = jnp.exp(g_tot)
        oacc_s[r0:r0 + pr, :] = jnp.zeros((pr, 2 * GLA_DV), F32)
    st_s[...] = jnp.zeros_like(st_s)

    ri = lax.broadcasted_iota(jnp.int32, (2 * ck, ck), 0)
    ci = lax.broadcasted_iota(jnp.int32, (2 * ck, ck), 1)
    rr = jnp.where(ri >= ck, ri - ck, ri)
    tri = (ci <= rr, ci >= rr)
    lane = lax.broadcasted_iota(jnp.int32, (ck, w), 1)
    br = lax.broadcasted_iota(jnp.int32, (2 * GLA_DV, w), 0)
    bc = lax.broadcasted_iota(jnp.int32, (2 * GLA_DV, w), 1)
    blk = (br < GLA_DV) == (bc < GLA_DK)
    nt = (((1,), (1,)), ((), ()))
    tn = (((0,), (0,)), ((), ()))

    def chunk(d, r0):
        qd = qd_s[d, pl.ds(r0, ck), :]
        ki = ki_s[d, pl.ds(r0, ck), :]
        ke = ke_s[d, pl.ds(r0, ck), :]
        v = v_ref[0, pl.ds(r0, ck), :]
        dec = dec_s[d, pl.ds(r0, 8), :][0:1, :]
        zq = jnp.zeros_like(qd)
        qs = jnp.concatenate([jnp.where(lane < GLA_DK, qd, zq), jnp.where(lane >= GLA_DK, qd, zq)], axis=0)
        att = lax.dot_general(qs, ki, nt, preferred_element_type=F32)
        att = jnp.where(tri[d], att, 0.0).astype(BF16)
        oi = jnp.dot(att, v, preferred_element_type=F32)
        o_intra = jnp.concatenate([oi[:ck, :GLA_DV], oi[ck:, GLA_DV:]], axis=1)
        st = st_s[d]
        o_inter = lax.dot_general(qd, st.astype(BF16), nt, preferred_element_type=F32)
        s_loc = lax.dot_general(v, ke, tn, preferred_element_type=F32)
        st_s[d] = st * dec + jnp.where(blk, s_loc, 0.0)
        oacc_s[pl.ds(r0, ck), :] = oacc_s[pl.ds(r0, ck), :] + o_intra + o_inter

    def body(n, _):
        chunk(0, pl.multiple_of(n * ck, ck))
        nb = jnp.where(n < ncc, ncc - 1 - n, nc - 1 - (n - ncc))
        chunk(1, pl.multiple_of(nb * ck, ck))
        return 0

    lax.fori_loop(0, nc, body, 0, unroll=GLA_UNROLL)

    ng = ng_ref[...]
    rt = _row_tile(t, 544)
    for r0 in range(0, t, rt):
        o = oacc_s[r0:r0 + rt, :]
        r = r_ref[0, r0:r0 + rt, :]
        outs = []
        for hh in range(2):
            oh = o[:, hh * GLA_DV:(hh + 1) * GLA_DV]
            outs.append(oh * lax.rsqrt(jnp.mean(oh * oh, axis=-1, keepdims=True) + RMS_EPS) * ng)
        o_ref[0, r0:r0 + rt, :] = (jnp.concatenate(outs, axis=1) * (r * _sigmoid(r))).astype(BF16)


def _gla(a16, a32, wa, ba, ng, *, ctx_len):
    b, t, _ = a16.shape
    assert t % 256 == 0 and ctx_len % GLA_CHUNK == 0 and (t // GLA_CHUNK) % GLA_UNROLL == 0
    return pl.pallas_call(
        functools.partial(_gla_kernel, t=t, ctx_len=ctx_len),
        grid=(b, 2),
        in_specs=[
            pl.BlockSpec((1, t, 128), lambda bi, p: (bi, 0, A32_GLQ // 128 + p)),
            pl.BlockSpec((1, t, 128), lambda bi, p: (bi, 0, A32_GLK // 128 + p)),
            pl.BlockSpec((1, t, 256), lambda bi, p: (bi, 0, A16_GV // 256 + p)),
            pl.BlockSpec((1, t, 128), lambda bi, p: (bi, 0, A32_GLA // 128)),
            pl.BlockSpec((1, t, 256), lambda bi, p: (bi, 0, A32_GLR // 256 + p)),
            pl.BlockSpec((1, 128, 256), lambda bi, p: (p, 0, 0)),
            pl.BlockSpec((1, 1, 256), lambda bi, p: (p, 0, 0)),
            pl.BlockSpec((1, GLA_DV), lambda bi, p: (0, 0)),
        ],
        out_specs=pl.BlockSpec((1, t, 256), lambda bi, p: (bi, 0, p)),
        out_shape=jax.ShapeDtypeStruct((b, t, GLA_HEADS * GLA_DV), BF16),
        scratch_shapes=[
            pltpu.VMEM((2, t, 128), BF16), pltpu.VMEM((2, t, 128), BF16), pltpu.VMEM((2, t, 128), BF16),
            pltpu.VMEM((2, t, 128), F32),
            pltpu.VMEM((t, 256), F32),
            pltpu.VMEM((2, 256, 128), F32),
        ],
        compiler_params=_cparams(("parallel", "parallel")),
        name="gla_bidir",
    )(a32, a32, a16, a32, a32, wa, ba, ng)


def _merge_kernel(x_ref, od_ref, og_ref, gates_ref, cvb_ref, cvc_ref, cvx_ref,
                  cvc_p_ref, cvx_p_ref, cvc_n_ref, cvx_n_ref, cw_ref,
                  wdo_ref, wco_ref, wgo_ref, wo_ref, modl_ref, modc_ref, n2g_ref,
                  rw_ref, rb_ref, xo_ref, h2_ref, rt_ref, *, tm, ctx_len, t):
    i = pl.program_id(1)
    pos = i * tm + lax.broadcasted_iota(jnp.int32, (tm, 1), 0)
    is_ctx = pos < ctx_len
    row = lax.broadcasted_iota(jnp.int32, (tm, 1), 0)

    z = cvc_ref[0] * cvx_ref[0]
    z_before = cvc_p_ref[0, 7:8, :] * cvx_p_ref[0, 7:8, :]
    z_after = cvc_n_ref[0, 0:1, :] * cvx_n_ref[0, 0:1, :]
    z_prev = jnp.where(row == 0, z_before, pltpu.roll(z, 1, 0))
    z_prev = jnp.where((pos == 0) | (pos == ctx_len), 0.0, z_prev)
    z_next = jnp.where(row == tm - 1, z_after, pltpu.roll(z, tm - 1, 0))
    z_next = jnp.where((pos == ctx_len - 1) | (pos == t - 1), 0.0, z_next)
    conv = z_prev * cw_ref[0:1, :] + z * cw_ref[1:2, :] + z_next * cw_ref[2:3, :]
    zc = (cvb_ref[0] * conv).astype(BF16)

    y_diff = jnp.dot(od_ref[0], wdo_ref[...], preferred_element_type=F32)
    y_conv = jnp.dot(zc, wco_ref[...], preferred_element_type=F32)
    y_gla = jnp.dot(og_ref[0], wgo_ref[...], preferred_element_type=F32)
    d = D_MODEL
    mix = (_sigmoid(gates_ref[0, :, 0:d]) * y_diff
           + _sigmoid(gates_ref[0, :, d:2 * d]) * y_conv
           + _sigmoid(gates_ref[0, :, 2 * d:3 * d]) * y_gla)
    m = jnp.dot(mix.astype(BF16), wo_ref[...], preferred_element_type=F32)

    g1 = jnp.where(is_ctx, modc_ref[2:3, :], modl_ref[0, 2:3, :])
    x_new = x_ref[0] + g1 * m
    xo_ref[0] = x_new
    sh2 = jnp.where(is_ctx, modc_ref[3:4, :], modl_ref[0, 3:4, :])
    sc2 = jnp.where(is_ctx, modc_ref[4:5, :], modl_ref[0, 4:5, :])
    h2 = _norm_mod(x_new, n2g_ref[...], sh2, sc2)
    for cc in range(d // 128):
        h2_ref[0, pl.ds(cc, tm, stride=8), :] = h2[:, cc * 128:(cc + 1) * 128]

    logits = jnp.dot(h2.astype(BF16), rw_ref[...], preferred_element_type=F32) + rb_ref[...]
    lane = lax.broadcasted_iota(jnp.int32, logits.shape, 1)
    vals = logits
    tops = []
    idx_out = jnp.zeros(logits.shape, jnp.int32)
    for kk in range(TOP_K):
        mx = jnp.max(vals, axis=-1, keepdims=True)
        ix = jnp.min(jnp.where(vals == mx, lane, 128), axis=-1, keepdims=True)
        tops.append(mx)
        idx_out = jnp.where(lane == kk, ix, idx_out)
        vals = jnp.where(lane == ix, -jnp.inf, vals)
    es = [jnp.exp(v - tops[0]) for v in tops]
    den = es[0] + es[1] + es[2] + es[3]
    packed = idx_out.astype(F32)
    for kk in range(TOP_K):
        gk = es[kk] / den
        hi = gk.astype(BF16).astype(F32)
        mid = (gk - hi).astype(BF16).astype(F32)
        lo = gk - hi - mid
        for piece, val in enumerate((hi, mid, lo)):
            packed = jnp.where(lane == TOP_K * (piece + 1) + kk, val, packed)
    ri = lax.broadcasted_iota(jnp.int32, (tm, tm), 0)
    ci = lax.broadcasted_iota(jnp.int32, (tm, tm), 1)
    eye = jnp.where(ri == ci, 1.0, 0.0).astype(BF16)
    rows = lax.dot_general(packed.astype(BF16), eye, (((0,), (0,)), ((), ())), preferred_element_type=F32)
    rt_ref[0, 0] = rows[0:4 * TOP_K, :]


def _merge(x, od, og, a32, cw, wdo, wco, wgo, wo, modl, modc, n2g, rw, rb, *, ctx_len):
    b, t, d = x.shape
    tm = _row_tile(t, 544)
    nt8 = t // 8
    row = lambda bi, i: (bi, i, 0)
    const2 = lambda bi, i: (0, 0)
    return pl.pallas_call(
        functools.partial(_merge_kernel, tm=tm, ctx_len=ctx_len, t=t),
        grid=(b, t // tm),
        in_specs=[
            pl.BlockSpec((1, tm, d), row),
            pl.BlockSpec((1, tm, d), row),
            pl.BlockSpec((1, tm, GLA_HEADS * GLA_DV), row),
            pl.BlockSpec((1, tm, 3 * d), lambda bi, i: (bi, i, 0)),
            pl.BlockSpec((1, tm, CONV_W), lambda bi, i: (bi, i, A32_CVB // CONV_W)),
            pl.BlockSpec((1, tm, CONV_W), lambda bi, i: (bi, i, A32_CVC // CONV_W)),
            pl.BlockSpec((1, tm, CONV_W), lambda bi, i: (bi, i, A32_CVX // CONV_W)),
            pl.BlockSpec((1, 8, CONV_W), lambda bi, i: (bi, jnp.maximum(i * (tm // 8) - 1, 0),
                                                   A32_CVC // CONV_W)),
            pl.BlockSpec((1, 8, CONV_W), lambda bi, i: (bi, jnp.maximum(i * (tm // 8) - 1, 0),
                                                   A32_CVX // CONV_W)),
            pl.BlockSpec((1, 8, CONV_W), lambda bi, i: (bi, jnp.minimum((i + 1) * (tm // 8), nt8 - 1),
                                                   A32_CVC // CONV_W)),
            pl.BlockSpec((1, 8, CONV_W), lambda bi, i: (bi, jnp.minimum((i + 1) * (tm // 8), nt8 - 1),
                                                   A32_CVX // CONV_W)),
            pl.BlockSpec((8, CONV_W), const2),
            pl.BlockSpec((d, d), const2),
            pl.BlockSpec((CONV_W, d), const2),
            pl.BlockSpec((GLA_HEADS * GLA_DV, d), const2),
            pl.BlockSpec((d, d), const2),
            pl.BlockSpec((1, 8, d), lambda bi, i: (bi, 0, 0)),
            pl.BlockSpec((8, d), const2),
            pl.BlockSpec((1, d), const2),
            pl.BlockSpec((d, 128), const2),
            pl.BlockSpec((1, 128), const2),
        ],
        out_specs=[
            pl.BlockSpec((1, tm, d), row),
            pl.BlockSpec((1, tm * (d // 128), 128), row),
            pl.BlockSpec((1, 1, 4 * TOP_K, tm), lambda bi, i: (bi, i, 0, 0)),
        ],
        out_shape=[
            jax.ShapeDtypeStruct((b, t, d), F32),
            jax.ShapeDtypeStruct((b, t * (d // 128), 128), F32),
            jax.ShapeDtypeStruct((b, t // tm, 4 * TOP_K, tm), F32),
        ],
        compiler_params=_cparams(("parallel", "parallel")),
        name="merge_router",
    )(x, od, og, a32, a32, a32, a32, a32, a32, a32, a32, cw, wdo, wco, wgo, wo,
      modl, modc, n2g, rw, rb)


def _moe_kernel(be_ref, nv_ref, g_ref, w1g_ref, w1l_ref, b1g_ref, b1l_ref, w2_ref, b2_ref,
                src_hbm, dst_hbm, h2_hbm, out_hbm, src_s, dst_s, xbuf, ybuf, sem_p, sem_g, sem_s,
                *, nb, n_slots):
    i = pl.program_id(0)
    nv = nv_ref[0]
    bm = MOE_BLOCK
    cur = lax.rem(i, 2)
    nxt = 1 - cur

    def idx_copies(blk_src, blk_dst, slot):
        return (pltpu.make_async_copy(src_hbm.at[blk_src, 0], src_s.at[pl.ds(slot * bm, bm)], sem_p),
                pltpu.make_async_copy(dst_hbm.at[blk_dst, 0], dst_s.at[pl.ds(slot * bm, bm)], sem_p))

    def start_gather(slot, buf):
        for j in range(bm):
            r0 = pl.multiple_of(src_s[slot * bm + j], TOK_ROWS)
            pltpu.make_async_copy(h2_hbm.at[pl.ds(r0, TOK_ROWS)],
                                  xbuf.at[buf, pl.ds(j * TOK_ROWS, TOK_ROWS)],
                                  sem_g.at[buf]).start(priority=j % 2)

    def wait_gather(buf):
        pltpu.make_async_copy(h2_hbm.at[pl.ds(0, bm * TOK_ROWS)], xbuf.at[buf], sem_g.at[buf]).wait()

    def wait_scatter():
        pltpu.make_async_copy(ybuf.at[0], out_hbm.at[pl.ds(0, bm * TOK_ROWS)], sem_s).wait()

    @pl.when(i <= nv)
    def _():
        @pl.when(i == 0)
        def _():
            ybuf[...] = jnp.zeros_like(ybuf)
            for blk_src, blk_dst, slot in ((0, 0, 0), (1, 1, 1), (0, nb, 3)):
                for cp in idx_copies(blk_src, blk_dst, slot):
                    cp.start()
                    cp.wait()
            start_gather(0, 0)

        @pl.when(i >= 1)
        def _():
            for cp in idx_copies(0, 0, 0):
                cp.wait()
            wait_scatter()

        wait_gather(cur)

        @pl.when(i < nv)
        def _():
            blk = jnp.minimum(i + 2, nv - 1)
            for cp in idx_copies(blk, blk, lax.rem(i + 2, 4)):
                cp.start()

        start_gather(lax.rem(i + 1, 4), nxt)
        pslot = lax.rem(i + 3, 4)
        for j in range(bm):
            r0 = pl.multiple_of(dst_s[pslot * bm + j], TOK_ROWS)
            pltpu.make_async_copy(ybuf.at[nxt, pl.ds(j * TOK_ROWS, TOK_ROWS)],
                                  out_hbm.at[pl.ds(r0, TOK_ROWS)], sem_s).start(priority=j % 2)
        x = jnp.concatenate([xbuf[cur, pl.ds(cc, bm, stride=TOK_ROWS), :] for cc in range(TOK_ROWS)],
                            axis=1).astype(BF16)
        hg = jnp.dot(x, w1g_ref[0, 0], preferred_element_type=F32) + b1g_ref[0]
        hl = jnp.dot(x, w1l_ref[0, 0], preferred_element_type=F32) + b1l_ref[0]
        hg = jnp.minimum(hg, SWIGLU_LIMIT)
        hl = jnp.clip(hl, -SWIGLU_LIMIT, SWIGLU_LIMIT)
        act = hg * _sigmoid(SWIGLU_ALPHA * hg) * (hl + 1.0)
        y = jnp.dot(act.astype(BF16), w2_ref[0], preferred_element_type=F32) + b2_ref[0]
        gi = lax.broadcasted_iota(jnp.int32, (bm, bm), 0)
        gj = lax.broadcasted_iota(jnp.int32, (bm, bm), 1)
        y = y * jnp.sum(jnp.where(gi == gj, g_ref[0], 0.0), axis=1, keepdims=True)
        for cc in range(TOK_ROWS):
            ybuf[cur, pl.ds(cc, bm, stride=TOK_ROWS), :] = y[:, cc * 128:(cc + 1) * 128]

        @pl.when(i == nv)
        def _():
            wait_scatter()
            wait_gather(nxt)
            ybuf[...] = jnp.zeros_like(ybuf)
            fills = [pltpu.make_async_copy(
                ybuf.at[sl], out_hbm.at[pl.ds((n_slots + sl * bm) * TOK_ROWS, bm * TOK_ROWS)], sem_s)
                for sl in range(2)]
            for cp in fills:
                cp.start()
            for cp in fills:
                cp.wait()


def _moe(block_exp, n_valid, row_src, row_dst, row_gate, h2_flat, w1, b1g, b1l, w2, b2, *, n_slots):
    d = D_MODEL
    nb = row_src.shape[0]
    bm = MOE_BLOCK
    de = w1.shape[-1]
    blk = lambda i, nv: jnp.minimum(i, nv[0] - 1)
    wmap = lambda i, be, nv: (be[blk(i, nv)], 0, 0)
    return pl.pallas_call(
        functools.partial(_moe_kernel, nb=nb, n_slots=n_slots),
        grid_spec=pltpu.PrefetchScalarGridSpec(
            num_scalar_prefetch=2,
            grid=(nb + 1,),
            in_specs=[
                pl.BlockSpec((1, 1, bm), lambda i, be, nv: (blk(i, nv), 0, 0)),
                pl.BlockSpec((1, 1, d, de), lambda i, be, nv: (be[blk(i, nv)], 0, 0, 0)),
                pl.BlockSpec((1, 1, d, de), lambda i, be, nv: (be[blk(i, nv)], 1, 0, 0)),
                pl.BlockSpec((1, 1, de), wmap),
                pl.BlockSpec((1, 1, de), wmap),
                pl.BlockSpec((1, de, d), wmap),
                pl.BlockSpec((1, 1, d), wmap),
                pl.BlockSpec(memory_space=pl.ANY),
                pl.BlockSpec(memory_space=pl.ANY),
                pl.BlockSpec(memory_space=pl.ANY),
            ],
            out_specs=pl.BlockSpec(memory_space=pl.ANY),
            scratch_shapes=[
                pltpu.SMEM((4 * bm,), jnp.int32),
                pltpu.SMEM((4 * bm,), jnp.int32),
                pltpu.VMEM((2, bm * TOK_ROWS, 128), F32),
                pltpu.VMEM((2, bm * TOK_ROWS, 128), F32),
                pltpu.SemaphoreType.DMA,
                pltpu.SemaphoreType.DMA((2,)),
                pltpu.SemaphoreType.DMA,
            ],
        ),
        out_shape=jax.ShapeDtypeStruct(((n_slots + 2 * bm) * TOK_ROWS, 128), F32),
        compiler_params=_cparams(("arbitrary",)),
        name="moe_experts",
    )(block_exp, n_valid, row_gate, w1, w1, b1g, b1l, w2, b2, row_src, row_dst, h2_flat)


def _combine_kernel(x_ref, y4_ref, modl_ref, modc_ref, fg_ref, o_ref, *, tm, ctx_len, final, skip):
    i = pl.program_id(1) + skip
    d = D_MODEL
    ld = lambda k, cc: y4_ref[pl.ds(k * TOK_ROWS + cc, tm, stride=TOP_K * TOK_ROWS), :]
    f = jnp.concatenate([(ld(0, cc) + ld(1, cc)) + (ld(2, cc) + ld(3, cc)) for cc in range(TOK_ROWS)],
                        axis=1)
    pos = i * tm + lax.broadcasted_iota(jnp.int32, (tm, 1), 0)
    g2 = jnp.where(pos < ctx_len, modc_ref[5:6, :], modl_ref[0, 5:6, :])
    x_new = x_ref[0] + g2 * f
    if final:
        x_new = x_new * lax.rsqrt(jnp.mean(x_new * x_new, axis=-1, keepdims=True) + RMS_EPS) * fg_ref[...]
    o_ref[0] = x_new


def _combine(x, y4, modl, modc, fg, *, ctx_len, final):
    b, t, d = x.shape
    if final:
        tm = Q_TILE
        assert ctx_len % tm == 0
        skip = ctx_len // tm
    else:
        tm = _row_tile(t, 272)
        skip = 0
    npb = t // tm
    return pl.pallas_call(
        functools.partial(_combine_kernel, tm=tm, ctx_len=ctx_len, final=final, skip=skip),
        grid=(b, npb - skip),
        in_specs=[
            pl.BlockSpec((1, tm, d), lambda bi, i: (bi, i + skip, 0)),
            pl.BlockSpec((tm * TOP_K * TOK_ROWS, 128), lambda bi, i: (bi * (npb - skip) + i, 0)),
            pl.BlockSpec((1, 8, d), lambda bi, i: (bi, 0, 0)),
            pl.BlockSpec((8, d), lambda bi, i: (0, 0)),
            pl.BlockSpec((1, d), lambda bi, i: (0, 0)),
        ],
        out_specs=pl.BlockSpec((1, tm, d), lambda bi, i: (bi, i, 0)),
        out_shape=jax.ShapeDtypeStruct((b, t - skip * tm, d), F32),
        compiler_params=_cparams(("parallel", "parallel")),
        name="moe_combine",
    )(x, y4, modl, modc, fg)


def _routing(e_flat, gate_flat, *, tm, seq, ctx_len, latent_only):
    npair = e_flat.shape[0]
    t = seq + ctx_len
    bm = MOE_BLOCK
    nb = npair // bm + N_EXPERTS
    nr = nb * bm
    npad = nr - npair
    experts = jnp.arange(N_EXPERTS, dtype=jnp.int32)
    ids = jnp.arange(npair, dtype=jnp.int32)
    tok_of = lambda i: (i // (TOP_K * tm)) * tm + i % tm
    if latent_only:
        e_flat = jnp.where(tok_of(ids) % t >= ctx_len, e_flat, N_EXPERTS)
    counts = jnp.sum((e_flat[:, None] == experts[None, :]).astype(jnp.int32), axis=0)
    padded = (counts + bm - 1) // bm * bm
    pad_end = jnp.cumsum(padded)
    fill_end = jnp.cumsum(padded - counts)
    dummy = jnp.arange(npad, dtype=jnp.int32)
    e_dummy = jnp.sum((dummy[:, None] >= fill_end[None, :]).astype(jnp.int32), axis=1)
    kb = nr
    keys = jnp.concatenate([e_flat * kb + ids, e_dummy * kb + npair + dummy])
    gates = jnp.concatenate([gate_flat, jnp.zeros((npad,), F32)])
    skey, sgate = lax.sort((keys, gates), num_keys=1)
    e_row = skey // kb
    ident = skey - e_row * kb
    valid = (ident < npair) & (e_row < N_EXPERTS)
    pair = jnp.where(valid, ident, 0)
    tok = tok_of(pair)
    kk = (pair // tm) % TOP_K
    row_gate = jnp.where(valid, sgate, 0.0).reshape(nb, 1, bm)
    row_src = (tok * TOK_ROWS).reshape(nb, 1, bm)
    if latent_only:
        slot = (tok // t) * seq + tok % t - ctx_len
        n_slots = (npair // TOP_K // t) * seq * TOP_K
    else:
        slot = tok
        n_slots = npair
    j = jnp.arange(nr, dtype=jnp.int32)
    dump = n_slots + ((j // bm) % 2) * bm + j % bm
    row_dst = jnp.where(valid, slot * TOP_K + kk, dump).reshape(nb, bm)
    row_dst = jnp.concatenate([row_dst, (n_slots + bm + jnp.arange(bm, dtype=jnp.int32))[None, :]],
                              axis=0)[:, None, :] * TOK_ROWS
    blocks = jnp.arange(nb, dtype=jnp.int32) * bm
    block_exp = jnp.minimum(jnp.sum((blocks[:, None] >= pad_end[None, :]).astype(jnp.int32), axis=1),
                            N_EXPERTS - 1)
    n_valid = (pad_end[-1] // bm).astype(jnp.int32).reshape(1)
    return row_src, row_dst, row_gate, block_exp, n_valid, n_slots


def _rope_tables(ctx_len, seq_len):
    inv_freq = ROPE_BASE ** (-jnp.arange(0, ROT_AXIS_DIM, 2, dtype=F32) / ROT_AXIS_DIM)
    p = jnp.arange(seq_len, dtype=jnp.int32)
    row = (p // GRID_W).astype(F32)
    col = (p % GRID_W).astype(F32)
    lane = np.arange(128)
    dd = lane % DIFF_HEAD_DIM
    fidx = jnp.asarray((dd % ROT_AXIS_DIM) % (ROT_AXIS_DIM // 2))
    use_col = jnp.asarray(dd >= ROT_AXIS_DIM)
    second_half = jnp.asarray((dd % ROT_AXIS_DIM) >= ROT_AXIS_DIM // 2)
    posm = jnp.where(use_col[None, :], col[:, None], row[:, None])
    ang = posm * inv_freq[fidx][None, :]
    cos = jnp.cos(ang)
    sin = jnp.sin(ang)
    sa = jnp.where(second_half[None, :], sin, 0.0)
    sb = jnp.where(second_half[None, :], 0.0, -sin)
    pad = lambda a, v: jnp.concatenate([jnp.full((ctx_len, 128), v, F32), a], axis=0)
    return pad(cos, 1.0), pad(sa, 0.0), pad(sb, 0.0)


def kernel(x, c, ctx, c_ctx, ada_w, ada_b, norm1_g, norm2_g, w_in, diff_lambda, diff_subln_g,
           diff_w_out, conv_w, conv_w_out, gla_w_a2, gla_b_a, gla_norm_g, gla_w_out, w_o,
           router_w, router_b, moe_w1, moe_b1, moe_w2, moe_b2, final_norm_g):
    b, s, d = x.shape
    cl = ctx.shape[1]
    t = cl + s
    nl = w_in.shape[0]
    assert d == D_MODEL and b <= 15

    xs = jnp.concatenate([ctx, x], axis=1)
    cc = jnp.concatenate([c, c_ctx[None, :], jnp.zeros((15 - b, d), F32)], axis=0)
    mod = _ada(cc, ada_w, ada_b).reshape(nl, 16, N_ADA, d)
    cos, sa, sb = _rope_tables(cl, s)

    col_scale = np.ones((w_in.shape[-1],), np.float32)
    col_scale[0:1024] = DIFF_HEAD_DIM ** -0.5
    col_scale[4608:4864] = GLA_DK ** -0.5
    w_s = (w_in * jnp.asarray(col_scale)).astype(BF16)
    w_all = jnp.concatenate(
        [w_s[..., :W_GATES], w_s[..., W_GATES + 2 * GLA_RANK:], w_s[..., W_GATES:W_GATES + 2 * GLA_RANK],
         jnp.zeros((nl, d, W_ALL_COLS - w_in.shape[-1]), BF16)], axis=-1)

    wa = jnp.zeros((nl, 2, 128, 256), F32)
    for p in range(2):
        wa = wa.at[:, p, 0:GLA_RANK, 0:128].set(gla_w_a2[:, 0, :, p * 128:(p + 1) * 128])
        wa = wa.at[:, p, GLA_RANK:2 * GLA_RANK, 128:256].set(gla_w_a2[:, 1, :, p * 128:(p + 1) * 128])
    wa = wa.astype(BF16)
    ba = jnp.stack([jnp.concatenate([gla_b_a[:, 0, p * 128:(p + 1) * 128],
                                     gla_b_a[:, 1, p * 128:(p + 1) * 128]], axis=-1)
                    for p in range(2)], axis=1)[:, :, None, :]

    cw = jnp.concatenate([conv_w, jnp.zeros((nl, 5, CONV_W), F32)], axis=1)
    rw = jnp.concatenate([router_w, jnp.zeros((nl, d, 128 - N_EXPERTS), F32)], axis=-1).astype(BF16)
    rb = jnp.concatenate([router_b, jnp.full((nl, 128 - N_EXPERTS), -1e30, F32)], axis=-1)[:, None, :]
    w1 = jnp.moveaxis(moe_w1.reshape(nl, N_EXPERTS, d, D_EXPERT, 2), -1, 2).astype(BF16)
    b1 = moe_b1.reshape(nl, N_EXPERTS, 1, D_EXPERT, 2)
    b1g = b1[..., 0]
    b1l = b1[..., 1]
    w2 = moe_w2.astype(BF16)
    b2 = moe_b2[:, :, None, :]
    wdo = diff_w_out.astype(BF16)
    wco = conv_w_out.astype(BF16)
    wgo = gla_w_out.astype(BF16)
    wo = w_o.astype(BF16)

    for layer in range(nl):
        last = layer == nl - 1
        lam_init = 0.8 - 0.6 * math.exp(-0.3 * layer)
        ml = mod[layer]
        modl = jnp.concatenate([ml[:b], jnp.zeros((b, 2, d), F32)], axis=1)
        modc = jnp.concatenate([ml[b], jnp.zeros((2, d), F32)], axis=0)

        a16, a32 = _k1(xs, modl, modc, norm1_g[layer][None, :], cos, sa, sb, w_all[layer], ctx_len=cl)
        od = _attn(a16, diff_lambda[layer], diff_subln_g[layer][None, :],
                   ctx_len=cl, seq_len=s, lam_init=lam_init)
        og = _gla(a16, a32, wa[layer], ba[layer], gla_norm_g[layer][None, :], ctx_len=cl)
        xs, h2, rt = _merge(xs, od, og, a32, cw[layer], wdo[layer], wco[layer], wgo[layer],
                                   wo[layer], modl, modc, norm2_g[layer][None, :], rw[layer],
                                   rb[layer], ctx_len=cl)

        r = b * t
        tm_r = rt.shape[-1]
        rt = rt.reshape(-1, 4, TOP_K, tm_r)
        row_src, row_dst, row_gate, block_exp, n_valid, n_slots = _routing(
            rt[:, 0].astype(jnp.int32).reshape(-1), (rt[:, 1] + rt[:, 2] + rt[:, 3]).reshape(-1),
            tm=tm_r, seq=s, ctx_len=cl, latent_only=last)
        y4 = _moe(block_exp, n_valid, row_src, row_dst, row_gate, h2.reshape(r * TOK_ROWS, 128), w1[layer],
                  b1g[layer], b1l[layer], w2[layer], b2[layer], n_slots=n_slots)
        xs = _combine(xs, y4, modl, modc, final_norm_g[None, :], ctx_len=cl, final=last)

    return xs
```

```python
import functools
import math

import numpy as np
import jax
import jax.numpy as jnp
from jax import lax
from jax.experimental import pallas as pl
from jax.experimental.pallas import tpu as pltpu

F32 = jnp.float32
BF16 = jnp.bfloat16

D_MODEL = 1024
GRID_W = 64
RMS_EPS = 1e-6
N_ADA = 6
DIFF_HEADS = 8
DIFF_HEAD_DIM = 64
DIFF_V_DIM = 128
ROPE_BASE = 10000.0
ROT_AXIS_DIM = 32
CONV_W = 512
GLA_HEADS = 4
GLA_DK = 64
GLA_DV = 128
GLA_RANK = 16
GLA_TAU = 16.0
GLA_CHUNK = 64
N_EXPERTS = 32
TOP_K = 4
D_EXPERT = 1024
SWIGLU_LIMIT = 7.0
SWIGLU_ALPHA = 1.702
LOG2_E = 1.4426950408889634

A16_Q, A16_K, A16_V, A16_GV, A16_W = 0, 1024, 2048, 3072, 3584
A32_GATES, A32_CVB, A32_CVC, A32_CVX = 0, 3072, 3584, 4096
A32_GLQ, A32_GLK, A32_GLR, A32_GLA, A32_W = 4608, 4864, 5120, 5632, 5760
W_ALL_COLS = A16_W + A32_W
W_GATES, W_GLA = 6144, 9216


def _k1_chunks():
    out = []
    def add(w0, width, to16, o0, rope=False, ch=512):
        for c in range(0, width, ch):
            out.append((w0 + c, min(ch, width - c), to16, o0 + c, rope))
    add(0, 1024, True, A16_Q, rope=True)
    add(1024, 1024, True, A16_K, rope=True)
    add(2048, 1024, True, A16_V)
    add(3072, 512, False, A32_CVB)
    add(3584, 512, False, A32_CVC)
    add(4096, 512, False, A32_CVX)
    add(4608, 256, False, A32_GLQ)
    add(4864, 256, False, A32_GLK)
    add(5120, 512, True, A16_GV)
    add(5632, 512, False, A32_GLR)
    add(W_GATES, 3072, False, A32_GATES)
    add(W_GLA, 128, False, A32_GLA)
    return tuple(out)


K1_CHUNKS = _k1_chunks()

Q_TILE = 256
KV_SEGMENTS = 2
ATTN_HEADS = 4
MOE_BLOCK = 256
GLA_UNROLL = 2
TOK_ROWS = 8
VMEM_LIMIT = 56 * 1024 * 1024


def _cparams(sem):
    return pltpu.CompilerParams(dimension_semantics=sem, vmem_limit_bytes=VMEM_LIMIT)


def _row_tile(t, target):
    best = None
    for cand in range(16, target + 1, 16):
        if t % cand == 0:
            best = cand
    assert best is not None, t
    return best


def _sigmoid(v):
    return 1.0 / (1.0 + jnp.exp(-v))


def _ada_kernel(cc_ref, w_ref, b_ref, o_ref):
    a = cc_ref[...]
    a = a * _sigmoid(a)
    o_ref[0] = jnp.dot(a, w_ref[0], precision=lax.Precision.HIGHEST,
                       preferred_element_type=F32) + b_ref[0]


def _ada(cc, ada_w, ada_b):
    nl, d, n = ada_w.shape
    tn = 1024
    return pl.pallas_call(
        _ada_kernel,
        grid=(nl, n // tn),
        in_specs=[
            pl.BlockSpec((16, d), lambda l, j: (0, 0)),
            pl.BlockSpec((1, d, tn), lambda l, j: (l, 0, j)),
            pl.BlockSpec((1, 1, tn), lambda l, j: (l, 0, j)),
        ],
        out_specs=pl.BlockSpec((1, 16, tn), lambda l, j: (l, 0, j)),
        out_shape=jax.ShapeDtypeStruct((nl, 16, n), F32),
        compiler_params=_cparams(("parallel", "parallel")),
        name="ada_mod",
    )(cc, ada_w, ada_b.reshape(nl, 1, n))


def _norm_mod(x, ng, shift, scale):
    y = x * lax.rsqrt(jnp.mean(x * x, axis=-1, keepdims=True) + RMS_EPS) * ng
    return y * (1.0 + scale) + shift


def _k1_kernel(x_ref, modl_ref, modc_ref, ng_ref, cos_ref, sa_ref, sb_ref, w_ref,
               o16_ref, o32_ref, *, tm, ctx_len):
    i = pl.program_id(1)
    pos = i * tm + lax.broadcasted_iota(jnp.int32, (tm, 1), 0)
    is_ctx = pos < ctx_len
    shift = jnp.where(is_ctx, modc_ref[0:1, :], modl_ref[0, 0:1, :])
    scale = jnp.where(is_ctx, modc_ref[1:2, :], modl_ref[0, 1:2, :])
    h = _norm_mod(x_ref[0], ng_ref[...], shift, scale).astype(BF16)
    cos = cos_ref[...]
    sa = sa_ref[...]
    sb = sb_ref[...]
    for w0, width, to16, o0, rope in K1_CHUNKS:
        acc = jnp.dot(h, w_ref[0, :, w0:w0 + width], preferred_element_type=F32)
        if rope:
            parts = []
            for j in range(width // 128):
                a = acc[:, j * 128:(j + 1) * 128]
                parts.append(a * cos + pltpu.roll(a, 16, 1) * sa + pltpu.roll(a, 112, 1) * sb)
            acc = jnp.concatenate(parts, axis=1)
        if to16 and o0 < A16_K:
            acc = acc * LOG2_E
        if to16:
            o16_ref[0, :, o0:o0 + width] = acc.astype(BF16)
        else:
            o32_ref[0, :, o0:o0 + width] = acc


def _k1(x, modl, modc, ng, cos, sa, sb, w_all, *, layer, ctx_len):
    b, t, d = x.shape
    tm = _row_tile(t, 272)
    tbl = pl.BlockSpec((tm, 128), lambda bi, i: (i, 0))
    return pl.pallas_call(
        functools.partial(_k1_kernel, tm=tm, ctx_len=ctx_len),
        grid=(b, t // tm),
        in_specs=[
            pl.BlockSpec((1, tm, d), lambda bi, i: (bi, i, 0)),
            pl.BlockSpec((1, 8, d), lambda bi, i: (bi, 0, 0)),
            pl.BlockSpec((8, d), lambda bi, i: (0, 0)),
            pl.BlockSpec((1, d), lambda bi, i: (0, 0)),
            tbl, tbl, tbl,
            pl.BlockSpec((1, d, W_ALL_COLS), lambda bi, i: (layer, 0, 0), pipeline_mode=pl.Buffered(1)),
        ],
        out_specs=[
            pl.BlockSpec((1, tm, A16_W), lambda bi, i: (bi, i, 0)),
            pl.BlockSpec((1, tm, A32_W), lambda bi, i: (bi, i, 0)),
        ],
        out_shape=[
            jax.ShapeDtypeStruct((b, t, A16_W), BF16),
            jax.ShapeDtypeStruct((b, t, A32_W), F32),
        ],
        compiler_params=_cparams(("parallel", "parallel")),
        name="norm_inproj",
    )(x, modl, modc, ng, cos, sa, sb, w_all)


def _attn_kernel(lam_ref, g_ref, q_ref, k_ref, v_ref, o_ref, *, ctx_len, segments, lam_init):
    qi = pl.program_id(2)
    tq = Q_TILE
    hb = DIFF_V_DIM
    lv = lam_ref[...]
    lam = (jnp.exp(jnp.sum(lv[0:1] * lv[1:2], axis=1, keepdims=True))
           - jnp.exp(jnp.sum(lv[2:3] * lv[3:4], axis=1, keepdims=True)) + lam_init)

    def run(segs):
        qss = []
        for g in range(ATTN_HEADS):
            q = q_ref[0, :, g * hb:(g + 1) * hb]
            lane = lax.broadcasted_iota(jnp.int32, q.shape, 1)
            zero = jnp.zeros_like(q)
            qss.append(jnp.concatenate([jnp.where(lane < DIFF_HEAD_DIM, q, zero),
                                        jnp.where(lane >= DIFF_HEAD_DIM, q, zero)], axis=0))
        carries = [None] * ATTN_HEADS
        for r0, rows in segs:
            for g in range(ATTN_HEADS):
                k_t = k_ref[0, r0:r0 + rows, g * hb:(g + 1) * hb]
                v_t = v_ref[0, r0:r0 + rows, g * hb:(g + 1) * hb]
                s = lax.dot_general(qss[g], k_t, (((1,), (1,)), ((), ())), preferred_element_type=F32)
                smax = jnp.max(s, axis=-1, keepdims=True)
                if carries[g] is None:
                    p = jnp.exp2(s - smax)
                    carries[g] = (smax, jnp.sum(p, axis=-1, keepdims=True),
                                  jnp.dot(p.astype(BF16), v_t, preferred_element_type=F32))
                else:
                    m, l, acc = carries[g]
                    m_new = jnp.maximum(m, smax)
                    alpha = jnp.exp2(m - m_new)
                    p = jnp.exp2(s - m_new)
                    carries[g] = (m_new, alpha * l + jnp.sum(p, axis=-1, keepdims=True),
                                  alpha * acc + jnp.dot(p.astype(BF16), v_t, preferred_element_type=F32))
        for g in range(ATTN_HEADS):
            _, l, acc = carries[g]
            o = acc[:tq] / l[:tq] - lam * (acc[tq:] / l[tq:])
            o = o * lax.rsqrt(jnp.mean(o * o, axis=-1, keepdims=True) + RMS_EPS) * g_ref[...]
            o_ref[0, :, g * hb:(g + 1) * hb] = (o * (1.0 - lam_init)).astype(BF16)

    @pl.when(qi * tq >= ctx_len)
    def _():
        run(segments)

    @pl.when(qi * tq < ctx_len)
    def _():
        run([(0, ctx_len)])


def _attn(a16, lam_p, subln_g, *, ctx_len, seq_len, lam_init):
    b, t, _ = a16.shape
    assert ctx_len % Q_TILE == 0 and t % 256 == 0
    first = (t // 256 + KV_SEGMENTS - 1) // KV_SEGMENTS * 256
    bounds = [min(first * n, t) for n in range(KV_SEGMENTS + 1)]
    segments = [(lo, hi - lo) for lo, hi in zip(bounds[:-1], bounds[1:]) if hi > lo]
    hb = DIFF_V_DIM * ATTN_HEADS
    return pl.pallas_call(
        functools.partial(_attn_kernel, ctx_len=ctx_len, segments=segments, lam_init=lam_init),
        grid=(b, DIFF_HEADS // ATTN_HEADS, t // Q_TILE),
        in_specs=[
            pl.BlockSpec((4, DIFF_HEAD_DIM), lambda bi, h, qi: (0, 0)),
            pl.BlockSpec((1, DIFF_V_DIM), lambda bi, h, qi: (0, 0)),
            pl.BlockSpec((1, Q_TILE, hb), lambda bi, h, qi: (bi, qi, A16_Q // hb + h)),
            pl.BlockSpec((1, t, hb), lambda bi, h, qi: (bi, 0, A16_K // hb + h)),
            pl.BlockSpec((1, t, hb), lambda bi, h, qi: (bi, 0, A16_V // hb + h)),
        ],
        out_specs=pl.BlockSpec((1, Q_TILE, hb), lambda bi, h, qi: (bi, qi, h)),
        out_shape=jax.ShapeDtypeStruct((b, t, DIFF_HEADS * DIFF_V_DIM), BF16),
        compiler_params=_cparams(("parallel", "parallel", "parallel")),
        name="diff_attn",
    )(lam_p, subln_g, a16, a16, a16)


def _split3(x):
    hi = x.astype(BF16)
    r1 = x - hi.astype(F32)
    mid = r1.astype(BF16)
    lo = (r1 - mid.astype(F32)).astype(BF16)
    return jnp.concatenate([hi, mid, lo], axis=1)


def _sum3(y, w):
    return y[:, 0:w] + y[:, w:2 * w] + y[:, 2 * w:3 * w]


def _gla_kernel(q_ref, k_ref, v_ref, a_ref, r_ref, wa_ref, ba_ref, ng_ref, o_ref,
                qd_s, ki_s, ke_s, dec_s, oacc_s, st_s, *, t, ctx_len):
    ck = GLA_CHUNK
    nc = t // ck
    ncc = ctx_len // ck
    pr = 256
    w = 2 * GLA_DK

    ri = lax.broadcasted_iota(jnp.int32, (pr, pr), 0)
    ci = lax.broadcasted_iota(jnp.int32, (pr, pr), 1)
    same = (ri // ck) == (ci // ck)
    m_tot = jnp.where(same, 1.0, 0.0).astype(BF16)
    m_cum = (jnp.where(same & (ci <= ri), 1.0, 0.0).astype(BF16),
             jnp.where(same & (ci >= ri), 1.0, 0.0).astype(BF16))
    for r0 in range(0, t, pr):
        a = a_ref[0, r0:r0 + pr, :].astype(BF16)
        logit = jnp.dot(a, wa_ref[0], preferred_element_type=F32) + ba_ref[0]
        g = (jnp.minimum(logit, 0.0) - jnp.log1p(jnp.exp(-jnp.abs(logit)))) * (1.0 / GLA_TAU)
        q = q_ref[0, r0:r0 + pr, :]
        k = k_ref[0, r0:r0 + pr, :]
        for d in range(2):
            g3 = _split3(g[:, d * w:(d + 1) * w])
            g_cum = _sum3(jnp.dot(m_cum[d], g3, preferred_element_type=F32), w)
            g_tot = _sum3(jnp.dot(m_tot, g3, preferred_element_type=F32), w)
            qd_s[d, r0:r0 + pr, :] = (q * jnp.exp(g_cum)).astype(BF16)
            ki_s[d, r0:r0 + pr, :] = (k * jnp.exp(-g_cum)).astype(BF16)
            ke_s[d, r0:r0 + pr, :] = (k * jnp.exp(g_tot - g_cum)).astype(BF16)
            dec_s[d, r0:r0 + pr, :] = jnp.exp(g_tot)
        oacc_s[r0:r0 + pr, :] = jnp.zeros((pr, 2 * GLA_DV), F32)
    st_s[...] = jnp.zeros_like(st_s)

    ri = lax.broadcasted_iota(jnp.int32, (2 * ck, ck), 0)
    ci = lax.broadcasted_iota(jnp.int32, (2 * ck, ck), 1)
    rr = jnp.where(ri >= ck, ri - ck, ri)
    tri = (ci <= rr, ci >= rr)
    lane = lax.broadcasted_iota(jnp.int32, (ck, w), 1)
    br = lax.broadcasted_iota(jnp.int32, (2 * GLA_DV, w), 0)
    bc = lax.broadcasted_iota(jnp.int32, (2 * GLA_DV, w), 1)
    blk = (br < GLA_DV) == (bc < GLA_DK)
    nt = (((1,), (1,)), ((), ()))
    tn = (((0,), (0,)), ((), ()))

    def chunk(d, r0):
        qd = qd_s[d, pl.ds(r0, ck), :]
        ki = ki_s[d, pl.ds(r0, ck), :]
        ke = ke_s[d, pl.ds(r0, ck), :]
        v = v_ref[0, pl.ds(r0, ck), :]
        dec = dec_s[d, pl.ds(r0, 8), :][0:1, :]
        zq = jnp.zeros_like(qd)
        qs = jnp.concatenate([jnp.where(lane < GLA_DK, qd, zq), jnp.where(lane >= GLA_DK, qd, zq)], axis=0)
        att = lax.dot_general(qs, ki, nt, preferred_element_type=F32)
        att = jnp.where(tri[d], att, 0.0).astype(BF16)
        oi = jnp.dot(att, v, preferred_element_type=F32)
        o_intra = jnp.concatenate([oi[:ck, :GLA_DV], oi[ck:, GLA_DV:]], axis=1)
        st = st_s[d]
        o_inter = lax.dot_general(qd, st.astype(BF16), nt, preferred_element_type=F32)
        s_loc = lax.dot_general(v, ke, tn, preferred_element_type=F32)
        st_s[d] = st * dec + jnp.where(blk, s_loc, 0.0)
        oacc_s[pl.ds(r0, ck), :] = oacc_s[pl.ds(r0, ck), :] + o_intra + o_inter

    def body(n, _):
        chunk(0, pl.multiple_of(n * ck, ck))
        nb = jnp.where(n < ncc, ncc - 1 - n, nc - 1 - (n - ncc))
        chunk(1, pl.multiple_of(nb * ck, ck))
        return 0

    lax.fori_loop(0, nc, body, 0, unroll=GLA_UNROLL)

    ng = ng_ref[...]
    rt = _row_tile(t, 544)
    for r0 in range(0, t, rt):
        o = oacc_s[r0:r0 + rt, :]
        r = r_ref[0, r0:r0 + rt, :]
        outs = []
        for hh in range(2):
            oh = o[:, hh * GLA_DV:(hh + 1) * GLA_DV]
            outs.append(oh * lax.rsqrt(jnp.mean(oh * oh, axis=-1, keepdims=True) + RMS_EPS) * ng)
        o_ref[0, r0:r0 + rt, :] = (jnp.concatenate(outs, axis=1) * (r * _sigmoid(r))).astype(BF16)


def _gla(a16, a32, wa, ba, ng, *, ctx_len):
    b, t, _ = a16.shape
    assert t % 256 == 0 and ctx_len % GLA_CHUNK == 0 and (t // GLA_CHUNK) % GLA_UNROLL == 0
    return pl.pallas_call(
        functools.partial(_gla_kernel, t=t, ctx_len=ctx_len),
        grid=(b, 2),
        in_specs=[
            pl.BlockSpec((1, t, 128), lambda bi, p: (bi, 0, A32_GLQ // 128 + p)),
            pl.BlockSpec((1, t, 128), lambda bi, p: (bi, 0, A32_GLK // 128 + p)),
            pl.BlockSpec((1, t, 256), lambda bi, p: (bi, 0, A16_GV // 256 + p)),
            pl.BlockSpec((1, t, 128), lambda bi, p: (bi, 0, A32_GLA // 128)),
            pl.BlockSpec((1, t, 256), lambda bi, p: (bi, 0, A32_GLR // 256 + p)),
            pl.BlockSpec((1, 128, 256), lambda bi, p: (p, 0, 0)),
            pl.BlockSpec((1, 1, 256), lambda bi, p: (p, 0, 0)),
            pl.BlockSpec((1, GLA_DV), lambda bi, p: (0, 0)),
        ],
        out_specs=pl.BlockSpec((1, t, 256), lambda bi, p: (bi, 0, p)),
        out_shape=jax.ShapeDtypeStruct((b, t, GLA_HEADS * GLA_DV), BF16),
        scratch_shapes=[
            pltpu.VMEM((2, t, 128), BF16), pltpu.VMEM((2, t, 128), BF16), pltpu.VMEM((2, t, 128), BF16),
            pltpu.VMEM((2, t, 128), F32),
            pltpu.VMEM((t, 256), F32),
            pltpu.VMEM((2, 256, 128), F32),
        ],
        compiler_params=_cparams(("parallel", "parallel")),
        name="gla_bidir",
    )(a32, a32, a16, a32, a32, wa, ba, ng)


def _merge_kernel(x_ref, od_ref, og_ref, gates_ref, cvb_ref, cvc_ref, cvx_ref,
                  cvc_p_ref, cvx_p_ref, cvc_n_ref, cvx_n_ref, cw_ref,
                  wdo_ref, wco_ref, wgo_ref, wo_ref, modl_ref, modc_ref, n2g_ref,
                  rw_ref, rb_ref, xo_ref, h2_ref, rt_ref, *, tm, ctx_len, t):
    i = pl.program_id(1)
    pos = i * tm + lax.broadcasted_iota(jnp.int32, (tm, 1), 0)
    is_ctx = pos < ctx_len
    row = lax.broadcasted_iota(jnp.int32, (tm, 1), 0)

    z = cvc_ref[0] * cvx_ref[0]
    z_before = cvc_p_ref[0, 7:8, :] * cvx_p_ref[0, 7:8, :]
    z_after = cvc_n_ref[0, 0:1, :] * cvx_n_ref[0, 0:1, :]
    z_prev = jnp.where(row == 0, z_before, pltpu.roll(z, 1, 0))
    z_prev = jnp.where((pos == 0) | (pos == ctx_len), 0.0, z_prev)
    z_next = jnp.where(row == tm - 1, z_after, pltpu.roll(z, tm - 1, 0))
    z_next = jnp.where((pos == ctx_len - 1) | (pos == t - 1), 0.0, z_next)
    conv = z_prev * cw_ref[0:1, :] + z * cw_ref[1:2, :] + z_next * cw_ref[2:3, :]
    zc = (cvb_ref[0] * conv).astype(BF16)

    y_diff = jnp.dot(od_ref[0], wdo_ref[0], preferred_element_type=F32)
    y_conv = jnp.dot(zc, wco_ref[0], preferred_element_type=F32)
    y_gla = jnp.dot(og_ref[0], wgo_ref[0], preferred_element_type=F32)
    d = D_MODEL
    mix = (_sigmoid(gates_ref[0, :, 0:d]) * y_diff
           + _sigmoid(gates_ref[0, :, d:2 * d]) * y_conv
           + _sigmoid(gates_ref[0, :, 2 * d:3 * d]) * y_gla)
    m = jnp.dot(mix.astype(BF16), wo_ref[0], preferred_element_type=F32)

    g1 = jnp.where(is_ctx, modc_ref[2:3, :], modl_ref[0, 2:3, :])
    x_new = x_ref[0] + g1 * m
    xo_ref[0] = x_new
    sh2 = jnp.where(is_ctx, modc_ref[3:4, :], modl_ref[0, 3:4, :])
    sc2 = jnp.where(is_ctx, modc_ref[4:5, :], modl_ref[0, 4:5, :])
    h2 = _norm_mod(x_new, n2g_ref[...], sh2, sc2)
    for cc in range(d // 128):
        h2_ref[0, pl.ds(cc, tm, stride=8), :] = h2[:, cc * 128:(cc + 1) * 128]

    logits = jnp.dot(h2.astype(BF16), rw_ref[...], preferred_element_type=F32) + rb_ref[...]
    lane = lax.broadcasted_iota(jnp.int32, logits.shape, 1)
    vals = logits
    tops = []
    idx_out = jnp.zeros(logits.shape, jnp.int32)
    for kk in range(TOP_K):
        mx = jnp.max(vals, axis=-1, keepdims=True)
        ix = jnp.min(jnp.where(vals == mx, lane, 128), axis=-1, keepdims=True)
        tops.append(mx)
        idx_out = jnp.where(lane == kk, ix, idx_out)
        vals = jnp.where(lane == ix, -jnp.inf, vals)
    es = [jnp.exp(v - tops[0]) for v in tops]
    den = es[0] + es[1] + es[2] + es[3]
    packed = idx_out.astype(F32)
    for kk in range(TOP_K):
        gk = es[kk] / den
        hi = gk.astype(BF16).astype(F32)
        mid = (gk - hi).astype(BF16).astype(F32)
        lo = gk - hi - mid
        for piece, val in enumerate((hi, mid, lo)):
            packed = jnp.where(lane == TOP_K * (piece + 1) + kk, val, packed)
    ri = lax.broadcasted_iota(jnp.int32, (tm, tm), 0)
    ci = lax.broadcasted_iota(jnp.int32, (tm, tm), 1)
    eye = jnp.where(ri == ci, 1.0, 0.0).astype(BF16)
    rows = lax.dot_general(packed.astype(BF16), eye, (((0,), (0,)), ((), ())), preferred_element_type=F32)
    rt_ref[0, 0] = rows[0:4 * TOP_K, :]


def _merge(x, od, og, a32, cw, wdo, wco, wgo, wo, modl, modc, n2g, rw, rb, *, layer, ctx_len):
    b, t, d = x.shape
    tm = _row_tile(t, 544)
    nt8 = t // 8
    row = lambda bi, i: (bi, i, 0)
    const2 = lambda bi, i: (0, 0)
    wl = lambda bi, i: (layer, 0, 0)
    return pl.pallas_call(
        functools.partial(_merge_kernel, tm=tm, ctx_len=ctx_len, t=t),
        grid=(b, t // tm),
        in_specs=[
            pl.BlockSpec((1, tm, d), row),
            pl.BlockSpec((1, tm, d), row),
            pl.BlockSpec((1, tm, GLA_HEADS * GLA_DV), row),
            pl.BlockSpec((1, tm, 3 * d), lambda bi, i: (bi, i, 0)),
            pl.BlockSpec((1, tm, CONV_W), lambda bi, i: (bi, i, A32_CVB // CONV_W)),
            pl.BlockSpec((1, tm, CONV_W), lambda bi, i: (bi, i, A32_CVC // CONV_W)),
            pl.BlockSpec((1, tm, CONV_W), lambda bi, i: (bi, i, A32_CVX // CONV_W)),
            pl.BlockSpec((1, 8, CONV_W), lambda bi, i: (bi, jnp.maximum(i * (tm // 8) - 1, 0),
                                                   A32_CVC // CONV_W)),
            pl.BlockSpec((1, 8, CONV_W), lambda bi, i: (bi, jnp.maximum(i * (tm // 8) - 1, 0),
                                                   A32_CVX // CONV_W)),
            pl.BlockSpec((1, 8, CONV_W), lambda bi, i: (bi, jnp.minimum((i + 1) * (tm // 8), nt8 - 1),
                                                   A32_CVC // CONV_W)),
            pl.BlockSpec((1, 8, CONV_W), lambda bi, i: (bi, jnp.minimum((i + 1) * (tm // 8), nt8 - 1),
                                                   A32_CVX // CONV_W)),
            pl.BlockSpec((8, CONV_W), const2),
            pl.BlockSpec((1, d, d), wl),
            pl.BlockSpec((1, CONV_W, d), wl),
            pl.BlockSpec((1, GLA_HEADS * GLA_DV, d), wl),
            pl.BlockSpec((1, d, d), wl),
            pl.BlockSpec((1, 8, d), lambda bi, i: (bi, 0, 0)),
            pl.BlockSpec((8, d), const2),
            pl.BlockSpec((1, d), const2),
            pl.BlockSpec((d, 128), const2),
            pl.BlockSpec((1, 128), const2),
        ],
        out_specs=[
            pl.BlockSpec((1, tm, d), row),
            pl.BlockSpec((1, tm * (d // 128), 128), row),
            pl.BlockSpec((1, 1, 4 * TOP_K, tm), lambda bi, i: (bi, i, 0, 0)),
        ],
        out_shape=[
            jax.ShapeDtypeStruct((b, t, d), F32),
            jax.ShapeDtypeStruct((b, t * (d // 128), 128), F32),
            jax.ShapeDtypeStruct((b, t // tm, 4 * TOP_K, tm), F32),
        ],
        compiler_params=_cparams(("parallel", "parallel")),
        name="merge_router",
    )(x, od, og, a32, a32, a32, a32, a32, a32, a32, a32, cw, wdo, wco, wgo, wo,
      modl, modc, n2g, rw, rb)


def _moe_kernel(be_ref, nv_ref, g_ref, w1g_ref, w1l_ref, b1g_ref, b1l_ref, w2_ref, b2_ref,
                src_hbm, dst_hbm, h2_hbm, out_hbm, src_s, dst_s, xbuf, ybuf, sem_p, sem_g, sem_s,
                *, nb, n_slots):
    i = pl.program_id(0)
    nv = nv_ref[0]
    bm = MOE_BLOCK
    cur = lax.rem(i, 2)
    nxt = 1 - cur

    def idx_copies(blk_src, blk_dst, slot):
        return (pltpu.make_async_copy(src_hbm.at[blk_src, 0], src_s.at[pl.ds(slot * bm, bm)], sem_p),
                pltpu.make_async_copy(dst_hbm.at[blk_dst, 0], dst_s.at[pl.ds(slot * bm, bm)], sem_p))

    def start_gather(slot, buf):
        for j in range(bm):
            r0 = pl.multiple_of(src_s[slot * bm + j], TOK_ROWS)
            pltpu.make_async_copy(h2_hbm.at[pl.ds(r0, TOK_ROWS)],
                                  xbuf.at[buf, pl.ds(j * TOK_ROWS, TOK_ROWS)],
                                  sem_g.at[buf]).start(priority=j % 2)

    def wait_gather(buf):
        pltpu.make_async_copy(h2_hbm.at[pl.ds(0, bm * TOK_ROWS)], xbuf.at[buf], sem_g.at[buf]).wait()

    def wait_scatter():
        pltpu.make_async_copy(ybuf.at[0], out_hbm.at[pl.ds(0, bm * TOK_ROWS)], sem_s).wait()

    @pl.when(i <= nv)
    def _():
        @pl.when(i == 0)
        def _():
            ybuf[...] = jnp.zeros_like(ybuf)
            for blk_src, blk_dst, slot in ((0, 0, 0), (1, 1, 1), (0, nb, 3)):
                for cp in idx_copies(blk_src, blk_dst, slot):
                    cp.start()
                    cp.wait()
            start_gather(0, 0)

        @pl.when(i >= 1)
        def _():
            for cp in idx_copies(0, 0, 0):
                cp.wait()
            wait_scatter()

        wait_gather(cur)

        @pl.when(i < nv)
        def _():
            blk = jnp.minimum(i + 2, nv - 1)
            for cp in idx_copies(blk, blk, lax.rem(i + 2, 4)):
                cp.start()

        start_gather(lax.rem(i + 1, 4), nxt)
        pslot = lax.rem(i + 3, 4)
        for j in range(bm):
            r0 = pl.multiple_of(dst_s[pslot * bm + j], TOK_ROWS)
            pltpu.make_async_copy(ybuf.at[nxt, pl.ds(j * TOK_ROWS, TOK_ROWS)],
                                  out_hbm.at[pl.ds(r0, TOK_ROWS)], sem_s).start(priority=j % 2)
        x = jnp.concatenate([xbuf[cur, pl.ds(cc, bm, stride=TOK_ROWS), :] for cc in range(TOK_ROWS)],
                            axis=1).astype(BF16)
        hg = jnp.dot(x, w1g_ref[0, 0, 0], preferred_element_type=F32) + b1g_ref[0]
        hl = jnp.dot(x, w1l_ref[0, 0, 0], preferred_element_type=F32) + b1l_ref[0]
        hg = jnp.minimum(hg, SWIGLU_LIMIT)
        hl = jnp.clip(hl, -SWIGLU_LIMIT, SWIGLU_LIMIT)
        act = hg * _sigmoid(SWIGLU_ALPHA * hg) * (hl + 1.0)
        y = jnp.dot(act.astype(BF16), w2_ref[0, 0], preferred_element_type=F32) + b2_ref[0]
        gi = lax.broadcasted_iota(jnp.int32, (bm, bm), 0)
        gj = lax.broadcasted_iota(jnp.int32, (bm, bm), 1)
        y = y * jnp.sum(jnp.where(gi == gj, g_ref[0], 0.0), axis=1, keepdims=True)
        for cc in range(TOK_ROWS):
            ybuf[cur, pl.ds(cc, bm, stride=TOK_ROWS), :] = y[:, cc * 128:(cc + 1) * 128]

        @pl.when(i == nv)
        def _():
            wait_scatter()
            wait_gather(nxt)
            ybuf[...] = jnp.zeros_like(ybuf)
            fills = [pltpu.make_async_copy(
                ybuf.at[sl], out_hbm.at[pl.ds((n_slots + sl * bm) * TOK_ROWS, bm * TOK_ROWS)], sem_s)
                for sl in range(2)]
            for cp in fills:
                cp.start()
            for cp in fills:
                cp.wait()


def _moe(block_exp, n_valid, row_src, row_dst, row_gate, h2_flat, w1, b1g, b1l, w2, b2, *, layer, n_slots):
    d = D_MODEL
    nb = row_src.shape[0]
    bm = MOE_BLOCK
    de = w1.shape[-1]
    blk = lambda i, nv: jnp.minimum(i, nv[0] - 1)
    wmap = lambda i, be, nv: (be[blk(i, nv)], 0, 0)
    return pl.pallas_call(
        functools.partial(_moe_kernel, nb=nb, n_slots=n_slots),
        grid_spec=pltpu.PrefetchScalarGridSpec(
            num_scalar_prefetch=2,
            grid=(nb + 1,),
            in_specs=[
                pl.BlockSpec((1, 1, bm), lambda i, be, nv: (blk(i, nv), 0, 0)),
                pl.BlockSpec((1, 1, 1, d, de), lambda i, be, nv: (layer, be[blk(i, nv)], 0, 0, 0)),
                pl.BlockSpec((1, 1, 1, d, de), lambda i, be, nv: (layer, be[blk(i, nv)], 1, 0, 0)),
                pl.BlockSpec((1, 1, de), wmap),
                pl.BlockSpec((1, 1, de), wmap),
                pl.BlockSpec((1, 1, de, d), lambda i, be, nv: (layer, be[blk(i, nv)], 0, 0)),
                pl.BlockSpec((1, 1, d), wmap),
                pl.BlockSpec(memory_space=pl.ANY),
                pl.BlockSpec(memory_space=pl.ANY),
                pl.BlockSpec(memory_space=pl.ANY),
            ],
            out_specs=pl.BlockSpec(memory_space=pl.ANY),
            scratch_shapes=[
                pltpu.SMEM((4 * bm,), jnp.int32),
                pltpu.SMEM((4 * bm,), jnp.int32),
                pltpu.VMEM((2, bm * TOK_ROWS, 128), F32),
                pltpu.VMEM((2, bm * TOK_ROWS, 128), F32),
                pltpu.SemaphoreType.DMA,
                pltpu.SemaphoreType.DMA((2,)),
                pltpu.SemaphoreType.DMA,
            ],
        ),
        out_shape=jax.ShapeDtypeStruct(((n_slots + 2 * bm) * TOK_ROWS, 128), F32),
        compiler_params=_cparams(("arbitrary",)),
        name="moe_experts",
    )(block_exp, n_valid, row_gate, w1, w1, b1g, b1l, w2, b2, row_src, row_dst, h2_flat)


def _combine_kernel(x_ref, y4_ref, modl_ref, modc_ref, fg_ref, o_ref, *, tm, ctx_len, final, skip):
    i = pl.program_id(1) + skip
    d = D_MODEL
    ld = lambda k, cc: y4_ref[pl.ds(k * TOK_ROWS + cc, tm, stride=TOP_K * TOK_ROWS), :]
    f = jnp.concatenate([(ld(0, cc) + ld(1, cc)) + (ld(2, cc) + ld(3, cc)) for cc in range(TOK_ROWS)],
                        axis=1)
    pos = i * tm + lax.broadcasted_iota(jnp.int32, (tm, 1), 0)
    g2 = jnp.where(pos < ctx_len, modc_ref[5:6, :], modl_ref[0, 5:6, :])
    x_new = x_ref[0] + g2 * f
    if final:
        x_new = x_new * lax.rsqrt(jnp.mean(x_new * x_new, axis=-1, keepdims=True) + RMS_EPS) * fg_ref[...]
    o_ref[0] = x_new


def _combine(x, y4, modl, modc, fg, *, ctx_len, final):
    b, t, d = x.shape
    if final:
        tm = Q_TILE
        assert ctx_len % tm == 0
        skip = ctx_len // tm
    else:
        tm = _row_tile(t, 272)
        skip = 0
    npb = t // tm
    return pl.pallas_call(
        functools.partial(_combine_kernel, tm=tm, ctx_len=ctx_len, final=final, skip=skip),
        grid=(b, npb - skip),
        in_specs=[
            pl.BlockSpec((1, tm, d), lambda bi, i: (bi, i + skip, 0)),
            pl.BlockSpec((tm * TOP_K * TOK_ROWS, 128), lambda bi, i: (bi * (npb - skip) + i, 0)),
            pl.BlockSpec((1, 8, d), lambda bi, i: (bi, 0, 0)),
            pl.BlockSpec((8, d), lambda bi, i: (0, 0)),
            pl.BlockSpec((1, d), lambda bi, i: (0, 0)),
        ],
        out_specs=pl.BlockSpec((1, tm, d), lambda bi, i: (bi, i, 0)),
        out_shape=jax.ShapeDtypeStruct((b, t - skip * tm, d), F32),
        compiler_params=_cparams(("parallel", "parallel")),
        name="moe_combine",
    )(x, y4, modl, modc, fg)


def _routing(e_flat, gate_flat, *, tm, seq, ctx_len, latent_only):
    npair = e_flat.shape[0]
    t = seq + ctx_len
    bm = MOE_BLOCK
    nb = npair // bm + N_EXPERTS
    nr = nb * bm
    npad = nr - npair
    experts = jnp.arange(N_EXPERTS, dtype=jnp.int32)
    ids = jnp.arange(npair, dtype=jnp.int32)
    tok_of = lambda i: (i // (TOP_K * tm)) * tm + i % tm
    if latent_only:
        e_flat = jnp.where(tok_of(ids) % t >= ctx_len, e_flat, N_EXPERTS)
    counts = jnp.sum((e_flat[:, None] == experts[None, :]).astype(jnp.int32), axis=0)
    padded = (counts + bm - 1) // bm * bm
    pad_end = jnp.cumsum(padded)
    fill_end = jnp.cumsum(padded - counts)
    dummy = jnp.arange(npad, dtype=jnp.int32)
    e_dummy = jnp.sum((dummy[:, None] >= fill_end[None, :]).astype(jnp.int32), axis=1)
    kb = nr
    keys = jnp.concatenate([e_flat * kb + ids, e_dummy * kb + npair + dummy])
    gates = jnp.concatenate([gate_flat, jnp.zeros((npad,), F32)])
    skey, sgate = lax.sort((keys, gates), num_keys=1)
    e_row = skey // kb
    ident = skey - e_row * kb
    valid = (ident < npair) & (e_row < N_EXPERTS)
    pair = jnp.where(valid, ident, 0)
    tok = tok_of(pair)
    kk = (pair // tm) % TOP_K
    row_gate = jnp.where(valid, sgate, 0.0).reshape(nb, 1, bm)
    row_src = (tok * TOK_ROWS).reshape(nb, 1, bm)
    if latent_only:
        slot = (tok // t) * seq + tok % t - ctx_len
        n_slots = (npair // TOP_K // t) * seq * TOP_K
    else:
        slot = tok
        n_slots = npair
    j = jnp.arange(nr, dtype=jnp.int32)
    dump = n_slots + ((j // bm) % 2) * bm + j % bm
    row_dst = jnp.where(valid, slot * TOP_K + kk, dump).reshape(nb, bm)
    row_dst = jnp.concatenate([row_dst, (n_slots + bm + jnp.arange(bm, dtype=jnp.int32))[None, :]],
                              axis=0)[:, None, :] * TOK_ROWS
    blocks = jnp.arange(nb, dtype=jnp.int32) * bm
    block_exp = jnp.minimum(jnp.sum((blocks[:, None] >= pad_end[None, :]).astype(jnp.int32), axis=1),
                            N_EXPERTS - 1)
    n_valid = (pad_end[-1] // bm).astype(jnp.int32).reshape(1)
    return row_src, row_dst, row_gate, block_exp, n_valid, n_slots


def _rope_tables(ctx_len, seq_len):
    inv_freq = ROPE_BASE ** (-jnp.arange(0, ROT_AXIS_DIM, 2, dtype=F32) / ROT_AXIS_DIM)
    p = jnp.arange(seq_len, dtype=jnp.int32)
    row = (p // GRID_W).astype(F32)
    col = (p % GRID_W).astype(F32)
    lane = np.arange(128)
    dd = lane % DIFF_HEAD_DIM
    fidx = jnp.asarray((dd % ROT_AXIS_DIM) % (ROT_AXIS_DIM // 2))
    use_col = jnp.asarray(dd >= ROT_AXIS_DIM)
    second_half = jnp.asarray((dd % ROT_AXIS_DIM) >= ROT_AXIS_DIM // 2)
    posm = jnp.where(use_col[None, :], col[:, None], row[:, None])
    ang = posm * inv_freq[fidx][None, :]
    cos = jnp.cos(ang)
    sin = jnp.sin(ang)
    sa = jnp.where(second_half[None, :], sin, 0.0)
    sb = jnp.where(second_half[None, :], 0.0, -sin)
    pad = lambda a, v: jnp.concatenate([jnp.full((ctx_len, 128), v, F32), a], axis=0)
    return pad(cos, 1.0), pad(sa, 0.0), pad(sb, 0.0)


def kernel(x, c, ctx, c_ctx, ada_w, ada_b, norm1_g, norm2_g, w_in, diff_lambda, diff_subln_g,
           diff_w_out, conv_w, conv_w_out, gla_w_a2, gla_b_a, gla_norm_g, gla_w_out, w_o,
           router_w, router_b, moe_w1, moe_b1, moe_w2, moe_b2, final_norm_g):
    b, s, d = x.shape
    cl = ctx.shape[1]
    t = cl + s
    nl = w_in.shape[0]
    assert d == D_MODEL and b <= 15

    xs = jnp.concatenate([ctx, x], axis=1)
    cc = jnp.concatenate([c, c_ctx[None, :], jnp.zeros((15 - b, d), F32)], axis=0)
    mod = _ada(cc, ada_w, ada_b).reshape(nl, 16, N_ADA, d)
    cos, sa, sb = _rope_tables(cl, s)

    col_scale = np.ones((w_in.shape[-1],), np.float32)
    col_scale[0:1024] = DIFF_HEAD_DIM ** -0.5
    col_scale[4608:4864] = GLA_DK ** -0.5
    w_s = (w_in * jnp.asarray(col_scale)).astype(BF16)
    w_all = jnp.concatenate(
        [w_s[..., :W_GATES], w_s[..., W_GATES + 2 * GLA_RANK:], w_s[..., W_GATES:W_GATES + 2 * GLA_RANK],
         jnp.zeros((nl, d, W_ALL_COLS - w_in.shape[-1]), BF16)], axis=-1)

    wa = jnp.zeros((nl, 2, 128, 256), F32)
    for p in range(2):
        wa = wa.at[:, p, 0:GLA_RANK, 0:128].set(gla_w_a2[:, 0, :, p * 128:(p + 1) * 128])
        wa = wa.at[:, p, GLA_RANK:2 * GLA_RANK, 128:256].set(gla_w_a2[:, 1, :, p * 128:(p + 1) * 128])
    wa = wa.astype(BF16)
    ba = jnp.stack([jnp.concatenate([gla_b_a[:, 0, p * 128:(p + 1) * 128],
                                     gla_b_a[:, 1, p * 128:(p + 1) * 128]], axis=-1)
                    for p in range(2)], axis=1)[:, :, None, :]

    cw = jnp.concatenate([conv_w, jnp.zeros((nl, 5, CONV_W), F32)], axis=1)
    rw = jnp.concatenate([router_w, jnp.zeros((nl, d, 128 - N_EXPERTS), F32)], axis=-1).astype(BF16)
    rb = jnp.concatenate([router_b, jnp.full((nl, 128 - N_EXPERTS), -1e30, F32)], axis=-1)[:, None, :]
    w1 = jnp.moveaxis(moe_w1.reshape(nl, N_EXPERTS, d, D_EXPERT, 2), -1, 2).astype(BF16)
    b1 = moe_b1.reshape(nl, N_EXPERTS, 1, D_EXPERT, 2)
    b1g = b1[..., 0]
    b1l = b1[..., 1]
    w2 = moe_w2.astype(BF16)
    b2 = moe_b2[:, :, None, :]
    wdo = diff_w_out.astype(BF16)
    wco = conv_w_out.astype(BF16)
    wgo = gla_w_out.astype(BF16)
    wo = w_o.astype(BF16)

    for layer in range(nl):
        last = layer == nl - 1
        lam_init = 0.8 - 0.6 * math.exp(-0.3 * layer)
        ml = mod[layer]
        modl = jnp.concatenate([ml[:b], jnp.zeros((b, 2, d), F32)], axis=1)
        modc = jnp.concatenate([ml[b], jnp.zeros((2, d), F32)], axis=0)

        a16, a32 = _k1(xs, modl, modc, norm1_g[layer][None, :], cos, sa, sb, w_all, layer=layer, ctx_len=cl)
        od = _attn(a16, diff_lambda[layer], diff_subln_g[layer][None, :],
                   ctx_len=cl, seq_len=s, lam_init=lam_init)
        og = _gla(a16, a32, wa[layer], ba[layer], gla_norm_g[layer][None, :], ctx_len=cl)
        xs, h2, rt = _merge(xs, od, og, a32, cw[layer], wdo, wco, wgo, wo, modl, modc,
                            norm2_g[layer][None, :], rw[layer], rb[layer], layer=layer, ctx_len=cl)

        r = b * t
        tm_r = rt.shape[-1]
        rt = rt.reshape(-1, 4, TOP_K, tm_r)
        row_src, row_dst, row_gate, block_exp, n_valid, n_slots = _routing(
            rt[:, 0].astype(jnp.int32).reshape(-1), (rt[:, 1] + rt[:, 2] + rt[:, 3]).reshape(-1),
            tm=tm_r, seq=s, ctx_len=cl, latent_only=last)
        y4 = _moe(block_exp, n_valid, row_src, row_dst, row_gate, h2.reshape(r * TOK_ROWS, 128), w1,
                  b1g[layer], b1l[layer], w2, b2[layer], layer=layer, n_slots=n_slots)
        xs = _combine(xs, y4, modl, modc, final_norm_g[None, :], ctx_len=cl, final=last)

    return xs
```

```python
import functools
import math

import numpy as np
import jax
import jax.numpy as jnp
from jax import lax
from jax.experimental import pallas as pl
from jax.experimental.pallas import tpu as pltpu

F32 = jnp.float32
BF16 = jnp.bfloat16

D_MODEL = 1024
GRID_W = 64
RMS_EPS = 1e-6
N_ADA = 6
DIFF_HEADS = 8
DIFF_HEAD_DIM = 64
DIFF_V_DIM = 128
ROPE_BASE = 10000.0
ROT_AXIS_DIM = 32
CONV_W = 512
GLA_HEADS = 4
GLA_DK = 64
GLA_DV = 128
GLA_RANK = 16
GLA_TAU = 16.0
GLA_CHUNK = 64
N_EXPERTS = 32
TOP_K = 4
D_EXPERT = 1024
SWIGLU_LIMIT = 7.0
SWIGLU_ALPHA = 1.702
LOG2_E = 1.4426950408889634

A16_Q, A16_K, A16_V, A16_GV, A16_W = 0, 1024, 2048, 3072, 3584
A32_GATES, A32_CVB, A32_CVC, A32_CVX = 0, 3072, 3584, 4096
A32_GLQ, A32_GLK, A32_GLR, A32_GLA, A32_W = 4608, 4864, 5120, 5632, 5760
W_ALL_COLS = A16_W + A32_W
W_GATES, W_GLA = 6144, 9216


def _k1_chunks():
    out = []
    def add(w0, width, to16, o0, rope=False, ch=512):
        for c in range(0, width, ch):
            out.append((w0 + c, min(ch, width - c), to16, o0 + c, rope))
    add(0, 1024, True, A16_Q, rope=True)
    add(1024, 1024, True, A16_K, rope=True)
    add(2048, 1024, True, A16_V)
    add(3072, 512, False, A32_CVB)
    add(3584, 512, False, A32_CVC)
    add(4096, 512, False, A32_CVX)
    add(4608, 256, False, A32_GLQ)
    add(4864, 256, False, A32_GLK)
    add(5120, 512, True, A16_GV)
    add(5632, 512, False, A32_GLR)
    add(W_GATES, 3072, False, A32_GATES)
    add(W_GLA, 128, False, A32_GLA)
    return tuple(out)


K1_CHUNKS = _k1_chunks()

Q_TILE = 256
KV_SEGMENTS = 2
ATTN_HEADS = 4
MOE_BLOCK = 256
GLA_UNROLL = 2
TOK_ROWS = 8
VMEM_LIMIT = 56 * 1024 * 1024


def _cparams(sem):
    return pltpu.CompilerParams(dimension_semantics=sem, vmem_limit_bytes=VMEM_LIMIT)


def _row_tile(t, target):
    best = None
    for cand in range(16, target + 1, 16):
        if t % cand == 0:
            best = cand
    assert best is not None, t
    return best


def _sigmoid(v):
    return 1.0 / (1.0 + jnp.exp(-v))


def _ada_kernel(cc_ref, w_ref, b_ref, o_ref):
    a = cc_ref[...]
    a = a * _sigmoid(a)
    o_ref[0] = jnp.dot(a, w_ref[0], precision=lax.Precision.HIGHEST,
                       preferred_element_type=F32) + b_ref[0]


def _ada(cc, ada_w, ada_b):
    nl, d, n = ada_w.shape
    tn = 1024
    return pl.pallas_call(
        _ada_kernel,
        grid=(nl, n // tn),
        in_specs=[
            pl.BlockSpec((16, d), lambda l, j: (0, 0)),
            pl.BlockSpec((1, d, tn), lambda l, j: (l, 0, j)),
            pl.BlockSpec((1, 1, tn), lambda l, j: (l, 0, j)),
        ],
        out_specs=pl.BlockSpec((1, 16, tn), lambda l, j: (l, 0, j)),
        out_shape=jax.ShapeDtypeStruct((nl, 16, n), F32),
        compiler_params=_cparams(("parallel", "parallel")),
        name="ada_mod",
    )(cc, ada_w, ada_b.reshape(nl, 1, n))


def _norm_mod(x, ng, shift, scale):
    y = x * lax.rsqrt(jnp.mean(x * x, axis=-1, keepdims=True) + RMS_EPS) * ng
    return y * (1.0 + scale) + shift


def _k1_kernel(x_ref, modl_ref, modc_ref, ng_ref, cos_ref, sa_ref, sb_ref, w_ref,
               o16_ref, o32_ref, *, tm, ctx_len):
    i = pl.program_id(1)
    pos = i * tm + lax.broadcasted_iota(jnp.int32, (tm, 1), 0)
    is_ctx = pos < ctx_len
    shift = jnp.where(is_ctx, modc_ref[0:1, :], modl_ref[0, 0:1, :])
    scale = jnp.where(is_ctx, modc_ref[1:2, :], modl_ref[0, 1:2, :])
    h = _norm_mod(x_ref[0], ng_ref[...], shift, scale).astype(BF16)
    cos = cos_ref[...]
    sa = sa_ref[...]
    sb = sb_ref[...]
    for w0, width, to16, o0, rope in K1_CHUNKS:
        acc = jnp.dot(h, w_ref[0, :, w0:w0 + width], preferred_element_type=F32)
        if rope:
            parts = []
            for j in range(width // 128):
                a = acc[:, j * 128:(j + 1) * 128]
                parts.append(a * cos + pltpu.roll(a, 16, 1) * sa + pltpu.roll(a, 112, 1) * sb)
            acc = jnp.concatenate(parts, axis=1)
        if to16 and o0 < A16_K:
            acc = acc * LOG2_E
        if to16:
            o16_ref[0, :, o0:o0 + width] = acc.astype(BF16)
        else:
            o32_ref[0, :, o0:o0 + width] = acc


def _k1(x, modl, modc, ng, cos, sa, sb, w_all, *, layer, ctx_len):
    b, t, d = x.shape
    tm = _row_tile(t, 272)
    tbl = pl.BlockSpec((tm, 128), lambda bi, i: (i, 0))
    return pl.pallas_call(
        functools.partial(_k1_kernel, tm=tm, ctx_len=ctx_len),
        grid=(b, t // tm),
        in_specs=[
            pl.BlockSpec((1, tm, d), lambda bi, i: (bi, i, 0)),
            pl.BlockSpec((1, 8, d), lambda bi, i: (bi, 0, 0)),
            pl.BlockSpec((8, d), lambda bi, i: (0, 0)),
            pl.BlockSpec((1, d), lambda bi, i: (0, 0)),
            tbl, tbl, tbl,
            pl.BlockSpec((1, d, W_ALL_COLS), lambda bi, i: (layer, 0, 0), pipeline_mode=pl.Buffered(1)),
        ],
        out_specs=[
            pl.BlockSpec((1, tm, A16_W), lambda bi, i: (bi, i, 0)),
            pl.BlockSpec((1, tm, A32_W), lambda bi, i: (bi, i, 0)),
        ],
        out_shape=[
            jax.ShapeDtypeStruct((b, t, A16_W), BF16),
            jax.ShapeDtypeStruct((b, t, A32_W), F32),
        ],
        compiler_params=_cparams(("parallel", "parallel")),
        name="norm_inproj",
    )(x, modl, modc, ng, cos, sa, sb, w_all)


def _attn_kernel(lam_ref, g_ref, q_ref, k_ref, v_ref, o_ref, *, ctx_len, segments, lam_init):
    qi = pl.program_id(2)
    tq = Q_TILE
    hb = DIFF_V_DIM
    lv = lam_ref[...]
    lam = (jnp.exp(jnp.sum(lv[0:1] * lv[1:2], axis=1, keepdims=True))
           - jnp.exp(jnp.sum(lv[2:3] * lv[3:4], axis=1, keepdims=True)) + lam_init)

    def run(segs):
        qss = []
        for g in range(ATTN_HEADS):
            q = q_ref[0, :, g * hb:(g + 1) * hb]
            lane = lax.broadcasted_iota(jnp.int32, q.shape, 1)
            zero = jnp.zeros_like(q)
            qss.append(jnp.concatenate([jnp.where(lane < DIFF_HEAD_DIM, q, zero),
                                        jnp.where(lane >= DIFF_HEAD_DIM, q, zero)], axis=0))
        carries = [None] * ATTN_HEADS
        for r0, rows in segs:
            for g in range(ATTN_HEADS):
                k_t = k_ref[0, r0:r0 + rows, g * hb:(g + 1) * hb]
                v_t = v_ref[0, r0:r0 + rows, g * hb:(g + 1) * hb]
                s = lax.dot_general(qss[g], k_t, (((1,), (1,)), ((), ())), preferred_element_type=F32)
                smax = jnp.max(s, axis=-1, keepdims=True)
                if carries[g] is None:
                    p = jnp.exp2(s - smax)
                    carries[g] = (smax, jnp.sum(p, axis=-1, keepdims=True),
                                  jnp.dot(p.astype(BF16), v_t, preferred_element_type=F32))
                else:
                    m, l, acc = carries[g]
                    m_new = jnp.maximum(m, smax)
                    alpha = jnp.exp2(m - m_new)
                    p = jnp.exp2(s - m_new)
                    carries[g] = (m_new, alpha * l + jnp.sum(p, axis=-1, keepdims=True),
                                  alpha * acc + jnp.dot(p.astype(BF16), v_t, preferred_element_type=F32))
        for g in range(ATTN_HEADS):
            _, l, acc = carries[g]
            o = acc[:tq] / l[:tq] - lam * (acc[tq:] / l[tq:])
            o = o * lax.rsqrt(jnp.mean(o * o, axis=-1, keepdims=True) + RMS_EPS) * g_ref[...]
            o_ref[0, :, g * hb:(g + 1) * hb] = (o * (1.0 - lam_init)).astype(BF16)

    @pl.when(qi * tq >= ctx_len)
    def _():
        run(segments)

    @pl.when(qi * tq < ctx_len)
    def _():
        run([(0, ctx_len)])


def _attn(a16, lam_p, subln_g, *, ctx_len, seq_len, lam_init):
    b, t, _ = a16.shape
    assert ctx_len % Q_TILE == 0 and t % 256 == 0
    first = (t // 256 + KV_SEGMENTS - 1) // KV_SEGMENTS * 256
    bounds = [min(first * n, t) for n in range(KV_SEGMENTS + 1)]
    segments = [(lo, hi - lo) for lo, hi in zip(bounds[:-1], bounds[1:]) if hi > lo]
    hb = DIFF_V_DIM * ATTN_HEADS
    return pl.pallas_call(
        functools.partial(_attn_kernel, ctx_len=ctx_len, segments=segments, lam_init=lam_init),
        grid=(b, DIFF_HEADS // ATTN_HEADS, t // Q_TILE),
        in_specs=[
            pl.BlockSpec((4, DIFF_HEAD_DIM), lambda bi, h, qi: (0, 0)),
            pl.BlockSpec((1, DIFF_V_DIM), lambda bi, h, qi: (0, 0)),
            pl.BlockSpec((1, Q_TILE, hb), lambda bi, h, qi: (bi, qi, A16_Q // hb + h)),
            pl.BlockSpec((1, t, hb), lambda bi, h, qi: (bi, 0, A16_K // hb + h)),
            pl.BlockSpec((1, t, hb), lambda bi, h, qi: (bi, 0, A16_V // hb + h)),
        ],
        out_specs=pl.BlockSpec((1, Q_TILE, hb), lambda bi, h, qi: (bi, qi, h)),
        out_shape=jax.ShapeDtypeStruct((b, t, DIFF_HEADS * DIFF_V_DIM), BF16),
        compiler_params=_cparams(("parallel", "parallel", "parallel")),
        name="diff_attn",
    )(lam_p, subln_g, a16, a16, a16)


def _split3(x):
    hi = x.astype(BF16)
    r1 = x - hi.astype(F32)
    mid = r1.astype(BF16)
    lo = (r1 - mid.astype(F32)).astype(BF16)
    return jnp.concatenate([hi, mid, lo], axis=1)


def _sum3(y, w):
    return y[:, 0:w] + y[:, w:2 * w] + y[:, 2 * w:3 * w]


def _gla_kernel(q_ref, k_ref, v_ref, a_ref, r_ref, wa_ref, ba_ref, ng_ref, o_ref,
                qd_s, ki_s, ke_s, dec_s, oacc_s, st_s, *, t, ctx_len):
    ck = GLA_CHUNK
    nc = t // ck
    ncc = ctx_len // ck
    pr = 256
    w = 2 * GLA_DK

    ri = lax.broadcasted_iota(jnp.int32, (pr, pr), 0)
    ci = lax.broadcasted_iota(jnp.int32, (pr, pr), 1)
    same = (ri // ck) == (ci // ck)
    m_tot = jnp.where(same, 1.0, 0.0).astype(BF16)
    m_cum = (jnp.where(same & (ci <= ri), 1.0, 0.0).astype(BF16),
             jnp.where(same & (ci >= ri), 1.0, 0.0).astype(BF16))
    for r0 in range(0, t, pr):
        a = a_ref[0, r0:r0 + pr, :].astype(BF16)
        logit = jnp.dot(a, wa_ref[0], preferred_element_type=F32) + ba_ref[0]
        g = (jnp.minimum(logit, 0.0) - jnp.log1p(jnp.exp(-jnp.abs(logit)))) * (1.0 / GLA_TAU)
        q = q_ref[0, r0:r0 + pr, :]
        k = k_ref[0, r0:r0 + pr, :]
        for d in range(2):
            g3 = _split3(g[:, d * w:(d + 1) * w])
            g_cum = _sum3(jnp.dot(m_cum[d], g3, preferred_element_type=F32), w)
            g_tot = _sum3(jnp.dot(m_tot, g3, preferred_element_type=F32), w)
            qd_s[d, r0:r0 + pr, :] = (q * jnp.exp(g_cum)).astype(BF16)
            ki_s[d, r0:r0 + pr, :] = (k * jnp.exp(-g_cum)).astype(BF16)
            ke_s[d, r0:r0 + pr, :] = (k * jnp.exp(g_tot - g_cum)).astype(BF16)
            dec_s[d, r0:r0 + pr, :] = jnp.exp(g_tot)
        oacc_s[r0:r0 + pr, :] = jnp.zeros((pr, 2 * GLA_DV), F32)
    st_s[...] = jnp.zeros_like(st_s)

    ri = lax.broadcasted_iota(jnp.int32, (2 * ck, ck), 0)
    ci = lax.broadcasted_iota(jnp.int32, (2 * ck, ck), 1)
    rr = jnp.where(ri >= ck, ri - ck, ri)
    tri = (ci <= rr, ci >= rr)
    lane = lax.broadcasted_iota(jnp.int32, (ck, w), 1)
    br = lax.broadcasted_iota(jnp.int32, (2 * GLA_DV, w), 0)
    bc = lax.broadcasted_iota(jnp.int32, (2 * GLA_DV, w), 1)
    blk = (br < GLA_DV) == (bc < GLA_DK)
    nt = (((1,), (1,)), ((), ()))
    tn = (((0,), (0,)), ((), ()))

    def chunk(d, r0):
        qd = qd_s[d, pl.ds(r0, ck), :]
        ki = ki_s[d, pl.ds(r0, ck), :]
        ke = ke_s[d, pl.ds(r0, ck), :]
        v = v_ref[0, pl.ds(r0, ck), :]
        dec = dec_s[d, pl.ds(r0, 8), :][0:1, :]
        zq = jnp.zeros_like(qd)
        qs = jnp.concatenate([jnp.where(lane < GLA_DK, qd, zq), jnp.where(lane >= GLA_DK, qd, zq)], axis=0)
        att = lax.dot_general(qs, ki, nt, preferred_element_type=F32)
        att = jnp.where(tri[d], att, 0.0).astype(BF16)
        oi = jnp.dot(att, v, preferred_element_type=F32)
        o_intra = jnp.concatenate([oi[:ck, :GLA_DV], oi[ck:, GLA_DV:]], axis=1)
        st = st_s[d]
        o_inter = lax.dot_general(qd, st.astype(BF16), nt, preferred_element_type=F32)
        s_loc = lax.dot_general(v, ke, tn, preferred_element_type=F32)
        st_s[d] = st * dec + jnp.where(blk, s_loc, 0.0)
        oacc_s[pl.ds(r0, ck), :] = oacc_s[pl.ds(r0, ck), :] + o_intra + o_inter

    def body(n, _):
        chunk(0, pl.multiple_of(n * ck, ck))
        nb = jnp.where(n < ncc, ncc - 1 - n, nc - 1 - (n - ncc))
        chunk(1, pl.multiple_of(nb * ck, ck))
        return 0

    lax.fori_loop(0, nc, body, 0, unroll=GLA_UNROLL)

    ng = ng_ref[...]
    rt = _row_tile(t, 544)
    for r0 in range(0, t, rt):
        o = oacc_s[r0:r0 + rt, :]
        r = r_ref[0, r0:r0 + rt, :]
        outs = []
        for hh in range(2):
            oh = o[:, hh * GLA_DV:(hh + 1) * GLA_DV]
            outs.append(oh * lax.rsqrt(jnp.mean(oh * oh, axis=-1, keepdims=True) + RMS_EPS) * ng)
        o_ref[0, r0:r0 + rt, :] = (jnp.concatenate(outs, axis=1) * (r * _sigmoid(r))).astype(BF16)


def _gla(a16, a32, wa, ba, ng, *, ctx_len):
    b, t, _ = a16.shape
    assert t % 256 == 0 and ctx_len % GLA_CHUNK == 0 and (t // GLA_CHUNK) % GLA_UNROLL == 0
    return pl.pallas_call(
        functools.partial(_gla_kernel, t=t, ctx_len=ctx_len),
        grid=(b, 2),
        in_specs=[
            pl.BlockSpec((1, t, 128), lambda bi, p: (bi, 0, A32_GLQ // 128 + p)),
            pl.BlockSpec((1, t, 128), lambda bi, p: (bi, 0, A32_GLK // 128 + p)),
            pl.BlockSpec((1, t, 256), lambda bi, p: (bi, 0, A16_GV // 256 + p)),
            pl.BlockSpec((1, t, 128), lambda bi, p: (bi, 0, A32_GLA // 128)),
            pl.BlockSpec((1, t, 256), lambda bi, p: (bi, 0, A32_GLR // 256 + p)),
            pl.BlockSpec((1, 128, 256), lambda bi, p: (p, 0, 0)),
            pl.BlockSpec((1, 1, 256), lambda bi, p: (p, 0, 0)),
            pl.BlockSpec((1, GLA_DV), lambda bi, p: (0, 0)),
        ],
        out_specs=pl.BlockSpec((1, t, 256), lambda bi, p: (bi, 0, p)),
        out_shape=jax.ShapeDtypeStruct((b, t, GLA_HEADS * GLA_DV), BF16),
        scratch_shapes=[
            pltpu.VMEM((2, t, 128), BF16), pltpu.VMEM((2, t, 128), BF16), pltpu.VMEM((2, t, 128), BF16),
            pltpu.VMEM((2, t, 128), F32),
            pltpu.VMEM((t, 256), F32),
            pltpu.VMEM((2, 256, 128), F32),
        ],
        compiler_params=_cparams(("parallel", "parallel")),
        name="gla_bidir",
    )(a32, a32, a16, a32, a32, wa, ba, ng)


def _merge_kernel(x_ref, od_ref, og_ref, gates_ref, cvb_ref, cvc_ref, cvx_ref,
                  cvc_p_ref, cvx_p_ref, cvc_n_ref, cvx_n_ref, cw_ref,
                  wdo_ref, wco_ref, wgo_ref, wo_ref, modl_ref, modc_ref, n2g_ref,
                  rw_ref, rb_ref, xo_ref, h2_ref, rt_ref, *, tm, ctx_len, t):
    i = pl.program_id(1)
    pos = i * tm + lax.broadcasted_iota(jnp.int32, (tm, 1), 0)
    is_ctx = pos < ctx_len
    row = lax.broadcasted_iota(jnp.int32, (tm, 1), 0)

    z = cvc_ref[0] * cvx_ref[0]
    z_before = cvc_p_ref[0, 7:8, :] * cvx_p_ref[0, 7:8, :]
    z_after = cvc_n_ref[0, 0:1, :] * cvx_n_ref[0, 0:1, :]
    z_prev = jnp.where(row == 0, z_before, pltpu.roll(z, 1, 0))
    z_prev = jnp.where((pos == 0) | (pos == ctx_len), 0.0, z_prev)
    z_next = jnp.where(row == tm - 1, z_after, pltpu.roll(z, tm - 1, 0))
    z_next = jnp.where((pos == ctx_len - 1) | (pos == t - 1), 0.0, z_next)
    conv = z_prev * cw_ref[0:1, :] + z * cw_ref[1:2, :] + z_next * cw_ref[2:3, :]
    zc = (cvb_ref[0] * conv).astype(BF16)

    y_diff = jnp.dot(od_ref[0], wdo_ref[0], preferred_element_type=F32)
    y_conv = jnp.dot(zc, wco_ref[0], preferred_element_type=F32)
    y_gla = jnp.dot(og_ref[0], wgo_ref[0], preferred_element_type=F32)
    d = D_MODEL
    mix = (_sigmoid(gates_ref[0, :, 0:d]) * y_diff
           + _sigmoid(gates_ref[0, :, d:2 * d]) * y_conv
           + _sigmoid(gates_ref[0, :, 2 * d:3 * d]) * y_gla)
    m = jnp.dot(mix.astype(BF16), wo_ref[0], preferred_element_type=F32)

    g1 = jnp.where(is_ctx, modc_ref[2:3, :], modl_ref[0, 2:3, :])
    x_new = x_ref[0] + g1 * m
    xo_ref[0] = x_new
    sh2 = jnp.where(is_ctx, modc_ref[3:4, :], modl_ref[0, 3:4, :])
    sc2 = jnp.where(is_ctx, modc_ref[4:5, :], modl_ref[0, 4:5, :])
    h2 = _norm_mod(x_new, n2g_ref[...], sh2, sc2)
    for cc in range(d // 128):
        h2_ref[0, pl.ds(cc, tm, stride=8), :] = h2[:, cc * 128:(cc + 1) * 128]

    logits = jnp.dot(h2.astype(BF16), rw_ref[...], preferred_element_type=F32) + rb_ref[...]
    lane = lax.broadcasted_iota(jnp.int32, logits.shape, 1)
    vals = logits
    tops = []
    idx_out = jnp.zeros(logits.shape, jnp.int32)
    for kk in range(TOP_K):
        mx = jnp.max(vals, axis=-1, keepdims=True)
        ix = jnp.min(jnp.where(vals == mx, lane, 128), axis=-1, keepdims=True)
        tops.append(mx)
        idx_out = jnp.where(lane == kk, ix, idx_out)
        vals = jnp.where(lane == ix, -jnp.inf, vals)
    es = [jnp.exp(v - tops[0]) for v in tops]
    den = es[0] + es[1] + es[2] + es[3]
    packed = idx_out.astype(F32)
    for kk in range(TOP_K):
        gk = es[kk] / den
        hi = gk.astype(BF16).astype(F32)
        mid = (gk - hi).astype(BF16).astype(F32)
        lo = gk - hi - mid
        for piece, val in enumerate((hi, mid, lo)):
            packed = jnp.where(lane == TOP_K * (piece + 1) + kk, val, packed)
    ri = lax.broadcasted_iota(jnp.int32, (tm, tm), 0)
    ci = lax.broadcasted_iota(jnp.int32, (tm, tm), 1)
    eye = jnp.where(ri == ci, 1.0, 0.0).astype(BF16)
    rows = lax.dot_general(packed.astype(BF16), eye, (((0,), (0,)), ((), ())), preferred_element_type=F32)
    rt_ref[0, 0] = rows[0:4 * TOP_K, :]


def _merge(x, od, og, a32, cw, wdo, wco, wgo, wo, modl, modc, n2g, rw, rb, *, layer, ctx_len):
    b, t, d = x.shape
    tm = _row_tile(t, 544)
    nt8 = t // 8
    row = lambda bi, i: (bi, i, 0)
    const2 = lambda bi, i: (0, 0)
    wl = lambda bi, i: (layer, 0, 0)
    return pl.pallas_call(
        functools.partial(_merge_kernel, tm=tm, ctx_len=ctx_len, t=t),
        grid=(b, t // tm),
        in_specs=[
            pl.BlockSpec((1, tm, d), row),
            pl.BlockSpec((1, tm, d), row),
            pl.BlockSpec((1, tm, GLA_HEADS * GLA_DV), row),
            pl.BlockSpec((1, tm, 3 * d), lambda bi, i: (bi, i, 0)),
            pl.BlockSpec((1, tm, CONV_W), lambda bi, i: (bi, i, A32_CVB // CONV_W)),
            pl.BlockSpec((1, tm, CONV_W), lambda bi, i: (bi, i, A32_CVC // CONV_W)),
            pl.BlockSpec((1, tm, CONV_W), lambda bi, i: (bi, i, A32_CVX // CONV_W)),
            pl.BlockSpec((1, 8, CONV_W), lambda bi, i: (bi, jnp.maximum(i * (tm // 8) - 1, 0),
                                                   A32_CVC // CONV_W)),
            pl.BlockSpec((1, 8, CONV_W), lambda bi, i: (bi, jnp.maximum(i * (tm // 8) - 1, 0),
                                                   A32_CVX // CONV_W)),
            pl.BlockSpec((1, 8, CONV_W), lambda bi, i: (bi, jnp.minimum((i + 1) * (tm // 8), nt8 - 1),
                                                   A32_CVC // CONV_W)),
            pl.BlockSpec((1, 8, CONV_W), lambda bi, i: (bi, jnp.minimum((i + 1) * (tm // 8), nt8 - 1),
                                                   A32_CVX // CONV_W)),
            pl.BlockSpec((8, CONV_W), const2),
            pl.BlockSpec((1, d, d), wl),
            pl.BlockSpec((1, CONV_W, d), wl),
            pl.BlockSpec((1, GLA_HEADS * GLA_DV, d), wl),
            pl.BlockSpec((1, d, d), wl),
            pl.BlockSpec((1, 8, d), lambda bi, i: (bi, 0, 0)),
            pl.BlockSpec((8, d), const2),
            pl.BlockSpec((1, d), const2),
            pl.BlockSpec((d, 128), const2),
            pl.BlockSpec((1, 128), const2),
        ],
        out_specs=[
            pl.BlockSpec((1, tm, d), row),
            pl.BlockSpec((1, tm * (d // 128), 128), row),
            pl.BlockSpec((1, 1, 4 * TOP_K, tm), lambda bi, i: (bi, i, 0, 0)),
        ],
        out_shape=[
            jax.ShapeDtypeStruct((b, t, d), F32),
            jax.ShapeDtypeStruct((b, t * (d // 128), 128), F32),
            jax.ShapeDtypeStruct((b, t // tm, 4 * TOP_K, tm), F32),
        ],
        compiler_params=_cparams(("parallel", "parallel")),
        name="merge_router",
    )(x, od, og, a32, a32, a32, a32, a32, a32, a32, a32, cw, wdo, wco, wgo, wo,
      modl, modc, n2g, rw, rb)


def _moe_kernel(be_ref, nv_ref, g_ref, w1g_ref, w1l_ref, b1g_ref, b1l_ref, w2_ref, b2_ref,
                src_hbm, dst_hbm, h2_hbm, out_hbm, src_s, dst_s, xbuf, ybuf, sem_p, sem_g, sem_s,
                *, nb, n_slots):
    i = pl.program_id(0)
    nv = nv_ref[0]
    bm = MOE_BLOCK
    cur = lax.rem(i, 2)
    nxt = 1 - cur

    def idx_copies(blk_src, blk_dst, slot):
        return (pltpu.make_async_copy(src_hbm.at[blk_src, 0], src_s.at[pl.ds(slot * bm, bm)], sem_p),
                pltpu.make_async_copy(dst_hbm.at[blk_dst, 0], dst_s.at[pl.ds(slot * bm, bm)], sem_p))

    def start_gather(slot, buf):
        for j in range(bm):
            r0 = pl.multiple_of(src_s[slot * bm + j], TOK_ROWS)
            pltpu.make_async_copy(h2_hbm.at[pl.ds(r0, TOK_ROWS)],
                                  xbuf.at[buf, pl.ds(j * TOK_ROWS, TOK_ROWS)],
                                  sem_g.at[buf]).start(priority=j % 2)

    def wait_gather(buf):
        pltpu.make_async_copy(h2_hbm.at[pl.ds(0, bm * TOK_ROWS)], xbuf.at[buf], sem_g.at[buf]).wait()

    def wait_scatter():
        pltpu.make_async_copy(ybuf.at[0], out_hbm.at[pl.ds(0, bm * TOK_ROWS)], sem_s).wait()

    @pl.when(i <= nv)
    def _():
        @pl.when(i == 0)
        def _():
            ybuf[...] = jnp.zeros_like(ybuf)
            for blk_src, blk_dst, slot in ((0, 0, 0), (1, 1, 1), (0, nb, 3)):
                for cp in idx_copies(blk_src, blk_dst, slot):
                    cp.start()
                    cp.wait()
            start_gather(0, 0)

        @pl.when(i >= 1)
        def _():
            for cp in idx_copies(0, 0, 0):
                cp.wait()
            wait_scatter()

        wait_gather(cur)

        @pl.when(i < nv)
        def _():
            blk = jnp.minimum(i + 2, nv - 1)
            for cp in idx_copies(blk, blk, lax.rem(i + 2, 4)):
                cp.start()

        start_gather(lax.rem(i + 1, 4), nxt)
        pslot = lax.rem(i + 3, 4)
        for j in range(bm):
            r0 = pl.multiple_of(dst_s[pslot * bm + j], TOK_ROWS)
            pltpu.make_async_copy(ybuf.at[nxt, pl.ds(j * TOK_ROWS, TOK_ROWS)],
                                  out_hbm.at[pl.ds(r0, TOK_ROWS)], sem_s).start(priority=j % 2)
        x = jnp.concatenate([xbuf[cur, pl.ds(cc, bm, stride=TOK_ROWS), :] for cc in range(TOK_ROWS)],
                            axis=1).astype(BF16)
        hg = jnp.dot(x, w1g_ref[0, 0, 0], preferred_element_type=F32) + b1g_ref[0]
        hl = jnp.dot(x, w1l_ref[0, 0, 0], preferred_element_type=F32) + b1l_ref[0]
        hg = jnp.minimum(hg, SWIGLU_LIMIT)
        hl = jnp.clip(hl, -SWIGLU_LIMIT, SWIGLU_LIMIT)
        act = hg * _sigmoid(SWIGLU_ALPHA * hg) * (hl + 1.0)
        y = jnp.dot(act.astype(BF16), w2_ref[0, 0], preferred_element_type=F32) + b2_ref[0]
        gi = lax.broadcasted_iota(jnp.int32, (bm, bm), 0)
        gj = lax.broadcasted_iota(jnp.int32, (bm, bm), 1)
        y = y * jnp.sum(jnp.where(gi == gj, g_ref[0], 0.0), axis=1, keepdims=True)
        for cc in range(TOK_ROWS):
            ybuf[cur, pl.ds(cc, bm, stride=TOK_ROWS), :] = y[:, cc * 128:(cc + 1) * 128]

        @pl.when(i == nv)
        def _():
            wait_scatter()
            wait_gather(nxt)
            ybuf[...] = jnp.zeros_like(ybuf)
            fills = [pltpu.make_async_copy(
                ybuf.at[sl], out_hbm.at[pl.ds((n_slots + sl * bm) * TOK_ROWS, bm * TOK_ROWS)], sem_s)
                for sl in range(2)]
            for cp in fills:
                cp.start()
            for cp in fills:
                cp.wait()


def _moe(block_exp, n_valid, row_src, row_dst, row_gate, h2_flat, w1, b1g, b1l, w2, b2, *, layer, n_slots):
    d = D_MODEL
    nb = row_src.shape[0]
    bm = MOE_BLOCK
    de = w1.shape[-1]
    blk = lambda i, nv: jnp.minimum(i, nv[0] - 1)
    wmap = lambda i, be, nv: (be[blk(i, nv)], 0, 0)
    return pl.pallas_call(
        functools.partial(_moe_kernel, nb=nb, n_slots=n_slots),
        grid_spec=pltpu.PrefetchScalarGridSpec(
            num_scalar_prefetch=2,
            grid=(nb + 1,),
            in_specs=[
                pl.BlockSpec((1, 1, bm), lambda i, be, nv: (blk(i, nv), 0, 0)),
                pl.BlockSpec((1, 1, 1, d, de), lambda i, be, nv: (layer, be[blk(i, nv)], 0, 0, 0)),
                pl.BlockSpec((1, 1, 1, d, de), lambda i, be, nv: (layer, be[blk(i, nv)], 1, 0, 0)),
                pl.BlockSpec((1, 1, de), wmap),
                pl.BlockSpec((1, 1, de), wmap),
                pl.BlockSpec((1, 1, de, d), lambda i, be, nv: (layer, be[blk(i, nv)], 0, 0)),
                pl.BlockSpec((1, 1, d), wmap),
                pl.BlockSpec(memory_space=pl.ANY),
                pl.BlockSpec(memory_space=pl.ANY),
                pl.BlockSpec(memory_space=pl.ANY),
            ],
            out_specs=pl.BlockSpec(memory_space=pl.ANY),
            scratch_shapes=[
                pltpu.SMEM((4 * bm,), jnp.int32),
                pltpu.SMEM((4 * bm,), jnp.int32),
                pltpu.VMEM((2, bm * TOK_ROWS, 128), F32),
                pltpu.VMEM((2, bm * TOK_ROWS, 128), F32),
                pltpu.SemaphoreType.DMA,
                pltpu.SemaphoreType.DMA((2,)),
                pltpu.SemaphoreType.DMA,
            ],
        ),
        out_shape=jax.ShapeDtypeStruct(((n_slots + 2 * bm) * TOK_ROWS, 128), F32),
        compiler_params=_cparams(("arbitrary",)),
        name="moe_experts",
    )(block_exp, n_valid, row_gate, w1, w1, b1g, b1l, w2, b2, row_src, row_dst, h2_flat)


def _combine_kernel(x_ref, y4_ref, modl_ref, modc_ref, fg_ref, o_ref, sum_s, *, tm, ctx_len, final, skip):
    i = pl.program_id(1) + skip
    tiles = (y4_ref[:, 0] + y4_ref[:, 1]) + (y4_ref[:, 2] + y4_ref[:, 3])
    sum_s[...] = tiles.reshape(tm * TOK_ROWS, 128)
    f = jnp.concatenate([sum_s[pl.ds(cc, tm, stride=TOK_ROWS), :] for cc in range(TOK_ROWS)], axis=1)
    pos = i * tm + lax.broadcasted_iota(jnp.int32, (tm, 1), 0)
    g2 = jnp.where(pos < ctx_len, modc_ref[5:6, :], modl_ref[0, 5:6, :])
    x_new = x_ref[0] + g2 * f
    if final:
        x_new = x_new * lax.rsqrt(jnp.mean(x_new * x_new, axis=-1, keepdims=True) + RMS_EPS) * fg_ref[...]
    o_ref[0] = x_new


def _combine(x, y4, modl, modc, fg, *, ctx_len, final):
    b, t, d = x.shape
    if final:
        tm = Q_TILE
        assert ctx_len % tm == 0
        skip = ctx_len // tm
    else:
        tm = _row_tile(t, 272)
        skip = 0
    npb = t // tm
    return pl.pallas_call(
        functools.partial(_combine_kernel, tm=tm, ctx_len=ctx_len, final=final, skip=skip),
        grid=(b, npb - skip),
        in_specs=[
            pl.BlockSpec((1, tm, d), lambda bi, i: (bi, i + skip, 0)),
            pl.BlockSpec((tm, TOP_K, TOK_ROWS, 128), lambda bi, i: (bi * (npb - skip) + i, 0, 0, 0)),
            pl.BlockSpec((1, 8, d), lambda bi, i: (bi, 0, 0)),
            pl.BlockSpec((8, d), lambda bi, i: (0, 0)),
            pl.BlockSpec((1, d), lambda bi, i: (0, 0)),
        ],
        out_specs=pl.BlockSpec((1, tm, d), lambda bi, i: (bi, i, 0)),
        out_shape=jax.ShapeDtypeStruct((b, t - skip * tm, d), F32),
        scratch_shapes=[pltpu.VMEM((tm * TOK_ROWS, 128), F32)],
        compiler_params=_cparams(("parallel", "parallel")),
        name="moe_combine",
    )(x, y4.reshape(-1, TOP_K, TOK_ROWS, 128), modl, modc, fg)


def _routing(e_flat, gate_flat, *, tm, seq, ctx_len, latent_only):
    npair = e_flat.shape[0]
    t = seq + ctx_len
    bm = MOE_BLOCK
    nb = npair // bm + N_EXPERTS
    nr = nb * bm
    npad = nr - npair
    experts = jnp.arange(N_EXPERTS, dtype=jnp.int32)
    ids = jnp.arange(npair, dtype=jnp.int32)
    tok_of = lambda i: (i // (TOP_K * tm)) * tm + i % tm
    if latent_only:
        e_flat = jnp.where(tok_of(ids) % t >= ctx_len, e_flat, N_EXPERTS)
    counts = jnp.sum((e_flat[:, None] == experts[None, :]).astype(jnp.int32), axis=0)
    padded = (counts + bm - 1) // bm * bm
    pad_end = jnp.cumsum(padded)
    fill_end = jnp.cumsum(padded - counts)
    dummy = jnp.arange(npad, dtype=jnp.int32)
    e_dummy = jnp.sum((dummy[:, None] >= fill_end[None, :]).astype(jnp.int32), axis=1)
    kb = nr
    keys = jnp.concatenate([e_flat * kb + ids, e_dummy * kb + npair + dummy])
    gates = jnp.concatenate([gate_flat, jnp.zeros((npad,), F32)])
    skey, sgate = lax.sort((keys, gates), num_keys=1)
    e_row = skey // kb
    ident = skey - e_row * kb
    valid = (ident < npair) & (e_row < N_EXPERTS)
    pair = jnp.where(valid, ident, 0)
    tok = tok_of(pair)
    kk = (pair // tm) % TOP_K
    row_gate = jnp.where(valid, sgate, 0.0).reshape(nb, 1, bm)
    row_src = (tok * TOK_ROWS).reshape(nb, 1, bm)
    if latent_only:
        slot = (tok // t) * seq + tok % t - ctx_len
        n_slots = (npair // TOP_K // t) * seq * TOP_K
    else:
        slot = tok
        n_slots = npair
    j = jnp.arange(nr, dtype=jnp.int32)
    dump = n_slots + ((j // bm) % 2) * bm + j % bm
    row_dst = jnp.where(valid, slot * TOP_K + kk, dump).reshape(nb, bm)
    row_dst = jnp.concatenate([row_dst, (n_slots + bm + jnp.arange(bm, dtype=jnp.int32))[None, :]],
                              axis=0)[:, None, :] * TOK_ROWS
    blocks = jnp.arange(nb, dtype=jnp.int32) * bm
    block_exp = jnp.minimum(jnp.sum((blocks[:, None] >= pad_end[None, :]).astype(jnp.int32), axis=1),
                            N_EXPERTS - 1)
    n_valid = (pad_end[-1] // bm).astype(jnp.int32).reshape(1)
    return row_src, row_dst, row_gate, block_exp, n_valid, n_slots


def _rope_tables(ctx_len, seq_len):
    inv_freq = ROPE_BASE ** (-jnp.arange(0, ROT_AXIS_DIM, 2, dtype=F32) / ROT_AXIS_DIM)
    p = jnp.arange(seq_len, dtype=jnp.int32)
    row = (p // GRID_W).astype(F32)
    col = (p % GRID_W).astype(F32)
    lane = np.arange(128)
    dd = lane % DIFF_HEAD_DIM
    fidx = jnp.asarray((dd % ROT_AXIS_DIM) % (ROT_AXIS_DIM // 2))
    use_col = jnp.asarray(dd >= ROT_AXIS_DIM)
    second_half = jnp.asarray((dd % ROT_AXIS_DIM) >= ROT_AXIS_DIM // 2)
    posm = jnp.where(use_col[None, :], col[:, None], row[:, None])
    ang = posm * inv_freq[fidx][None, :]
    cos = jnp.cos(ang)
    sin = jnp.sin(ang)
    sa = jnp.where(second_half[None, :], sin, 0.0)
    sb = jnp.where(second_half[None, :], 0.0, -sin)
    pad = lambda a, v: jnp.concatenate([jnp.full((ctx_len, 128), v, F32), a], axis=0)
    return pad(cos, 1.0), pad(sa, 0.0), pad(sb, 0.0)


def kernel(x, c, ctx, c_ctx, ada_w, ada_b, norm1_g, norm2_g, w_in, diff_lambda, diff_subln_g,
           diff_w_out, conv_w, conv_w_out, gla_w_a2, gla_b_a, gla_norm_g, gla_w_out, w_o,
           router_w, router_b, moe_w1, moe_b1, moe_w2, moe_b2, final_norm_g):
    b, s, d = x.shape
    cl = ctx.shape[1]
    t = cl + s
    nl = w_in.shape[0]
    assert d == D_MODEL and b <= 15

    xs = jnp.concatenate([ctx, x], axis=1)
    cc = jnp.concatenate([c, c_ctx[None, :], jnp.zeros((15 - b, d), F32)], axis=0)
    mod = _ada(cc, ada_w, ada_b).reshape(nl, 16, N_ADA, d)
    cos, sa, sb = _rope_tables(cl, s)

    col_scale = np.ones((w_in.shape[-1],), np.float32)
    col_scale[0:1024] = DIFF_HEAD_DIM ** -0.5
    col_scale[4608:4864] = GLA_DK ** -0.5
    w_s = (w_in * jnp.asarray(col_scale)).astype(BF16)
    w_all = jnp.concatenate(
        [w_s[..., :W_GATES], w_s[..., W_GATES + 2 * GLA_RANK:], w_s[..., W_GATES:W_GATES + 2 * GLA_RANK],
         jnp.zeros((nl, d, W_ALL_COLS - w_in.shape[-1]), BF16)], axis=-1)

    wa = jnp.zeros((nl, 2, 128, 256), F32)
    for p in range(2):
        wa = wa.at[:, p, 0:GLA_RANK, 0:128].set(gla_w_a2[:, 0, :, p * 128:(p + 1) * 128])
        wa = wa.at[:, p, GLA_RANK:2 * GLA_RANK, 128:256].set(gla_w_a2[:, 1, :, p * 128:(p + 1) * 128])
    wa = wa.astype(BF16)
    ba = jnp.stack([jnp.concatenate([gla_b_a[:, 0, p * 128:(p + 1) * 128],
                                     gla_b_a[:, 1, p * 128:(p + 1) * 128]], axis=-1)
                    for p in range(2)], axis=1)[:, :, None, :]

    cw = jnp.concatenate([conv_w, jnp.zeros((nl, 5, CONV_W), F32)], axis=1)
    rw = jnp.concatenate([router_w, jnp.zeros((nl, d, 128 - N_EXPERTS), F32)], axis=-1).astype(BF16)
    rb = jnp.concatenate([router_b, jnp.full((nl, 128 - N_EXPERTS), -1e30, F32)], axis=-1)[:, None, :]
    w1 = jnp.moveaxis(moe_w1.reshape(nl, N_EXPERTS, d, D_EXPERT, 2), -1, 2).astype(BF16)
    b1 = moe_b1.reshape(nl, N_EXPERTS, 1, D_EXPERT, 2)
    b1g = b1[..., 0]
    b1l = b1[..., 1]
    w2 = moe_w2.astype(BF16)
    b2 = moe_b2[:, :, None, :]
    wdo = diff_w_out.astype(BF16)
    wco = conv_w_out.astype(BF16)
    wgo = gla_w_out.astype(BF16)
    wo = w_o.astype(BF16)

    for layer in range(nl):
        last = layer == nl - 1
        lam_init = 0.8 - 0.6 * math.exp(-0.3 * layer)
        ml = mod[layer]
        modl = jnp.concatenate([ml[:b], jnp.zeros((b, 2, d), F32)], axis=1)
        modc = jnp.concatenate([ml[b], jnp.zeros((2, d), F32)], axis=0)

        a16, a32 = _k1(xs, modl, modc, norm1_g[layer][None, :], cos, sa, sb, w_all, layer=layer, ctx_len=cl)
        od = _attn(a16, diff_lambda[layer], diff_subln_g[layer][None, :],
                   ctx_len=cl, seq_len=s, lam_init=lam_init)
        og = _gla(a16, a32, wa[layer], ba[layer], gla_norm_g[layer][None, :], ctx_len=cl)
        xs, h2, rt = _merge(xs, od, og, a32, cw[layer], wdo, wco, wgo, wo, modl, modc,
                            norm2_g[layer][None, :], rw[layer], rb[layer], layer=layer, ctx_len=cl)

        r = b * t
        tm_r = rt.shape[-1]
        rt = rt.reshape(-1, 4, TOP_K, tm_r)
        row_src, row_dst, row_gate, block_exp, n_valid, n_slots = _routing(
            rt[:, 0].astype(jnp.int32).reshape(-1), (rt[:, 1] + rt[:, 2] + rt[:, 3]).reshape(-1),
            tm=tm_r, seq=s, ctx_len=cl, latent_only=last)
        y4 = _moe(block_exp, n_valid, row_src, row_dst, row_gate, h2.reshape(r * TOK_ROWS, 128), w1,
                  b1g[layer], b1l[layer], w2, b2[layer], layer=layer, n_slots=n_slots)
        xs = _combine(xs, y4, modl, modc, final_norm_g[None, :], ctx_len=cl, final=last)

    return xs
```

```python
import functools
import math

import numpy as np
import jax
import jax.numpy as jnp
from jax import lax
from jax.experimental import pallas as pl
from jax.experimental.pallas import tpu as pltpu

F32 = jnp.float32
BF16 = jnp.bfloat16

D_MODEL = 1024
GRID_W = 64
RMS_EPS = 1e-6
N_ADA = 6
DIFF_HEADS = 8
DIFF_HEAD_DIM = 64
DIFF_V_DIM = 128
ROPE_BASE = 10000.0
ROT_AXIS_DIM = 32
CONV_W = 512
GLA_HEADS = 4
GLA_DK = 64
GLA_DV = 128
GLA_RANK = 16
GLA_TAU = 16.0
GLA_CHUNK = 64
N_EXPERTS = 32
TOP_K = 4
D_EXPERT = 1024
SWIGLU_LIMIT = 7.0
SWIGLU_ALPHA = 1.702
LOG2_E = 1.4426950408889634

A16_Q, A16_K, A16_V, A16_GV, A16_W = 0, 1024, 2048, 3072, 3584
A32_GATES, A32_CVB, A32_CVC, A32_CVX = 0, 3072, 3584, 4096
A32_GLQ, A32_GLK, A32_GLR, A32_GLA, A32_W = 4608, 4864, 5120, 5632, 5760
W_ALL_COLS = A16_W + A32_W
W_GATES, W_GLA = 6144, 9216


def _k1_chunks():
    out = []
    def add(w0, width, to16, o0, rope=False, ch=512):
        for c in range(0, width, ch):
            out.append((w0 + c, min(ch, width - c), to16, o0 + c, rope))
    add(0, 1024, True, A16_Q, rope=True)
    add(1024, 1024, True, A16_K, rope=True)
    add(2048, 1024, True, A16_V)
    add(3072, 512, False, A32_CVB)
    add(3584, 512, False, A32_CVC)
    add(4096, 512, False, A32_CVX)
    add(4608, 256, False, A32_GLQ)
    add(4864, 256, False, A32_GLK)
    add(5120, 512, True, A16_GV)
    add(5632, 512, False, A32_GLR)
    add(W_GATES, 3072, False, A32_GATES)
    add(W_GLA, 128, False, A32_GLA)
    return tuple(out)


K1_CHUNKS = _k1_chunks()

Q_TILE = 256
KV_SEGMENTS = 2
ATTN_HEADS = 4
MOE_BLOCK = 256
GLA_UNROLL = 2
TOK_ROWS = 8
V7X_VMEM_BYTES = 64 * 1024 * 1024
VMEM_LIMIT = V7X_VMEM_BYTES - 8 * 1024 * 1024


def _cparams(sem):
    return pltpu.CompilerParams(dimension_semantics=sem, vmem_limit_bytes=VMEM_LIMIT)


def _row_tile(t, target):
    best = None
    for cand in range(16, target + 1, 16):
        if t % cand == 0:
            best = cand
    assert best is not None, t
    return best


def _sigmoid(v):
    return 1.0 / (1.0 + jnp.exp(-v))


def _ada_kernel(cc_ref, w_ref, b_ref, o_ref):
    a = cc_ref[...]
    a = a * _sigmoid(a)
    o_ref[0] = jnp.dot(a, w_ref[0], precision=lax.Precision.HIGHEST,
                       preferred_element_type=F32) + b_ref[0]


def _ada(cc, ada_w, ada_b):
    nl, d, n = ada_w.shape
    tn = 1024
    return pl.pallas_call(
        _ada_kernel,
        grid=(nl, n // tn),
        in_specs=[
            pl.BlockSpec((16, d), lambda l, j: (0, 0)),
            pl.BlockSpec((1, d, tn), lambda l, j: (l, 0, j)),
            pl.BlockSpec((1, 1, tn), lambda l, j: (l, 0, j)),
        ],
        out_specs=pl.BlockSpec((1, 16, tn), lambda l, j: (l, 0, j)),
        out_shape=jax.ShapeDtypeStruct((nl, 16, n), F32),
        compiler_params=_cparams(("parallel", "parallel")),
        name="ada_mod",
    )(cc, ada_w, ada_b.reshape(nl, 1, n))


def _norm_mod(x, ng, shift, scale):
    y = x * lax.rsqrt(jnp.mean(x * x, axis=-1, keepdims=True) + RMS_EPS) * ng
    return y * (1.0 + scale) + shift


def _k1_kernel(x_ref, modl_ref, modc_ref, ng_ref, cos_ref, sa_ref, sb_ref, w_ref,
               o16_ref, o32_ref, *, tm, ctx_len):
    i = pl.program_id(1)
    pos = i * tm + lax.broadcasted_iota(jnp.int32, (tm, 1), 0)
    is_ctx = pos < ctx_len
    shift = jnp.where(is_ctx, modc_ref[0:1, :], modl_ref[0, 0:1, :])
    scale = jnp.where(is_ctx, modc_ref[1:2, :], modl_ref[0, 1:2, :])
    h = _norm_mod(x_ref[0], ng_ref[...], shift, scale).astype(BF16)
    cos = cos_ref[...]
    sa = sa_ref[...]
    sb = sb_ref[...]
    for w0, width, to16, o0, rope in K1_CHUNKS:
        acc = jnp.dot(h, w_ref[0, :, w0:w0 + width], preferred_element_type=F32)
        if rope:
            parts = []
            for j in range(width // 128):
                a = acc[:, j * 128:(j + 1) * 128]
                parts.append(a * cos + pltpu.roll(a, 16, 1) * sa + pltpu.roll(a, 112, 1) * sb)
            acc = jnp.concatenate(parts, axis=1)
        if to16 and o0 < A16_K:
            acc = acc * LOG2_E
        if to16:
            o16_ref[0, :, o0:o0 + width] = acc.astype(BF16)
        else:
            o32_ref[0, :, o0:o0 + width] = acc


def _k1(x, modl, modc, ng, cos, sa, sb, w_all, *, layer, ctx_len):
    b, t, d = x.shape
    tm = _row_tile(t, 272)
    tbl = pl.BlockSpec((tm, 128), lambda bi, i: (i, 0))
    return pl.pallas_call(
        functools.partial(_k1_kernel, tm=tm, ctx_len=ctx_len),
        grid=(b, t // tm),
        in_specs=[
            pl.BlockSpec((1, tm, d), lambda bi, i: (bi, i, 0)),
            pl.BlockSpec((1, 8, d), lambda bi, i: (bi, 0, 0)),
            pl.BlockSpec((8, d), lambda bi, i: (0, 0)),
            pl.BlockSpec((1, d), lambda bi, i: (0, 0)),
            tbl, tbl, tbl,
            pl.BlockSpec((1, d, W_ALL_COLS), lambda bi, i: (layer, 0, 0), pipeline_mode=pl.Buffered(1)),
        ],
        out_specs=[
            pl.BlockSpec((1, tm, A16_W), lambda bi, i: (bi, i, 0)),
            pl.BlockSpec((1, tm, A32_W), lambda bi, i: (bi, i, 0)),
        ],
        out_shape=[
            jax.ShapeDtypeStruct((b, t, A16_W), BF16),
            jax.ShapeDtypeStruct((b, t, A32_W), F32),
        ],
        compiler_params=_cparams(("parallel", "parallel")),
        name="norm_inproj",
    )(x, modl, modc, ng, cos, sa, sb, w_all)


def _attn_kernel(lam_ref, g_ref, q_ref, k_ref, v_ref, o_ref, *, ctx_len, segments, lam_init):
    qi = pl.program_id(2)
    tq = Q_TILE
    hb = DIFF_V_DIM
    lv = lam_ref[...]
    lam = (jnp.exp(jnp.sum(lv[0:1] * lv[1:2], axis=1, keepdims=True))
           - jnp.exp(jnp.sum(lv[2:3] * lv[3:4], axis=1, keepdims=True)) + lam_init)

    def run(segs):
        qss = []
        for g in range(ATTN_HEADS):
            q = q_ref[0, :, g * hb:(g + 1) * hb]
            lane = lax.broadcasted_iota(jnp.int32, q.shape, 1)
            zero = jnp.zeros_like(q)
            qss.append(jnp.concatenate([jnp.where(lane < DIFF_HEAD_DIM, q, zero),
                                        jnp.where(lane >= DIFF_HEAD_DIM, q, zero)], axis=0))
        carries = [None] * ATTN_HEADS
        for r0, rows in segs:
            for g in range(ATTN_HEADS):
                k_t = k_ref[0, r0:r0 + rows, g * hb:(g + 1) * hb]
                v_t = v_ref[0, r0:r0 + rows, g * hb:(g + 1) * hb]
                s = lax.dot_general(qss[g], k_t, (((1,), (1,)), ((), ())), preferred_element_type=F32)
                smax = jnp.max(s, axis=-1, keepdims=True)
                if carries[g] is None:
                    p = jnp.exp2(s - smax)
                    carries[g] = (smax, jnp.sum(p, axis=-1, keepdims=True),
                                  jnp.dot(p.astype(BF16), v_t, preferred_element_type=F32))
                else:
                    m, l, acc = carries[g]
                    m_new = jnp.maximum(m, smax)
                    alpha = jnp.exp2(m - m_new)
                    p = jnp.exp2(s - m_new)
                    carries[g] = (m_new, alpha * l + jnp.sum(p, axis=-1, keepdims=True),
                                  alpha * acc + jnp.dot(p.astype(BF16), v_t, preferred_element_type=F32))
        for g in range(ATTN_HEADS):
            _, l, acc = carries[g]
            o = acc[:tq] / l[:tq] - lam * (acc[tq:] / l[tq:])
            o = o * lax.rsqrt(jnp.mean(o * o, axis=-1, keepdims=True) + RMS_EPS) * g_ref[...]
            o_ref[0, :, g * hb:(g + 1) * hb] = (o * (1.0 - lam_init)).astype(BF16)

    @pl.when(qi * tq >= ctx_len)
    def _():
        run(segments)

    @pl.when(qi * tq < ctx_len)
    def _():
        run([(0, ctx_len)])


def _attn(a16, lam_p, subln_g, *, ctx_len, seq_len, lam_init):
    b, t, _ = a16.shape
    assert ctx_len % Q_TILE == 0 and t % 256 == 0
    first = (t // 256 + KV_SEGMENTS - 1) // KV_SEGMENTS * 256
    bounds = [min(first * n, t) for n in range(KV_SEGMENTS + 1)]
    segments = [(lo, hi - lo) for lo, hi in zip(bounds[:-1], bounds[1:]) if hi > lo]
    hb = DIFF_V_DIM * ATTN_HEADS
    return pl.pallas_call(
        functools.partial(_attn_kernel, ctx_len=ctx_len, segments=segments, lam_init=lam_init),
        grid=(b, DIFF_HEADS // ATTN_HEADS, t // Q_TILE),
        in_specs=[
            pl.BlockSpec((4, DIFF_HEAD_DIM), lambda bi, h, qi: (0, 0)),
            pl.BlockSpec((1, DIFF_V_DIM), lambda bi, h, qi: (0, 0)),
            pl.BlockSpec((1, Q_TILE, hb), lambda bi, h, qi: (bi, qi, A16_Q // hb + h)),
            pl.BlockSpec((1, t, hb), lambda bi, h, qi: (bi, 0, A16_K // hb + h)),
            pl.BlockSpec((1, t, hb), lambda bi, h, qi: (bi, 0, A16_V // hb + h)),
        ],
        out_specs=pl.BlockSpec((1, Q_TILE, hb), lambda bi, h, qi: (bi, qi, h)),
        out_shape=jax.ShapeDtypeStruct((b, t, DIFF_HEADS * DIFF_V_DIM), BF16),
        compiler_params=_cparams(("parallel", "parallel", "parallel")),
        name="diff_attn",
    )(lam_p, subln_g, a16, a16, a16)


def _split3(x):
    hi = x.astype(BF16)
    r1 = x - hi.astype(F32)
    mid = r1.astype(BF16)
    lo = (r1 - mid.astype(F32)).astype(BF16)
    return jnp.concatenate([hi, mid, lo], axis=1)


def _sum3(y, w):
    return y[:, 0:w] + y[:, w:2 * w] + y[:, 2 * w:3 * w]


def _gla_kernel(q_ref, k_ref, v_ref, a_ref, r_ref, wa_ref, ba_ref, ng_ref, o_ref,
                qd_s, ki_s, ke_s, dec_s, oacc_s, st_s, *, t, ctx_len):
    ck = GLA_CHUNK
    nc = t // ck
    ncc = ctx_len // ck
    pr = 256
    w = 2 * GLA_DK

    ri = lax.broadcasted_iota(jnp.int32, (pr, pr), 0)
    ci = lax.broadcasted_iota(jnp.int32, (pr, pr), 1)
    same = (ri // ck) == (ci // ck)
    m_tot = jnp.where(same, 1.0, 0.0).astype(BF16)
    m_cum = (jnp.where(same & (ci <= ri), 1.0, 0.0).astype(BF16),
             jnp.where(same & (ci >= ri), 1.0, 0.0).astype(BF16))
    for r0 in range(0, t, pr):
        a = a_ref[0, r0:r0 + pr, :].astype(BF16)
        logit = jnp.dot(a, wa_ref[0], preferred_element_type=F32) + ba_ref[0]
        g = (jnp.minimum(logit, 0.0) - jnp.log1p(jnp.exp(-jnp.abs(logit)))) * (1.0 / GLA_TAU)
        q = q_ref[0, r0:r0 + pr, :]
        k = k_ref[0, r0:r0 + pr, :]
        for d in range(2):
            g3 = _split3(g[:, d * w:(d + 1) * w])
            g_cum = _sum3(jnp.dot(m_cum[d], g3, preferred_element_type=F32), w)
            g_tot = _sum3(jnp.dot(m_tot, g3, preferred_element_type=F32), w)
            qd_s[d, r0:r0 + pr, :] = (q * jnp.exp(g_cum)).astype(BF16)
            ki_s[d, r0:r0 + pr, :] = (k * jnp.exp(-g_cum)).astype(BF16)
            ke_s[d, r0:r0 + pr, :] = (k * jnp.exp(g_tot - g_cum)).astype(BF16)
            dec_s[d, r0:r0 + pr, :] = jnp.exp(g_tot)
        oacc_s[r0:r0 + pr, :] = jnp.zeros((pr, 2 * GLA_DV), F32)
    st_s[...] = jnp.zeros_like(st_s)

    ri = lax.broadcasted_iota(jnp.int32, (2 * ck, ck), 0)
    ci = lax.broadcasted_iota(jnp.int32, (2 * ck, ck), 1)
    rr = jnp.where(ri >= ck, ri - ck, ri)
    tri = (ci <= rr, ci >= rr)
    lane = lax.broadcasted_iota(jnp.int32, (ck, w), 1)
    br = lax.broadcasted_iota(jnp.int32, (2 * GLA_DV, w), 0)
    bc = lax.broadcasted_iota(jnp.int32, (2 * GLA_DV, w), 1)
    blk = (br < GLA_DV) == (bc < GLA_DK)
    nt = (((1,), (1,)), ((), ()))
    tn = (((0,), (0,)), ((), ()))

    def chunk(d, r0):
        qd = qd_s[d, pl.ds(r0, ck), :]
        ki = ki_s[d, pl.ds(r0, ck), :]
        ke = ke_s[d, pl.ds(r0, ck), :]
        v = v_ref[0, pl.ds(r0, ck), :]
        dec = dec_s[d, pl.ds(r0, 8), :][0:1, :]
        zq = jnp.zeros_like(qd)
        qs = jnp.concatenate([jnp.where(lane < GLA_DK, qd, zq), jnp.where(lane >= GLA_DK, qd, zq)], axis=0)
        att = lax.dot_general(qs, ki, nt, preferred_element_type=F32)
        att = jnp.where(tri[d], att, 0.0).astype(BF16)
        oi = jnp.dot(att, v, preferred_element_type=F32)
        o_intra = jnp.concatenate([oi[:ck, :GLA_DV], oi[ck:, GLA_DV:]], axis=1)
        st = st_s[d]
        o_inter = lax.dot_general(qd, st.astype(BF16), nt, preferred_element_type=F32)
        s_loc = lax.dot_general(v, ke, tn, preferred_element_type=F32)
        st_s[d] = st * dec + jnp.where(blk, s_loc, 0.0)
        oacc_s[pl.ds(r0, ck), :] = oacc_s[pl.ds(r0, ck), :] + o_intra + o_inter

    def body(n, _):
        chunk(0, pl.multiple_of(n * ck, ck))
        nb = jnp.where(n < ncc, ncc - 1 - n, nc - 1 - (n - ncc))
        chunk(1, pl.multiple_of(nb * ck, ck))
        return 0

    lax.fori_loop(0, nc, body, 0, unroll=GLA_UNROLL)

    ng = ng_ref[...]
    rt = _row_tile(t, 544)
    for r0 in range(0, t, rt):
        o = oacc_s[r0:r0 + rt, :]
        r = r_ref[0, r0:r0 + rt, :]
        outs = []
        for hh in range(2):
            oh = o[:, hh * GLA_DV:(hh + 1) * GLA_DV]
            outs.append(oh * lax.rsqrt(jnp.mean(oh * oh, axis=-1, keepdims=True) + RMS_EPS) * ng)
        o_ref[0, r0:r0 + rt, :] = (jnp.concatenate(outs, axis=1) * (r * _sigmoid(r))).astype(BF16)


def _gla(a16, a32, wa, ba, ng, *, ctx_len):
    b, t, _ = a16.shape
    assert t % 256 == 0 and ctx_len % GLA_CHUNK == 0 and (t // GLA_CHUNK) % GLA_UNROLL == 0
    return pl.pallas_call(
        functools.partial(_gla_kernel, t=t, ctx_len=ctx_len),
        grid=(b, 2),
        in_specs=[
            pl.BlockSpec((1, t, 128), lambda bi, p: (bi, 0, A32_GLQ // 128 + p)),
            pl.BlockSpec((1, t, 128), lambda bi, p: (bi, 0, A32_GLK // 128 + p)),
            pl.BlockSpec((1, t, 256), lambda bi, p: (bi, 0, A16_GV // 256 + p)),
            pl.BlockSpec((1, t, 128), lambda bi, p: (bi, 0, A32_GLA // 128)),
            pl.BlockSpec((1, t, 256), lambda bi, p: (bi, 0, A32_GLR // 256 + p)),
            pl.BlockSpec((1, 128, 256), lambda bi, p: (p, 0, 0)),
            pl.BlockSpec((1, 1, 256), lambda bi, p: (p, 0, 0)),
            pl.BlockSpec((1, GLA_DV), lambda bi, p: (0, 0)),
        ],
        out_specs=pl.BlockSpec((1, t, 256), lambda bi, p: (bi, 0, p)),
        out_shape=jax.ShapeDtypeStruct((b, t, GLA_HEADS * GLA_DV), BF16),
        scratch_shapes=[
            pltpu.VMEM((2, t, 128), BF16), pltpu.VMEM((2, t, 128), BF16), pltpu.VMEM((2, t, 128), BF16),
            pltpu.VMEM((2, t, 128), F32),
            pltpu.VMEM((t, 256), F32),
            pltpu.VMEM((2, 256, 128), F32),
        ],
        compiler_params=_cparams(("parallel", "parallel")),
        name="gla_bidir",
    )(a32, a32, a16, a32, a32, wa, ba, ng)


def _merge_kernel(x_ref, od_ref, og_ref, gates_ref, cvb_ref, cvc_ref, cvx_ref,
                  cvc_p_ref, cvx_p_ref, cvc_n_ref, cvx_n_ref, cw_ref,
                  wdo_ref, wco_ref, wgo_ref, wo_ref, modl_ref, modc_ref, n2g_ref,
                  rw_ref, rb_ref, xo_ref, h2_ref, rt_ref, *, tm, ctx_len, t):
    i = pl.program_id(1)
    pos = i * tm + lax.broadcasted_iota(jnp.int32, (tm, 1), 0)
    is_ctx = pos < ctx_len
    row = lax.broadcasted_iota(jnp.int32, (tm, 1), 0)

    z = cvc_ref[0] * cvx_ref[0]
    z_before = cvc_p_ref[0, 7:8, :] * cvx_p_ref[0, 7:8, :]
    z_after = cvc_n_ref[0, 0:1, :] * cvx_n_ref[0, 0:1, :]
    z_prev = jnp.where(row == 0, z_before, pltpu.roll(z, 1, 0))
    z_prev = jnp.where((pos == 0) | (pos == ctx_len), 0.0, z_prev)
    z_next = jnp.where(row == tm - 1, z_after, pltpu.roll(z, tm - 1, 0))
    z_next = jnp.where((pos == ctx_len - 1) | (pos == t - 1), 0.0, z_next)
    conv = z_prev * cw_ref[0:1, :] + z * cw_ref[1:2, :] + z_next * cw_ref[2:3, :]
    zc = (cvb_ref[0] * conv).astype(BF16)

    y_diff = jnp.dot(od_ref[0], wdo_ref[0], preferred_element_type=F32)
    y_conv = jnp.dot(zc, wco_ref[0], preferred_element_type=F32)
    y_gla = jnp.dot(og_ref[0], wgo_ref[0], preferred_element_type=F32)
    d = D_MODEL
    mix = (_sigmoid(gates_ref[0, :, 0:d]) * y_diff
           + _sigmoid(gates_ref[0, :, d:2 * d]) * y_conv
           + _sigmoid(gates_ref[0, :, 2 * d:3 * d]) * y_gla)
    m = jnp.dot(mix.astype(BF16), wo_ref[0], preferred_element_type=F32)

    g1 = jnp.where(is_ctx, modc_ref[2:3, :], modl_ref[0, 2:3, :])
    x_new = x_ref[0] + g1 * m
    xo_ref[0] = x_new
    sh2 = jnp.where(is_ctx, modc_ref[3:4, :], modl_ref[0, 3:4, :])
    sc2 = jnp.where(is_ctx, modc_ref[4:5, :], modl_ref[0, 4:5, :])
    h2 = _norm_mod(x_new, n2g_ref[...], sh2, sc2)
    for cc in range(d // 128):
        h2_ref[0, pl.ds(cc, tm, stride=8), :] = h2[:, cc * 128:(cc + 1) * 128]

    logits = jnp.dot(h2.astype(BF16), rw_ref[...], preferred_element_type=F32) + rb_ref[...]
    lane = lax.broadcasted_iota(jnp.int32, logits.shape, 1)
    vals = logits
    tops = []
    idx_out = jnp.zeros(logits.shape, jnp.int32)
    for kk in range(TOP_K):
        mx = jnp.max(vals, axis=-1, keepdims=True)
        ix = jnp.min(jnp.where(vals == mx, lane, 128), axis=-1, keepdims=True)
        tops.append(mx)
        idx_out = jnp.where(lane == kk, ix, idx_out)
        vals = jnp.where(lane == ix, -jnp.inf, vals)
    es = [jnp.exp(v - tops[0]) for v in tops]
    den = es[0] + es[1] + es[2] + es[3]
    packed = idx_out.astype(F32)
    for kk in range(TOP_K):
        gk = es[kk] / den
        hi = gk.astype(BF16).astype(F32)
        mid = (gk - hi).astype(BF16).astype(F32)
        lo = gk - hi - mid
        for piece, val in enumerate((hi, mid, lo)):
            packed = jnp.where(lane == TOP_K * (piece + 1) + kk, val, packed)
    ri = lax.broadcasted_iota(jnp.int32, (tm, tm), 0)
    ci = lax.broadcasted_iota(jnp.int32, (tm, tm), 1)
    eye = jnp.where(ri == ci, 1.0, 0.0).astype(BF16)
    rows = lax.dot_general(packed.astype(BF16), eye, (((0,), (0,)), ((), ())), preferred_element_type=F32)
    rt_ref[0, 0] = rows[0:4 * TOP_K, :]


def _merge(x, od, og, a32, cw, wdo, wco, wgo, wo, modl, modc, n2g, rw, rb, *, layer, ctx_len):
    b, t, d = x.shape
    tm = _row_tile(t, 544)
    nt8 = t // 8
    row = lambda bi, i: (bi, i, 0)
    const2 = lambda bi, i: (0, 0)
    wl = lambda bi, i: (layer, 0, 0)
    return pl.pallas_call(
        functools.partial(_merge_kernel, tm=tm, ctx_len=ctx_len, t=t),
        grid=(b, t // tm),
        in_specs=[
            pl.BlockSpec((1, tm, d), row),
            pl.BlockSpec((1, tm, d), row),
            pl.BlockSpec((1, tm, GLA_HEADS * GLA_DV), row),
            pl.BlockSpec((1, tm, 3 * d), lambda bi, i: (bi, i, 0)),
            pl.BlockSpec((1, tm, CONV_W), lambda bi, i: (bi, i, A32_CVB // CONV_W)),
            pl.BlockSpec((1, tm, CONV_W), lambda bi, i: (bi, i, A32_CVC // CONV_W)),
            pl.BlockSpec((1, tm, CONV_W), lambda bi, i: (bi, i, A32_CVX // CONV_W)),
            pl.BlockSpec((1, 8, CONV_W), lambda bi, i: (bi, jnp.maximum(i * (tm // 8) - 1, 0),
                                                   A32_CVC // CONV_W)),
            pl.BlockSpec((1, 8, CONV_W), lambda bi, i: (bi, jnp.maximum(i * (tm // 8) - 1, 0),
                                                   A32_CVX // CONV_W)),
            pl.BlockSpec((1, 8, CONV_W), lambda bi, i: (bi, jnp.minimum((i + 1) * (tm // 8), nt8 - 1),
                                                   A32_CVC // CONV_W)),
            pl.BlockSpec((1, 8, CONV_W), lambda bi, i: (bi, jnp.minimum((i + 1) * (tm // 8), nt8 - 1),
                                                   A32_CVX // CONV_W)),
            pl.BlockSpec((8, CONV_W), const2),
            pl.BlockSpec((1, d, d), wl),
            pl.BlockSpec((1, CONV_W, d), wl),
            pl.BlockSpec((1, GLA_HEADS * GLA_DV, d), wl),
            pl.BlockSpec((1, d, d), wl),
            pl.BlockSpec((1, 8, d), lambda bi, i: (bi, 0, 0)),
            pl.BlockSpec((8, d), const2),
            pl.BlockSpec((1, d), const2),
            pl.BlockSpec((d, 128), const2),
            pl.BlockSpec((1, 128), const2),
        ],
        out_specs=[
            pl.BlockSpec((1, tm, d), row),
            pl.BlockSpec((1, tm * (d // 128), 128), row),
            pl.BlockSpec((1, 1, 4 * TOP_K, tm), lambda bi, i: (bi, i, 0, 0)),
        ],
        out_shape=[
            jax.ShapeDtypeStruct((b, t, d), F32),
            jax.ShapeDtypeStruct((b, t * (d // 128), 128), F32),
            jax.ShapeDtypeStruct((b, t // tm, 4 * TOP_K, tm), F32),
        ],
        compiler_params=_cparams(("parallel", "parallel")),
        name="merge_router",
    )(x, od, og, a32, a32, a32, a32, a32, a32, a32, a32, cw, wdo, wco, wgo, wo,
      modl, modc, n2g, rw, rb)


def _moe_kernel(be_ref, nv_ref, g_ref, w1g_ref, w1l_ref, b1g_ref, b1l_ref, w2_ref, b2_ref,
                src_hbm, dst_hbm, h2_hbm, out_hbm, src_s, dst_s, xbuf, ybuf, sem_p, sem_g, sem_s,
                *, nb, n_slots):
    i = pl.program_id(0)
    nv = nv_ref[0]
    bm = MOE_BLOCK
    cur = lax.rem(i, 2)
    nxt = 1 - cur

    def idx_copies(blk_src, blk_dst, slot):
        return (pltpu.make_async_copy(src_hbm.at[blk_src, 0], src_s.at[pl.ds(slot * bm, bm)], sem_p),
                pltpu.make_async_copy(dst_hbm.at[blk_dst, 0], dst_s.at[pl.ds(slot * bm, bm)], sem_p))

    def start_gather(slot, buf):
        for j in range(bm):
            r0 = pl.multiple_of(src_s[slot * bm + j], TOK_ROWS)
            pltpu.make_async_copy(h2_hbm.at[pl.ds(r0, TOK_ROWS)],
                                  xbuf.at[buf, pl.ds(j * TOK_ROWS, TOK_ROWS)],
                                  sem_g.at[buf]).start(priority=j % 2)

    def wait_gather(buf):
        pltpu.make_async_copy(h2_hbm.at[pl.ds(0, bm * TOK_ROWS)], xbuf.at[buf], sem_g.at[buf]).wait()

    def wait_scatter():
        pltpu.make_async_copy(ybuf.at[0], out_hbm.at[pl.ds(0, bm * TOK_ROWS)], sem_s).wait()

    @pl.when(i <= nv)
    def _():
        @pl.when(i == 0)
        def _():
            ybuf[...] = jnp.zeros_like(ybuf)
            for blk_src, blk_dst, slot in ((0, 0, 0), (1, 1, 1), (0, nb, 3)):
                for cp in idx_copies(blk_src, blk_dst, slot):
                    cp.start()
                    cp.wait()
            start_gather(0, 0)

        @pl.when(i >= 1)
        def _():
            for cp in idx_copies(0, 0, 0):
                cp.wait()
            wait_scatter()

        wait_gather(cur)

        @pl.when(i < nv)
        def _():
            blk = jnp.minimum(i + 2, nv - 1)
            for cp in idx_copies(blk, blk, lax.rem(i + 2, 4)):
                cp.start()

        start_gather(lax.rem(i + 1, 4), nxt)
        pslot = lax.rem(i + 3, 4)
        for j in range(bm):
            r0 = pl.multiple_of(dst_s[pslot * bm + j], TOK_ROWS)
            pltpu.make_async_copy(ybuf.at[nxt, pl.ds(j * TOK_ROWS, TOK_ROWS)],
                                  out_hbm.at[pl.ds(r0, TOK_ROWS)], sem_s).start(priority=j % 2)
        x = jnp.concatenate([xbuf[cur, pl.ds(cc, bm, stride=TOK_ROWS), :] for cc in range(TOK_ROWS)],
                            axis=1).astype(BF16)
        hg = jnp.dot(x, w1g_ref[0, 0, 0], preferred_element_type=F32) + b1g_ref[0]
        hl = jnp.dot(x, w1l_ref[0, 0, 0], preferred_element_type=F32) + b1l_ref[0]
        hg = jnp.minimum(hg, SWIGLU_LIMIT)
        hl = jnp.clip(hl, -SWIGLU_LIMIT, SWIGLU_LIMIT)
        act = hg * _sigmoid(SWIGLU_ALPHA * hg) * (hl + 1.0)
        y = jnp.dot(act.astype(BF16), w2_ref[0, 0], preferred_element_type=F32) + b2_ref[0]
        gi = lax.broadcasted_iota(jnp.int32, (bm, bm), 0)
        gj = lax.broadcasted_iota(jnp.int32, (bm, bm), 1)
        y = y * jnp.sum(jnp.where(gi == gj, g_ref[0], 0.0), axis=1, keepdims=True)
        for cc in range(TOK_ROWS):
            ybuf[cur, pl.ds(cc, bm, stride=TOK_ROWS), :] = y[:, cc * 128:(cc + 1) * 128]

        @pl.when(i == nv)
        def _():
            wait_scatter()
            wait_gather(nxt)
            ybuf[...] = jnp.zeros_like(ybuf)
            fills = [pltpu.make_async_copy(
                ybuf.at[sl], out_hbm.at[pl.ds((n_slots + sl * bm) * TOK_ROWS, bm * TOK_ROWS)], sem_s)
                for sl in range(2)]
            for cp in fills:
                cp.start()
            for cp in fills:
                cp.wait()


def _moe(block_exp, n_valid, row_src, row_dst, row_gate, h2_flat, w1, b1g, b1l, w2, b2, *, layer, n_slots):
    d = D_MODEL
    nb = row_src.shape[0]
    bm = MOE_BLOCK
    de = w1.shape[-1]
    blk = lambda i, nv: jnp.minimum(i, nv[0] - 1)
    wmap = lambda i, be, nv: (be[blk(i, nv)], 0, 0)
    return pl.pallas_call(
        functools.partial(_moe_kernel, nb=nb, n_slots=n_slots),
        grid_spec=pltpu.PrefetchScalarGridSpec(
            num_scalar_prefetch=2,
            grid=(nb + 1,),
            in_specs=[
                pl.BlockSpec((1, 1, bm), lambda i, be, nv: (blk(i, nv), 0, 0)),
                pl.BlockSpec((1, 1, 1, d, de), lambda i, be, nv: (layer, be[blk(i, nv)], 0, 0, 0)),
                pl.BlockSpec((1, 1, 1, d, de), lambda i, be, nv: (layer, be[blk(i, nv)], 1, 0, 0)),
                pl.BlockSpec((1, 1, de), wmap),
                pl.BlockSpec((1, 1, de), wmap),
                pl.BlockSpec((1, 1, de, d), lambda i, be, nv: (layer, be[blk(i, nv)], 0, 0)),
                pl.BlockSpec((1, 1, d), wmap),
                pl.BlockSpec(memory_space=pl.ANY),
                pl.BlockSpec(memory_space=pl.ANY),
                pl.BlockSpec(memory_space=pl.ANY),
            ],
            out_specs=pl.BlockSpec(memory_space=pl.ANY),
            scratch_shapes=[
                pltpu.SMEM((4 * bm,), jnp.int32),
                pltpu.SMEM((4 * bm,), jnp.int32),
                pltpu.VMEM((2, bm * TOK_ROWS, 128), F32),
                pltpu.VMEM((2, bm * TOK_ROWS, 128), F32),
                pltpu.SemaphoreType.DMA,
                pltpu.SemaphoreType.DMA((2,)),
                pltpu.SemaphoreType.DMA,
            ],
        ),
        out_shape=jax.ShapeDtypeStruct(((n_slots + 2 * bm) * TOK_ROWS, 128), F32),
        compiler_params=_cparams(("arbitrary",)),
        name="moe_experts",
    )(block_exp, n_valid, row_gate, w1, w1, b1g, b1l, w2, b2, row_src, row_dst, h2_flat)


def _combine_kernel(x_ref, y4_ref, modl_ref, modc_ref, fg_ref, o_ref, sum_s, *, tm, ctx_len, final, skip):
    i = pl.program_id(1) + skip
    tiles = (y4_ref[:, 0] + y4_ref[:, 1]) + (y4_ref[:, 2] + y4_ref[:, 3])
    sum_s[...] = tiles.reshape(tm * TOK_ROWS, 128)
    f = jnp.concatenate([sum_s[pl.ds(cc, tm, stride=TOK_ROWS), :] for cc in range(TOK_ROWS)], axis=1)
    pos = i * tm + lax.broadcasted_iota(jnp.int32, (tm, 1), 0)
    g2 = jnp.where(pos < ctx_len, modc_ref[5:6, :], modl_ref[0, 5:6, :])
    x_new = x_ref[0] + g2 * f
    if final:
        x_new = x_new * lax.rsqrt(jnp.mean(x_new * x_new, axis=-1, keepdims=True) + RMS_EPS) * fg_ref[...]
    o_ref[0] = x_new


def _combine(x, y4, modl, modc, fg, *, ctx_len, final):
    b, t, d = x.shape
    if final:
        tm = Q_TILE
        assert ctx_len % tm == 0
        skip = ctx_len // tm
    else:
        tm = _row_tile(t, 272)
        skip = 0
    npb = t // tm
    return pl.pallas_call(
        functools.partial(_combine_kernel, tm=tm, ctx_len=ctx_len, final=final, skip=skip),
        grid=(b, npb - skip),
        in_specs=[
            pl.BlockSpec((1, tm, d), lambda bi, i: (bi, i + skip, 0)),
            pl.BlockSpec((tm, TOP_K, TOK_ROWS, 128), lambda bi, i: (bi * (npb - skip) + i, 0, 0, 0)),
            pl.BlockSpec((1, 8, d), lambda bi, i: (bi, 0, 0)),
            pl.BlockSpec((8, d), lambda bi, i: (0, 0)),
            pl.BlockSpec((1, d), lambda bi, i: (0, 0)),
        ],
        out_specs=pl.BlockSpec((1, tm, d), lambda bi, i: (bi, i, 0)),
        out_shape=jax.ShapeDtypeStruct((b, t - skip * tm, d), F32),
        scratch_shapes=[pltpu.VMEM((tm * TOK_ROWS, 128), F32)],
        compiler_params=_cparams(("parallel", "parallel")),
        name="moe_combine",
    )(x, y4.reshape(-1, TOP_K, TOK_ROWS, 128), modl, modc, fg)


def _routing(e_flat, gate_flat, *, tm, seq, ctx_len, latent_only):
    npair = e_flat.shape[0]
    t = seq + ctx_len
    bm = MOE_BLOCK
    nb = npair // bm + N_EXPERTS
    nr = nb * bm
    npad = nr - npair
    experts = jnp.arange(N_EXPERTS, dtype=jnp.int32)
    ids = jnp.arange(npair, dtype=jnp.int32)
    tok_of = lambda i: (i // (TOP_K * tm)) * tm + i % tm
    if latent_only:
        e_flat = jnp.where(tok_of(ids) % t >= ctx_len, e_flat, N_EXPERTS)
    counts = jnp.sum((e_flat[:, None] == experts[None, :]).astype(jnp.int32), axis=0)
    padded = (counts + bm - 1) // bm * bm
    pad_end = jnp.cumsum(padded)
    fill_end = jnp.cumsum(padded - counts)
    dummy = jnp.arange(npad, dtype=jnp.int32)
    e_dummy = jnp.sum((dummy[:, None] >= fill_end[None, :]).astype(jnp.int32), axis=1)
    kb = nr
    keys = jnp.concatenate([e_flat * kb + ids, e_dummy * kb + npair + dummy])
    gates = jnp.concatenate([gate_flat, jnp.zeros((npad,), F32)])
    skey, sgate = lax.sort((keys, gates), num_keys=1)
    e_row = skey // kb
    ident = skey - e_row * kb
    valid = (ident < npair) & (e_row < N_EXPERTS)
    pair = jnp.where(valid, ident, 0)
    tok = tok_of(pair)
    kk = (pair // tm) % TOP_K
    row_gate = jnp.where(valid, sgate, 0.0).reshape(nb, 1, bm)
    row_src = (tok * TOK_ROWS).reshape(nb, 1, bm)
    if latent_only:
        slot = (tok // t) * seq + tok % t - ctx_len
        n_slots = (npair // TOP_K // t) * seq * TOP_K
    else:
        slot = tok
        n_slots = npair
    j = jnp.arange(nr, dtype=jnp.int32)
    dump = n_slots + ((j // bm) % 2) * bm + j % bm
    row_dst = jnp.where(valid, slot * TOP_K + kk, dump).reshape(nb, bm)
    row_dst = jnp.concatenate([row_dst, (n_slots + bm + jnp.arange(bm, dtype=jnp.int32))[None, :]],
                              axis=0)[:, None, :] * TOK_ROWS
    blocks = jnp.arange(nb, dtype=jnp.int32) * bm
    block_exp = jnp.minimum(jnp.sum((blocks[:, None] >= pad_end[None, :]).astype(jnp.int32), axis=1),
                            N_EXPERTS - 1)
    n_valid = (pad_end[-1] // bm).astype(jnp.int32).reshape(1)
    return row_src, row_dst, row_gate, block_exp, n_valid, n_slots


def _rope_tables(ctx_len, seq_len):
    inv_freq = ROPE_BASE ** (-jnp.arange(0, ROT_AXIS_DIM, 2, dtype=F32) / ROT_AXIS_DIM)
    p = jnp.arange(seq_len, dtype=jnp.int32)
    row = (p // GRID_W).astype(F32)
    col = (p % GRID_W).astype(F32)
    lane = np.arange(128)
    dd = lane % DIFF_HEAD_DIM
    fidx = jnp.asarray((dd % ROT_AXIS_DIM) % (ROT_AXIS_DIM // 2))
    use_col = jnp.asarray(dd >= ROT_AXIS_DIM)
    second_half = jnp.asarray((dd % ROT_AXIS_DIM) >= ROT_AXIS_DIM // 2)
    posm = jnp.where(use_col[None, :], col[:, None], row[:, None])
    ang = posm * inv_freq[fidx][None, :]
    cos = jnp.cos(ang)
    sin = jnp.sin(ang)
    sa = jnp.where(second_half[None, :], sin, 0.0)
    sb = jnp.where(second_half[None, :], 0.0, -sin)
    pad = lambda a, v: jnp.concatenate([jnp.full((ctx_len, 128), v, F32), a], axis=0)
    return pad(cos, 1.0), pad(sa, 0.0), pad(sb, 0.0)


def kernel(x, c, ctx, c_ctx, ada_w, ada_b, norm1_g, norm2_g, w_in, diff_lambda, diff_subln_g,
           diff_w_out, conv_w, conv_w_out, gla_w_a2, gla_b_a, gla_norm_g, gla_w_out, w_o,
           router_w, router_b, moe_w1, moe_b1, moe_w2, moe_b2, final_norm_g):
    b, s, d = x.shape
    cl = ctx.shape[1]
    t = cl + s
    nl = w_in.shape[0]
    assert d == D_MODEL and b <= 15

    xs = jnp.concatenate([ctx, x], axis=1)
    cc = jnp.concatenate([c, c_ctx[None, :], jnp.zeros((15 - b, d), F32)], axis=0)
    mod = _ada(cc, ada_w, ada_b).reshape(nl, 16, N_ADA, d)
    cos, sa, sb = _rope_tables(cl, s)

    col_scale = np.ones((w_in.shape[-1],), np.float32)
    col_scale[0:1024] = DIFF_HEAD_DIM ** -0.5
    col_scale[4608:4864] = GLA_DK ** -0.5
    w_s = (w_in * jnp.asarray(col_scale)).astype(BF16)
    w_all = jnp.concatenate(
        [w_s[..., :W_GATES], w_s[..., W_GATES + 2 * GLA_RANK:], w_s[..., W_GATES:W_GATES + 2 * GLA_RANK],
         jnp.zeros((nl, d, W_ALL_COLS - w_in.shape[-1]), BF16)], axis=-1)

    wa = jnp.zeros((nl, 2, 128, 256), F32)
    for p in range(2):
        wa = wa.at[:, p, 0:GLA_RANK, 0:128].set(gla_w_a2[:, 0, :, p * 128:(p + 1) * 128])
        wa = wa.at[:, p, GLA_RANK:2 * GLA_RANK, 128:256].set(gla_w_a2[:, 1, :, p * 128:(p + 1) * 128])
    wa = wa.astype(BF16)
    ba = jnp.stack([jnp.concatenate([gla_b_a[:, 0, p * 128:(p + 1) * 128],
                                     gla_b_a[:, 1, p * 128:(p + 1) * 128]], axis=-1)
                    for p in range(2)], axis=1)[:, :, None, :]

    cw = jnp.concatenate([conv_w, jnp.zeros((nl, 5, CONV_W), F32)], axis=1)
    rw = jnp.concatenate([router_w, jnp.zeros((nl, d, 128 - N_EXPERTS), F32)], axis=-1).astype(BF16)
    rb = jnp.concatenate([router_b, jnp.full((nl, 128 - N_EXPERTS), -jnp.inf, F32)], axis=-1)[:, None, :]
    w1 = jnp.moveaxis(moe_w1.reshape(nl, N_EXPERTS, d, D_EXPERT, 2), -1, 2).astype(BF16)
    b1 = moe_b1.reshape(nl, N_EXPERTS, 1, D_EXPERT, 2)
    b1g = b1[..., 0]
    b1l = b1[..., 1]
    w2 = moe_w2.astype(BF16)
    b2 = moe_b2[:, :, None, :]
    wdo = diff_w_out.astype(BF16)
    wco = conv_w_out.astype(BF16)
    wgo = gla_w_out.astype(BF16)
    wo = w_o.astype(BF16)

    for layer in range(nl):
        last = layer == nl - 1
        lam_init = 0.8 - 0.6 * math.exp(-0.3 * layer)
        ml = mod[layer]
        modl = jnp.concatenate([ml[:b], jnp.zeros((b, 2, d), F32)], axis=1)
        modc = jnp.concatenate([ml[b], jnp.zeros((2, d), F32)], axis=0)

        a16, a32 = _k1(xs, modl, modc, norm1_g[layer][None, :], cos, sa, sb, w_all, layer=layer, ctx_len=cl)
        od = _attn(a16, diff_lambda[layer], diff_subln_g[layer][None, :],
                   ctx_len=cl, seq_len=s, lam_init=lam_init)
        og = _gla(a16, a32, wa[layer], ba[layer], gla_norm_g[layer][None, :], ctx_len=cl)
        xs, h2, rt = _merge(xs, od, og, a32, cw[layer], wdo, wco, wgo, wo, modl, modc,
                            norm2_g[layer][None, :], rw[layer], rb[layer], layer=layer, ctx_len=cl)

        r = b * t
        tm_r = rt.shape[-1]
        rt = rt.reshape(-1, 4, TOP_K, tm_r)
        row_src, row_dst, row_gate, block_exp, n_valid, n_slots = _routing(
            rt[:, 0].astype(jnp.int32).reshape(-1), (rt[:, 1] + rt[:, 2] + rt[:, 3]).reshape(-1),
            tm=tm_r, seq=s, ctx_len=cl, latent_only=last)
        y4 = _moe(block_exp, n_valid, row_src, row_dst, row_gate, h2.reshape(r * TOK_ROWS, 128), w1,
                  b1g[layer], b1l[layer], w2, b2[layer], layer=layer, n_slots=n_slots)
        xs = _combine(xs, y4, modl, modc, final_norm_g[None, :], ctx_len=cl, final=last)

    return xs
```

```python
import functools
import math

import numpy as np
import jax
import jax.numpy as jnp
from jax import lax
from jax.experimental import pallas as pl
from jax.experimental.pallas import tpu as pltpu

F32 = jnp.float32
BF16 = jnp.bfloat16

D_MODEL = 1024
GRID_W = 64
RMS_EPS = 1e-6
N_ADA = 6
DIFF_HEADS = 8
DIFF_HEAD_DIM = 64
DIFF_V_DIM = 128
ROPE_BASE = 10000.0
ROT_AXIS_DIM = 32
CONV_W = 512
GLA_HEADS = 4
GLA_DK = 64
GLA_DV = 128
GLA_RANK = 16
GLA_TAU = 16.0
GLA_CHUNK = 64
N_EXPERTS = 32
TOP_K = 4
D_EXPERT = 1024
SWIGLU_LIMIT = 7.0
SWIGLU_ALPHA = 1.702
LOG2_E = 1.4426950408889634

A16_Q, A16_K, A16_V, A16_GV, A16_W = 0, 1024, 2048, 3072, 3584
A32_GATES, A32_CVB, A32_CVC, A32_CVX = 0, 3072, 3584, 4096
A32_GLQ, A32_GLK, A32_GLR, A32_GLA, A32_W = 4608, 4864, 5120, 5632, 5760
W_ALL_COLS = A16_W + A32_W
W_GATES, W_GLA = 6144, 9216


def _k1_chunks():
    out = []
    def add(w0, width, to16, o0, rope=False, ch=512):
        for c in range(0, width, ch):
            out.append((w0 + c, min(ch, width - c), to16, o0 + c, rope))
    add(0, 1024, True, A16_Q, rope=True)
    add(1024, 1024, True, A16_K, rope=True)
    add(2048, 1024, True, A16_V)
    add(3072, 512, False, A32_CVB)
    add(3584, 512, False, A32_CVC)
    add(4096, 512, False, A32_CVX)
    add(4608, 256, False, A32_GLQ)
    add(4864, 256, False, A32_GLK)
    add(5120, 512, True, A16_GV)
    add(5632, 512, False, A32_GLR)
    add(W_GATES, 3072, False, A32_GATES)
    add(W_GLA, 128, False, A32_GLA)
    return tuple(out)


K1_CHUNKS = _k1_chunks()

Q_TILE = 256
KV_SEGMENTS = 2
ATTN_HEADS = 4
MOE_BLOCK = 256
GLA_UNROLL = 2
TOK_ROWS = 8
V7X_VMEM_BYTES = 64 * 1024 * 1024
VMEM_LIMIT = V7X_VMEM_BYTES - 8 * 1024 * 1024


def _cparams(sem):
    return pltpu.CompilerParams(dimension_semantics=sem, vmem_limit_bytes=VMEM_LIMIT)


def _row_tile(t, target):
    best = None
    for cand in range(16, target + 1, 16):
        if t % cand == 0:
            best = cand
    assert best is not None, t
    return best


def _sigmoid(v):
    return 1.0 / (1.0 + jnp.exp(-v))


def _ada_kernel(cc_ref, w_ref, b_ref, o_ref):
    a = cc_ref[...]
    a = a * _sigmoid(a)
    o_ref[0] = jnp.dot(a, w_ref[0], precision=lax.Precision.HIGHEST,
                       preferred_element_type=F32) + b_ref[0]


def _ada(cc, ada_w, ada_b):
    nl, d, n = ada_w.shape
    tn = 1024
    return pl.pallas_call(
        _ada_kernel,
        grid=(nl, n // tn),
        in_specs=[
            pl.BlockSpec((16, d), lambda l, j: (0, 0)),
            pl.BlockSpec((1, d, tn), lambda l, j: (l, 0, j)),
            pl.BlockSpec((1, 1, tn), lambda l, j: (l, 0, j)),
        ],
        out_specs=pl.BlockSpec((1, 16, tn), lambda l, j: (l, 0, j)),
        out_shape=jax.ShapeDtypeStruct((nl, 16, n), F32),
        compiler_params=_cparams(("parallel", "parallel")),
        name="ada_mod",
    )(cc, ada_w, ada_b.reshape(nl, 1, n))


def _norm_mod(x, ng, shift, scale):
    y = x * lax.rsqrt(jnp.mean(x * x, axis=-1, keepdims=True) + RMS_EPS) * ng
    return y * (1.0 + scale) + shift


def _k1_kernel(x_ref, modl_ref, modc_ref, ng_ref, cos_ref, sa_ref, sb_ref, w_ref,
               o16_ref, o32_ref, *, tm, ctx_len):
    i = pl.program_id(1)
    pos = i * tm + lax.broadcasted_iota(jnp.int32, (tm, 1), 0)
    is_ctx = pos < ctx_len
    shift = jnp.where(is_ctx, modc_ref[0:1, :], modl_ref[0, 0:1, :])
    scale = jnp.where(is_ctx, modc_ref[1:2, :], modl_ref[0, 1:2, :])
    h = _norm_mod(x_ref[0], ng_ref[...], shift, scale).astype(BF16)
    cos = cos_ref[...]
    sa = sa_ref[...]
    sb = sb_ref[...]
    for w0, width, to16, o0, rope in K1_CHUNKS:
        acc = jnp.dot(h, w_ref[0, :, w0:w0 + width], preferred_element_type=F32)
        if rope:
            parts = []
            for j in range(width // 128):
                a = acc[:, j * 128:(j + 1) * 128]
                parts.append(a * cos + pltpu.roll(a, 16, 1) * sa + pltpu.roll(a, 112, 1) * sb)
            acc = jnp.concatenate(parts, axis=1)
        if to16 and o0 < A16_K:
            acc = acc * LOG2_E
        if to16:
            o16_ref[0, :, o0:o0 + width] = acc.astype(BF16)
        else:
            o32_ref[0, :, o0:o0 + width] = acc


def _k1(x, modl, modc, ng, cos, sa, sb, w_all, *, layer, ctx_len):
    b, t, d = x.shape
    tm = _row_tile(t, 272)
    tbl = pl.BlockSpec((tm, 128), lambda bi, i: (i, 0))
    return pl.pallas_call(
        functools.partial(_k1_kernel, tm=tm, ctx_len=ctx_len),
        grid=(b, t // tm),
        in_specs=[
            pl.BlockSpec((1, tm, d), lambda bi, i: (bi, i, 0)),
            pl.BlockSpec((1, 8, d), lambda bi, i: (bi, 0, 0)),
            pl.BlockSpec((8, d), lambda bi, i: (0, 0)),
            pl.BlockSpec((1, d), lambda bi, i: (0, 0)),
            tbl, tbl, tbl,
            pl.BlockSpec((1, d, W_ALL_COLS), lambda bi, i: (layer, 0, 0), pipeline_mode=pl.Buffered(1)),
        ],
        out_specs=[
            pl.BlockSpec((1, tm, A16_W), lambda bi, i: (bi, i, 0)),
            pl.BlockSpec((1, tm, A32_W), lambda bi, i: (bi, i, 0)),
        ],
        out_shape=[
            jax.ShapeDtypeStruct((b, t, A16_W), BF16),
            jax.ShapeDtypeStruct((b, t, A32_W), F32),
        ],
        compiler_params=_cparams(("parallel", "parallel")),
        name="norm_inproj",
    )(x, modl, modc, ng, cos, sa, sb, w_all)


def _attn_kernel(lam_ref, g_ref, q_ref, k_ref, v_ref, o_ref, *, ctx_len, segments, lam_init):
    qi = pl.program_id(2)
    tq = Q_TILE
    hb = DIFF_V_DIM
    lv = lam_ref[...]
    lam = (jnp.exp(jnp.sum(lv[0:1] * lv[1:2], axis=1, keepdims=True))
           - jnp.exp(jnp.sum(lv[2:3] * lv[3:4], axis=1, keepdims=True)) + lam_init)

    def run(segs):
        qss = []
        for g in range(ATTN_HEADS):
            q = q_ref[0, :, g * hb:(g + 1) * hb]
            lane = lax.broadcasted_iota(jnp.int32, q.shape, 1)
            zero = jnp.zeros_like(q)
            qss.append(jnp.concatenate([jnp.where(lane < DIFF_HEAD_DIM, q, zero),
                                        jnp.where(lane >= DIFF_HEAD_DIM, q, zero)], axis=0))
        carries = [None] * ATTN_HEADS
        for r0, rows in segs:
            for g in range(ATTN_HEADS):
                k_t = k_ref[0, r0:r0 + rows, g * hb:(g + 1) * hb]
                v_t = v_ref[0, r0:r0 + rows, g * hb:(g + 1) * hb]
                s = lax.dot_general(qss[g], k_t, (((1,), (1,)), ((), ())), preferred_element_type=F32)
                smax = jnp.max(s, axis=-1, keepdims=True)
                if carries[g] is None:
                    p = jnp.exp2(s - smax)
                    carries[g] = (smax, jnp.sum(p, axis=-1, keepdims=True),
                                  jnp.dot(p.astype(BF16), v_t, preferred_element_type=F32))
                else:
                    m, l, acc = carries[g]
                    m_new = jnp.maximum(m, smax)
                    alpha = jnp.exp2(m - m_new)
                    p = jnp.exp2(s - m_new)
                    carries[g] = (m_new, alpha * l + jnp.sum(p, axis=-1, keepdims=True),
                                  alpha * acc + jnp.dot(p.astype(BF16), v_t, preferred_element_type=F32))
        for g in range(ATTN_HEADS):
            _, l, acc = carries[g]
            o = acc[:tq] / l[:tq] - lam * (acc[tq:] / l[tq:])
            o = o * lax.rsqrt(jnp.mean(o * o, axis=-1, keepdims=True) + RMS_EPS) * g_ref[...]
            o_ref[0, :, g * hb:(g + 1) * hb] = (o * (1.0 - lam_init)).astype(BF16)

    @pl.when(qi * tq >= ctx_len)
    def _():
        run(segments)

    @pl.when(qi * tq < ctx_len)
    def _():
        run([(0, ctx_len)])


def _attn(a16, lam_p, subln_g, *, ctx_len, seq_len, lam_init):
    b, t, _ = a16.shape
    assert ctx_len % Q_TILE == 0 and t % 256 == 0
    first = (t // 256 + KV_SEGMENTS - 1) // KV_SEGMENTS * 256
    bounds = [min(first * n, t) for n in range(KV_SEGMENTS + 1)]
    segments = [(lo, hi - lo) for lo, hi in zip(bounds[:-1], bounds[1:]) if hi > lo]
    hb = DIFF_V_DIM * ATTN_HEADS
    return pl.pallas_call(
        functools.partial(_attn_kernel, ctx_len=ctx_len, segments=segments, lam_init=lam_init),
        grid=(b, DIFF_HEADS // ATTN_HEADS, t // Q_TILE),
        in_specs=[
            pl.BlockSpec((4, DIFF_HEAD_DIM), lambda bi, h, qi: (0, 0)),
            pl.BlockSpec((1, DIFF_V_DIM), lambda bi, h, qi: (0, 0)),
            pl.BlockSpec((1, Q_TILE, hb), lambda bi, h, qi: (bi, qi, A16_Q // hb + h)),
            pl.BlockSpec((1, t, hb), lambda bi, h, qi: (bi, 0, A16_K // hb + h)),
            pl.BlockSpec((1, t, hb), lambda bi, h, qi: (bi, 0, A16_V // hb + h)),
        ],
        out_specs=pl.BlockSpec((1, Q_TILE, hb), lambda bi, h, qi: (bi, qi, h)),
        out_shape=jax.ShapeDtypeStruct((b, t, DIFF_HEADS * DIFF_V_DIM), BF16),
        compiler_params=_cparams(("parallel", "parallel", "parallel")),
        name="diff_attn",
    )(lam_p, subln_g, a16, a16, a16)


def _split3(x):
    hi = x.astype(BF16)
    r1 = x - hi.astype(F32)
    mid = r1.astype(BF16)
    lo = (r1 - mid.astype(F32)).astype(BF16)
    return jnp.concatenate([hi, mid, lo], axis=1)


def _sum3(y, w):
    return y[:, 0:w] + y[:, w:2 * w] + y[:, 2 * w:3 * w]


def _gla_kernel(q_ref, k_ref, v_ref, a_ref, r_ref, wa_ref, ba_ref, ng_ref, o_ref,
                qd_s, ki_s, ke_s, dec_s, oacc_s, st_s, *, t, ctx_len):
    ck = GLA_CHUNK
    nc = t // ck
    ncc = ctx_len // ck
    pr = 256
    w = 2 * GLA_DK

    ri = lax.broadcasted_iota(jnp.int32, (pr, pr), 0)
    ci = lax.broadcasted_iota(jnp.int32, (pr, pr), 1)
    same = (ri // ck) == (ci // ck)
    m_tot = jnp.where(same, 1.0, 0.0).astype(BF16)
    m_cum = (jnp.where(same & (ci <= ri), 1.0, 0.0).astype(BF16),
             jnp.where(same & (ci >= ri), 1.0, 0.0).astype(BF16))
    for r0 in range(0, t, pr):
        a = a_ref[0, r0:r0 + pr, :].astype(BF16)
        logit = jnp.dot(a, wa_ref[0], preferred_element_type=F32) + ba_ref[0]
        g = (jnp.minimum(logit, 0.0) - jnp.log1p(jnp.exp(-jnp.abs(logit)))) * (1.0 / GLA_TAU)
        q = q_ref[0, r0:r0 + pr, :]
        k = k_ref[0, r0:r0 + pr, :]
        for d in range(2):
            g3 = _split3(g[:, d * w:(d + 1) * w])
            g_cum = _sum3(jnp.dot(m_cum[d], g3, preferred_element_type=F32), w)
            g_tot = _sum3(jnp.dot(m_tot, g3, preferred_element_type=F32), w)
            qd_s[d, r0:r0 + pr, :] = (q * jnp.exp(g_cum)).astype(BF16)
            ki_s[d, r0:r0 + pr, :] = (k * jnp.exp(-g_cum)).astype(BF16)
            ke_s[d, r0:r0 + pr, :] = (k * jnp.exp(g_tot - g_cum)).astype(BF16)
            dec_s[d, r0:r0 + pr, :] = jnp.exp(g_tot)
        oacc_s[r0:r0 + pr, :] = jnp.zeros((pr, 2 * GLA_DV), F32)
    st_s[...] = jnp.zeros_like(st_s)

    ri = lax.broadcasted_iota(jnp.int32, (2 * ck, ck), 0)
    ci = lax.broadcasted_iota(jnp.int32, (2 * ck, ck), 1)
    rr = jnp.where(ri >= ck, ri - ck, ri)
    tri = (ci <= rr, ci >= rr)
    lane = lax.broadcasted_iota(jnp.int32, (ck, w), 1)
    br = lax.broadcasted_iota(jnp.int32, (2 * GLA_DV, w), 0)
    bc = lax.broadcasted_iota(jnp.int32, (2 * GLA_DV, w), 1)
    blk = (br < GLA_DV) == (bc < GLA_DK)
    nt = (((1,), (1,)), ((), ()))
    tn = (((0,), (0,)), ((), ()))

    def chunk(d, r0):
        qd = qd_s[d, pl.ds(r0, ck), :]
        ki = ki_s[d, pl.ds(r0, ck), :]
        ke = ke_s[d, pl.ds(r0, ck), :]
        v = v_ref[0, pl.ds(r0, ck), :]
        dec = dec_s[d, pl.ds(r0, 8), :][0:1, :]
        zq = jnp.zeros_like(qd)
        qs = jnp.concatenate([jnp.where(lane < GLA_DK, qd, zq), jnp.where(lane >= GLA_DK, qd, zq)], axis=0)
        att = lax.dot_general(qs, ki, nt, preferred_element_type=F32)
        att = jnp.where(tri[d], att, 0.0).astype(BF16)
        oi = jnp.dot(att, v, preferred_element_type=F32)
        o_intra = jnp.concatenate([oi[:ck, :GLA_DV], oi[ck:, GLA_DV:]], axis=1)
        st = st_s[d]
        o_inter = lax.dot_general(qd, st.astype(BF16), nt, preferred_element_type=F32)
        s_loc = lax.dot_general(v, ke, tn, preferred_element_type=F32)
        st_s[d] = st * dec + jnp.where(blk, s_loc, 0.0)
        oacc_s[pl.ds(r0, ck), :] = oacc_s[pl.ds(r0, ck), :] + o_intra + o_inter

    def body(n, _):
        chunk(0, pl.multiple_of(n * ck, ck))
        nb = jnp.where(n < ncc, ncc - 1 - n, nc - 1 - (n - ncc))
        chunk(1, pl.multiple_of(nb * ck, ck))
        return 0

    lax.fori_loop(0, nc, body, 0, unroll=GLA_UNROLL)

    ng = ng_ref[...]
    rt = _row_tile(t, 544)
    for r0 in range(0, t, rt):
        o = oacc_s[r0:r0 + rt, :]
        r = r_ref[0, r0:r0 + rt, :]
        outs = []
        for hh in range(2):
            oh = o[:, hh * GLA_DV:(hh + 1) * GLA_DV]
            outs.append(oh * lax.rsqrt(jnp.mean(oh * oh, axis=-1, keepdims=True) + RMS_EPS) * ng)
        o_ref[0, r0:r0 + rt, :] = (jnp.concatenate(outs, axis=1) * (r * _sigmoid(r))).astype(BF16)


def _gla(a16, a32, wa, ba, ng, *, ctx_len):
    b, t, _ = a16.shape
    assert t % 256 == 0 and ctx_len % GLA_CHUNK == 0 and (t // GLA_CHUNK) % GLA_UNROLL == 0
    return pl.pallas_call(
        functools.partial(_gla_kernel, t=t, ctx_len=ctx_len),
        grid=(b, 2),
        in_specs=[
            pl.BlockSpec((1, t, 128), lambda bi, p: (bi, 0, A32_GLQ // 128 + p)),
            pl.BlockSpec((1, t, 128), lambda bi, p: (bi, 0, A32_GLK // 128 + p)),
            pl.BlockSpec((1, t, 256), lambda bi, p: (bi, 0, A16_GV // 256 + p)),
            pl.BlockSpec((1, t, 128), lambda bi, p: (bi, 0, A32_GLA // 128)),
            pl.BlockSpec((1, t, 256), lambda bi, p: (bi, 0, A32_GLR // 256 + p)),
            pl.BlockSpec((1, 128, 256), lambda bi, p: (p, 0, 0)),
            pl.BlockSpec((1, 1, 256), lambda bi, p: (p, 0, 0)),
            pl.BlockSpec((1, GLA_DV), lambda bi, p: (0, 0)),
        ],
        out_specs=pl.BlockSpec((1, t, 256), lambda bi, p: (bi, 0, p)),
        out_shape=jax.ShapeDtypeStruct((b, t, GLA_HEADS * GLA_DV), BF16),
        scratch_shapes=[
            pltpu.VMEM((2, t, 128), BF16), pltpu.VMEM((2, t, 128), BF16), pltpu.VMEM((2, t, 128), BF16),
            pltpu.VMEM((2, t, 128), F32),
            pltpu.VMEM((t, 256), F32),
            pltpu.VMEM((2, 256, 128), F32),
        ],
        compiler_params=_cparams(("parallel", "parallel")),
        name="gla_bidir",
    )(a32, a32, a16, a32, a32, wa, ba, ng)


def _merge_kernel(x_ref, od_ref, og_ref, gates_ref, cvb_ref, cvc_ref, cvx_ref,
                  cvc_p_ref, cvx_p_ref, cvc_n_ref, cvx_n_ref, cw_ref,
                  wdo_ref, wco_ref, wgo_ref, wo_ref, modl_ref, modc_ref, n2g_ref,
                  rw_ref, rb_ref, xo_ref, h2_ref, rt_ref, *, tm, ctx_len, t):
    i = pl.program_id(1)
    pos = i * tm + lax.broadcasted_iota(jnp.int32, (tm, 1), 0)
    is_ctx = pos < ctx_len
    row = lax.broadcasted_iota(jnp.int32, (tm, 1), 0)

    z = cvc_ref[0] * cvx_ref[0]
    z_before = cvc_p_ref[0, 7:8, :] * cvx_p_ref[0, 7:8, :]
    z_after = cvc_n_ref[0, 0:1, :] * cvx_n_ref[0, 0:1, :]
    z_prev = jnp.where(row == 0, z_before, pltpu.roll(z, 1, 0))
    z_prev = jnp.where((pos == 0) | (pos == ctx_len), 0.0, z_prev)
    z_next = jnp.where(row == tm - 1, z_after, pltpu.roll(z, tm - 1, 0))
    z_next = jnp.where((pos == ctx_len - 1) | (pos == t - 1), 0.0, z_next)
    conv = z_prev * cw_ref[0:1, :] + z * cw_ref[1:2, :] + z_next * cw_ref[2:3, :]
    zc = (cvb_ref[0] * conv).astype(BF16)

    y_diff = jnp.dot(od_ref[0], wdo_ref[0], preferred_element_type=F32)
    y_conv = jnp.dot(zc, wco_ref[0], preferred_element_type=F32)
    y_gla = jnp.dot(og_ref[0], wgo_ref[0], preferred_element_type=F32)
    d = D_MODEL
    mix = (_sigmoid(gates_ref[0, :, 0:d]) * y_diff
           + _sigmoid(gates_ref[0, :, d:2 * d]) * y_conv
           + _sigmoid(gates_ref[0, :, 2 * d:3 * d]) * y_gla)
    m = jnp.dot(mix.astype(BF16), wo_ref[0], preferred_element_type=F32)

    g1 = jnp.where(is_ctx, modc_ref[2:3, :], modl_ref[0, 2:3, :])
    x_new = x_ref[0] + g1 * m
    xo_ref[0] = x_new
    sh2 = jnp.where(is_ctx, modc_ref[3:4, :], modl_ref[0, 3:4, :])
    sc2 = jnp.where(is_ctx, modc_ref[4:5, :], modl_ref[0, 4:5, :])
    h2 = _norm_mod(x_new, n2g_ref[...], sh2, sc2)
    for cc in range(d // 128):
        h2_ref[0, pl.ds(cc, tm, stride=8), :] = h2[:, cc * 128:(cc + 1) * 128]

    logits = jnp.dot(h2.astype(BF16), rw_ref[...], preferred_element_type=F32) + rb_ref[...]
    lane = lax.broadcasted_iota(jnp.int32, logits.shape, 1)
    vals = logits
    tops = []
    idx_out = jnp.zeros(logits.shape, jnp.int32)
    for kk in range(TOP_K):
        mx = jnp.max(vals, axis=-1, keepdims=True)
        ix = jnp.min(jnp.where(vals == mx, lane, 128), axis=-1, keepdims=True)
        tops.append(mx)
        idx_out = jnp.where(lane == kk, ix, idx_out)
        vals = jnp.where(lane == ix, -jnp.inf, vals)
    es = [jnp.exp(v - tops[0]) for v in tops]
    den = es[0] + es[1] + es[2] + es[3]
    packed = idx_out.astype(F32)
    for kk in range(TOP_K):
        gk = es[kk] / den
        hi = gk.astype(BF16).astype(F32)
        mid = (gk - hi).astype(BF16).astype(F32)
        lo = gk - hi - mid
        for piece, val in enumerate((hi, mid, lo)):
            packed = jnp.where(lane == TOP_K * (piece + 1) + kk, val, packed)
    ri = lax.broadcasted_iota(jnp.int32, (tm, tm), 0)
    ci = lax.broadcasted_iota(jnp.int32, (tm, tm), 1)
    eye = jnp.where(ri == ci, 1.0, 0.0).astype(BF16)
    rows = lax.dot_general(packed.astype(BF16), eye, (((0,), (0,)), ((), ())), preferred_element_type=F32)
    rt_ref[0, 0] = rows[0:4 * TOP_K, :]


def _merge(x, od, og, a32, cw, wdo, wco, wgo, wo, modl, modc, n2g, rw, rb, *, layer, ctx_len):
    b, t, d = x.shape
    tm = _row_tile(t, 544)
    nt8 = t // 8
    row = lambda bi, i: (bi, i, 0)
    const2 = lambda bi, i: (0, 0)
    wl = lambda bi, i: (layer, 0, 0)
    return pl.pallas_call(
        functools.partial(_merge_kernel, tm=tm, ctx_len=ctx_len, t=t),
        grid=(b, t // tm),
        in_specs=[
            pl.BlockSpec((1, tm, d), row),
            pl.BlockSpec((1, tm, d), row),
            pl.BlockSpec((1, tm, GLA_HEADS * GLA_DV), row),
            pl.BlockSpec((1, tm, 3 * d), lambda bi, i: (bi, i, 0)),
            pl.BlockSpec((1, tm, CONV_W), lambda bi, i: (bi, i, A32_CVB // CONV_W)),
            pl.BlockSpec((1, tm, CONV_W), lambda bi, i: (bi, i, A32_CVC // CONV_W)),
            pl.BlockSpec((1, tm, CONV_W), lambda bi, i: (bi, i, A32_CVX // CONV_W)),
            pl.BlockSpec((1, 8, CONV_W), lambda bi, i: (bi, jnp.maximum(i * (tm // 8) - 1, 0),
                                                   A32_CVC // CONV_W)),
            pl.BlockSpec((1, 8, CONV_W), lambda bi, i: (bi, jnp.maximum(i * (tm // 8) - 1, 0),
                                                   A32_CVX // CONV_W)),
            pl.BlockSpec((1, 8, CONV_W), lambda bi, i: (bi, jnp.minimum((i + 1) * (tm // 8), nt8 - 1),
                                                   A32_CVC // CONV_W)),
            pl.BlockSpec((1, 8, CONV_W), lambda bi, i: (bi, jnp.minimum((i + 1) * (tm // 8), nt8 - 1),
                                                   A32_CVX // CONV_W)),
            pl.BlockSpec((8, CONV_W), const2),
            pl.BlockSpec((1, d, d), wl),
            pl.BlockSpec((1, CONV_W, d), wl),
            pl.BlockSpec((1, GLA_HEADS * GLA_DV, d), wl),
            pl.BlockSpec((1, d, d), wl),
            pl.BlockSpec((1, 8, d), lambda bi, i: (bi, 0, 0)),
            pl.BlockSpec((8, d), const2),
            pl.BlockSpec((1, d), const2),
            pl.BlockSpec((d, 128), const2),
            pl.BlockSpec((1, 128), const2),
        ],
        out_specs=[
            pl.BlockSpec((1, tm, d), row),
            pl.BlockSpec((1, tm * (d // 128), 128), row),
            pl.BlockSpec((1, 1, 4 * TOP_K, tm), lambda bi, i: (bi, i, 0, 0)),
        ],
        out_shape=[
            jax.ShapeDtypeStruct((b, t, d), F32),
            jax.ShapeDtypeStruct((b, t * (d // 128), 128), F32),
            jax.ShapeDtypeStruct((b, t // tm, 4 * TOP_K, tm), F32),
        ],
        compiler_params=_cparams(("parallel", "parallel")),
        name="merge_router",
    )(x, od, og, a32, a32, a32, a32, a32, a32, a32, a32, cw, wdo, wco, wgo, wo,
      modl, modc, n2g, rw, rb)


def _moe_kernel(be_ref, nv_ref, g_ref, w1g_ref, w1l_ref, b1g_ref, b1l_ref, w2_ref, b2_ref,
                src_hbm, dst_hbm, h2_hbm, out_hbm, src_s, dst_s, xbuf, ybuf, sem_p, sem_g, sem_s,
                *, nb, n_slots):
    i = pl.program_id(0)
    nv = nv_ref[0]
    bm = MOE_BLOCK
    cur = lax.rem(i, 2)
    nxt = 1 - cur

    def idx_copies(blk_src, blk_dst, slot):
        return (pltpu.make_async_copy(src_hbm.at[blk_src, 0], src_s.at[pl.ds(slot * bm, bm)], sem_p),
                pltpu.make_async_copy(dst_hbm.at[blk_dst, 0], dst_s.at[pl.ds(slot * bm, bm)], sem_p))

    def start_gather(slot, buf):
        for j in range(bm):
            r0 = pl.multiple_of(src_s[slot * bm + j], TOK_ROWS)
            pltpu.make_async_copy(h2_hbm.at[pl.ds(r0, TOK_ROWS)],
                                  xbuf.at[buf, pl.ds(j * TOK_ROWS, TOK_ROWS)],
                                  sem_g.at[buf]).start(priority=j % 2)

    def wait_gather(buf):
        pltpu.make_async_copy(h2_hbm.at[pl.ds(0, bm * TOK_ROWS)], xbuf.at[buf], sem_g.at[buf]).wait()

    def wait_scatter():
        pltpu.make_async_copy(ybuf.at[0], out_hbm.at[pl.ds(0, bm * TOK_ROWS)], sem_s).wait()

    @pl.when(i <= nv)
    def _():
        @pl.when(i == 0)
        def _():
            ybuf[...] = jnp.zeros_like(ybuf)
            for blk_src, blk_dst, slot in ((0, 0, 0), (1, 1, 1), (0, nb, 3)):
                for cp in idx_copies(blk_src, blk_dst, slot):
                    cp.start()
                    cp.wait()
            start_gather(0, 0)

        @pl.when(i >= 1)
        def _():
            for cp in idx_copies(0, 0, 0):
                cp.wait()
            wait_scatter()

        wait_gather(cur)

        @pl.when(i < nv)
        def _():
            blk = jnp.minimum(i + 2, nv - 1)
            for cp in idx_copies(blk, blk, lax.rem(i + 2, 4)):
                cp.start()

        start_gather(lax.rem(i + 1, 4), nxt)
        pslot = lax.rem(i + 3, 4)
        for j in range(bm):
            r0 = pl.multiple_of(dst_s[pslot * bm + j], TOK_ROWS)
            pltpu.make_async_copy(ybuf.at[nxt, pl.ds(j * TOK_ROWS, TOK_ROWS)],
                                  out_hbm.at[pl.ds(r0, TOK_ROWS)], sem_s).start(priority=j % 2)
        x = jnp.concatenate([xbuf[cur, pl.ds(cc, bm, stride=TOK_ROWS), :] for cc in range(TOK_ROWS)],
                            axis=1).astype(BF16)
        hg = jnp.dot(x, w1g_ref[0, 0, 0], preferred_element_type=F32) + b1g_ref[0]
        hl = jnp.dot(x, w1l_ref[0, 0, 0], preferred_element_type=F32) + b1l_ref[0]
        hg = jnp.minimum(hg, SWIGLU_LIMIT)
        hl = jnp.clip(hl, -SWIGLU_LIMIT, SWIGLU_LIMIT)
        act = hg * _sigmoid(SWIGLU_ALPHA * hg) * (hl + 1.0)
        y = jnp.dot(act.astype(BF16), w2_ref[0, 0].astype(BF16), preferred_element_type=F32) + b2_ref[0]
        gi = lax.broadcasted_iota(jnp.int32, (bm, bm), 0)
        gj = lax.broadcasted_iota(jnp.int32, (bm, bm), 1)
        y = y * jnp.sum(jnp.where(gi == gj, g_ref[0], 0.0), axis=1, keepdims=True)
        for cc in range(TOK_ROWS):
            ybuf[cur, pl.ds(cc, bm, stride=TOK_ROWS), :] = y[:, cc * 128:(cc + 1) * 128]

        @pl.when(i == nv)
        def _():
            wait_scatter()
            wait_gather(nxt)
            ybuf[...] = jnp.zeros_like(ybuf)
            fills = [pltpu.make_async_copy(
                ybuf.at[sl], out_hbm.at[pl.ds((n_slots + sl * bm) * TOK_ROWS, bm * TOK_ROWS)], sem_s)
                for sl in range(2)]
            for cp in fills:
                cp.start()
            for cp in fills:
                cp.wait()


def _moe(block_exp, n_valid, row_src, row_dst, row_gate, h2_flat, w1, b1g, b1l, w2, b2, *, layer, n_slots):
    d = D_MODEL
    nb = row_src.shape[0]
    bm = MOE_BLOCK
    de = w1.shape[-1]
    blk = lambda i, nv: jnp.minimum(i, nv[0] - 1)
    wmap = lambda i, be, nv: (be[blk(i, nv)], 0, 0)
    return pl.pallas_call(
        functools.partial(_moe_kernel, nb=nb, n_slots=n_slots),
        grid_spec=pltpu.PrefetchScalarGridSpec(
            num_scalar_prefetch=2,
            grid=(nb + 1,),
            in_specs=[
                pl.BlockSpec((1, 1, bm), lambda i, be, nv: (blk(i, nv), 0, 0)),
                pl.BlockSpec((1, 1, 1, d, de), lambda i, be, nv: (layer, be[blk(i, nv)], 0, 0, 0)),
                pl.BlockSpec((1, 1, 1, d, de), lambda i, be, nv: (layer, be[blk(i, nv)], 1, 0, 0)),
                pl.BlockSpec((1, 1, de), wmap),
                pl.BlockSpec((1, 1, de), wmap),
                pl.BlockSpec((1, 1, de, d), lambda i, be, nv: (layer, be[blk(i, nv)], 0, 0)),
                pl.BlockSpec((1, 1, d), wmap),
                pl.BlockSpec(memory_space=pl.ANY),
                pl.BlockSpec(memory_space=pl.ANY),
                pl.BlockSpec(memory_space=pl.ANY),
            ],
            out_specs=pl.BlockSpec(memory_space=pl.ANY),
            scratch_shapes=[
                pltpu.SMEM((4 * bm,), jnp.int32),
                pltpu.SMEM((4 * bm,), jnp.int32),
                pltpu.VMEM((2, bm * TOK_ROWS, 128), F32),
                pltpu.VMEM((2, bm * TOK_ROWS, 128), F32),
                pltpu.SemaphoreType.DMA,
                pltpu.SemaphoreType.DMA((2,)),
                pltpu.SemaphoreType.DMA,
            ],
        ),
        out_shape=jax.ShapeDtypeStruct(((n_slots + 2 * bm) * TOK_ROWS, 128), F32),
        compiler_params=_cparams(("arbitrary",)),
        name="moe_experts",
    )(block_exp, n_valid, row_gate, w1, w1, b1g, b1l, w2, b2, row_src, row_dst, h2_flat)


def _combine_kernel(x_ref, y4_ref, modl_ref, modc_ref, fg_ref, o_ref, sum_s, *, tm, ctx_len, final, skip):
    i = pl.program_id(1) + skip
    tiles = (y4_ref[:, 0] + y4_ref[:, 1]) + (y4_ref[:, 2] + y4_ref[:, 3])
    sum_s[...] = tiles.reshape(tm * TOK_ROWS, 128)
    f = jnp.concatenate([sum_s[pl.ds(cc, tm, stride=TOK_ROWS), :] for cc in range(TOK_ROWS)], axis=1)
    pos = i * tm + lax.broadcasted_iota(jnp.int32, (tm, 1), 0)
    g2 = jnp.where(pos < ctx_len, modc_ref[5:6, :], modl_ref[0, 5:6, :])
    x_new = x_ref[0] + g2 * f
    if final:
        x_new = x_new * lax.rsqrt(jnp.mean(x_new * x_new, axis=-1, keepdims=True) + RMS_EPS) * fg_ref[...]
    o_ref[0] = x_new


def _combine(x, y4, modl, modc, fg, *, ctx_len, final):
    b, t, d = x.shape
    if final:
        tm = Q_TILE
        assert ctx_len % tm == 0
        skip = ctx_len // tm
    else:
        tm = _row_tile(t, 272)
        skip = 0
    npb = t // tm
    return pl.pallas_call(
        functools.partial(_combine_kernel, tm=tm, ctx_len=ctx_len, final=final, skip=skip),
        grid=(b, npb - skip),
        in_specs=[
            pl.BlockSpec((1, tm, d), lambda bi, i: (bi, i + skip, 0)),
            pl.BlockSpec((tm, TOP_K, TOK_ROWS, 128), lambda bi, i: (bi * (npb - skip) + i, 0, 0, 0)),
            pl.BlockSpec((1, 8, d), lambda bi, i: (bi, 0, 0)),
            pl.BlockSpec((8, d), lambda bi, i: (0, 0)),
            pl.BlockSpec((1, d), lambda bi, i: (0, 0)),
        ],
        out_specs=pl.BlockSpec((1, tm, d), lambda bi, i: (bi, i, 0)),
        out_shape=jax.ShapeDtypeStruct((b, t - skip * tm, d), F32),
        scratch_shapes=[pltpu.VMEM((tm * TOK_ROWS, 128), F32)],
        compiler_params=_cparams(("parallel", "parallel")),
        name="moe_combine",
    )(x, y4.reshape(-1, TOP_K, TOK_ROWS, 128), modl, modc, fg)


def _routing(e_flat, gate_flat, *, tm, seq, ctx_len, latent_only):
    npair = e_flat.shape[0]
    t = seq + ctx_len
    bm = MOE_BLOCK
    nb = npair // bm + N_EXPERTS
    nr = nb * bm
    npad = nr - npair
    experts = jnp.arange(N_EXPERTS, dtype=jnp.int32)
    ids = jnp.arange(npair, dtype=jnp.int32)
    tok_of = lambda i: (i // (TOP_K * tm)) * tm + i % tm
    if latent_only:
        e_flat = jnp.where(tok_of(ids) % t >= ctx_len, e_flat, N_EXPERTS)
    counts = jnp.sum((e_flat[:, None] == experts[None, :]).astype(jnp.int32), axis=0)
    padded = (counts + bm - 1) // bm * bm
    pad_end = jnp.cumsum(padded)
    fill_end = jnp.cumsum(padded - counts)
    dummy = jnp.arange(npad, dtype=jnp.int32)
    e_dummy = jnp.sum((dummy[:, None] >= fill_end[None, :]).astype(jnp.int32), axis=1)
    kb = nr
    keys = jnp.concatenate([e_flat * kb + ids, e_dummy * kb + npair + dummy])
    gates = jnp.concatenate([gate_flat, jnp.zeros((npad,), F32)])
    skey, sgate = lax.sort((keys, gates), num_keys=1)
    e_row = skey // kb
    ident = skey - e_row * kb
    valid = (ident < npair) & (e_row < N_EXPERTS)
    pair = jnp.where(valid, ident, 0)
    tok = tok_of(pair)
    kk = (pair // tm) % TOP_K
    row_gate = jnp.where(valid, sgate, 0.0).reshape(nb, 1, bm)
    row_src = (tok * TOK_ROWS).reshape(nb, 1, bm)
    if latent_only:
        slot = (tok // t) * seq + tok % t - ctx_len
        n_slots = (npair // TOP_K // t) * seq * TOP_K
    else:
        slot = tok
        n_slots = npair
    j = jnp.arange(nr, dtype=jnp.int32)
    dump = n_slots + ((j // bm) % 2) * bm + j % bm
    row_dst = jnp.where(valid, slot * TOP_K + kk, dump).reshape(nb, bm)
    row_dst = jnp.concatenate([row_dst, (n_slots + bm + jnp.arange(bm, dtype=jnp.int32))[None, :]],
                              axis=0)[:, None, :] * TOK_ROWS
    blocks = jnp.arange(nb, dtype=jnp.int32) * bm
    block_exp = jnp.minimum(jnp.sum((blocks[:, None] >= pad_end[None, :]).astype(jnp.int32), axis=1),
                            N_EXPERTS - 1)
    n_valid = (pad_end[-1] // bm).astype(jnp.int32).reshape(1)
    return row_src, row_dst, row_gate, block_exp, n_valid, n_slots


def _rope_tables(ctx_len, seq_len):
    inv_freq = ROPE_BASE ** (-jnp.arange(0, ROT_AXIS_DIM, 2, dtype=F32) / ROT_AXIS_DIM)
    p = jnp.arange(seq_len, dtype=jnp.int32)
    row = (p // GRID_W).astype(F32)
    col = (p % GRID_W).astype(F32)
    lane = np.arange(128)
    dd = lane % DIFF_HEAD_DIM
    fidx = jnp.asarray((dd % ROT_AXIS_DIM) % (ROT_AXIS_DIM // 2))
    use_col = jnp.asarray(dd >= ROT_AXIS_DIM)
    second_half = jnp.asarray((dd % ROT_AXIS_DIM) >= ROT_AXIS_DIM // 2)
    posm = jnp.where(use_col[None, :], col[:, None], row[:, None])
    ang = posm * inv_freq[fidx][None, :]
    cos = jnp.cos(ang)
    sin = jnp.sin(ang)
    sa = jnp.where(second_half[None, :], sin, 0.0)
    sb = jnp.where(second_half[None, :], 0.0, -sin)
    pad = lambda a, v: jnp.concatenate([jnp.full((ctx_len, 128), v, F32), a], axis=0)
    return pad(cos, 1.0), pad(sa, 0.0), pad(sb, 0.0)


def kernel(x, c, ctx, c_ctx, ada_w, ada_b, norm1_g, norm2_g, w_in, diff_lambda, diff_subln_g,
           diff_w_out, conv_w, conv_w_out, gla_w_a2, gla_b_a, gla_norm_g, gla_w_out, w_o,
           router_w, router_b, moe_w1, moe_b1, moe_w2, moe_b2, final_norm_g):
    b, s, d = x.shape
    cl = ctx.shape[1]
    t = cl + s
    nl = w_in.shape[0]
    assert d == D_MODEL and b <= 15

    xs = jnp.concatenate([ctx, x], axis=1)
    cc = jnp.concatenate([c, c_ctx[None, :], jnp.zeros((15 - b, d), F32)], axis=0)
    mod = _ada(cc, ada_w, ada_b).reshape(nl, 16, N_ADA, d)
    cos, sa, sb = _rope_tables(cl, s)

    col_scale = np.ones((w_in.shape[-1],), np.float32)
    col_scale[0:1024] = DIFF_HEAD_DIM ** -0.5
    col_scale[4608:4864] = GLA_DK ** -0.5
    w_s = (w_in * jnp.asarray(col_scale)).astype(BF16)
    w_all = jnp.concatenate(
        [w_s[..., :W_GATES], w_s[..., W_GATES + 2 * GLA_RANK:], w_s[..., W_GATES:W_GATES + 2 * GLA_RANK],
         jnp.zeros((nl, d, W_ALL_COLS - w_in.shape[-1]), BF16)], axis=-1)

    wa = jnp.zeros((nl, 2, 128, 256), F32)
    for p in range(2):
        wa = wa.at[:, p, 0:GLA_RANK, 0:128].set(gla_w_a2[:, 0, :, p * 128:(p + 1) * 128])
        wa = wa.at[:, p, GLA_RANK:2 * GLA_RANK, 128:256].set(gla_w_a2[:, 1, :, p * 128:(p + 1) * 128])
    wa = wa.astype(BF16)
    ba = jnp.stack([jnp.concatenate([gla_b_a[:, 0, p * 128:(p + 1) * 128],
                                     gla_b_a[:, 1, p * 128:(p + 1) * 128]], axis=-1)
                    for p in range(2)], axis=1)[:, :, None, :]

    cw = jnp.concatenate([conv_w, jnp.zeros((nl, 5, CONV_W), F32)], axis=1)
    rw = jnp.concatenate([router_w, jnp.zeros((nl, d, 128 - N_EXPERTS), F32)], axis=-1).astype(BF16)
    rb = jnp.concatenate([router_b, jnp.full((nl, 128 - N_EXPERTS), -jnp.inf, F32)], axis=-1)[:, None, :]
    w1 = jnp.moveaxis(moe_w1.reshape(nl, N_EXPERTS, d, D_EXPERT, 2), -1, 2).astype(BF16)
    b1 = moe_b1.reshape(nl, N_EXPERTS, 1, D_EXPERT, 2)
    b1g = b1[..., 0]
    b1l = b1[..., 1]
    w2 = moe_w2
    b2 = moe_b2[:, :, None, :]
    wdo = diff_w_out.astype(BF16)
    wco = conv_w_out.astype(BF16)
    wgo = gla_w_out.astype(BF16)
    wo = w_o.astype(BF16)

    for layer in range(nl):
        last = layer == nl - 1
        lam_init = 0.8 - 0.6 * math.exp(-0.3 * layer)
        ml = mod[layer]
        modl = jnp.concatenate([ml[:b], jnp.zeros((b, 2, d), F32)], axis=1)
        modc = jnp.concatenate([ml[b], jnp.zeros((2, d), F32)], axis=0)

        a16, a32 = _k1(xs, modl, modc, norm1_g[layer][None, :], cos, sa, sb, w_all, layer=layer, ctx_len=cl)
        od = _attn(a16, diff_lambda[layer], diff_subln_g[layer][None, :],
                   ctx_len=cl, seq_len=s, lam_init=lam_init)
        og = _gla(a16, a32, wa[layer], ba[layer], gla_norm_g[layer][None, :], ctx_len=cl)
        xs, h2, rt = _merge(xs, od, og, a32, cw[layer], wdo, wco, wgo, wo, modl, modc,
                            norm2_g[layer][None, :], rw[layer], rb[layer], layer=layer, ctx_len=cl)

        r = b * t
        tm_r = rt.shape[-1]
        rt = rt.reshape(-1, 4, TOP_K, tm_r)
        row_src, row_dst, row_gate, block_exp, n_valid, n_slots = _routing(
            rt[:, 0].astype(jnp.int32).reshape(-1), (rt[:, 1] + rt[:, 2] + rt[:, 3]).reshape(-1),
            tm=tm_r, seq=s, ctx_len=cl, latent_only=last)
        y4 = _moe(block_exp, n_valid, row_src, row_dst, row_gate, h2.reshape(r * TOK_ROWS, 128), w1,
                  b1g[layer], b1l[layer], w2, b2[layer], layer=layer, n_slots=n_slots)
        xs = _combine(xs, y4, modl, modc, final_norm_g[None, :], ctx_len=cl, final=last)

    return xs
```

```python
import functools
import math

import numpy as np
import jax
import jax.numpy as jnp
from jax import lax
from jax.experimental import pallas as pl
from jax.experimental.pallas import tpu as pltpu

F32 = jnp.float32
BF16 = jnp.bfloat16

D_MODEL = 1024
GRID_W = 64
RMS_EPS = 1e-6
N_ADA = 6
DIFF_HEADS = 8
DIFF_HEAD_DIM = 64
DIFF_V_DIM = 128
ROPE_BASE = 10000.0
ROT_AXIS_DIM = 32
CONV_W = 512
GLA_HEADS = 4
GLA_DK = 64
GLA_DV = 128
GLA_RANK = 16
GLA_TAU = 16.0
GLA_CHUNK = 64
N_EXPERTS = 32
TOP_K = 4
D_EXPERT = 1024
SWIGLU_LIMIT = 7.0
SWIGLU_ALPHA = 1.702
LOG2_E = 1.4426950408889634

A16_Q, A16_K, A16_V, A16_GV, A16_W = 0, 1024, 2048, 3072, 3584
A32_GATES, A32_CVB, A32_CVC, A32_CVX = 0, 3072, 3584, 4096
A32_GLQ, A32_GLK, A32_GLR, A32_GLA, A32_W = 4608, 4864, 5120, 5632, 5760
W_ALL_COLS = A16_W + A32_W
W_GATES, W_GLA = 6144, 9216


def _k1_chunks():
    out = []
    def add(w0, width, to16, o0, rope=False, ch=512):
        for c in range(0, width, ch):
            out.append((w0 + c, min(ch, width - c), to16, o0 + c, rope))
    add(0, 1024, True, A16_Q, rope=True)
    add(1024, 1024, True, A16_K, rope=True)
    add(2048, 1024, True, A16_V)
    add(3072, 512, False, A32_CVB)
    add(3584, 512, False, A32_CVC)
    add(4096, 512, False, A32_CVX)
    add(4608, 256, False, A32_GLQ)
    add(4864, 256, False, A32_GLK)
    add(5120, 512, True, A16_GV)
    add(5632, 512, False, A32_GLR)
    add(W_GATES, 3072, False, A32_GATES)
    add(W_GLA, 128, False, A32_GLA)
    return tuple(out)


K1_CHUNKS = _k1_chunks()

Q_TILE = 256
KV_SEGMENTS = 2
ATTN_HEADS = 4
MOE_BLOCK = 256
GLA_UNROLL = 4
TOK_ROWS = 8
V7X_VMEM_BYTES = 64 * 1024 * 1024
VMEM_LIMIT = V7X_VMEM_BYTES - 8 * 1024 * 1024


def _cparams(sem):
    return pltpu.CompilerParams(dimension_semantics=sem, vmem_limit_bytes=VMEM_LIMIT)


def _row_tile(t, target):
    best = None
    for cand in range(16, target + 1, 16):
        if t % cand == 0:
            best = cand
    assert best is not None, t
    return best


def _sigmoid(v):
    return 1.0 / (1.0 + jnp.exp(-v))


def _ada_kernel(cc_ref, w_ref, b_ref, o_ref):
    a = cc_ref[...]
    a = a * _sigmoid(a)
    o_ref[0] = jnp.dot(a, w_ref[0], precision=lax.Precision.HIGHEST,
                       preferred_element_type=F32) + b_ref[0]


def _ada(cc, ada_w, ada_b):
    nl, d, n = ada_w.shape
    tn = 1024
    return pl.pallas_call(
        _ada_kernel,
        grid=(nl, n // tn),
        in_specs=[
            pl.BlockSpec((16, d), lambda l, j: (0, 0)),
            pl.BlockSpec((1, d, tn), lambda l, j: (l, 0, j)),
            pl.BlockSpec((1, 1, tn), lambda l, j: (l, 0, j)),
        ],
        out_specs=pl.BlockSpec((1, 16, tn), lambda l, j: (l, 0, j)),
        out_shape=jax.ShapeDtypeStruct((nl, 16, n), F32),
        compiler_params=_cparams(("parallel", "parallel")),
        name="ada_mod",
    )(cc, ada_w, ada_b.reshape(nl, 1, n))


def _norm_mod(x, ng, shift, scale):
    y = x * lax.rsqrt(jnp.mean(x * x, axis=-1, keepdims=True) + RMS_EPS) * ng
    return y * (1.0 + scale) + shift


def _k1_kernel(x_ref, modl_ref, modc_ref, ng_ref, cos_ref, sa_ref, sb_ref, w_ref,
               o16_ref, o32_ref, *, tm, ctx_len):
    i = pl.program_id(1)
    pos = i * tm + lax.broadcasted_iota(jnp.int32, (tm, 1), 0)
    is_ctx = pos < ctx_len
    shift = jnp.where(is_ctx, modc_ref[0:1, :], modl_ref[0, 0:1, :])
    scale = jnp.where(is_ctx, modc_ref[1:2, :], modl_ref[0, 1:2, :])
    h = _norm_mod(x_ref[0], ng_ref[...], shift, scale).astype(BF16)
    cos = cos_ref[...]
    sa = sa_ref[...]
    sb = sb_ref[...]
    for w0, width, to16, o0, rope in K1_CHUNKS:
        acc = jnp.dot(h, w_ref[0, :, w0:w0 + width], preferred_element_type=F32)
        if rope:
            parts = []
            for j in range(width // 128):
                a = acc[:, j * 128:(j + 1) * 128]
                parts.append(a * cos + pltpu.roll(a, 16, 1) * sa + pltpu.roll(a, 112, 1) * sb)
            acc = jnp.concatenate(parts, axis=1)
        if to16 and o0 < A16_K:
            acc = acc * LOG2_E
        if to16:
            o16_ref[0, :, o0:o0 + width] = acc.astype(BF16)
        else:
            o32_ref[0, :, o0:o0 + width] = acc


def _k1(x, modl, modc, ng, cos, sa, sb, w_all, *, layer, ctx_len):
    b, t, d = x.shape
    tm = _row_tile(t, 272)
    tbl = pl.BlockSpec((tm, 128), lambda bi, i: (i, 0))
    return pl.pallas_call(
        functools.partial(_k1_kernel, tm=tm, ctx_len=ctx_len),
        grid=(b, t // tm),
        in_specs=[
            pl.BlockSpec((1, tm, d), lambda bi, i: (bi, i, 0)),
            pl.BlockSpec((1, 8, d), lambda bi, i: (bi, 0, 0)),
            pl.BlockSpec((8, d), lambda bi, i: (0, 0)),
            pl.BlockSpec((1, d), lambda bi, i: (0, 0)),
            tbl, tbl, tbl,
            pl.BlockSpec((1, d, W_ALL_COLS), lambda bi, i: (layer, 0, 0), pipeline_mode=pl.Buffered(1)),
        ],
        out_specs=[
            pl.BlockSpec((1, tm, A16_W), lambda bi, i: (bi, i, 0)),
            pl.BlockSpec((1, tm, A32_W), lambda bi, i: (bi, i, 0)),
        ],
        out_shape=[
            jax.ShapeDtypeStruct((b, t, A16_W), BF16),
            jax.ShapeDtypeStruct((b, t, A32_W), F32),
        ],
        compiler_params=_cparams(("parallel", "parallel")),
        name="norm_inproj",
    )(x, modl, modc, ng, cos, sa, sb, w_all)


def _attn_kernel(lam_ref, g_ref, q_ref, k_ref, v_ref, o_ref, *, ctx_len, segments, lam_init):
    qi = pl.program_id(2)
    tq = Q_TILE
    hb = DIFF_V_DIM
    lv = lam_ref[...]
    lam = (jnp.exp(jnp.sum(lv[0:1] * lv[1:2], axis=1, keepdims=True))
           - jnp.exp(jnp.sum(lv[2:3] * lv[3:4], axis=1, keepdims=True)) + lam_init)

    def run(segs):
        qss = []
        for g in range(ATTN_HEADS):
            q = q_ref[0, :, g * hb:(g + 1) * hb]
            lane = lax.broadcasted_iota(jnp.int32, q.shape, 1)
            zero = jnp.zeros_like(q)
            qss.append(jnp.concatenate([jnp.where(lane < DIFF_HEAD_DIM, q, zero),
                                        jnp.where(lane >= DIFF_HEAD_DIM, q, zero)], axis=0))
        carries = [None] * ATTN_HEADS
        for r0, rows in segs:
            for g in range(ATTN_HEADS):
                k_t = k_ref[0, r0:r0 + rows, g * hb:(g + 1) * hb]
                v_t = v_ref[0, r0:r0 + rows, g * hb:(g + 1) * hb]
                s = lax.dot_general(qss[g], k_t, (((1,), (1,)), ((), ())), preferred_element_type=F32)
                smax = jnp.max(s, axis=-1, keepdims=True)
                if carries[g] is None:
                    p = jnp.exp2(s - smax)
                    carries[g] = (smax, jnp.sum(p, axis=-1, keepdims=True),
                                  jnp.dot(p.astype(BF16), v_t, preferred_element_type=F32))
                else:
                    m, l, acc = carries[g]
                    m_new = jnp.maximum(m, smax)
                    alpha = jnp.exp2(m - m_new)
                    p = jnp.exp2(s - m_new)
                    carries[g] = (m_new, alpha * l + jnp.sum(p, axis=-1, keepdims=True),
                                  alpha * acc + jnp.dot(p.astype(BF16), v_t, preferred_element_type=F32))
        for g in range(ATTN_HEADS):
            _, l, acc = carries[g]
            o = acc[:tq] / l[:tq] - lam * (acc[tq:] / l[tq:])
            o = o * lax.rsqrt(jnp.mean(o * o, axis=-1, keepdims=True) + RMS_EPS) * g_ref[...]
            o_ref[0, :, g * hb:(g + 1) * hb] = (o * (1.0 - lam_init)).astype(BF16)

    @pl.when(qi * tq >= ctx_len)
    def _():
        run(segments)

    @pl.when(qi * tq < ctx_len)
    def _():
        run([(0, ctx_len)])


def _attn(a16, lam_p, subln_g, *, ctx_len, seq_len, lam_init):
    b, t, _ = a16.shape
    assert ctx_len % Q_TILE == 0 and t % 256 == 0
    first = (t // 256 + KV_SEGMENTS - 1) // KV_SEGMENTS * 256
    bounds = [min(first * n, t) for n in range(KV_SEGMENTS + 1)]
    segments = [(lo, hi - lo) for lo, hi in zip(bounds[:-1], bounds[1:]) if hi > lo]
    hb = DIFF_V_DIM * ATTN_HEADS
    return pl.pallas_call(
        functools.partial(_attn_kernel, ctx_len=ctx_len, segments=segments, lam_init=lam_init),
        grid=(b, DIFF_HEADS // ATTN_HEADS, t // Q_TILE),
        in_specs=[
            pl.BlockSpec((4, DIFF_HEAD_DIM), lambda bi, h, qi: (0, 0)),
            pl.BlockSpec((1, DIFF_V_DIM), lambda bi, h, qi: (0, 0)),
            pl.BlockSpec((1, Q_TILE, hb), lambda bi, h, qi: (bi, qi, A16_Q // hb + h)),
            pl.BlockSpec((1, t, hb), lambda bi, h, qi: (bi, 0, A16_K // hb + h)),
            pl.BlockSpec((1, t, hb), lambda bi, h, qi: (bi, 0, A16_V // hb + h)),
        ],
        out_specs=pl.BlockSpec((1, Q_TILE, hb), lambda bi, h, qi: (bi, qi, h)),
        out_shape=jax.ShapeDtypeStruct((b, t, DIFF_HEADS * DIFF_V_DIM), BF16),
        compiler_params=_cparams(("parallel", "parallel", "parallel")),
        name="diff_attn",
    )(lam_p, subln_g, a16, a16, a16)


def _split3(x):
    hi = x.astype(BF16)
    r1 = x - hi.astype(F32)
    mid = r1.astype(BF16)
    lo = (r1 - mid.astype(F32)).astype(BF16)
    return jnp.concatenate([hi, mid, lo], axis=1)


def _sum3(y, w):
    return y[:, 0:w] + y[:, w:2 * w] + y[:, 2 * w:3 * w]


def _gla_kernel(q_ref, k_ref, v_ref, a_ref, r_ref, wa_ref, ba_ref, ng_ref, o_ref,
                qd_s, ki_s, ke_s, dec_s, oacc_s, st_s, *, t, ctx_len):
    ck = GLA_CHUNK
    nc = t // ck
    ncc = ctx_len // ck
    pr = 256
    w = 2 * GLA_DK

    ri = lax.broadcasted_iota(jnp.int32, (pr, pr), 0)
    ci = lax.broadcasted_iota(jnp.int32, (pr, pr), 1)
    same = (ri // ck) == (ci // ck)
    m_tot = jnp.where(same, 1.0, 0.0).astype(BF16)
    m_cum = (jnp.where(same & (ci <= ri), 1.0, 0.0).astype(BF16),
             jnp.where(same & (ci >= ri), 1.0, 0.0).astype(BF16))
    for r0 in range(0, t, pr):
        a = a_ref[0, r0:r0 + pr, :].astype(BF16)
        logit = jnp.dot(a, wa_ref[0], preferred_element_type=F32) + ba_ref[0]
        g = (jnp.minimum(logit, 0.0) - jnp.log1p(jnp.exp(-jnp.abs(logit)))) * (1.0 / GLA_TAU)
        q = q_ref[0, r0:r0 + pr, :]
        k = k_ref[0, r0:r0 + pr, :]
        for d in range(2):
            g3 = _split3(g[:, d * w:(d + 1) * w])
            g_cum = _sum3(jnp.dot(m_cum[d], g3, preferred_element_type=F32), w)
            g_tot = _sum3(jnp.dot(m_tot, g3, preferred_element_type=F32), w)
            qd_s[d, r0:r0 + pr, :] = (q * jnp.exp(g_cum)).astype(BF16)
            ki_s[d, r0:r0 + pr, :] = (k * jnp.exp(-g_cum)).astype(BF16)
            ke_s[d, r0:r0 + pr, :] = (k * jnp.exp(g_tot - g_cum)).astype(BF16)
            dec_s[d, r0:r0 + pr, :] = jnp.exp(g_tot)
        oacc_s[r0:r0 + pr, :] = jnp.zeros((pr, 2 * GLA_DV), F32)
    st_s[...] = jnp.zeros_like(st_s)

    ri = lax.broadcasted_iota(jnp.int32, (2 * ck, ck), 0)
    ci = lax.broadcasted_iota(jnp.int32, (2 * ck, ck), 1)
    rr = jnp.where(ri >= ck, ri - ck, ri)
    tri = (ci <= rr, ci >= rr)
    lane = lax.broadcasted_iota(jnp.int32, (ck, w), 1)
    br = lax.broadcasted_iota(jnp.int32, (2 * GLA_DV, w), 0)
    bc = lax.broadcasted_iota(jnp.int32, (2 * GLA_DV, w), 1)
    blk = (br < GLA_DV) == (bc < GLA_DK)
    nt = (((1,), (1,)), ((), ()))
    tn = (((0,), (0,)), ((), ()))

    def chunk(d, r0):
        qd = qd_s[d, pl.ds(r0, ck), :]
        ki = ki_s[d, pl.ds(r0, ck), :]
        ke = ke_s[d, pl.ds(r0, ck), :]
        v = v_ref[0, pl.ds(r0, ck), :]
        dec = dec_s[d, pl.ds(r0, 8), :][0:1, :]
        zq = jnp.zeros_like(qd)
        qs = jnp.concatenate([jnp.where(lane < GLA_DK, qd, zq), jnp.where(lane >= GLA_DK, qd, zq)], axis=0)
        att = lax.dot_general(qs, ki, nt, preferred_element_type=F32)
        att = jnp.where(tri[d], att, 0.0).astype(BF16)
        oi = jnp.dot(att, v, preferred_element_type=F32)
        o_intra = jnp.concatenate([oi[:ck, :GLA_DV], oi[ck:, GLA_DV:]], axis=1)
        st = st_s[d]
        o_inter = lax.dot_general(qd, st.astype(BF16), nt, preferred_element_type=F32)
        s_loc = lax.dot_general(v, ke, tn, preferred_element_type=F32)
        st_s[d] = st * dec + jnp.where(blk, s_loc, 0.0)
        oacc_s[pl.ds(r0, ck), :] = oacc_s[pl.ds(r0, ck), :] + o_intra + o_inter

    def body(n, _):
        chunk(0, pl.multiple_of(n * ck, ck))
        nb = jnp.where(n < ncc, ncc - 1 - n, nc - 1 - (n - ncc))
        chunk(1, pl.multiple_of(nb * ck, ck))
        return 0

    lax.fori_loop(0, nc, body, 0, unroll=GLA_UNROLL)

    ng = ng_ref[...]
    rt = _row_tile(t, 544)
    for r0 in range(0, t, rt):
        o = oacc_s[r0:r0 + rt, :]
        r = r_ref[0, r0:r0 + rt, :]
        outs = []
        for hh in range(2):
            oh = o[:, hh * GLA_DV:(hh + 1) * GLA_DV]
            outs.append(oh * lax.rsqrt(jnp.mean(oh * oh, axis=-1, keepdims=True) + RMS_EPS) * ng)
        o_ref[0, r0:r0 + rt, :] = (jnp.concatenate(outs, axis=1) * (r * _sigmoid(r))).astype(BF16)


def _gla(a16, a32, wa, ba, ng, *, ctx_len):
    b, t, _ = a16.shape
    assert t % 256 == 0 and ctx_len % GLA_CHUNK == 0 and (t // GLA_CHUNK) % GLA_UNROLL == 0
    return pl.pallas_call(
        functools.partial(_gla_kernel, t=t, ctx_len=ctx_len),
        grid=(b, 2),
        in_specs=[
            pl.BlockSpec((1, t, 128), lambda bi, p: (bi, 0, A32_GLQ // 128 + p)),
            pl.BlockSpec((1, t, 128), lambda bi, p: (bi, 0, A32_GLK // 128 + p)),
            pl.BlockSpec((1, t, 256), lambda bi, p: (bi, 0, A16_GV // 256 + p)),
            pl.BlockSpec((1, t, 128), lambda bi, p: (bi, 0, A32_GLA // 128)),
            pl.BlockSpec((1, t, 256), lambda bi, p: (bi, 0, A32_GLR // 256 + p)),
            pl.BlockSpec((1, 128, 256), lambda bi, p: (p, 0, 0)),
            pl.BlockSpec((1, 1, 256), lambda bi, p: (p, 0, 0)),
            pl.BlockSpec((1, GLA_DV), lambda bi, p: (0, 0)),
        ],
        out_specs=pl.BlockSpec((1, t, 256), lambda bi, p: (bi, 0, p)),
        out_shape=jax.ShapeDtypeStruct((b, t, GLA_HEADS * GLA_DV), BF16),
        scratch_shapes=[
            pltpu.VMEM((2, t, 128), BF16), pltpu.VMEM((2, t, 128), BF16), pltpu.VMEM((2, t, 128), BF16),
            pltpu.VMEM((2, t, 128), F32),
            pltpu.VMEM((t, 256), F32),
            pltpu.VMEM((2, 256, 128), F32),
        ],
        compiler_params=_cparams(("parallel", "parallel")),
        name="gla_bidir",
    )(a32, a32, a16, a32, a32, wa, ba, ng)


def _merge_kernel(x_ref, od_ref, og_ref, gates_ref, cvb_ref, cvc_ref, cvx_ref,
                  cvc_p_ref, cvx_p_ref, cvc_n_ref, cvx_n_ref, cw_ref,
                  wdo_ref, wco_ref, wgo_ref, wo_ref, modl_ref, modc_ref, n2g_ref,
                  rw_ref, rb_ref, xo_ref, h2_ref, rt_ref, *, tm, ctx_len, t):
    i = pl.program_id(1)
    pos = i * tm + lax.broadcasted_iota(jnp.int32, (tm, 1), 0)
    is_ctx = pos < ctx_len
    row = lax.broadcasted_iota(jnp.int32, (tm, 1), 0)

    z = cvc_ref[0] * cvx_ref[0]
    z_before = cvc_p_ref[0, 7:8, :] * cvx_p_ref[0, 7:8, :]
    z_after = cvc_n_ref[0, 0:1, :] * cvx_n_ref[0, 0:1, :]
    z_prev = jnp.where(row == 0, z_before, pltpu.roll(z, 1, 0))
    z_prev = jnp.where((pos == 0) | (pos == ctx_len), 0.0, z_prev)
    z_next = jnp.where(row == tm - 1, z_after, pltpu.roll(z, tm - 1, 0))
    z_next = jnp.where((pos == ctx_len - 1) | (pos == t - 1), 0.0, z_next)
    conv = z_prev * cw_ref[0:1, :] + z * cw_ref[1:2, :] + z_next * cw_ref[2:3, :]
    zc = (cvb_ref[0] * conv).astype(BF16)

    y_diff = jnp.dot(od_ref[0], wdo_ref[0], preferred_element_type=F32)
    y_conv = jnp.dot(zc, wco_ref[0], preferred_element_type=F32)
    y_gla = jnp.dot(og_ref[0], wgo_ref[0], preferred_element_type=F32)
    d = D_MODEL
    mix = (_sigmoid(gates_ref[0, :, 0:d]) * y_diff
           + _sigmoid(gates_ref[0, :, d:2 * d]) * y_conv
           + _sigmoid(gates_ref[0, :, 2 * d:3 * d]) * y_gla)
    m = jnp.dot(mix.astype(BF16), wo_ref[0], preferred_element_type=F32)

    g1 = jnp.where(is_ctx, modc_ref[2:3, :], modl_ref[0, 2:3, :])
    x_new = x_ref[0] + g1 * m
    xo_ref[0] = x_new
    sh2 = jnp.where(is_ctx, modc_ref[3:4, :], modl_ref[0, 3:4, :])
    sc2 = jnp.where(is_ctx, modc_ref[4:5, :], modl_ref[0, 4:5, :])
    h2 = _norm_mod(x_new, n2g_ref[...], sh2, sc2)
    for cc in range(d // 128):
        h2_ref[0, pl.ds(cc, tm, stride=8), :] = h2[:, cc * 128:(cc + 1) * 128]

    logits = jnp.dot(h2.astype(BF16), rw_ref[...], preferred_element_type=F32) + rb_ref[...]
    lane = lax.broadcasted_iota(jnp.int32, logits.shape, 1)
    vals = logits
    tops = []
    idx_out = jnp.zeros(logits.shape, jnp.int32)
    for kk in range(TOP_K):
        mx = jnp.max(vals, axis=-1, keepdims=True)
        ix = jnp.min(jnp.where(vals == mx, lane, 128), axis=-1, keepdims=True)
        tops.append(mx)
        idx_out = jnp.where(lane == kk, ix, idx_out)
        vals = jnp.where(lane == ix, -jnp.inf, vals)
    es = [jnp.exp(v - tops[0]) for v in tops]
    den = es[0] + es[1] + es[2] + es[3]
    packed = idx_out.astype(F32)
    for kk in range(TOP_K):
        gk = es[kk] / den
        hi = gk.astype(BF16).astype(F32)
        mid = (gk - hi).astype(BF16).astype(F32)
        lo = gk - hi - mid
        for piece, val in enumerate((hi, mid, lo)):
            packed = jnp.where(lane == TOP_K * (piece + 1) + kk, val, packed)
    ri = lax.broadcasted_iota(jnp.int32, (tm, tm), 0)
    ci = lax.broadcasted_iota(jnp.int32, (tm, tm), 1)
    eye = jnp.where(ri == ci, 1.0, 0.0).astype(BF16)
    rows = lax.dot_general(packed.astype(BF16), eye, (((0,), (0,)), ((), ())), preferred_element_type=F32)
    rt_ref[0, 0] = rows[0:4 * TOP_K, :]


def _merge(x, od, og, a32, cw, wdo, wco, wgo, wo, modl, modc, n2g, rw, rb, *, layer, ctx_len):
    b, t, d = x.shape
    tm = _row_tile(t, 544)
    nt8 = t // 8
    row = lambda bi, i: (bi, i, 0)
    const2 = lambda bi, i: (0, 0)
    wl = lambda bi, i: (layer, 0, 0)
    return pl.pallas_call(
        functools.partial(_merge_kernel, tm=tm, ctx_len=ctx_len, t=t),
        grid=(b, t // tm),
        in_specs=[
            pl.BlockSpec((1, tm, d), row),
            pl.BlockSpec((1, tm, d), row),
            pl.BlockSpec((1, tm, GLA_HEADS * GLA_DV), row),
            pl.BlockSpec((1, tm, 3 * d), lambda bi, i: (bi, i, 0)),
            pl.BlockSpec((1, tm, CONV_W), lambda bi, i: (bi, i, A32_CVB // CONV_W)),
            pl.BlockSpec((1, tm, CONV_W), lambda bi, i: (bi, i, A32_CVC // CONV_W)),
            pl.BlockSpec((1, tm, CONV_W), lambda bi, i: (bi, i, A32_CVX // CONV_W)),
            pl.BlockSpec((1, 8, CONV_W), lambda bi, i: (bi, jnp.maximum(i * (tm // 8) - 1, 0),
                                                   A32_CVC // CONV_W)),
            pl.BlockSpec((1, 8, CONV_W), lambda bi, i: (bi, jnp.maximum(i * (tm // 8) - 1, 0),
                                                   A32_CVX // CONV_W)),
            pl.BlockSpec((1, 8, CONV_W), lambda bi, i: (bi, jnp.minimum((i + 1) * (tm // 8), nt8 - 1),
                                                   A32_CVC // CONV_W)),
            pl.BlockSpec((1, 8, CONV_W), lambda bi, i: (bi, jnp.minimum((i + 1) * (tm // 8), nt8 - 1),
                                                   A32_CVX // CONV_W)),
            pl.BlockSpec((8, CONV_W), const2),
            pl.BlockSpec((1, d, d), wl),
            pl.BlockSpec((1, CONV_W, d), wl),
            pl.BlockSpec((1, GLA_HEADS * GLA_DV, d), wl),
            pl.BlockSpec((1, d, d), wl),
            pl.BlockSpec((1, 8, d), lambda bi, i: (bi, 0, 0)),
            pl.BlockSpec((8, d), const2),
            pl.BlockSpec((1, d), const2),
            pl.BlockSpec((d, 128), const2),
            pl.BlockSpec((1, 128), const2),
        ],
        out_specs=[
            pl.BlockSpec((1, tm, d), row),
            pl.BlockSpec((1, tm * (d // 128), 128), row),
            pl.BlockSpec((1, 1, 4 * TOP_K, tm), lambda bi, i: (bi, i, 0, 0)),
        ],
        out_shape=[
            jax.ShapeDtypeStruct((b, t, d), F32),
            jax.ShapeDtypeStruct((b, t * (d // 128), 128), F32),
            jax.ShapeDtypeStruct((b, t // tm, 4 * TOP_K, tm), F32),
        ],
        compiler_params=_cparams(("parallel", "parallel")),
        name="merge_router",
    )(x, od, og, a32, a32, a32, a32, a32, a32, a32, a32, cw, wdo, wco, wgo, wo,
      modl, modc, n2g, rw, rb)


def _moe_kernel(be_ref, nv_ref, g_ref, w1g_ref, w1l_ref, b1g_ref, b1l_ref, w2_ref, b2_ref,
                src_hbm, dst_hbm, h2_hbm, out_hbm, src_s, dst_s, xbuf, ybuf, sem_p, sem_g, sem_s,
                *, nb, n_slots):
    i = pl.program_id(0)
    nv = nv_ref[0]
    bm = MOE_BLOCK
    cur = lax.rem(i, 2)
    nxt = 1 - cur

    def idx_copies(blk_src, blk_dst, slot):
        return (pltpu.make_async_copy(src_hbm.at[blk_src, 0], src_s.at[pl.ds(slot * bm, bm)], sem_p),
                pltpu.make_async_copy(dst_hbm.at[blk_dst, 0], dst_s.at[pl.ds(slot * bm, bm)], sem_p))

    def start_gather(slot, buf):
        for j in range(bm):
            r0 = pl.multiple_of(src_s[slot * bm + j], TOK_ROWS)
            pltpu.make_async_copy(h2_hbm.at[pl.ds(r0, TOK_ROWS)],
                                  xbuf.at[buf, pl.ds(j * TOK_ROWS, TOK_ROWS)],
                                  sem_g.at[buf]).start(priority=j % 2)

    def wait_gather(buf):
        pltpu.make_async_copy(h2_hbm.at[pl.ds(0, bm * TOK_ROWS)], xbuf.at[buf], sem_g.at[buf]).wait()

    def wait_scatter():
        pltpu.make_async_copy(ybuf.at[0], out_hbm.at[pl.ds(0, bm * TOK_ROWS)], sem_s).wait()

    @pl.when(i <= nv)
    def _():
        @pl.when(i == 0)
        def _():
            ybuf[...] = jnp.zeros_like(ybuf)
            for blk_src, blk_dst, slot in ((0, 0, 0), (1, 1, 1), (0, nb, 3)):
                for cp in idx_copies(blk_src, blk_dst, slot):
                    cp.start()
                    cp.wait()
            start_gather(0, 0)

        @pl.when(i >= 1)
        def _():
            for cp in idx_copies(0, 0, 0):
                cp.wait()
            wait_scatter()

        wait_gather(cur)

        @pl.when(i < nv)
        def _():
            blk = jnp.minimum(i + 2, nv - 1)
            for cp in idx_copies(blk, blk, lax.rem(i + 2, 4)):
                cp.start()

        start_gather(lax.rem(i + 1, 4), nxt)
        pslot = lax.rem(i + 3, 4)
        for j in range(bm):
            r0 = pl.multiple_of(dst_s[pslot * bm + j], TOK_ROWS)
            pltpu.make_async_copy(ybuf.at[nxt, pl.ds(j * TOK_ROWS, TOK_ROWS)],
                                  out_hbm.at[pl.ds(r0, TOK_ROWS)], sem_s).start(priority=j % 2)
        x = jnp.concatenate([xbuf[cur, pl.ds(cc, bm, stride=TOK_ROWS), :] for cc in range(TOK_ROWS)],
                            axis=1).astype(BF16)
        hg = jnp.dot(x, w1g_ref[0, 0, 0], preferred_element_type=F32) + b1g_ref[0]
        hl = jnp.dot(x, w1l_ref[0, 0, 0], preferred_element_type=F32) + b1l_ref[0]
        hg = jnp.minimum(hg, SWIGLU_LIMIT)
        hl = jnp.clip(hl, -SWIGLU_LIMIT, SWIGLU_LIMIT)
        act = hg * _sigmoid(SWIGLU_ALPHA * hg) * (hl + 1.0)
        y = jnp.dot(act.astype(BF16), w2_ref[0, 0].astype(BF16), preferred_element_type=F32) + b2_ref[0]
        gi = lax.broadcasted_iota(jnp.int32, (bm, bm), 0)
        gj = lax.broadcasted_iota(jnp.int32, (bm, bm), 1)
        y = y * jnp.sum(jnp.where(gi == gj, g_ref[0], 0.0), axis=1, keepdims=True)
        for cc in range(TOK_ROWS):
            ybuf[cur, pl.ds(cc, bm, stride=TOK_ROWS), :] = y[:, cc * 128:(cc + 1) * 128]

        @pl.when(i == nv)
        def _():
            wait_scatter()
            wait_gather(nxt)
            ybuf[...] = jnp.zeros_like(ybuf)
            fills = [pltpu.make_async_copy(
                ybuf.at[sl], out_hbm.at[pl.ds((n_slots + sl * bm) * TOK_ROWS, bm * TOK_ROWS)], sem_s)
                for sl in range(2)]
            for cp in fills:
                cp.start()
            for cp in fills:
                cp.wait()


def _moe(block_exp, n_valid, row_src, row_dst, row_gate, h2_flat, w1, b1g, b1l, w2, b2, *, layer, n_slots):
    d = D_MODEL
    nb = row_src.shape[0]
    bm = MOE_BLOCK
    de = w1.shape[-1]
    blk = lambda i, nv: jnp.minimum(i, nv[0] - 1)
    wmap = lambda i, be, nv: (be[blk(i, nv)], 0, 0)
    return pl.pallas_call(
        functools.partial(_moe_kernel, nb=nb, n_slots=n_slots),
        grid_spec=pltpu.PrefetchScalarGridSpec(
            num_scalar_prefetch=2,
            grid=(nb + 1,),
            in_specs=[
                pl.BlockSpec((1, 1, bm), lambda i, be, nv: (blk(i, nv), 0, 0)),
                pl.BlockSpec((1, 1, 1, d, de), lambda i, be, nv: (layer, be[blk(i, nv)], 0, 0, 0)),
                pl.BlockSpec((1, 1, 1, d, de), lambda i, be, nv: (layer, be[blk(i, nv)], 1, 0, 0)),
                pl.BlockSpec((1, 1, de), wmap),
                pl.BlockSpec((1, 1, de), wmap),
                pl.BlockSpec((1, 1, de, d), lambda i, be, nv: (layer, be[blk(i, nv)], 0, 0)),
                pl.BlockSpec((1, 1, d), wmap),
                pl.BlockSpec(memory_space=pl.ANY),
                pl.BlockSpec(memory_space=pl.ANY),
                pl.BlockSpec(memory_space=pl.ANY),
            ],
            out_specs=pl.BlockSpec(memory_space=pl.ANY),
            scratch_shapes=[
                pltpu.SMEM((4 * bm,), jnp.int32),
                pltpu.SMEM((4 * bm,), jnp.int32),
                pltpu.VMEM((2, bm * TOK_ROWS, 128), F32),
                pltpu.VMEM((2, bm * TOK_ROWS, 128), F32),
                pltpu.SemaphoreType.DMA,
                pltpu.SemaphoreType.DMA((2,)),
                pltpu.SemaphoreType.DMA,
            ],
        ),
        out_shape=jax.ShapeDtypeStruct(((n_slots + 2 * bm) * TOK_ROWS, 128), F32),
        compiler_params=_cparams(("arbitrary",)),
        name="moe_experts",
    )(block_exp, n_valid, row_gate, w1, w1, b1g, b1l, w2, b2, row_src, row_dst, h2_flat)


def _combine_kernel(x_ref, y4_ref, modl_ref, modc_ref, fg_ref, o_ref, sum_s, *, tm, ctx_len, final, skip):
    i = pl.program_id(1) + skip
    tiles = (y4_ref[:, 0] + y4_ref[:, 1]) + (y4_ref[:, 2] + y4_ref[:, 3])
    sum_s[...] = tiles.reshape(tm * TOK_ROWS, 128)
    f = jnp.concatenate([sum_s[pl.ds(cc, tm, stride=TOK_ROWS), :] for cc in range(TOK_ROWS)], axis=1)
    pos = i * tm + lax.broadcasted_iota(jnp.int32, (tm, 1), 0)
    g2 = jnp.where(pos < ctx_len, modc_ref[5:6, :], modl_ref[0, 5:6, :])
    x_new = x_ref[0] + g2 * f
    if final:
        x_new = x_new * lax.rsqrt(jnp.mean(x_new * x_new, axis=-1, keepdims=True) + RMS_EPS) * fg_ref[...]
    o_ref[0] = x_new


def _combine(x, y4, modl, modc, fg, *, ctx_len, final):
    b, t, d = x.shape
    if final:
        tm = Q_TILE
        assert ctx_len % tm == 0
        skip = ctx_len // tm
    else:
        tm = _row_tile(t, 272)
        skip = 0
    npb = t // tm
    return pl.pallas_call(
        functools.partial(_combine_kernel, tm=tm, ctx_len=ctx_len, final=final, skip=skip),
        grid=(b, npb - skip),
        in_specs=[
            pl.BlockSpec((1, tm, d), lambda bi, i: (bi, i + skip, 0)),
            pl.BlockSpec((tm, TOP_K, TOK_ROWS, 128), lambda bi, i: (bi * (npb - skip) + i, 0, 0, 0)),
            pl.BlockSpec((1, 8, d), lambda bi, i: (bi, 0, 0)),
            pl.BlockSpec((8, d), lambda bi, i: (0, 0)),
            pl.BlockSpec((1, d), lambda bi, i: (0, 0)),
        ],
        out_specs=pl.BlockSpec((1, tm, d), lambda bi, i: (bi, i, 0)),
        out_shape=jax.ShapeDtypeStruct((b, t - skip * tm, d), F32),
        scratch_shapes=[pltpu.VMEM((tm * TOK_ROWS, 128), F32)],
        compiler_params=_cparams(("parallel", "parallel")),
        name="moe_combine",
    )(x, y4.reshape(-1, TOP_K, TOK_ROWS, 128), modl, modc, fg)


def _routing(e_flat, gate_flat, *, tm, seq, ctx_len, latent_only):
    npair = e_flat.shape[0]
    t = seq + ctx_len
    bm = MOE_BLOCK
    nb = npair // bm + N_EXPERTS
    nr = nb * bm
    npad = nr - npair
    experts = jnp.arange(N_EXPERTS, dtype=jnp.int32)
    ids = jnp.arange(npair, dtype=jnp.int32)
    tok_of = lambda i: (i // (TOP_K * tm)) * tm + i % tm
    if latent_only:
        e_flat = jnp.where(tok_of(ids) % t >= ctx_len, e_flat, N_EXPERTS)
    counts = jnp.sum((e_flat[:, None] == experts[None, :]).astype(jnp.int32), axis=0)
    padded = (counts + bm - 1) // bm * bm
    pad_end = jnp.cumsum(padded)
    fill_end = jnp.cumsum(padded - counts)
    dummy = jnp.arange(npad, dtype=jnp.int32)
    e_dummy = jnp.sum((dummy[:, None] >= fill_end[None, :]).astype(jnp.int32), axis=1)
    kb = nr
    keys = jnp.concatenate([e_flat * kb + ids, e_dummy * kb + npair + dummy])
    gates = jnp.concatenate([gate_flat, jnp.zeros((npad,), F32)])
    skey, sgate = lax.sort((keys, gates), num_keys=1)
    e_row = skey // kb
    ident = skey - e_row * kb
    valid = (ident < npair) & (e_row < N_EXPERTS)
    pair = jnp.where(valid, ident, 0)
    tok = tok_of(pair)
    kk = (pair // tm) % TOP_K
    row_gate = jnp.where(valid, sgate, 0.0).reshape(nb, 1, bm)
    row_src = (tok * TOK_ROWS).reshape(nb, 1, bm)
    if latent_only:
        slot = (tok // t) * seq + tok % t - ctx_len
        n_slots = (npair // TOP_K // t) * seq * TOP_K
    else:
        slot = tok
        n_slots = npair
    j = jnp.arange(nr, dtype=jnp.int32)
    dump = n_slots + ((j // bm) % 2) * bm + j % bm
    row_dst = jnp.where(valid, slot * TOP_K + kk, dump).reshape(nb, bm)
    row_dst = jnp.concatenate([row_dst, (n_slots + bm + jnp.arange(bm, dtype=jnp.int32))[None, :]],
                              axis=0)[:, None, :] * TOK_ROWS
    blocks = jnp.arange(nb, dtype=jnp.int32) * bm
    block_exp = jnp.minimum(jnp.sum((blocks[:, None] >= pad_end[None, :]).astype(jnp.int32), axis=1),
                            N_EXPERTS - 1)
    n_valid = (pad_end[-1] // bm).astype(jnp.int32).reshape(1)
    return row_src, row_dst, row_gate, block_exp, n_valid, n_slots


def _rope_tables(ctx_len, seq_len):
    inv_freq = ROPE_BASE ** (-jnp.arange(0, ROT_AXIS_DIM, 2, dtype=F32) / ROT_AXIS_DIM)
    p = jnp.arange(seq_len, dtype=jnp.int32)
    row = (p // GRID_W).astype(F32)
    col = (p % GRID_W).astype(F32)
    lane = np.arange(128)
    dd = lane % DIFF_HEAD_DIM
    fidx = jnp.asarray((dd % ROT_AXIS_DIM) % (ROT_AXIS_DIM // 2))
    use_col = jnp.asarray(dd >= ROT_AXIS_DIM)
    second_half = jnp.asarray((dd % ROT_AXIS_DIM) >= ROT_AXIS_DIM // 2)
    posm = jnp.where(use_col[None, :], col[:, None], row[:, None])
    ang = posm * inv_freq[fidx][None, :]
    cos = jnp.cos(ang)
    sin = jnp.sin(ang)
    sa = jnp.where(second_half[None, :], sin, 0.0)
    sb = jnp.where(second_half[None, :], 0.0, -sin)
    pad = lambda a, v: jnp.concatenate([jnp.full((ctx_len, 128), v, F32), a], axis=0)
    return pad(cos, 1.0), pad(sa, 0.0), pad(sb, 0.0)


def kernel(x, c, ctx, c_ctx, ada_w, ada_b, norm1_g, norm2_g, w_in, diff_lambda, diff_subln_g,
           diff_w_out, conv_w, conv_w_out, gla_w_a2, gla_b_a, gla_norm_g, gla_w_out, w_o,
           router_w, router_b, moe_w1, moe_b1, moe_w2, moe_b2, final_norm_g):
    b, s, d = x.shape
    cl = ctx.shape[1]
    t = cl + s
    nl = w_in.shape[0]
    assert d == D_MODEL and b <= 15

    xs = jnp.concatenate([ctx, x], axis=1)
    cc = jnp.concatenate([c, c_ctx[None, :], jnp.zeros((15 - b, d), F32)], axis=0)
    mod = _ada(cc, ada_w, ada_b).reshape(nl, 16, N_ADA, d)
    cos, sa, sb = _rope_tables(cl, s)

    col_scale = np.ones((w_in.shape[-1],), np.float32)
    col_scale[0:1024] = DIFF_HEAD_DIM ** -0.5
    col_scale[4608:4864] = GLA_DK ** -0.5
    w_s = (w_in * jnp.asarray(col_scale)).astype(BF16)
    w_all = jnp.concatenate(
        [w_s[..., :W_GATES], w_s[..., W_GATES + 2 * GLA_RANK:], w_s[..., W_GATES:W_GATES + 2 * GLA_RANK],
         jnp.zeros((nl, d, W_ALL_COLS - w_in.shape[-1]), BF16)], axis=-1)

    wa = jnp.zeros((nl, 2, 128, 256), F32)
    for p in range(2):
        wa = wa.at[:, p, 0:GLA_RANK, 0:128].set(gla_w_a2[:, 0, :, p * 128:(p + 1) * 128])
        wa = wa.at[:, p, GLA_RANK:2 * GLA_RANK, 128:256].set(gla_w_a2[:, 1, :, p * 128:(p + 1) * 128])
    wa = wa.astype(BF16)
    ba = jnp.stack([jnp.concatenate([gla_b_a[:, 0, p * 128:(p + 1) * 128],
                                     gla_b_a[:, 1, p * 128:(p + 1) * 128]], axis=-1)
                    for p in range(2)], axis=1)[:, :, None, :]

    cw = jnp.concatenate([conv_w, jnp.zeros((nl, 5, CONV_W), F32)], axis=1)
    rw = jnp.concatenate([router_w, jnp.zeros((nl, d, 128 - N_EXPERTS), F32)], axis=-1).astype(BF16)
    rb = jnp.concatenate([router_b, jnp.full((nl, 128 - N_EXPERTS), -jnp.inf, F32)], axis=-1)[:, None, :]
    w1 = jnp.moveaxis(moe_w1.reshape(nl, N_EXPERTS, d, D_EXPERT, 2), -1, 2).astype(BF16)
    b1 = moe_b1.reshape(nl, N_EXPERTS, 1, D_EXPERT, 2)
    b1g = b1[..., 0]
    b1l = b1[..., 1]
    w2 = moe_w2
    b2 = moe_b2[:, :, None, :]
    wdo = diff_w_out.astype(BF16)
    wco = conv_w_out.astype(BF16)
    wgo = gla_w_out.astype(BF16)
    wo = w_o.astype(BF16)

    for layer in range(nl):
        last = layer == nl - 1
        lam_init = 0.8 - 0.6 * math.exp(-0.3 * layer)
        ml = mod[layer]
        modl = jnp.concatenate([ml[:b], jnp.zeros((b, 2, d), F32)], axis=1)
        modc = jnp.concatenate([ml[b], jnp.zeros((2, d), F32)], axis=0)

        a16, a32 = _k1(xs, modl, modc, norm1_g[layer][None, :], cos, sa, sb, w_all, layer=layer, ctx_len=cl)
        od = _attn(a16, diff_lambda[layer], diff_subln_g[layer][None, :],
                   ctx_len=cl, seq_len=s, lam_init=lam_init)
        og = _gla(a16, a32, wa[layer], ba[layer], gla_norm_g[layer][None, :], ctx_len=cl)
        xs, h2, rt = _merge(xs, od, og, a32, cw[layer], wdo, wco, wgo, wo, modl, modc,
                            norm2_g[layer][None, :], rw[layer], rb[layer], layer=layer, ctx_len=cl)

        r = b * t
        tm_r = rt.shape[-1]
        rt = rt.reshape(-1, 4, TOP_K, tm_r)
        row_src, row_dst, row_gate, block_exp, n_valid, n_slots = _routing(
            rt[:, 0].astype(jnp.int32).reshape(-1), (rt[:, 1] + rt[:, 2] + rt[:, 3]).reshape(-1),
            tm=tm_r, seq=s, ctx_len=cl, latent_only=last)
        y4 = _moe(block_exp, n_valid, row_src, row_dst, row_gate, h2.reshape(r * TOK_ROWS, 128), w1,
                  b1g[layer], b1l[layer], w2, b2[layer], layer=layer, n_slots=n_slots)
        xs = _combine(xs, y4, modl, modc, final_norm_g[None, :], ctx_len=cl, final=last)

    return xs
```
